```python
import math
import jax, jax.numpy as jnp
from jax import lax
import numpy as np

D_MODEL = 4096
BATCH = 4
SEQ = 2048
DEPTH = 2
DEC_BATCH = 32
DEC_SEQ = 1
PAST_LEN = 16384
PAGE_SIZE = 128

N_META = 16
N_MIXERS = 2
N_LAYERS_A = (DEPTH + 1) // 2
N_LAYERS_B = DEPTH // 2
HG_EXPAND = 128
HG_HEADS = D_MODEL // HG_EXPAND
HG_DK = HG_EXPAND
HG_DV = D_MODEL // HG_HEADS
HG_F = HG_HEADS * HG_DK
HG_CHUNK = 16
SW_HEADS = 64
SW_KV = 8
SW_HD = D_MODEL // SW_HEADS
SW_GROUP = SW_HEADS // SW_KV
WINDOW = 128
SW_BLOCK = 128
REL_BUCKETS = 32
REL_MAX_DIST = 128
EPS = 1e-6
NEG = -1e30

kernel_name = "hgrn2_swa_sink_hybrid_step"


def rms_norm(x, g):
    xf = x.astype(jnp.float32)
    y = xf * lax.rsqrt(jnp.mean(xf * xf, axis=-1, keepdims=True) + EPS)
    return (y * g.astype(jnp.float32)).astype(x.dtype)


def t5_bucket(dist):
    max_exact = REL_BUCKETS // 2
    d = jnp.maximum(dist, 1).astype(jnp.float32)
    large = max_exact + (jnp.log(d / max_exact) / math.log(REL_MAX_DIST / max_exact)
                         * (REL_BUCKETS - max_exact)).astype(jnp.int32)
    large = jnp.minimum(large, REL_BUCKETS - 1)
    return jnp.where(dist < max_exact, dist, large)


def rel_bias_heads(dist, table):
    b = table.astype(jnp.float32)[t5_bucket(jnp.maximum(dist, 0))]
    b = jnp.transpose(b, (2, 0, 1))
    return b.reshape(SW_KV, SW_GROUP, dist.shape[0], dist.shape[1])


def hgrn_project(h, w_in, lb):
    B, T, _ = h.shape
    proj = h @ w_in
    q = proj[..., :HG_F]
    fp = proj[..., HG_F:2 * HG_F].astype(jnp.float32)
    i = proj[..., 2 * HG_F:3 * HG_F]
    g = proj[..., 3 * HG_F:]
    q = jax.nn.silu(q.astype(jnp.float32)).reshape(B, T, HG_HEADS, HG_DK)
    f = lb + (1.0 - lb) * jax.nn.sigmoid(fp)
    logf = jnp.log(f).reshape(B, T, HG_HEADS, HG_DK)
    k = ((1.0 - lb) * jax.nn.sigmoid(-fp)).reshape(B, T, HG_HEADS, HG_DK)
    v = i.astype(jnp.float32).reshape(B, T, HG_HEADS, HG_DV)
    return q, k, v, logf, g


def hgrn_chunked(q, k, v, logf, S0):
    B, T, H, _ = q.shape
    N = T // HG_CHUNK
    def r(a):
        return a.reshape(B, N, HG_CHUNK, H, a.shape[-1]).transpose(1, 0, 3, 2, 4)
    causal = jnp.tril(jnp.ones((HG_CHUNK, HG_CHUNK), dtype=bool))
    def step(S, xs):
        qc, kc, vc, lc = xs
        b = jnp.cumsum(lc, axis=2)
        qt = qc * jnp.exp(b)
        kt = kc * jnp.exp(-b)
        A = jnp.where(causal, jnp.einsum('bhtk,bhsk->bhts', qt, kt), 0.0)
        o = jnp.einsum('bhtk,bhkv->bhtv', qt, S) + jnp.einsum('bhts,bhsv->bhtv', A, vc)
        bl = b[:, :, -1:, :]
        S = jnp.exp(bl[:, :, 0, :])[..., None] * S + jnp.einsum('bhsk,bhsv->bhkv', kc * jnp.exp(bl - b), vc)
        return S, o
    S, o = lax.scan(step, S0, (r(q), r(k), r(v), r(logf)))
    o = o.transpose(1, 0, 3, 2, 4).reshape(B, T, H, HG_DV)
    return o, S


def hgrn_recurrent(q, k, v, logf, S0):
    def step(S, xs):
        qt, kt, vt, lt = xs
        S = jnp.exp(lt)[..., None] * S + kt[..., None] * vt[..., None, :]
        return S, jnp.einsum('bhk,bhkv->bhv', qt, S)
    tr = lambda a: a.transpose(1, 0, 2, 3)
    S, o = lax.scan(step, S0, (tr(q), tr(k), tr(v), tr(logf)))
    return tr(o), S


def hgrn_out(o, g, onorm, w_out):
    B, T = o.shape[:2]
    o = o * lax.rsqrt(jnp.mean(o * o, axis=-1, keepdims=True) + EPS)
    o = o.reshape(B, T, D_MODEL) * onorm.astype(jnp.float32)
    return (o.astype(g.dtype) * jax.nn.silu(g)) @ w_out


def swa_project(h, w_in):
    B, T, _ = h.shape
    proj = h @ w_in
    nq, nk = SW_HEADS * SW_HD, SW_KV * SW_HD
    q = proj[..., :nq].reshape(B, T, SW_HEADS, SW_HD)
    k = proj[..., nq:nq + nk].reshape(B, T, SW_KV, SW_HD)
    v = proj[..., nq + nk:nq + 2 * nk].reshape(B, T, SW_KV, SW_HD)
    g = proj[..., nq + 2 * nk:]
    return q, k, v, g


def sink_softmax_av(s, sinks, v):
    sk = sinks.astype(jnp.float32).reshape(SW_KV, SW_GROUP, 1, 1)
    m = jnp.maximum(jnp.max(s, axis=-1, keepdims=True), sk)
    p = jnp.exp(s - m)
    den = jnp.sum(p, axis=-1, keepdims=True) + jnp.exp(sk - m)
    return jnp.einsum('bngqk,bnkd->bngqd', p / den, v.astype(jnp.float32))


def swa_prompt(q, k, v, sinks, table):
    B, T = q.shape[:2]
    pad = (-T) % SW_BLOCK
    Tp = T + pad
    nb = Tp // SW_BLOCK
    scale = SW_HD ** -0.5
    padf = lambda a: jnp.pad(a, ((0, 0), (pad, 0), (0, 0), (0, 0)))
    qb = padf(q).reshape(B, nb, SW_BLOCK, SW_KV, SW_GROUP, SW_HD).transpose(1, 0, 3, 4, 2, 5)
    def band(a):
        ab = padf(a).reshape(B, nb, SW_BLOCK, SW_KV, SW_HD)
        prev = jnp.concatenate([jnp.zeros_like(ab[:, :1]), ab[:, :-1]], axis=1)
        return jnp.concatenate([prev, ab], axis=2).transpose(1, 0, 3, 2, 4)
    kband, vband = band(k), band(v)
    key_pos = (jnp.arange(nb)[:, None] - 1) * SW_BLOCK + jnp.arange(2 * SW_BLOCK)[None, :]
    key_valid = key_pos >= pad
    dist = jnp.arange(SW_BLOCK)[:, None] + SW_BLOCK - jnp.arange(2 * SW_BLOCK)[None, :]
    in_win = (dist >= 0) & (dist <= WINDOW)
    bias = rel_bias_heads(dist, table)
    def block(xs):
        qk, kk, vk, kvv = xs
        s = jnp.einsum('bngqd,bnkd->bngqk', qk, kk).astype(jnp.float32) * scale + bias
        s = jnp.where(in_win & kvv[None, :], s, NEG)
        return sink_softmax_av(s, sinks, vk)
    o = lax.map(block, (qb, kband, vband, key_valid))
    o = o.transpose(1, 0, 4, 2, 3, 5).reshape(B, Tp, SW_HEADS, SW_HD)
    return o[:, pad:]


def swa_sample(q, k, v, ck, cv, sinks, table):
    B, T = q.shape[:2]
    R = ck.shape[1]
    kall = jnp.concatenate([ck.astype(k.dtype), k], axis=1)
    vall = jnp.concatenate([cv.astype(v.dtype), v], axis=1)
    qpos = PAST_LEN + jnp.arange(T)
    kpos = jnp.concatenate([PAST_LEN - R + jnp.arange(R), PAST_LEN + jnp.arange(T)])
    dist = qpos[:, None] - kpos[None, :]
    in_win = (dist >= 0) & (dist <= WINDOW)
    bias = rel_bias_heads(dist, table)
    qg = q.reshape(B, T, SW_KV, SW_GROUP, SW_HD)
    s = jnp.einsum('bqngd,bknd->bngqk', qg, kall).astype(jnp.float32) * (SW_HD ** -0.5) + bias
    s = jnp.where(in_win, s, NEG)
    o = sink_softmax_av(s, sinks, vall.transpose(0, 2, 1, 3))
    o = o.transpose(0, 3, 1, 2, 4).reshape(B, T, SW_HEADS, SW_HD)
    return o, kall[:, -R:], vall[:, -R:]


def swa_out(o, g, w_out):
    B, T = o.shape[:2]
    return (o.reshape(B, T, D_MODEL).astype(g.dtype) * jax.nn.silu(g)) @ w_out


def setup_inputs(seed: int = 0) -> dict:
    key = jax.random.key(seed)
    ks = jax.random.split(key, 18)
    nrm = lambda k, shp, s=1.0: jax.random.normal(k, shp, jnp.float32) * s
    win_rows = min(WINDOW, PAST_LEN)
    sw_in_cols = SW_HEADS * SW_HD + 2 * SW_KV * SW_HD + D_MODEL
    hg_in_cols = 3 * HG_F + D_MODEL
    return {
        "x_prompt": nrm(ks[0], (BATCH, SEQ, D_MODEL)),
        "x_sample": nrm(ks[1], (DEC_BATCH, DEC_SEQ, D_MODEL)),
        "state_hgrn": nrm(ks[2], (N_LAYERS_A, DEC_BATCH, HG_HEADS, HG_DK, HG_DV), 0.5),
        "cache_k_win": nrm(ks[3], (N_LAYERS_B, DEC_BATCH, win_rows, SW_KV, SW_HD)),
        "cache_v_win": nrm(ks[4], (N_LAYERS_B, DEC_BATCH, win_rows, SW_KV, SW_HD)),
        "meta_tokens": nrm(ks[5], (N_META, D_MODEL)),
        "rel_bias": nrm(ks[6], (REL_BUCKETS, SW_HEADS), 0.5),
        "hg_lower_bounds": nrm(ks[7], (DEPTH + 1, HG_F), 0.5),
        "hg_norm": 1.0 + nrm(ks[8], (N_LAYERS_A, D_MODEL), 0.02),
        "hg_w_in": nrm(ks[9], (N_LAYERS_A, D_MODEL, hg_in_cols), D_MODEL ** -0.5),
        "hg_onorm": 1.0 + nrm(ks[10], (N_LAYERS_A, D_MODEL), 0.02),
        "hg_w_out": nrm(ks[11], (N_LAYERS_A, D_MODEL, D_MODEL), D_MODEL ** -0.5),
        "sw_norm": 1.0 + nrm(ks[12], (N_LAYERS_B, D_MODEL), 0.02),
        "sw_w_in": nrm(ks[13], (N_LAYERS_B, D_MODEL, sw_in_cols), D_MODEL ** -0.5),
        "sw_sinks": nrm(ks[14], (N_LAYERS_B, SW_HEADS), 0.5),
        "sw_w_out": nrm(ks[15], (N_LAYERS_B, D_MODEL, D_MODEL), D_MODEL ** -0.5),
        "final_norm": 1.0 + nrm(ks[16], (D_MODEL,), 0.02),
    }


def reference(x_prompt, x_sample, state_hgrn, cache_k_win, cache_v_win, meta_tokens, rel_bias,
              hg_lower_bounds, hg_norm, hg_w_in, hg_onorm, hg_w_out,
              sw_norm, sw_w_in, sw_sinks, sw_w_out, final_norm):
    lb_all = jnp.cumsum(jax.nn.softmax(hg_lower_bounds.astype(jnp.float32), axis=0), axis=0)
    meta = jnp.broadcast_to(meta_tokens.astype(x_prompt.dtype)[None], (x_prompt.shape[0], N_META, D_MODEL))
    xp = jnp.concatenate([meta, x_prompt], axis=1)
    xs = x_sample
    hg_sp, hg_ss, kp_l, vp_l, ks_l, vs_l = [], [], [], [], [], []
    for i in range(DEPTH):
        if i % N_MIXERS == 0:
            a = i // N_MIXERS
            lb = lb_all[i]
            hp = rms_norm(xp, hg_norm[a])
            q, k, v, lf, g = hgrn_project(hp, hg_w_in[a], lb)
            S0 = jnp.zeros((xp.shape[0], HG_HEADS, HG_DK, HG_DV), jnp.float32)
            o, Sp = hgrn_chunked(q, k, v, lf, S0)
            xp = xp + hgrn_out(o, g, hg_onorm[a], hg_w_out[a])
            hs = rms_norm(xs, hg_norm[a])
            q, k, v, lf, g = hgrn_project(hs, hg_w_in[a], lb)
            o, Ss = hgrn_recurrent(q, k, v, lf, state_hgrn[a].astype(jnp.float32))
            xs = xs + hgrn_out(o, g, hg_onorm[a], hg_w_out[a])
            hg_sp.append(Sp.astype(state_hgrn.dtype))
            hg_ss.append(Ss.astype(state_hgrn.dtype))
        else:
            b = i // N_MIXERS
            hp = rms_norm(xp, sw_norm[b])
            q, k, v, g = swa_project(hp, sw_w_in[b])
            o = swa_prompt(q, k, v, sw_sinks[b], rel_bias)
            xp = xp + swa_out(o, g, sw_w_out[b])
            kp_l.append(k[:, -WINDOW:].astype(cache_k_win.dtype))
            vp_l.append(v[:, -WINDOW:].astype(cache_v_win.dtype))
            hs = rms_norm(xs, sw_norm[b])
            q, k, v, g = swa_project(hs, sw_w_in[b])
            o, nk, nv = swa_sample(q, k, v, cache_k_win[b], cache_v_win[b], sw_sinks[b], rel_bias)
            xs = xs + swa_out(o, g, sw_w_out[b])
            ks_l.append(nk.astype(cache_k_win.dtype))
            vs_l.append(nv.astype(cache_v_win.dtype))
    y_prompt = rms_norm(xp, final_norm)[:, N_META:]
    y_sample = rms_norm(xs, final_norm)
    return (y_prompt, y_sample, jnp.stack(hg_sp), jnp.stack(kp_l), jnp.stack(vp_l),
            jnp.stack(hg_ss), jnp.stack(ks_l), jnp.stack(vs_l))
```

```python
import functools
import math

import numpy as np
import jax
import jax.numpy as jnp
from jax import lax
from jax.experimental import pallas as pl
from jax.experimental.pallas import tpu as pltpu

D_MODEL = 4096
BATCH = 4
SEQ = 2048
DEC_BATCH = 32
N_META = 16
HG_HEADS = 32
HG_DK = 128
HG_F = HG_HEADS * HG_DK
SW_HEADS = 64
SW_KV = 8
SW_HD = 64
SW_GROUP = SW_HEADS // SW_KV
WINDOW = 128
REL_BUCKETS = 32
REL_MAX_DIST = 128
EPS = 1e-6
NEG = -1e30

LANES = 128
V7X_VMEM_LIMIT = 58 * 1024 * 1024

CHUNK = 128
N_SMALL = 64
MAIN_ROWS = BATCH * SEQ

F32 = jnp.float32
BF16 = jnp.bfloat16


def _nt(a, b):
    return lax.dot_general(a, b, (((1,), (1,)), ((), ())), preferred_element_type=F32)


def _nn(a, b):
    return jnp.dot(a, b, preferred_element_type=F32)


def _sigmoid(x):
    return 1.0 / (1.0 + jnp.exp(-x))


def _rmsnorm_kernel(x_ref, g_ref, o_ref):
    x = x_ref[...]
    ms = jnp.mean(x * x, axis=-1, keepdims=True)
    o_ref[...] = (x * lax.rsqrt(ms + EPS) * g_ref[...]).astype(o_ref.dtype)


def _rmsnorm(x, g, out_dtype, tm):
    m, d = x.shape
    return pl.pallas_call(
        _rmsnorm_kernel,
        grid=(m // tm,),
        in_specs=[pl.BlockSpec((tm, d), lambda i: (i, 0)),
                  pl.BlockSpec((1, d), lambda i: (0, 0))],
        out_specs=pl.BlockSpec((tm, d), lambda i: (i, 0)),
        out_shape=jax.ShapeDtypeStruct((m, d), out_dtype),
        compiler_params=pltpu.CompilerParams(dimension_semantics=("parallel",)),
        name="rmsnorm",
    )(x, g.reshape(1, d))


def _proj_kernel(a_ref, as_ref, w_ref, *rest, k_chunk, has_res):
    if has_res:
        r_ref, rs_ref, o_ref, os_ref, wb_ref = rest
    else:
        o_ref, os_ref, wb_ref = rest
    i = pl.program_id(1)

    @pl.when(i == 0)
    def _():
        def cast(c, carry):
            rows = pl.ds(pl.multiple_of(c * k_chunk, k_chunk), k_chunk)
            wb_ref[rows, :] = w_ref[rows, :].astype(BF16)
            return carry
        lax.fori_loop(0, w_ref.shape[0] // k_chunk, cast, 0)
        small = _nn(as_ref[...], wb_ref[...])
        if has_res:
            small = small + rs_ref[...]
        os_ref[...] = small

    out = _nn(a_ref[...], wb_ref[...])
    if has_res:
        out = out + r_ref[...]
    o_ref[...] = out


def _proj(a, a_small, w, res=None, res_small=None, *, tm, tn):
    m, k = a.shape
    n = w.shape[1]
    ms = a_small.shape[0]
    has_res = res is not None
    in_specs = [pl.BlockSpec((tm, k), lambda j, i: (i, 0)),
                pl.BlockSpec((ms, k), lambda j, i: (0, 0)),
                pl.BlockSpec((k, tn), lambda j, i: (0, j))]
    args = [a, a_small, w]
    if has_res:
        in_specs += [pl.BlockSpec((tm, tn), lambda j, i: (i, j)),
                     pl.BlockSpec((ms, tn), lambda j, i: (0, j))]
        args += [res, res_small]
    return pl.pallas_call(
        functools.partial(_proj_kernel, k_chunk=512, has_res=has_res),
        grid=(n // tn, m // tm),
        in_specs=in_specs,
        out_specs=[pl.BlockSpec((tm, tn), lambda j, i: (i, j)),
                   pl.BlockSpec((ms, tn), lambda j, i: (0, j))],
        out_shape=[jax.ShapeDtypeStruct((m, n), F32),
                   jax.ShapeDtypeStruct((ms, n), F32)],
        scratch_shapes=[pltpu.VMEM((k, tn), BF16)],
        compiler_params=pltpu.CompilerParams(
            dimension_semantics=("arbitrary", "arbitrary"),
            vmem_limit_bytes=V7X_VMEM_LIMIT),
        name="proj_res" if has_res else "proj",
    )(*args)


def _hgrn_consts():
    t = np.arange(CHUNK)[:, None]
    s = np.arange(CHUNK)[None, :]
    tri = s <= t
    ref0 = s <= 32 * (t // 32) + 15
    ref1 = s <= 64 * (t // 64) + 31
    ref2 = s <= 63 + 0 * t
    last = np.ones_like(tri)
    mall = np.concatenate([tri, ref0, ref1, ref2, last], axis=0).astype(np.float32)
    lvl = np.full((CHUNK, CHUNK), 3, np.int32)
    lvl[(t >= 64) & (s < 64)] = 2
    lvl[(t // 64 == s // 64) & (t % 64 >= 32) & (s % 64 < 32)] = 1
    lvl[(t // 32 == s // 32) & (s <= t)] = 0
    return jnp.asarray(mall, BF16), jnp.asarray(lvl)


def _lower_bound(lbraw):
    mx = jnp.max(lbraw, axis=0, keepdims=True)
    e = jnp.exp(lbraw - mx)
    return e[0:1, :] / jnp.sum(e, axis=0, keepdims=True)


def _head_out(o, graw, onorm):
    ms = jnp.mean(o * o, axis=-1, keepdims=True)
    return (o * lax.rsqrt(ms + EPS) * onorm) * (graw * _sigmoid(graw))


def _hgrn_chunk(qraw, fp, v, graw, lb, onorm, st, mall, lvl, row_valid):
    sig = _sigmoid(fp)
    f = lb + (1.0 - lb) * sig
    logf = jnp.log(f)
    k = (1.0 - lb) * (1.0 - sig)
    if row_valid is not None:
        logf = jnp.where(row_valid, logf, 0.0)
        k = jnp.where(row_valid, k, 0.0)
    q = qraw * _sigmoid(qraw)

    hi = logf.astype(BF16)
    mid = (logf - hi.astype(F32)).astype(BF16)
    cs = _nn(mall, jnp.concatenate([hi, mid], axis=1))
    cs = cs[:, :LANES] + cs[:, LANES:]
    b = cs[0:CHUNK]
    r0 = cs[CHUNK:2 * CHUNK]
    r1 = cs[2 * CHUNK:3 * CHUNK]
    r2 = cs[3 * CHUNK:4 * CHUNK]
    bl = cs[4 * CHUNK:5 * CHUNK]

    a0 = _nt((q * jnp.exp(b - r0)).astype(BF16), (k * jnp.exp(r0 - b)).astype(BF16))
    a1 = _nt((q * jnp.exp(jnp.minimum(b - r1, 0.0))).astype(BF16),
             (k * jnp.exp(jnp.minimum(r1 - b, 0.0))).astype(BF16))
    a2 = _nt((q * jnp.exp(jnp.minimum(b - r2, 0.0))).astype(BF16),
             (k * jnp.exp(jnp.minimum(r2 - b, 0.0))).astype(BF16))
    a = jnp.where(lvl == 0, a0, jnp.where(lvl == 1, a1, jnp.where(lvl == 2, a2, 0.0)))

    qe = q * jnp.exp(b)
    kd = k * jnp.exp(bl - b)
    vt = v.T
    lhs = jnp.concatenate([a, qe], axis=1).astype(BF16)
    rhs = jnp.concatenate([vt, st], axis=1).astype(BF16)
    o = _nt(lhs, rhs)
    st_new = st * jnp.exp(bl[0:1, :]) + _nn(vt.astype(BF16), kd.astype(BF16))
    return _head_out(o, graw, onorm).astype(BF16), st_new


def _hgrn_scan_kernel(q_ref, f_ref, v_ref, g_ref, lb_ref, on_ref, s0_ref, mall_ref, lvl_ref,
                      a_ref, s_ref, st_ref, *, hb, n_chunks, n_pad):
    t = pl.program_id(2)

    @pl.when(t == 0)
    def _():
        for h in range(hb):
            st_ref[h] = s0_ref[h].T

    lb_all = _lower_bound(lb_ref[...])
    mall = mall_ref[...]
    lvl = lvl_ref[...]
    row_valid = None
    if n_pad:
        row_valid = lax.broadcasted_iota(jnp.int32, (CHUNK, LANES), 0) >= n_pad

    def body(c, carry):
        rows = pl.ds(pl.multiple_of(c * CHUNK, CHUNK), CHUNK)
        for h in range(hb):
            cols = slice(h * LANES, (h + 1) * LANES)
            a, st_new = _hgrn_chunk(
                q_ref[0, rows, cols], f_ref[0, rows, cols], v_ref[0, rows, cols],
                g_ref[0, rows, cols], lb_all[:, cols], on_ref[:, cols], st_ref[h],
                mall, lvl, row_valid)
            a_ref[0, rows, cols] = a
            st_ref[h] = st_new
        return carry

    lax.fori_loop(0, n_chunks, body, 0)

    @pl.when(t == pl.num_programs(2) - 1)
    def _():
        for h in range(hb):
            s_ref[0, h] = st_ref[h].T


def _hgrn_scan(proj, lbraw, onorm, s0, *, tb, hb, n_pad=0):
    bsz, tlen, _ = proj.shape
    mall, lvl = _hgrn_consts()
    hcols = HG_F // (hb * LANES)
    sect = lambda s: (lambda b, h, t: (b, t, s * hcols + h))
    blk = (1, tb, hb * LANES)
    return pl.pallas_call(
        functools.partial(_hgrn_scan_kernel, hb=hb, n_chunks=tb // CHUNK, n_pad=n_pad),
        grid=(bsz, HG_HEADS // hb, tlen // tb),
        in_specs=[pl.BlockSpec(blk, sect(0)), pl.BlockSpec(blk, sect(1)),
                  pl.BlockSpec(blk, sect(2)), pl.BlockSpec(blk, sect(3)),
                  pl.BlockSpec((3, hb * LANES), lambda b, h, t: (0, h)),
                  pl.BlockSpec((1, hb * LANES), lambda b, h, t: (0, h)),
                  pl.BlockSpec((hb, HG_DK, HG_DK), lambda b, h, t: (h, 0, 0)),
                  pl.BlockSpec((5 * CHUNK, CHUNK), lambda b, h, t: (0, 0)),
                  pl.BlockSpec((CHUNK, CHUNK), lambda b, h, t: (0, 0))],
        out_specs=[pl.BlockSpec(blk, lambda b, h, t: (b, t, h)),
                   pl.BlockSpec((1, hb, HG_DK, HG_DK), lambda b, h, t: (b, h, 0, 0))],
        out_shape=[jax.ShapeDtypeStruct((bsz, tlen, D_MODEL), BF16),
                   jax.ShapeDtypeStruct((bsz, HG_HEADS, HG_DK, HG_DK), F32)],
        scratch_shapes=[pltpu.VMEM((hb, HG_DK, HG_DK), F32)],
        compiler_params=pltpu.CompilerParams(
            dimension_semantics=("parallel", "parallel", "arbitrary")),
        name="hgrn_scan",
    )(proj, proj, proj, proj, lbraw, onorm.reshape(1, D_MODEL), s0, mall, lvl)


def _hgrn_step_kernel(q_ref, f_ref, v_ref, g_ref, lb_ref, on_ref, s0_ref, a_ref, s_ref):
    lb = _lower_bound(lb_ref[...])[0]
    fp = f_ref[0]
    qraw = q_ref[0]
    v = v_ref[0]
    sig = _sigmoid(fp)
    f = lb + (1.0 - lb) * sig
    decay = jnp.exp(jnp.log(f))
    k = (1.0 - lb) * (1.0 - sig)
    q = qraw * _sigmoid(qraw)
    dt, kt, qt = decay.T, k.T, q.T
    rows = []
    for h in range(HG_HEADS):
        s_new = dt[:, h:h + 1] * s0_ref[0, 0, h] + kt[:, h:h + 1] * v[h:h + 1, :]
        s_ref[0, 0, h] = s_new
        rows.append(jnp.sum(qt[:, h:h + 1] * s_new, axis=0, keepdims=True))
    o = jnp.concatenate(rows, axis=0)
    a_ref[0] = _head_out(o, g_ref[0], on_ref[...]).astype(BF16)


def _hgrn_step(q, f, v, g, lbraw, onorm, state):
    bsz = q.shape[0]
    vec = pl.BlockSpec((1, HG_HEADS, HG_DK), lambda b: (b, 0, 0))
    full = pl.BlockSpec((HG_HEADS, HG_DK), lambda b: (0, 0))
    lbspec = pl.BlockSpec((3, HG_HEADS, HG_DK), lambda b: (0, 0, 0))
    sspec = pl.BlockSpec((1, 1, HG_HEADS, HG_DK, HG_DK), lambda b: (0, b, 0, 0, 0))
    return pl.pallas_call(
        _hgrn_step_kernel,
        grid=(bsz,),
        in_specs=[vec, vec, vec, vec, lbspec, full, sspec],
        out_specs=[vec, sspec],
        out_shape=[jax.ShapeDtypeStruct((bsz, HG_HEADS, HG_DK), BF16),
                   jax.ShapeDtypeStruct(state.shape, state.dtype)],
        compiler_params=pltpu.CompilerParams(dimension_semantics=("parallel",)),
        name="hgrn_step",
    )(q, f, v, g, lbraw.reshape(3, HG_HEADS, HG_DK), onorm.reshape(HG_HEADS, HG_DK), state)


def _t5_bucket(dist):
    max_exact = REL_BUCKETS // 2
    d = jnp.maximum(dist, 1).astype(F32)
    large = max_exact + (jnp.log(d / max_exact) / math.log(REL_MAX_DIST / max_exact)
                         * (REL_BUCKETS - max_exact)).astype(jnp.int32)
    large = jnp.minimum(large, REL_BUCKETS - 1)
    return jnp.where(dist < max_exact, dist, large)


def _prompt_bias(table, first):
    qi = jnp.arange(WINDOW)[:, None]
    ki = jnp.arange(2 * WINDOW)[None, :]
    dist = qi + WINDOW - ki
    ok = (dist >= 0) & (dist <= WINDOW)
    if first:
        ok = ok & (ki >= WINDOW - N_META)
    bias = table.astype(F32)[_t5_bucket(jnp.maximum(dist, 0))]
    bias = jnp.where(ok[:, :, None], bias, NEG)
    bias = bias.reshape(WINDOW, 2 * WINDOW, SW_KV, SW_GROUP // 2, 2)
    bias = jnp.transpose(bias, (2, 3, 0, 4, 1))
    return bias.reshape(SW_KV, (SW_GROUP // 2) * WINDOW, 4 * WINDOW)


def _expand_band(pair, u):
    lane = lax.broadcasted_iota(jnp.int32, pair.shape, 1)
    rolled = pltpu.roll(pair, SW_HD, axis=1)
    lo_src, hi_src = (pair, rolled) if u == 0 else (rolled, pair)
    top = jnp.where(lane < SW_HD, lo_src, 0.0)
    bot = jnp.where(lane >= SW_HD, hi_src, 0.0)
    return jnp.concatenate([top, bot], axis=0).astype(BF16)


def _swa_block(q4, kband, vband, bias, sink_a, sink_b, u):
    kx = _expand_band(kband, u)
    vx = _expand_band(vband, u)
    s = _nt(q4, kx) + bias
    half = 2 * WINDOW
    ps, dens = [], []
    for s_h, sink in ((s[:, :half], sink_a), (s[:, half:], sink_b)):
        m = jnp.maximum(jnp.max(s_h, axis=-1, keepdims=True), sink)
        p = jnp.exp(s_h - m)
        dens.append(jnp.sum(p, axis=-1, keepdims=True) + jnp.exp(sink - m))
        ps.append(p)
    pv = _nn(jnp.concatenate(ps, axis=1).astype(BF16), vx)
    lane = lax.broadcasted_iota(jnp.int32, pv.shape, 1)
    return pv / jnp.where(lane < SW_HD, dens[0], dens[1])


def _swa_prompt_kernel(sink_ref, q_ref, g_ref, k_ref, v_ref, km_ref, vm_ref, bias_ref,
                       o_ref, *, n_blocks):
    p = pl.program_id(1)
    qb = pl.program_id(2)
    n_pairs = SW_GROUP // 2

    def sink_cols(u):
        cols = []
        for half in range(2):
            cols.append(jnp.concatenate(
                [jnp.full((WINDOW, 1), sink_ref[(p * 2 + u) * SW_GROUP + 2 * j + half], F32)
                 for j in range(n_pairs)], axis=0))
        return cols

    sinks = [sink_cols(u) for u in range(2)]

    def block(blk, carry):
        tok = qb * n_blocks + blk
        rows = pl.ds(pl.multiple_of(blk * WINDOW, WINDOW), WINDOW)
        cur = pl.ds(pl.multiple_of(tok * WINDOW, WINDOW), WINDOW)
        prev = pl.ds(pl.multiple_of(jnp.maximum(tok - 1, 0) * WINDOW, WINDOW), WINDOW)
        is_first = tok == 0
        kband = jnp.concatenate([jnp.where(is_first, km_ref[...], k_ref[0, prev, :]),
                                 k_ref[0, cur, :]], axis=0)
        vband = jnp.concatenate([jnp.where(is_first, vm_ref[...], v_ref[0, prev, :]),
                                 v_ref[0, cur, :]], axis=0)
        for u in range(2):
            base = u * SW_GROUP * SW_HD
            q4 = jnp.concatenate(
                [q_ref[0, rows, base + j * LANES:base + (j + 1) * LANES] for j in range(n_pairs)],
                axis=0)
            q4 = (q4 * (SW_HD ** -0.5)).astype(BF16)
            bias = bias_ref[jnp.where(is_first, 0, 1), u]
            o4 = _swa_block(q4, kband, vband, bias, sinks[u][0], sinks[u][1], u)
            o = jnp.concatenate([o4[j * WINDOW:(j + 1) * WINDOW] for j in range(n_pairs)], axis=1)
            graw = g_ref[0, rows, base:base + SW_GROUP * SW_HD]
            o_ref[0, rows, base:base + SW_GROUP * SW_HD] = (o * (graw * _sigmoid(graw))).astype(BF16)
        return carry

    lax.fori_loop(0, n_blocks, block, 0)


def _swa_prompt(proj, meta_k, meta_v, sinks, bias, *, tq):
    bsz, tlen, _ = proj.shape
    pw = 2 * SW_GROUP * SW_HD
    qblocks = SW_HEADS * SW_HD // pw
    kcol0 = SW_HEADS * SW_HD // LANES
    vcol0 = kcol0 + SW_KV * SW_HD // LANES
    gblk0 = (SW_HEADS * SW_HD + 2 * SW_KV * SW_HD) // pw
    return pl.pallas_call(
        functools.partial(_swa_prompt_kernel, n_blocks=tq // WINDOW),
        grid=(bsz, qblocks, tlen // tq),
        in_specs=[pl.BlockSpec(memory_space=pltpu.SMEM),
                  pl.BlockSpec((1, tq, pw), lambda b, p, t: (b, t, p)),
                  pl.BlockSpec((1, tq, pw), lambda b, p, t: (b, t, gblk0 + p)),
                  pl.BlockSpec((1, tlen, LANES), lambda b, p, t: (b, 0, kcol0 + p)),
                  pl.BlockSpec((1, tlen, LANES), lambda b, p, t: (b, 0, vcol0 + p)),
                  pl.BlockSpec((WINDOW, LANES), lambda b, p, t: (0, p)),
                  pl.BlockSpec((WINDOW, LANES), lambda b, p, t: (0, p)),
                  pl.BlockSpec((2, 2, 4 * WINDOW, 4 * WINDOW), lambda b, p, t: (0, p, 0, 0))],
        out_specs=pl.BlockSpec((1, tq, pw), lambda b, p, t: (b, t, p)),
        out_shape=jax.ShapeDtypeStruct((bsz, tlen, D_MODEL), BF16),
        compiler_params=pltpu.CompilerParams(
            dimension_semantics=("parallel", "parallel", "arbitrary"),
            vmem_limit_bytes=V7X_VMEM_LIMIT),
        name="swa_prompt",
    )(sinks, proj, proj, proj, proj, meta_k, meta_v, bias)


def _swa_sample_kernel(q_ref, g_ref, kn_ref, vn_ref, ck_ref, cv_ref, bc_ref, bn_ref, sink_ref,
                       o_ref, nk_ref, nv_ref):
    nkv = SW_KV * SW_HD
    q = q_ref[0] * (SW_HD ** -0.5)
    row = lax.broadcasted_iota(jnp.int32, (SW_HEADS, nkv), 0)
    col = lax.broadcasted_iota(jnp.int32, (SW_HEADS, nkv), 1)
    own = (row // SW_GROUP) == (col // SW_HD)
    qx = jnp.where(own, jnp.concatenate([q] * SW_KV, axis=1), 0.0)
    ck, cv = ck_ref[0], cv_ref[0]
    kn, vn = kn_ref[0], vn_ref[0]
    sink = sink_ref[...]

    s_c = _nt(qx.astype(BF16), ck.astype(BF16)) + bc_ref[...]
    s_n = jnp.sum(qx * kn, axis=-1, keepdims=True) + bn_ref[...]
    m = jnp.maximum(jnp.maximum(jnp.max(s_c, axis=-1, keepdims=True), s_n), sink)
    p_c = jnp.exp(s_c - m)
    p_n = jnp.exp(s_n - m)
    den = jnp.sum(p_c, axis=-1, keepdims=True) + p_n + jnp.exp(sink - m)
    o_all = _nn(p_c.astype(BF16), cv.astype(BF16))
    o_all = o_all + p_n * vn
    o_all = jnp.where(own, o_all, 0.0)
    o = o_all[:, 0:SW_HD]
    for n in range(1, SW_KV):
        o = o + o_all[:, n * SW_HD:(n + 1) * SW_HD]
    graw = g_ref[0]
    o_ref[0] = ((o / den) * (graw * _sigmoid(graw))).astype(BF16)

    r = ck.shape[0]
    last = lax.broadcasted_iota(jnp.int32, ck.shape, 0) == r - 1
    nk_ref[0] = jnp.where(last, kn, pltpu.roll(ck, r - 1, axis=0))
    nv_ref[0] = jnp.where(last, vn, pltpu.roll(cv, r - 1, axis=0))


def _swa_sample(q, g, kn, vn, ck, cv, bias_c, bias_n, sinks):
    bsz, r, nkv = ck.shape
    head = pl.BlockSpec((1, SW_HEADS, SW_HD), lambda b: (b, 0, 0))
    new = pl.BlockSpec((1, 1, nkv), lambda b: (b, 0, 0))
    cache = pl.BlockSpec((1, r, nkv), lambda b: (b, 0, 0))
    return pl.pallas_call(
        _swa_sample_kernel,
        grid=(bsz,),
        in_specs=[head, head, new, new, cache, cache,
                  pl.BlockSpec((SW_HEADS, r), lambda b: (0, 0)),
                  pl.BlockSpec((SW_HEADS, 1), lambda b: (0, 0)),
                  pl.BlockSpec((SW_HEADS, 1), lambda b: (0, 0))],
        out_specs=[head, cache, cache],
        out_shape=[jax.ShapeDtypeStruct((bsz, SW_HEADS, SW_HD), BF16),
                   jax.ShapeDtypeStruct(ck.shape, ck.dtype),
                   jax.ShapeDtypeStruct(cv.shape, cv.dtype)],
        compiler_params=pltpu.CompilerParams(dimension_semantics=("parallel",)),
        name="swa_sample",
    )(q, g, kn, vn, ck, cv, bias_c, bias_n, sinks.reshape(SW_HEADS, 1))


def kernel(x_prompt, x_sample, state_hgrn, cache_k_win, cache_v_win, meta_tokens, rel_bias,
           hg_lower_bounds, hg_norm, hg_w_in, hg_onorm, hg_w_out,
           sw_norm, sw_w_in, sw_sinks, sw_w_out, final_norm):
    n_samp = x_sample.shape[0]
    samp = slice(N_META, N_META + n_samp)
    x_main = x_prompt.reshape(MAIN_ROWS, D_MODEL)
    x_small = jnp.concatenate(
        [meta_tokens.astype(F32), x_sample.reshape(n_samp, D_MODEL),
         jnp.zeros((N_SMALL - N_META - n_samp, D_MODEL), F32)], axis=0)

    h_main = _rmsnorm(x_main, hg_norm[0], BF16, 256)
    h_small = _rmsnorm(x_small, hg_norm[0], BF16, N_SMALL)
    p_main, p_small = _proj(h_main, h_small, hg_w_in[0], tm=512, tn=1024)

    meta_proj = jnp.pad(p_small[:N_META], ((CHUNK - N_META, 0), (0, 0)))[None]
    zero_state = jnp.zeros((HG_HEADS, HG_DK, HG_DK), F32)
    a_meta, s_meta = _hgrn_scan(meta_proj, hg_lower_bounds, hg_onorm[0], zero_state,
                                tb=CHUNK, hb=2, n_pad=CHUNK - N_META)
    a_main, s_prompt = _hgrn_scan(p_main.reshape(BATCH, SEQ, -1), hg_lower_bounds, hg_onorm[0],
                                  s_meta[0], tb=512, hb=2)
    sect = lambda s: p_small[samp, s * HG_F:(s + 1) * HG_F].reshape(n_samp, HG_HEADS, HG_DK)
    a_samp, s_sample = _hgrn_step(sect(0), sect(1), sect(2), sect(3), hg_lower_bounds,
                                  hg_onorm[0], state_hgrn)
    a_small = jnp.concatenate(
        [a_meta[0, CHUNK - N_META:], a_samp.reshape(n_samp, D_MODEL),
         jnp.zeros((N_SMALL - N_META - n_samp, D_MODEL), BF16)], axis=0)
    x1_main, x1_small = _proj(a_main.reshape(MAIN_ROWS, D_MODEL), a_small, hg_w_out[0],
                              x_main, x_small, tm=1024, tn=512)

    h_main = _rmsnorm(x1_main, sw_norm[0], BF16, 256)
    h_small = _rmsnorm(x1_small, sw_norm[0], BF16, N_SMALL)
    p_main, p_small = _proj(h_main, h_small, sw_w_in[0], tm=512, tn=1024)
    nq, nkv = SW_HEADS * SW_HD, SW_KV * SW_HD
    k_main = p_main[:, nq:nq + nkv].reshape(BATCH, SEQ, SW_KV, SW_HD)
    v_main = p_main[:, nq + nkv:nq + 2 * nkv].reshape(BATCH, SEQ, SW_KV, SW_HD)
    meta_kv = jnp.pad(p_small[:N_META, nq:nq + 2 * nkv], ((WINDOW - N_META, 0), (0, 0)))
    a_main = _swa_prompt(p_main.reshape(BATCH, SEQ, -1), meta_kv[:, :nkv], meta_kv[:, nkv:],
                         sw_sinks[0],
                         jnp.stack([_prompt_bias(rel_bias, True), _prompt_bias(rel_bias, False)]),
                         tq=512)

    r = cache_k_win.shape[2]
    table = rel_bias.astype(F32)
    bias_c = table[_t5_bucket(r - jnp.arange(r))].T
    bias_n = table[_t5_bucket(jnp.zeros((1,), jnp.int32))].T
    q_s = p_small[samp, :nq].reshape(n_samp, SW_HEADS, SW_HD)
    g_s = p_small[samp, nq + 2 * nkv:].reshape(n_samp, SW_HEADS, SW_HD)
    a_samp, k_samp, v_samp = _swa_sample(
        q_s, g_s, p_small[samp, nq:nq + nkv].reshape(n_samp, 1, nkv),
        p_small[samp, nq + nkv:nq + 2 * nkv].reshape(n_samp, 1, nkv),
        cache_k_win[0].reshape(n_samp, r, nkv), cache_v_win[0].reshape(n_samp, r, nkv),
        bias_c, bias_n, sw_sinks[0])
    a_small = jnp.concatenate(
        [jnp.zeros((N_META, D_MODEL), BF16), a_samp.reshape(n_samp, D_MODEL),
         jnp.zeros((N_SMALL - N_META - n_samp, D_MODEL), BF16)], axis=0)
    x2_main, x2_small = _proj(a_main.reshape(MAIN_ROWS, D_MODEL), a_small, sw_w_out[0],
                              x1_main, x1_small, tm=1024, tn=512)

    y_prompt = _rmsnorm(x2_main, final_norm, F32, 256).reshape(BATCH, SEQ, D_MODEL)
    y_sample = _rmsnorm(x2_small, final_norm, F32, N_SMALL)[samp].reshape(n_samp, 1, D_MODEL)
    return (y_prompt, y_sample, s_prompt[None],
            k_main[:, -WINDOW:][None].astype(cache_k_win.dtype),
            v_main[:, -WINDOW:][None].astype(cache_v_win.dtype),
            s_sample,
            k_samp.reshape(1, n_samp, r, SW_KV, SW_HD), v_samp.reshape(1, n_samp, r, SW_KV, SW_HD))
```

```python
import functools
import math

import numpy as np
import jax
import jax.numpy as jnp
from jax import lax
from jax.experimental import pallas as pl
from jax.experimental.pallas import tpu as pltpu

D_MODEL = 4096
BATCH = 4
SEQ = 2048
DEC_BATCH = 32
N_META = 16
HG_HEADS = 32
HG_DK = 128
HG_F = HG_HEADS * HG_DK
SW_HEADS = 64
SW_KV = 8
SW_HD = 64
SW_GROUP = SW_HEADS // SW_KV
WINDOW = 128
REL_BUCKETS = 32
REL_MAX_DIST = 128
EPS = 1e-6
NEG = -1e30
LOG2E = math.log2(math.e)

LANES = 128
V7X_VMEM_LIMIT = 58 * 1024 * 1024

CHUNK = 128
N_SMALL = 64
MAIN_ROWS = BATCH * SEQ

F32 = jnp.float32
BF16 = jnp.bfloat16


def _nt(a, b):
    return lax.dot_general(a, b, (((1,), (1,)), ((), ())), preferred_element_type=F32)


def _nn(a, b):
    return jnp.dot(a, b, preferred_element_type=F32)


def _sigmoid(x):
    return 1.0 / (1.0 + jnp.exp(-x))


def _silu(x):
    h = 0.5 * x
    return h + h * jnp.tanh(h)


def _rmsnorm_kernel(x_ref, g_ref, o_ref):
    x = x_ref[...]
    ms = jnp.mean(x * x, axis=-1, keepdims=True)
    o_ref[...] = (x * lax.rsqrt(ms + EPS) * g_ref[...]).astype(o_ref.dtype)


def _rmsnorm(x, g, out_dtype, tm):
    m, d = x.shape
    return pl.pallas_call(
        _rmsnorm_kernel,
        grid=(m // tm,),
        in_specs=[pl.BlockSpec((tm, d), lambda i: (i, 0)),
                  pl.BlockSpec((1, d), lambda i: (0, 0))],
        out_specs=pl.BlockSpec((tm, d), lambda i: (i, 0)),
        out_shape=jax.ShapeDtypeStruct((m, d), out_dtype),
        compiler_params=pltpu.CompilerParams(dimension_semantics=("parallel",)),
        name="rmsnorm",
    )(x, g.reshape(1, d))


def _proj_kernel(a_ref, as_ref, w_ref, *rest, k_chunk, has_res):
    if has_res:
        r_ref, rs_ref, o_ref, os_ref, wb_ref = rest
    else:
        o_ref, os_ref, wb_ref = rest
    i = pl.program_id(1)

    @pl.when(i == 0)
    def _():
        def cast(c, carry):
            rows = pl.ds(pl.multiple_of(c * k_chunk, k_chunk), k_chunk)
            wb_ref[rows, :] = w_ref[rows, :].astype(BF16)
            return carry
        lax.fori_loop(0, w_ref.shape[0] // k_chunk, cast, 0)
        small = _nn(as_ref[...], wb_ref[...])
        if has_res:
            small = small + rs_ref[...]
        os_ref[...] = small

    out = _nn(a_ref[...], wb_ref[...])
    if has_res:
        out = out + r_ref[...]
    o_ref[...] = out


def _proj(a, a_small, w, res=None, res_small=None, *, tm, tn):
    m, k = a.shape
    n = w.shape[1]
    ms = a_small.shape[0]
    has_res = res is not None
    in_specs = [pl.BlockSpec((tm, k), lambda j, i: (i, 0)),
                pl.BlockSpec((ms, k), lambda j, i: (0, 0)),
                pl.BlockSpec((k, tn), lambda j, i: (0, j))]
    args = [a, a_small, w]
    if has_res:
        in_specs += [pl.BlockSpec((tm, tn), lambda j, i: (i, j)),
                     pl.BlockSpec((ms, tn), lambda j, i: (0, j))]
        args += [res, res_small]
    return pl.pallas_call(
        functools.partial(_proj_kernel, k_chunk=512, has_res=has_res),
        grid=(n // tn, m // tm),
        in_specs=in_specs,
        out_specs=[pl.BlockSpec((tm, tn), lambda j, i: (i, j)),
                   pl.BlockSpec((ms, tn), lambda j, i: (0, j))],
        out_shape=[jax.ShapeDtypeStruct((m, n), F32),
                   jax.ShapeDtypeStruct((ms, n), F32)],
        scratch_shapes=[pltpu.VMEM((k, tn), BF16)],
        compiler_params=pltpu.CompilerParams(
            dimension_semantics=("arbitrary", "arbitrary"),
            vmem_limit_bytes=V7X_VMEM_LIMIT),
        name="proj_res" if has_res else "proj",
    )(*args)


def _hgrn_consts():
    t = np.arange(CHUNK)[:, None]
    s = np.arange(CHUNK)[None, :]
    tri = s <= t
    ends = np.array([15, 47, 79, 111, 31, 95, 63, 127] + [-1] * 8)[:, None]
    mall = np.concatenate([tri, s <= ends], axis=0).astype(np.float32)
    lvl = np.full((CHUNK, CHUNK), 3, np.int32)
    lvl[(t >= 64) & (s < 64)] = 2
    lvl[(t // 64 == s // 64) & (t % 64 >= 32) & (s % 64 < 32)] = 1
    lvl[(t // 32 == s // 32) & (s <= t)] = 0
    return jnp.asarray(mall, BF16), jnp.asarray(lvl)


def _lower_bound(lbraw):
    mx = jnp.max(lbraw, axis=0, keepdims=True)
    e = jnp.exp(lbraw - mx)
    return e[0:1, :] / jnp.sum(e, axis=0, keepdims=True)


def _head_out(o, graw, onorm):
    ms = jnp.mean(o * o, axis=-1, keepdims=True)
    return (o * lax.rsqrt(ms + EPS) * onorm) * _silu(graw)


def _hgrn_gates(qraw, fp, lb, mall, row_valid):
    sig = _sigmoid(fp)
    f = lb + (1.0 - lb) * sig
    logf = jnp.log2(f)
    k = (1.0 - lb) * (1.0 - sig)
    if row_valid is not None:
        logf = jnp.where(row_valid, logf, 0.0)
        k = jnp.where(row_valid, k, 0.0)
    q = _silu(qraw)

    hi = logf.astype(BF16)
    mid = (logf - hi.astype(F32)).astype(BF16)
    cs = _nn(mall, jnp.concatenate([hi, mid], axis=1))
    return q, k, cs[:, :LANES] + cs[:, LANES:]


def _hgrn_scores(q, k, cs):
    b = cs[0:CHUNK]
    ref = cs[CHUNK:CHUNK + 8]
    bl = ref[7:8]
    cat = lambda parts: jnp.concatenate(parts, axis=0)
    d0 = cat([b[32 * i:32 * i + 32] - ref[i:i + 1] for i in range(4)])
    d1 = cat([b[64 * i:64 * i + 64] - ref[4 + i:5 + i] for i in range(2)])
    d2 = b - ref[6:7]
    a0 = _nt((q * jnp.exp2(d0)).astype(BF16), (k * jnp.exp2(-d0)).astype(BF16))
    a1 = _nt((cat([q[32:64], q[96:128]]) * jnp.exp2(cat([d1[32:64], d1[96:128]]))).astype(BF16),
             (k * jnp.exp2(-d1)).astype(BF16))
    a2 = _nt((q[64:128] * jnp.exp2(d2[64:128])).astype(BF16),
             (k * jnp.exp2(-d2)).astype(BF16))
    qe = (q * jnp.exp2(b)).astype(BF16)
    kd = (k * jnp.exp2(bl - b)).astype(BF16)
    return a0, a1, a2, qe, kd, jnp.exp2(bl)


def _hgrn_apply(a0, a1, a2, qe, kd, decay, v, st, lvl):
    is0, is1, is2 = lvl == 0, lvl == 1, lvl == 2
    a = jnp.concatenate([
        jnp.where(is0[0:32], a0[0:32], 0.0),
        jnp.where(is0[32:64], a0[32:64], jnp.where(is1[32:64], a1[0:32], 0.0)),
        jnp.where(is0[64:96], a0[64:96], jnp.where(is2[64:96], a2[0:32], 0.0)),
        jnp.where(is0[96:128], a0[96:128],
                  jnp.where(is1[96:128], a1[32:64], jnp.where(is2[96:128], a2[32:64], 0.0))),
    ], axis=0)
    vt = v.T.astype(BF16)
    lhs = jnp.concatenate([a.astype(BF16), qe], axis=1)
    rhs = jnp.concatenate([vt, st.astype(BF16)], axis=1)
    return _nt(lhs, rhs), st * decay + _nn(vt, kd)


def _hgrn_scan_kernel(q_ref, f_ref, v_ref, g_ref, lb_ref, on_ref, s0_ref, mall_ref, lvl_ref,
                      a_ref, s_ref, st_ref, *, hb, n_chunks, n_pad):
    t = pl.program_id(2)

    @pl.when(t == 0)
    def _():
        for h in range(hb):
            st_ref[h] = s0_ref[h].T

    lb_all = _lower_bound(lb_ref[...])
    row_valid = None
    if n_pad:
        row_valid = lax.broadcasted_iota(jnp.int32, (CHUNK, LANES), 0) >= n_pad
    heads = [slice(h * LANES, (h + 1) * LANES) for h in range(hb)]

    def body(c, carry):
        rows = pl.ds(pl.multiple_of(c * CHUNK, CHUNK), CHUNK)
        gates = [_hgrn_gates(q_ref[0, rows, cols], f_ref[0, rows, cols], lb_all[:, cols],
                             mall_ref[...], row_valid) for cols in heads]
        scores = [_hgrn_scores(*g) for g in gates]
        outs = [_hgrn_apply(*sc, v_ref[0, rows, cols], st_ref[h], lvl_ref[...])
                for h, (sc, cols) in enumerate(zip(scores, heads))]
        for h, ((o, st_new), cols) in enumerate(zip(outs, heads)):
            st_ref[h] = st_new
            a_ref[0, rows, cols] = _head_out(o, g_ref[0, rows, cols], on_ref[:, cols]).astype(BF16)
        return carry

    lax.fori_loop(0, n_chunks, body, 0)

    @pl.when(t == pl.num_programs(2) - 1)
    def _():
        for h in range(hb):
            s_ref[0, h] = st_ref[h].T


def _hgrn_scan(proj, lbraw, onorm, s0, *, tb, hb, n_pad=0):
    bsz, tlen, _ = proj.shape
    mall, lvl = _hgrn_consts()
    hcols = HG_F // (hb * LANES)
    sect = lambda s: (lambda b, h, t: (b, t, s * hcols + h))
    blk = (1, tb, hb * LANES)
    return pl.pallas_call(
        functools.partial(_hgrn_scan_kernel, hb=hb, n_chunks=tb // CHUNK, n_pad=n_pad),
        grid=(bsz, HG_HEADS // hb, tlen // tb),
        in_specs=[pl.BlockSpec(blk, sect(0)), pl.BlockSpec(blk, sect(1)),
                  pl.BlockSpec(blk, sect(2)), pl.BlockSpec(blk, sect(3)),
                  pl.BlockSpec((3, hb * LANES), lambda b, h, t: (0, h)),
                  pl.BlockSpec((1, hb * LANES), lambda b, h, t: (0, h)),
                  pl.BlockSpec((hb, HG_DK, HG_DK), lambda b, h, t: (h, 0, 0)),
                  pl.BlockSpec((CHUNK + 16, CHUNK), lambda b, h, t: (0, 0)),
                  pl.BlockSpec((CHUNK, CHUNK), lambda b, h, t: (0, 0))],
        out_specs=[pl.BlockSpec(blk, lambda b, h, t: (b, t, h)),
                   pl.BlockSpec((1, hb, HG_DK, HG_DK), lambda b, h, t: (b, h, 0, 0))],
        out_shape=[jax.ShapeDtypeStruct((bsz, tlen, D_MODEL), BF16),
                   jax.ShapeDtypeStruct((bsz, HG_HEADS, HG_DK, HG_DK), F32)],
        scratch_shapes=[pltpu.VMEM((hb, HG_DK, HG_DK), F32)],
        compiler_params=pltpu.CompilerParams(
            dimension_semantics=("parallel", "parallel", "arbitrary")),
        name="hgrn_scan",
    )(proj, proj, proj, proj, lbraw, onorm.reshape(1, D_MODEL), s0, mall, lvl)


def _hgrn_step_kernel(q_ref, f_ref, v_ref, g_ref, lb_ref, on_ref, s0_ref, a_ref, s_ref):
    lb = _lower_bound(lb_ref[...])[0]
    fp = f_ref[0]
    qraw = q_ref[0]
    v = v_ref[0]
    sig = _sigmoid(fp)
    f = lb + (1.0 - lb) * sig
    decay = jnp.exp(jnp.log(f))
    k = (1.0 - lb) * (1.0 - sig)
    q = _silu(qraw)
    dt, kt, qt = decay.T, k.T, q.T
    rows = []
    for h in range(HG_HEADS):
        s_new = dt[:, h:h + 1] * s0_ref[0, 0, h] + kt[:, h:h + 1] * v[h:h + 1, :]
        s_ref[0, 0, h] = s_new
        rows.append(jnp.sum(qt[:, h:h + 1] * s_new, axis=0, keepdims=True))
    o = jnp.concatenate(rows, axis=0)
    a_ref[0] = _head_out(o, g_ref[0], on_ref[...]).astype(BF16)


def _hgrn_step(q, f, v, g, lbraw, onorm, state):
    bsz = q.shape[0]
    vec = pl.BlockSpec((1, HG_HEADS, HG_DK), lambda b: (b, 0, 0))
    full = pl.BlockSpec((HG_HEADS, HG_DK), lambda b: (0, 0))
    lbspec = pl.BlockSpec((3, HG_HEADS, HG_DK), lambda b: (0, 0, 0))
    sspec = pl.BlockSpec((1, 1, HG_HEADS, HG_DK, HG_DK), lambda b: (0, b, 0, 0, 0))
    return pl.pallas_call(
        _hgrn_step_kernel,
        grid=(bsz,),
        in_specs=[vec, vec, vec, vec, lbspec, full, sspec],
        out_specs=[vec, sspec],
        out_shape=[jax.ShapeDtypeStruct((bsz, HG_HEADS, HG_DK), BF16),
                   jax.ShapeDtypeStruct(state.shape, state.dtype)],
        compiler_params=pltpu.CompilerParams(dimension_semantics=("parallel",)),
        name="hgrn_step",
    )(q, f, v, g, lbraw.reshape(3, HG_HEADS, HG_DK), onorm.reshape(HG_HEADS, HG_DK), state)


def _t5_bucket(dist):
    max_exact = REL_BUCKETS // 2
    d = jnp.maximum(dist, 1).astype(F32)
    large = max_exact + (jnp.log(d / max_exact) / math.log(REL_MAX_DIST / max_exact)
                         * (REL_BUCKETS - max_exact)).astype(jnp.int32)
    large = jnp.minimum(large, REL_BUCKETS - 1)
    return jnp.where(dist < max_exact, dist, large)


def _prompt_bias(table):
    band = 2 * WINDOW
    n_dist = band + WINDOW - 1
    dist = jnp.arange(n_dist) - (WINDOW - 1)
    vals = table.astype(F32)[_t5_bucket(jnp.maximum(dist, 0))]
    vals = jnp.where(((dist >= 0) & (dist <= WINDOW))[:, None], vals, NEG) * LOG2E
    rows = jnp.tile(vals.T, (1, band + 1))[:, :band * (n_dist + 1)]
    toep = rows.reshape(SW_HEADS, band, n_dist + 1)[:, ::-1, :WINDOW]
    toep = toep.reshape(SW_KV, SW_GROUP // 2, 2, band, WINDOW)
    rest = jnp.transpose(toep, (0, 2, 3, 1, 4)).reshape(SW_KV, 2 * band, 4 * WINDOW)
    key = lax.broadcasted_iota(jnp.int32, rest.shape, 1) % band
    first = jnp.where(key < WINDOW - N_META, NEG * LOG2E, rest)
    return jnp.stack([first, rest])


def _expand_band(pair, u):
    lane = lax.broadcasted_iota(jnp.int32, pair.shape, 1)
    rolled = pltpu.roll(pair, SW_HD, axis=1)
    lo_src, hi_src = (pair, rolled) if u == 0 else (rolled, pair)
    top = jnp.where(lane < SW_HD, lo_src, 0.0)
    bot = jnp.where(lane >= SW_HD, hi_src, 0.0)
    return jnp.concatenate([top, bot], axis=0).astype(BF16)


def _swa_probs(s, sinks):
    band = 2 * WINDOW
    ps, ms = [], []
    for half in range(2):
        sh = s[half * band:(half + 1) * band]
        m = jnp.maximum(jnp.max(sh, axis=0, keepdims=True), sinks[half])
        ps.append(jnp.exp2(sh - m).astype(BF16))
        ms.append(m)
    return jnp.concatenate(ps, axis=0), ms


def _swa_values(pt, vxt, ms, sinks):
    ot = _nn(vxt, pt)
    outs = []
    for half in range(2):
        den = ot[2 * SW_HD + 8 * half:2 * SW_HD + 8 * half + 1] + jnp.exp2(sinks[half] - ms[half])
        outs.append(ot[SW_HD * half:SW_HD * (half + 1)] * (1.0 / den))
    return jnp.concatenate(outs, axis=0)


def _swa_prompt_kernel(sink_ref, q_ref, g_ref, k_ref, v_ref, km_ref, vm_ref, bias_ref,
                       o_ref, *, n_blocks):
    p = pl.program_id(1)
    qb = pl.program_id(2)
    n_pairs = SW_GROUP // 2
    band = 2 * WINDOW

    def sink_rows(u):
        return [jnp.concatenate(
            [jnp.full((1, WINDOW), sink_ref[(p * 2 + u) * SW_GROUP + 2 * j + half] * LOG2E, F32)
             for j in range(n_pairs)], axis=1) for half in range(2)]

    sinks = [sink_rows(u) for u in range(2)]
    orow = lax.broadcasted_iota(jnp.int32, (16, 2 * band), 0)
    ocol = lax.broadcasted_iota(jnp.int32, (16, 2 * band), 1)
    ones_rows = jnp.where((orow < 8) == (ocol < band), 1.0, 0.0).astype(BF16)
    zeros = jnp.zeros((SW_HD, band), BF16)

    def block(blk, carry):
        tok = qb * n_blocks + blk
        rows = pl.ds(pl.multiple_of(blk * WINDOW, WINDOW), WINDOW)
        cur = pl.ds(pl.multiple_of(tok * WINDOW, WINDOW), WINDOW)
        prev = pl.ds(pl.multiple_of(jnp.maximum(tok - 1, 0) * WINDOW, WINDOW), WINDOW)
        is_first = tok == 0
        kband = jnp.concatenate([jnp.where(is_first, km_ref[...], k_ref[0, prev, :]),
                                 k_ref[0, cur, :]], axis=0)
        vband = jnp.concatenate([jnp.where(is_first, vm_ref[...], v_ref[0, prev, :]),
                                 v_ref[0, cur, :]], axis=0)
        vbt = vband.T.astype(BF16)
        bias_sel = jnp.where(is_first, 0, 1)
        units = range(2)
        scores = []
        for u in units:
            base = u * SW_GROUP * SW_HD
            q4 = jnp.concatenate(
                [q_ref[0, rows, base + j * LANES:base + (j + 1) * LANES] for j in range(n_pairs)],
                axis=0)
            q4 = (q4 * (SW_HD ** -0.5 * LOG2E)).astype(BF16)
            scores.append(_nt(_expand_band(kband, u), q4) + bias_ref[bias_sel, u])
        probs = [_swa_probs(scores[u], sinks[u]) for u in units]
        outs = []
        for u in units:
            vt = vbt[u * SW_HD:(u + 1) * SW_HD]
            vxt = jnp.concatenate([jnp.concatenate([vt, zeros], axis=1),
                                   jnp.concatenate([zeros, vt], axis=1), ones_rows], axis=0)
            outs.append(_swa_values(probs[u][0], vxt, probs[u][1], sinks[u]))
        for u in units:
            base = u * SW_GROUP * SW_HD
            o = jnp.concatenate([outs[u][:, j * WINDOW:(j + 1) * WINDOW].T for j in range(n_pairs)],
                                axis=1)
            graw = g_ref[0, rows, base:base + SW_GROUP * SW_HD]
            o_ref[0, rows, base:base + SW_GROUP * SW_HD] = (o * _silu(graw)).astype(BF16)
        return carry

    lax.fori_loop(0, n_blocks, block, 0)


def _swa_prompt(proj, meta_k, meta_v, sinks, bias, *, tq):
    bsz, tlen, _ = proj.shape
    pw = 2 * SW_GROUP * SW_HD
    qblocks = SW_HEADS * SW_HD // pw
    kcol0 = SW_HEADS * SW_HD // LANES
    vcol0 = kcol0 + SW_KV * SW_HD // LANES
    gblk0 = (SW_HEADS * SW_HD + 2 * SW_KV * SW_HD) // pw
    return pl.pallas_call(
        functools.partial(_swa_prompt_kernel, n_blocks=tq // WINDOW),
        grid=(bsz, qblocks, tlen // tq),
        in_specs=[pl.BlockSpec(memory_space=pltpu.SMEM),
                  pl.BlockSpec((1, tq, pw), lambda b, p, t: (b, t, p)),
                  pl.BlockSpec((1, tq, pw), lambda b, p, t: (b, t, gblk0 + p)),
                  pl.BlockSpec((1, tlen, LANES), lambda b, p, t: (b, 0, kcol0 + p)),
                  pl.BlockSpec((1, tlen, LANES), lambda b, p, t: (b, 0, vcol0 + p)),
                  pl.BlockSpec((WINDOW, LANES), lambda b, p, t: (0, p)),
                  pl.BlockSpec((WINDOW, LANES), lambda b, p, t: (0, p)),
                  pl.BlockSpec((2, 2, 4 * WINDOW, 4 * WINDOW), lambda b, p, t: (0, p, 0, 0))],
        out_specs=pl.BlockSpec((1, tq, pw), lambda b, p, t: (b, t, p)),
        out_shape=jax.ShapeDtypeStruct((bsz, tlen, D_MODEL), BF16),
        compiler_params=pltpu.CompilerParams(
            dimension_semantics=("parallel", "parallel", "arbitrary"),
            vmem_limit_bytes=V7X_VMEM_LIMIT),
        name="swa_prompt",
    )(sinks, proj, proj, proj, proj, meta_k, meta_v, bias)


def _swa_sample_kernel(q_ref, g_ref, kn_ref, vn_ref, ck_ref, cv_ref, bc_ref, bn_ref, sink_ref,
                       o_ref, nk_ref, nv_ref):
    nkv = SW_KV * SW_HD
    q = q_ref[0] * (SW_HD ** -0.5)
    row = lax.broadcasted_iota(jnp.int32, (SW_HEADS, nkv), 0)
    col = lax.broadcasted_iota(jnp.int32, (SW_HEADS, nkv), 1)
    own = (row // SW_GROUP) == (col // SW_HD)
    qx = jnp.where(own, jnp.concatenate([q] * SW_KV, axis=1), 0.0)
    ck, cv = ck_ref[0], cv_ref[0]
    kn, vn = kn_ref[0], vn_ref[0]
    sink = sink_ref[...]

    s_c = _nt(qx.astype(BF16), ck.astype(BF16)) + bc_ref[...]
    s_n = jnp.sum(qx * kn, axis=-1, keepdims=True) + bn_ref[...]
    m = jnp.maximum(jnp.maximum(jnp.max(s_c, axis=-1, keepdims=True), s_n), sink)
    p_c = jnp.exp(s_c - m)
    p_n = jnp.exp(s_n - m)
    den = jnp.sum(p_c, axis=-1, keepdims=True) + p_n + jnp.exp(sink - m)
    o_all = _nn(p_c.astype(BF16), cv.astype(BF16))
    o_all = o_all + p_n * vn
    o_all = jnp.where(own, o_all, 0.0)
    o = o_all[:, 0:SW_HD]
    for n in range(1, SW_KV):
        o = o + o_all[:, n * SW_HD:(n + 1) * SW_HD]
    graw = g_ref[0]
    o_ref[0] = ((o / den) * _silu(graw)).astype(BF16)

    r = ck.shape[0]
    last = lax.broadcasted_iota(jnp.int32, ck.shape, 0) == r - 1
    nk_ref[0] = jnp.where(last, kn, pltpu.roll(ck, r - 1, axis=0))
    nv_ref[0] = jnp.where(last, vn, pltpu.roll(cv, r - 1, axis=0))


def _swa_sample(q, g, kn, vn, ck, cv, bias_c, bias_n, sinks):
    bsz, r, nkv = ck.shape
    head = pl.BlockSpec((1, SW_HEADS, SW_HD), lambda b: (b, 0, 0))
    new = pl.BlockSpec((1, 1, nkv), lambda b: (b, 0, 0))
    cache = pl.BlockSpec((1, r, nkv), lambda b: (b, 0, 0))
    return pl.pallas_call(
        _swa_sample_kernel,
        grid=(bsz,),
        in_specs=[head, head, new, new, cache, cache,
                  pl.BlockSpec((SW_HEADS, r), lambda b: (0, 0)),
                  pl.BlockSpec((SW_HEADS, 1), lambda b: (0, 0)),
                  pl.BlockSpec((SW_HEADS, 1), lambda b: (0, 0))],
        out_specs=[head, cache, cache],
        out_shape=[jax.ShapeDtypeStruct((bsz, SW_HEADS, SW_HD), BF16),
                   jax.ShapeDtypeStruct(ck.shape, ck.dtype),
                   jax.ShapeDtypeStruct(cv.shape, cv.dtype)],
        compiler_params=pltpu.CompilerParams(dimension_semantics=("parallel",)),
        name="swa_sample",
    )(q, g, kn, vn, ck, cv, bias_c, bias_n, sinks.reshape(SW_HEADS, 1))


def kernel(x_prompt, x_sample, state_hgrn, cache_k_win, cache_v_win, meta_tokens, rel_bias,
           hg_lower_bounds, hg_norm, hg_w_in, hg_onorm, hg_w_out,
           sw_norm, sw_w_in, sw_sinks, sw_w_out, final_norm):
    n_samp = x_sample.shape[0]
    samp = slice(N_META, N_META + n_samp)
    x_main = x_prompt.reshape(MAIN_ROWS, D_MODEL)
    x_small = jnp.concatenate(
        [meta_tokens.astype(F32), x_sample.reshape(n_samp, D_MODEL),
         jnp.zeros((N_SMALL - N_META - n_samp, D_MODEL), F32)], axis=0)

    h_main = _rmsnorm(x_main, hg_norm[0], BF16, 256)
    h_small = _rmsnorm(x_small, hg_norm[0], BF16, N_SMALL)
    p_main, p_small = _proj(h_main, h_small, hg_w_in[0], tm=512, tn=1024)

    meta_proj = jnp.pad(p_small[:N_META], ((CHUNK - N_META, 0), (0, 0)))[None]
    zero_state = jnp.zeros((HG_HEADS, HG_DK, HG_DK), F32)
    a_meta, s_meta = _hgrn_scan(meta_proj, hg_lower_bounds, hg_onorm[0], zero_state,
                                tb=CHUNK, hb=2, n_pad=CHUNK - N_META)
    a_main, s_prompt = _hgrn_scan(p_main.reshape(BATCH, SEQ, -1), hg_lower_bounds, hg_onorm[0],
                                  s_meta[0], tb=512, hb=8)
    sect = lambda s: p_small[samp, s * HG_F:(s + 1) * HG_F].reshape(n_samp, HG_HEADS, HG_DK)
    a_samp, s_sample = _hgrn_step(sect(0), sect(1), sect(2), sect(3), hg_lower_bounds,
                                  hg_onorm[0], state_hgrn)
    a_small = jnp.concatenate(
        [a_meta[0, CHUNK - N_META:], a_samp.reshape(n_samp, D_MODEL),
         jnp.zeros((N_SMALL - N_META - n_samp, D_MODEL), BF16)], axis=0)
    x1_main, x1_small = _proj(a_main.reshape(MAIN_ROWS, D_MODEL), a_small, hg_w_out[0],
                              x_main, x_small, tm=1024, tn=512)

    h_main = _rmsnorm(x1_main, sw_norm[0], BF16, 256)
    h_small = _rmsnorm(x1_small, sw_norm[0], BF16, N_SMALL)
    p_main, p_small = _proj(h_main, h_small, sw_w_in[0], tm=512, tn=1024)
    nq, nkv = SW_HEADS * SW_HD, SW_KV * SW_HD
    k_main = p_main[:, nq:nq + nkv].reshape(BATCH, SEQ, SW_KV, SW_HD)
    v_main = p_main[:, nq + nkv:nq + 2 * nkv].reshape(BATCH, SEQ, SW_KV, SW_HD)
    meta_kv = jnp.pad(p_small[:N_META, nq:nq + 2 * nkv], ((WINDOW - N_META, 0), (0, 0)))
    a_main = _swa_prompt(p_main.reshape(BATCH, SEQ, -1), meta_kv[:, :nkv], meta_kv[:, nkv:],
                         sw_sinks[0], _prompt_bias(rel_bias), tq=512)

    r = cache_k_win.shape[2]
    table = rel_bias.astype(F32)
    bias_c = table[_t5_bucket(r - jnp.arange(r))].T
    bias_n = table[_t5_bucket(jnp.zeros((1,), jnp.int32))].T
    q_s = p_small[samp, :nq].reshape(n_samp, SW_HEADS, SW_HD)
    g_s = p_small[samp, nq + 2 * nkv:].reshape(n_samp, SW_HEADS, SW_HD)
    a_samp, k_samp, v_samp = _swa_sample(
        q_s, g_s, p_small[samp, nq:nq + nkv].reshape(n_samp, 1, nkv),
        p_small[samp, nq + nkv:nq + 2 * nkv].reshape(n_samp, 1, nkv),
        cache_k_win[0].reshape(n_samp, r, nkv), cache_v_win[0].reshape(n_samp, r, nkv),
        bias_c, bias_n, sw_sinks[0])
    a_small = jnp.concatenate(
        [jnp.zeros((N_META, D_MODEL), BF16), a_samp.reshape(n_samp, D_MODEL),
         jnp.zeros((N_SMALL - N_META - n_samp, D_MODEL), BF16)], axis=0)
    x2_main, x2_small = _proj(a_main.reshape(MAIN_ROWS, D_MODEL), a_small, sw_w_out[0],
                              x1_main, x1_small, tm=1024, tn=512)

    y_prompt = _rmsnorm(x2_main, final_norm, F32, 256).reshape(BATCH, SEQ, D_MODEL)
    y_sample = _rmsnorm(x2_small, final_norm, F32, N_SMALL)[samp].reshape(n_samp, 1, D_MODEL)
    return (y_prompt, y_sample, s_prompt[None],
            k_main[:, -WINDOW:][None].astype(cache_k_win.dtype),
            v_main[:, -WINDOW:][None].astype(cache_v_win.dtype),
            s_sample,
            k_samp.reshape(1, n_samp, r, SW_KV, SW_HD), v_samp.reshape(1, n_samp, r, SW_KV, SW_HD))
```

```python
import functools
import math

import numpy as np
import jax
import jax.numpy as jnp
from jax import lax
from jax.experimental import pallas as pl
from jax.experimental.pallas import tpu as pltpu

D_MODEL = 4096
BATCH = 4
SEQ = 2048
DEC_BATCH = 32
N_META = 16
HG_HEADS = 32
HG_DK = 128
HG_F = HG_HEADS * HG_DK
SW_HEADS = 64
SW_KV = 8
SW_HD = 64
SW_GROUP = SW_HEADS // SW_KV
WINDOW = 128
REL_BUCKETS = 32
REL_MAX_DIST = 128
EPS = 1e-6
NEG = -1e30
LOG2E = math.log2(math.e)

LANES = 128
V7X_VMEM_LIMIT = 58 * 1024 * 1024

CHUNK = 128
SWA_BLOCKS_PER_ITER = 2
N_SMALL = 64
MAIN_ROWS = BATCH * SEQ

F32 = jnp.float32
BF16 = jnp.bfloat16


def _nt(a, b):
    return lax.dot_general(a, b, (((1,), (1,)), ((), ())), preferred_element_type=F32)


def _nn(a, b):
    return jnp.dot(a, b, preferred_element_type=F32)


def _sigmoid(x):
    return 1.0 / (1.0 + jnp.exp(-x))


def _silu(x):
    h = 0.5 * x
    return h + h * jnp.tanh(h)


def _rmsnorm_kernel(x_ref, g_ref, o_ref):
    x = x_ref[...]
    ms = jnp.mean(x * x, axis=-1, keepdims=True)
    o_ref[...] = (x * lax.rsqrt(ms + EPS) * g_ref[...]).astype(o_ref.dtype)


def _rmsnorm(x, g, out_dtype, tm):
    m, d = x.shape
    return pl.pallas_call(
        _rmsnorm_kernel,
        grid=(m // tm,),
        in_specs=[pl.BlockSpec((tm, d), lambda i: (i, 0)),
                  pl.BlockSpec((1, d), lambda i: (0, 0))],
        out_specs=pl.BlockSpec((tm, d), lambda i: (i, 0)),
        out_shape=jax.ShapeDtypeStruct((m, d), out_dtype),
        compiler_params=pltpu.CompilerParams(dimension_semantics=("parallel",),
                                             vmem_limit_bytes=V7X_VMEM_LIMIT),
        name="rmsnorm",
    )(x, g.reshape(1, d))


def _proj_kernel(a_ref, as_ref, w_ref, *rest, k_chunk, has_res):
    if has_res:
        r_ref, rs_ref, o_ref, os_ref, wb_ref = rest
    else:
        o_ref, os_ref, wb_ref = rest
    i = pl.program_id(1)

    @pl.when(i == 0)
    def _():
        def cast(c, carry):
            rows = pl.ds(pl.multiple_of(c * k_chunk, k_chunk), k_chunk)
            wb_ref[rows, :] = w_ref[rows, :].astype(BF16)
            return carry
        lax.fori_loop(0, w_ref.shape[0] // k_chunk, cast, 0)
        small = _nn(as_ref[...], wb_ref[...])
        if has_res:
            small = small + rs_ref[...]
        os_ref[...] = small

    out = _nn(a_ref[...], wb_ref[...])
    if has_res:
        out = out + r_ref[...]
    o_ref[...] = out


def _proj(a, a_small, w, res=None, res_small=None, *, tm, tn):
    m, k = a.shape
    n = w.shape[1]
    ms = a_small.shape[0]
    has_res = res is not None
    in_specs = [pl.BlockSpec((tm, k), lambda j, i: (i, 0)),
                pl.BlockSpec((ms, k), lambda j, i: (0, 0)),
                pl.BlockSpec((k, tn), lambda j, i: (0, j))]
    args = [a, a_small, w]
    if has_res:
        in_specs += [pl.BlockSpec((tm, tn), lambda j, i: (i, j)),
                     pl.BlockSpec((ms, tn), lambda j, i: (0, j))]
        args += [res, res_small]
    return pl.pallas_call(
        functools.partial(_proj_kernel, k_chunk=512, has_res=has_res),
        grid=(n // tn, m // tm),
        in_specs=in_specs,
        out_specs=[pl.BlockSpec((tm, tn), lambda j, i: (i, j)),
                   pl.BlockSpec((ms, tn), lambda j, i: (0, j))],
        out_shape=[jax.ShapeDtypeStruct((m, n), F32),
                   jax.ShapeDtypeStruct((ms, n), F32)],
        scratch_shapes=[pltpu.VMEM((k, tn), BF16)],
        compiler_params=pltpu.CompilerParams(
            dimension_semantics=("arbitrary", "arbitrary"),
            vmem_limit_bytes=V7X_VMEM_LIMIT),
        name="proj_res" if has_res else "proj",
    )(*args)


def _hgrn_consts():
    t = np.arange(CHUNK)[:, None]
    s = np.arange(CHUNK)[None, :]
    tri = s <= t
    ends = np.array([15, 47, 79, 111, 31, 95, 63, 127] + [-1] * 8)[:, None]
    mall = np.concatenate([tri, s <= ends], axis=0).astype(np.float32)
    lvl = np.full((CHUNK, CHUNK), 3, np.int32)
    lvl[(t >= 64) & (s < 64)] = 2
    lvl[(t // 64 == s // 64) & (t % 64 >= 32) & (s % 64 < 32)] = 1
    lvl[(t // 32 == s // 32) & (s <= t)] = 0
    return jnp.asarray(mall, BF16), jnp.asarray(lvl)


def _lower_bound(lbraw):
    mx = jnp.max(lbraw, axis=0, keepdims=True)
    e = jnp.exp(lbraw - mx)
    return e[0:1, :] / jnp.sum(e, axis=0, keepdims=True)


def _head_out(o, graw, onorm):
    ms = jnp.mean(o * o, axis=-1, keepdims=True)
    return (o * lax.rsqrt(ms + EPS) * onorm) * _silu(graw)


def _hgrn_gates(qraw, fp, lb, mall, row_valid):
    sig = _sigmoid(fp)
    f = lb + (1.0 - lb) * sig
    logf = jnp.log2(f)
    k = (1.0 - lb) * (1.0 - sig)
    if row_valid is not None:
        logf = jnp.where(row_valid, logf, 0.0)
        k = jnp.where(row_valid, k, 0.0)
    q = _silu(qraw)

    hi = logf.astype(BF16)
    mid = (logf - hi.astype(F32)).astype(BF16)
    cs = _nn(mall, jnp.concatenate([hi, mid], axis=1))
    return q, k, cs[:, :LANES] + cs[:, LANES:]


def _hgrn_scores(q, k, cs):
    b = cs[0:CHUNK]
    ref = cs[CHUNK:CHUNK + 8]
    bl = ref[7:8]
    cat = lambda parts: jnp.concatenate(parts, axis=0)
    d0 = cat([b[32 * i:32 * i + 32] - ref[i:i + 1] for i in range(4)])
    d1 = cat([b[64 * i:64 * i + 64] - ref[4 + i:5 + i] for i in range(2)])
    d2 = b - ref[6:7]
    a0 = _nt((q * jnp.exp2(d0)).astype(BF16), (k * jnp.exp2(-d0)).astype(BF16))
    a1 = _nt((cat([q[32:64], q[96:128]]) * jnp.exp2(cat([d1[32:64], d1[96:128]]))).astype(BF16),
             (k * jnp.exp2(-d1)).astype(BF16))
    a2 = _nt((q[64:128] * jnp.exp2(d2[64:128])).astype(BF16),
             (k * jnp.exp2(-d2)).astype(BF16))
    qe = (q * jnp.exp2(b)).astype(BF16)
    kd = (k * jnp.exp2(bl - b)).astype(BF16)
    return a0, a1, a2, qe, kd, jnp.exp2(bl)


def _hgrn_apply(a0, a1, a2, qe, kd, decay, v, st, lvl):
    is0, is1, is2 = lvl == 0, lvl == 1, lvl == 2
    a = jnp.concatenate([
        jnp.where(is0[0:32], a0[0:32], 0.0),
        jnp.where(is0[32:64], a0[32:64], jnp.where(is1[32:64], a1[0:32], 0.0)),
        jnp.where(is0[64:96], a0[64:96], jnp.where(is2[64:96], a2[0:32], 0.0)),
        jnp.where(is0[96:128], a0[96:128],
                  jnp.where(is1[96:128], a1[32:64], jnp.where(is2[96:128], a2[32:64], 0.0))),
    ], axis=0)
    vt = v.T.astype(BF16)
    lhs = jnp.concatenate([a.astype(BF16), qe], axis=1)
    rhs = jnp.concatenate([vt, st.astype(BF16)], axis=1)
    return _nt(lhs, rhs), st * decay + _nn(vt, kd)


def _hgrn_scan_kernel(q_ref, f_ref, v_ref, g_ref, lb_ref, on_ref, s0_ref, mall_ref, lvl_ref,
                      a_ref, s_ref, st_ref, *, hb, n_chunks, n_pad):
    t = pl.program_id(2)

    @pl.when(t == 0)
    def _():
        for h in range(hb):
            st_ref[h] = s0_ref[h].T

    lb_all = _lower_bound(lb_ref[...])
    row_valid = None
    if n_pad:
        row_valid = lax.broadcasted_iota(jnp.int32, (CHUNK, LANES), 0) >= n_pad
    heads = [slice(h * LANES, (h + 1) * LANES) for h in range(hb)]

    def body(c, carry):
        rows = pl.ds(pl.multiple_of(c * CHUNK, CHUNK), CHUNK)
        gates = [_hgrn_gates(q_ref[0, rows, cols], f_ref[0, rows, cols], lb_all[:, cols],
                             mall_ref[...], row_valid) for cols in heads]
        scores = [_hgrn_scores(*g) for g in gates]
        outs = [_hgrn_apply(*sc, v_ref[0, rows, cols], st_ref[h], lvl_ref[...])
                for h, (sc, cols) in enumerate(zip(scores, heads))]
        for h, ((o, st_new), cols) in enumerate(zip(outs, heads)):
            st_ref[h] = st_new
            a_ref[0, rows, cols] = _head_out(o, g_ref[0, rows, cols], on_ref[:, cols]).astype(BF16)
        return carry

    lax.fori_loop(0, n_chunks, body, 0)

    @pl.when(t == pl.num_programs(2) - 1)
    def _():
        for h in range(hb):
            s_ref[0, h] = st_ref[h].T


def _hgrn_scan(proj, lbraw, onorm, s0, *, tb, hb, n_pad=0):
    bsz, tlen, _ = proj.shape
    mall, lvl = _hgrn_consts()
    hcols = HG_F // (hb * LANES)
    sect = lambda s: (lambda b, h, t: (b, t, s * hcols + h))
    blk = (1, tb, hb * LANES)
    return pl.pallas_call(
        functools.partial(_hgrn_scan_kernel, hb=hb, n_chunks=tb // CHUNK, n_pad=n_pad),
        grid=(bsz, HG_HEADS // hb, tlen // tb),
        in_specs=[pl.BlockSpec(blk, sect(0)), pl.BlockSpec(blk, sect(1)),
                  pl.BlockSpec(blk, sect(2)), pl.BlockSpec(blk, sect(3)),
                  pl.BlockSpec((3, hb * LANES), lambda b, h, t: (0, h)),
                  pl.BlockSpec((1, hb * LANES), lambda b, h, t: (0, h)),
                  pl.BlockSpec((hb, HG_DK, HG_DK), lambda b, h, t: (h, 0, 0)),
                  pl.BlockSpec((CHUNK + 16, CHUNK), lambda b, h, t: (0, 0)),
                  pl.BlockSpec((CHUNK, CHUNK), lambda b, h, t: (0, 0))],
        out_specs=[pl.BlockSpec(blk, lambda b, h, t: (b, t, h)),
                   pl.BlockSpec((1, hb, HG_DK, HG_DK), lambda b, h, t: (b, h, 0, 0))],
        out_shape=[jax.ShapeDtypeStruct((bsz, tlen, D_MODEL), BF16),
                   jax.ShapeDtypeStruct((bsz, HG_HEADS, HG_DK, HG_DK), F32)],
        scratch_shapes=[pltpu.VMEM((hb, HG_DK, HG_DK), F32)],
        compiler_params=pltpu.CompilerParams(
            dimension_semantics=("parallel", "parallel", "arbitrary")),
        name="hgrn_scan",
    )(proj, proj, proj, proj, lbraw, onorm.reshape(1, D_MODEL), s0, mall, lvl)


def _hgrn_step_kernel(q_ref, f_ref, v_ref, g_ref, lb_ref, on_ref, s0_ref, a_ref, s_ref):
    lb = _lower_bound(lb_ref[...])[0]
    fp = f_ref[0]
    qraw = q_ref[0]
    v = v_ref[0]
    sig = _sigmoid(fp)
    f = lb + (1.0 - lb) * sig
    decay = jnp.exp(jnp.log(f))
    k = (1.0 - lb) * (1.0 - sig)
    q = _silu(qraw)
    dt, kt, qt = decay.T, k.T, q.T
    rows = []
    for h in range(HG_HEADS):
        s_new = dt[:, h:h + 1] * s0_ref[0, 0, h] + kt[:, h:h + 1] * v[h:h + 1, :]
        s_ref[0, 0, h] = s_new
        rows.append(jnp.sum(qt[:, h:h + 1] * s_new, axis=0, keepdims=True))
    o = jnp.concatenate(rows, axis=0)
    a_ref[0] = _head_out(o, g_ref[0], on_ref[...]).astype(BF16)


def _hgrn_step(q, f, v, g, lbraw, onorm, state):
    bsz = q.shape[0]
    vec = pl.BlockSpec((1, HG_HEADS, HG_DK), lambda b: (b, 0, 0))
    full = pl.BlockSpec((HG_HEADS, HG_DK), lambda b: (0, 0))
    lbspec = pl.BlockSpec((3, HG_HEADS, HG_DK), lambda b: (0, 0, 0))
    sspec = pl.BlockSpec((1, 1, HG_HEADS, HG_DK, HG_DK), lambda b: (0, b, 0, 0, 0))
    return pl.pallas_call(
        _hgrn_step_kernel,
        grid=(bsz,),
        in_specs=[vec, vec, vec, vec, lbspec, full, sspec],
        out_specs=[vec, sspec],
        out_shape=[jax.ShapeDtypeStruct((bsz, HG_HEADS, HG_DK), BF16),
                   jax.ShapeDtypeStruct(state.shape, state.dtype)],
        compiler_params=pltpu.CompilerParams(dimension_semantics=("parallel",)),
        name="hgrn_step",
    )(q, f, v, g, lbraw.reshape(3, HG_HEADS, HG_DK), onorm.reshape(HG_HEADS, HG_DK), state)


def _t5_bucket(dist):
    max_exact = REL_BUCKETS // 2
    d = jnp.maximum(dist, 1).astype(F32)
    large = max_exact + (jnp.log(d / max_exact) / math.log(REL_MAX_DIST / max_exact)
                         * (REL_BUCKETS - max_exact)).astype(jnp.int32)
    large = jnp.minimum(large, REL_BUCKETS - 1)
    return jnp.where(dist < max_exact, dist, large)


def _prompt_bias_kernel(w_ref, o_ref):
    band = 2 * WINDOW
    key = lax.broadcasted_iota(jnp.int32, (band, WINDOW), 0)
    for g in range(SW_GROUP):
        j, half = divmod(g, 2)
        row = jnp.broadcast_to(w_ref[g:g + 1, :], (band, 3 * WINDOW))
        toep = pltpu.roll(row, 0, 1, stride=1, stride_axis=0)[:, :WINDOW]
        dst = (slice(half * band, (half + 1) * band), slice(j * WINDOW, (j + 1) * WINDOW))
        o_ref[1, 0, dst[0], dst[1]] = toep
        o_ref[0, 0, dst[0], dst[1]] = jnp.where(key < WINDOW - N_META, NEG * LOG2E, toep)


def _prompt_bias(table):
    band = 2 * WINDOW
    i = jnp.arange(3 * WINDOW)
    dist = jnp.where(i < WINDOW, i + WINDOW, i - band)
    vals = table.astype(F32)[_t5_bucket(jnp.maximum(dist, 0))]
    vals = jnp.where(((dist >= 0) & (dist <= WINDOW))[:, None], vals, NEG) * LOG2E
    return pl.pallas_call(
        _prompt_bias_kernel,
        grid=(SW_KV,),
        in_specs=[pl.BlockSpec((SW_GROUP, 3 * WINDOW), lambda n: (n, 0))],
        out_specs=pl.BlockSpec((2, 1, 2 * band, 4 * WINDOW), lambda n: (0, n, 0, 0)),
        out_shape=jax.ShapeDtypeStruct((2, SW_KV, 2 * band, 4 * WINDOW), F32),
        compiler_params=pltpu.CompilerParams(dimension_semantics=("parallel",)),
        name="prompt_bias",
    )(vals.T)


def _expand_band(pair, u):
    lane = lax.broadcasted_iota(jnp.int32, pair.shape, 1)
    rolled = pltpu.roll(pair, SW_HD, axis=1)
    lo_src, hi_src = (pair, rolled) if u == 0 else (rolled, pair)
    top = jnp.where(lane < SW_HD, lo_src, 0.0)
    bot = jnp.where(lane >= SW_HD, hi_src, 0.0)
    return jnp.concatenate([top, bot], axis=0).astype(BF16)


def _swa_probs(s, sinks):
    band = 2 * WINDOW
    ps, ms = [], []
    for half in range(2):
        sh = s[half * band:(half + 1) * band]
        m = jnp.maximum(jnp.max(sh, axis=0, keepdims=True), sinks[half])
        ps.append(jnp.exp2(sh - m).astype(BF16))
        ms.append(m)
    return jnp.concatenate(ps, axis=0), ms


def _swa_values(pt, vxt, ms, sinks):
    ot = _nn(vxt, pt)
    outs = []
    for half in range(2):
        den = ot[2 * SW_HD + 8 * half:2 * SW_HD + 8 * half + 1] + jnp.exp2(sinks[half] - ms[half])
        outs.append(ot[SW_HD * half:SW_HD * (half + 1)] * (1.0 / den))
    return jnp.concatenate(outs, axis=0)


def _swa_prompt_kernel(sink_ref, q_ref, g_ref, k_ref, v_ref, km_ref, vm_ref, bias_ref,
                       o_ref, *, n_blocks):
    p = pl.program_id(1)
    qb = pl.program_id(2)
    n_pairs = SW_GROUP // 2
    band = 2 * WINDOW

    def sink_rows(u):
        return [jnp.concatenate(
            [jnp.full((1, WINDOW), sink_ref[(p * 2 + u) * SW_GROUP + 2 * j + half] * LOG2E, F32)
             for j in range(n_pairs)], axis=1) for half in range(2)]

    sinks = [sink_rows(u) for u in range(2)]
    orow = lax.broadcasted_iota(jnp.int32, (16, 2 * band), 0)
    ocol = lax.broadcasted_iota(jnp.int32, (16, 2 * band), 1)
    ones_rows = jnp.where((orow < 8) == (ocol < band), 1.0, 0.0).astype(BF16)
    zeros = jnp.zeros((SW_HD, band), BF16)

    def band_of(blk):
        tok = qb * n_blocks + blk
        cur = pl.ds(pl.multiple_of(tok * WINDOW, WINDOW), WINDOW)
        prev = pl.ds(pl.multiple_of(jnp.maximum(tok - 1, 0) * WINDOW, WINDOW), WINDOW)
        is_first = tok == 0
        kband = jnp.concatenate([jnp.where(is_first, km_ref[...], k_ref[0, prev, :]),
                                 k_ref[0, cur, :]], axis=0)
        vband = jnp.concatenate([jnp.where(is_first, vm_ref[...], v_ref[0, prev, :]),
                                 v_ref[0, cur, :]], axis=0)
        return (pl.ds(pl.multiple_of(blk * WINDOW, WINDOW), WINDOW), kband,
                vband.T.astype(BF16), jnp.where(is_first, 0, 1))

    def blocks(it, carry):
        bands = [band_of(it * SWA_BLOCKS_PER_ITER + i) for i in range(SWA_BLOCKS_PER_ITER)]
        units = [(bnd, u) for bnd in bands for u in range(2)]
        scores = []
        for (rows, kband, _, bias_sel), u in units:
            base = u * SW_GROUP * SW_HD
            q4 = jnp.concatenate(
                [q_ref[0, rows, base + j * LANES:base + (j + 1) * LANES] for j in range(n_pairs)],
                axis=0)
            q4 = (q4 * (SW_HD ** -0.5 * LOG2E)).astype(BF16)
            scores.append(_nt(_expand_band(kband, u), q4) + bias_ref[bias_sel, u])
        probs = [_swa_probs(s, sinks[u]) for s, (_, u) in zip(scores, units)]
        outs = []
        for (pt, ms), ((_, _, vbt, _), u) in zip(probs, units):
            vt = vbt[u * SW_HD:(u + 1) * SW_HD]
            vxt = jnp.concatenate([jnp.concatenate([vt, zeros], axis=1),
                                   jnp.concatenate([zeros, vt], axis=1), ones_rows], axis=0)
            outs.append(_swa_values(pt, vxt, ms, sinks[u]))
        for ot, ((rows, _, _, _), u) in zip(outs, units):
            base = u * SW_GROUP * SW_HD
            o = jnp.concatenate([ot[:, j * WINDOW:(j + 1) * WINDOW].T for j in range(n_pairs)],
                                axis=1)
            graw = g_ref[0, rows, base:base + SW_GROUP * SW_HD]
            o_ref[0, rows, base:base + SW_GROUP * SW_HD] = (o * _silu(graw)).astype(BF16)
        return carry

    lax.fori_loop(0, n_blocks // SWA_BLOCKS_PER_ITER, blocks, 0)


def _swa_prompt(proj, meta_k, meta_v, sinks, bias, *, tq):
    bsz, tlen, _ = proj.shape
    pw = 2 * SW_GROUP * SW_HD
    qblocks = SW_HEADS * SW_HD // pw
    kcol0 = SW_HEADS * SW_HD // LANES
    vcol0 = kcol0 + SW_KV * SW_HD // LANES
    gblk0 = (SW_HEADS * SW_HD + 2 * SW_KV * SW_HD) // pw
    return pl.pallas_call(
        functools.partial(_swa_prompt_kernel, n_blocks=tq // WINDOW),
        grid=(bsz, qblocks, tlen // tq),
        in_specs=[pl.BlockSpec(memory_space=pltpu.SMEM),
                  pl.BlockSpec((1, tq, pw), lambda b, p, t: (b, t, p)),
                  pl.BlockSpec((1, tq, pw), lambda b, p, t: (b, t, gblk0 + p)),
                  pl.BlockSpec((1, tlen, LANES), lambda b, p, t: (b, 0, kcol0 + p)),
                  pl.BlockSpec((1, tlen, LANES), lambda b, p, t: (b, 0, vcol0 + p)),
                  pl.BlockSpec((WINDOW, LANES), lambda b, p, t: (0, p)),
                  pl.BlockSpec((WINDOW, LANES), lambda b, p, t: (0, p)),
                  pl.BlockSpec((2, 2, 4 * WINDOW, 4 * WINDOW), lambda b, p, t: (0, p, 0, 0))],
        out_specs=pl.BlockSpec((1, tq, pw), lambda b, p, t: (b, t, p)),
        out_shape=jax.ShapeDtypeStruct((bsz, tlen, D_MODEL), BF16),
        compiler_params=pltpu.CompilerParams(
            dimension_semantics=("parallel", "parallel", "arbitrary"),
            vmem_limit_bytes=V7X_VMEM_LIMIT),
        name="swa_prompt",
    )(sinks, proj, proj, proj, proj, meta_k, meta_v, bias)


def _swa_sample_kernel(q_ref, g_ref, kn_ref, vn_ref, ck_ref, cv_ref, bc_ref, bn_ref, sink_ref,
                       o_ref, nk_ref, nv_ref):
    nkv = SW_KV * SW_HD
    q = q_ref[0] * (SW_HD ** -0.5)
    row = lax.broadcasted_iota(jnp.int32, (SW_HEADS, nkv), 0)
    col = lax.broadcasted_iota(jnp.int32, (SW_HEADS, nkv), 1)
    own = (row // SW_GROUP) == (col // SW_HD)
    qx = jnp.where(own, jnp.concatenate([q] * SW_KV, axis=1), 0.0)
    ck, cv = ck_ref[0], cv_ref[0]
    kn, vn = kn_ref[0], vn_ref[0]
    sink = sink_ref[...]

    s_c = _nt(qx.astype(BF16), ck.astype(BF16)) + bc_ref[...]
    s_n = jnp.sum(qx * kn, axis=-1, keepdims=True) + bn_ref[...]
    m = jnp.maximum(jnp.maximum(jnp.max(s_c, axis=-1, keepdims=True), s_n), sink)
    p_c = jnp.exp(s_c - m)
    p_n = jnp.exp(s_n - m)
    den = jnp.sum(p_c, axis=-1, keepdims=True) + p_n + jnp.exp(sink - m)
    o_all = _nn(p_c.astype(BF16), cv.astype(BF16))
    o_all = o_all + p_n * vn
    o_all = jnp.where(own, o_all, 0.0)
    o = o_all[:, 0:SW_HD]
    for n in range(1, SW_KV):
        o = o + o_all[:, n * SW_HD:(n + 1) * SW_HD]
    graw = g_ref[0]
    o_ref[0] = ((o / den) * _silu(graw)).astype(BF16)

    r = ck.shape[0]
    last = lax.broadcasted_iota(jnp.int32, ck.shape, 0) == r - 1
    nk_ref[0] = jnp.where(last, kn, pltpu.roll(ck, r - 1, axis=0))
    nv_ref[0] = jnp.where(last, vn, pltpu.roll(cv, r - 1, axis=0))


def _swa_sample(q, g, kn, vn, ck, cv, bias_c, bias_n, sinks):
    bsz, r, nkv = ck.shape
    head = pl.BlockSpec((1, SW_HEADS, SW_HD), lambda b: (b, 0, 0))
    new = pl.BlockSpec((1, 1, nkv), lambda b: (b, 0, 0))
    cache = pl.BlockSpec((1, r, nkv), lambda b: (b, 0, 0))
    return pl.pallas_call(
        _swa_sample_kernel,
        grid=(bsz,),
        in_specs=[head, head, new, new, cache, cache,
                  pl.BlockSpec((SW_HEADS, r), lambda b: (0, 0)),
                  pl.BlockSpec((SW_HEADS, 1), lambda b: (0, 0)),
                  pl.BlockSpec((SW_HEADS, 1), lambda b: (0, 0))],
        out_specs=[head, cache, cache],
        out_shape=[jax.ShapeDtypeStruct((bsz, SW_HEADS, SW_HD), BF16),
                   jax.ShapeDtypeStruct(ck.shape, ck.dtype),
                   jax.ShapeDtypeStruct(cv.shape, cv.dtype)],
        compiler_params=pltpu.CompilerParams(dimension_semantics=("parallel",)),
        name="swa_sample",
    )(q, g, kn, vn, ck, cv, bias_c, bias_n, sinks.reshape(SW_HEADS, 1))


def kernel(x_prompt, x_sample, state_hgrn, cache_k_win, cache_v_win, meta_tokens, rel_bias,
           hg_lower_bounds, hg_norm, hg_w_in, hg_onorm, hg_w_out,
           sw_norm, sw_w_in, sw_sinks, sw_w_out, final_norm):
    n_samp = x_sample.shape[0]
    samp = slice(N_META, N_META + n_samp)
    x_main = x_prompt.reshape(MAIN_ROWS, D_MODEL)
    x_small = jnp.concatenate(
        [meta_tokens.astype(F32), x_sample.reshape(n_samp, D_MODEL),
         jnp.zeros((N_SMALL - N_META - n_samp, D_MODEL), F32)], axis=0)

    h_main = _rmsnorm(x_main, hg_norm[0], BF16, 512)
    h_small = _rmsnorm(x_small, hg_norm[0], BF16, N_SMALL)
    p_main, p_small = _proj(h_main, h_small, hg_w_in[0], tm=512, tn=1024)

    meta_proj = jnp.pad(p_small[:N_META], ((CHUNK - N_META, 0), (0, 0)))[None]
    zero_state = jnp.zeros((HG_HEADS, HG_DK, HG_DK), F32)
    a_meta, s_meta = _hgrn_scan(meta_proj, hg_lower_bounds, hg_onorm[0], zero_state,
                                tb=CHUNK, hb=2, n_pad=CHUNK - N_META)
    a_main, s_prompt = _hgrn_scan(p_main.reshape(BATCH, SEQ, -1), hg_lower_bounds, hg_onorm[0],
                                  s_meta[0], tb=512, hb=8)
    sect = lambda s: p_small[samp, s * HG_F:(s + 1) * HG_F].reshape(n_samp, HG_HEADS, HG_DK)
    a_samp, s_sample = _hgrn_step(sect(0), sect(1), sect(2), sect(3), hg_lower_bounds,
                                  hg_onorm[0], state_hgrn)
    a_small = jnp.concatenate(
        [a_meta[0, CHUNK - N_META:], a_samp.reshape(n_samp, D_MODEL),
         jnp.zeros((N_SMALL - N_META - n_samp, D_MODEL), BF16)], axis=0)
    x1_main, x1_small = _proj(a_main.reshape(MAIN_ROWS, D_MODEL), a_small, hg_w_out[0],
                              x_main, x_small, tm=1024, tn=512)

    h_main = _rmsnorm(x1_main, sw_norm[0], BF16, 512)
    h_small = _rmsnorm(x1_small, sw_norm[0], BF16, N_SMALL)
    p_main, p_small = _proj(h_main, h_small, sw_w_in[0], tm=512, tn=1024)
    nq, nkv = SW_HEADS * SW_HD, SW_KV * SW_HD
    kv_tail = p_main.reshape(BATCH, SEQ, -1)[:, -WINDOW:, nq:nq + 2 * nkv]
    k_tail = kv_tail[:, :, :nkv].reshape(1, BATCH, WINDOW, SW_KV, SW_HD)
    v_tail = kv_tail[:, :, nkv:].reshape(1, BATCH, WINDOW, SW_KV, SW_HD)
    meta_kv = jnp.pad(p_small[:N_META, nq:nq + 2 * nkv], ((WINDOW - N_META, 0), (0, 0)))
    a_main = _swa_prompt(p_main.reshape(BATCH, SEQ, -1), meta_kv[:, :nkv], meta_kv[:, nkv:],
                         sw_sinks[0], _prompt_bias(rel_bias), tq=512)

    r = cache_k_win.shape[2]
    table = rel_bias.astype(F32)
    bias_c = table[_t5_bucket(r - jnp.arange(r))].T
    bias_n = table[_t5_bucket(jnp.zeros((1,), jnp.int32))].T
    q_s = p_small[samp, :nq].reshape(n_samp, SW_HEADS, SW_HD)
    g_s = p_small[samp, nq + 2 * nkv:].reshape(n_samp, SW_HEADS, SW_HD)
    a_samp, k_samp, v_samp = _swa_sample(
        q_s, g_s, p_small[samp, nq:nq + nkv].reshape(n_samp, 1, nkv),
        p_small[samp, nq + nkv:nq + 2 * nkv].reshape(n_samp, 1, nkv),
        cache_k_win[0].reshape(n_samp, r, nkv), cache_v_win[0].reshape(n_samp, r, nkv),
        bias_c, bias_n, sw_sinks[0])
    a_small = jnp.concatenate(
        [jnp.zeros((N_META, D_MODEL), BF16), a_samp.reshape(n_samp, D_MODEL),
         jnp.zeros((N_SMALL - N_META - n_samp, D_MODEL), BF16)], axis=0)
    x2_main, x2_small = _proj(a_main.reshape(MAIN_ROWS, D_MODEL), a_small, sw_w_out[0],
                              x1_main, x1_small, tm=1024, tn=512)

    y_prompt = _rmsnorm(x2_main, final_norm, F32, 512).reshape(BATCH, SEQ, D_MODEL)
    y_sample = _rmsnorm(x2_small, final_norm, F32, N_SMALL)[samp].reshape(n_samp, 1, D_MODEL)
    return (y_prompt, y_sample, s_prompt[None],
            k_tail.astype(cache_k_win.dtype), v_tail.astype(cache_v_win.dtype),
            s_sample,
            k_samp.reshape(1, n_samp, r, SW_KV, SW_HD), v_samp.reshape(1, n_samp, r, SW_KV, SW_HD))
```

```python
import functools
import math

import numpy as np
import jax
import jax.numpy as jnp
from jax import lax
from jax.experimental import pallas as pl
from jax.experimental.pallas import tpu as pltpu

D_MODEL = 4096
BATCH = 4
SEQ = 2048
DEC_BATCH = 32
N_META = 16
HG_HEADS = 32
HG_DK = 128
HG_F = HG_HEADS * HG_DK
SW_HEADS = 64
SW_KV = 8
SW_HD = 64
SW_GROUP = SW_HEADS // SW_KV
WINDOW = 128
REL_BUCKETS = 32
REL_MAX_DIST = 128
EPS = 1e-6
NEG = -1e30
LOG2E = math.log2(math.e)

LANES = 128
V7X_VMEM_LIMIT = 58 * 1024 * 1024

CHUNK = 128
SWA_BLOCKS_PER_ITER = 2
N_SMALL = 64
MAIN_ROWS = BATCH * SEQ

F32 = jnp.float32
BF16 = jnp.bfloat16


def _nt(a, b):
    return lax.dot_general(a, b, (((1,), (1,)), ((), ())), preferred_element_type=F32)


def _nn(a, b):
    return jnp.dot(a, b, preferred_element_type=F32)


def _sigmoid(x):
    return 1.0 / (1.0 + jnp.exp(-x))


def _silu(x):
    h = 0.5 * x
    return h + h * jnp.tanh(h)


def _rmsnorm_kernel(x_ref, g_ref, o_ref):
    x = x_ref[...]
    ms = jnp.mean(x * x, axis=-1, keepdims=True)
    o_ref[...] = (x * lax.rsqrt(ms + EPS) * g_ref[...]).astype(o_ref.dtype)


def _rmsnorm(x, g, out_dtype, tm):
    m, d = x.shape
    return pl.pallas_call(
        _rmsnorm_kernel,
        grid=(m // tm,),
        in_specs=[pl.BlockSpec((tm, d), lambda i: (i, 0)),
                  pl.BlockSpec((1, d), lambda i: (0, 0))],
        out_specs=pl.BlockSpec((tm, d), lambda i: (i, 0)),
        out_shape=jax.ShapeDtypeStruct((m, d), out_dtype),
        compiler_params=pltpu.CompilerParams(dimension_semantics=("parallel",),
                                             vmem_limit_bytes=V7X_VMEM_LIMIT),
        name="rmsnorm",
    )(x, g.reshape(1, d))


def _proj_kernel(a_ref, as_ref, w_ref, *rest, k_chunk, has_res):
    if has_res:
        r_ref, rs_ref, o_ref, os_ref, wb_ref = rest
    else:
        o_ref, os_ref, wb_ref = rest
    i = pl.program_id(1)

    @pl.when(i == 0)
    def _():
        def cast(c, carry):
            rows = pl.ds(pl.multiple_of(c * k_chunk, k_chunk), k_chunk)
            wb_ref[rows, :] = w_ref[rows, :].astype(BF16)
            return carry
        lax.fori_loop(0, w_ref.shape[0] // k_chunk, cast, 0)
        small = _nn(as_ref[...], wb_ref[...])
        if has_res:
            small = small + rs_ref[...]
        os_ref[...] = small

    out = _nn(a_ref[...], wb_ref[...])
    if has_res:
        out = out + r_ref[...]
    o_ref[...] = out


def _proj(a, a_small, w, res=None, res_small=None, *, tm, tn):
    m, k = a.shape
    n = w.shape[1]
    ms = a_small.shape[0]
    has_res = res is not None
    in_specs = [pl.BlockSpec((tm, k), lambda j, i: (i, 0)),
                pl.BlockSpec((ms, k), lambda j, i: (0, 0)),
                pl.BlockSpec((k, tn), lambda j, i: (0, j))]
    args = [a, a_small, w]
    if has_res:
        in_specs += [pl.BlockSpec((tm, tn), lambda j, i: (i, j)),
                     pl.BlockSpec((ms, tn), lambda j, i: (0, j))]
        args += [res, res_small]
    return pl.pallas_call(
        functools.partial(_proj_kernel, k_chunk=512, has_res=has_res),
        grid=(n // tn, m // tm),
        in_specs=in_specs,
        out_specs=[pl.BlockSpec((tm, tn), lambda j, i: (i, j)),
                   pl.BlockSpec((ms, tn), lambda j, i: (0, j))],
        out_shape=[jax.ShapeDtypeStruct((m, n), F32),
                   jax.ShapeDtypeStruct((ms, n), F32)],
        scratch_shapes=[pltpu.VMEM((k, tn), BF16)],
        compiler_params=pltpu.CompilerParams(
            dimension_semantics=("arbitrary", "arbitrary"),
            vmem_limit_bytes=V7X_VMEM_LIMIT),
        name="proj_res" if has_res else "proj",
    )(*args)


def _hgrn_consts():
    t = np.arange(CHUNK)[:, None]
    s = np.arange(CHUNK)[None, :]
    tri = s <= t
    ends = np.array([15, 47, 79, 111, 31, 95, 63, 127] + [-1] * 8)[:, None]
    mall = np.concatenate([tri, s <= ends], axis=0).astype(np.float32)
    lvl = np.full((CHUNK, CHUNK), 3, np.int32)
    lvl[(t >= 64) & (s < 64)] = 2
    lvl[(t // 64 == s // 64) & (t % 64 >= 32) & (s % 64 < 32)] = 1
    lvl[(t // 32 == s // 32) & (s <= t)] = 0
    return jnp.asarray(mall, BF16), jnp.asarray(lvl)


def _lower_bound(lbraw):
    mx = jnp.max(lbraw, axis=0, keepdims=True)
    e = jnp.exp(lbraw - mx)
    return e[0:1, :] / jnp.sum(e, axis=0, keepdims=True)


def _head_out(o, graw, onorm):
    ms = jnp.mean(o * o, axis=-1, keepdims=True)
    return (o * lax.rsqrt(ms + EPS) * onorm) * _silu(graw)


def _hgrn_gates(qraw, fp, lb, mall, row_valid):
    sig = _sigmoid(fp)
    f = lb + (1.0 - lb) * sig
    logf = jnp.log2(f)
    k = 1.0 - f
    if row_valid is not None:
        logf = jnp.where(row_valid, logf, 0.0)
        k = jnp.where(row_valid, k, 0.0)
    q = _silu(qraw)

    hi = logf.astype(BF16)
    mid = (logf - hi.astype(F32)).astype(BF16)
    cs = _nn(mall, jnp.concatenate([hi, mid], axis=1))
    return q, k, cs[:, :LANES] + cs[:, LANES:]


def _hgrn_scores(q, k, cs):
    b = cs[0:CHUNK]
    ref = cs[CHUNK:CHUNK + 8]
    bl = ref[7:8]
    cat = lambda parts: jnp.concatenate(parts, axis=0)
    d0 = cat([b[32 * i:32 * i + 32] - ref[i:i + 1] for i in range(4)])
    d1 = [b[64 * i:64 * i + 64] - ref[4 + i:5 + i] for i in range(2)]
    d2 = b - ref[6:7]
    zero32 = jnp.zeros((32, LANES), BF16)
    a0 = _nt((q * jnp.exp2(d0)).astype(BF16), (k * jnp.exp2(-d0)).astype(BF16))
    q1 = cat([q[32:64] * jnp.exp2(d1[0][32:64]), q[96:128] * jnp.exp2(d1[1][32:64])])
    k1 = [(k[64 * i:64 * i + 32] * jnp.exp2(-d1[i][0:32])).astype(BF16) for i in range(2)]
    a1 = _nt(q1.astype(BF16), cat([k1[0], zero32, k1[1], zero32]))
    a2 = _nt((q[64:128] * jnp.exp2(d2[64:128])).astype(BF16),
             cat([(k[0:64] * jnp.exp2(-d2[0:64])).astype(BF16), zero32, zero32]))
    qe = (q * jnp.exp2(b)).astype(BF16)
    kd = (k * jnp.exp2(bl - b)).astype(BF16)
    return a0, a1, a2, qe, kd, jnp.exp2(bl)


def _hgrn_apply(a0, a1, a2, qe, kd, decay, v, st, lvl):
    is0, is1, is2 = lvl == 0, lvl == 1, lvl == 2
    a = jnp.concatenate([
        jnp.where(is0[0:32], a0[0:32], 0.0),
        jnp.where(is0[32:64], a0[32:64], jnp.where(is1[32:64], a1[0:32], 0.0)),
        jnp.where(is0[64:96], a0[64:96], jnp.where(is2[64:96], a2[0:32], 0.0)),
        jnp.where(is0[96:128], a0[96:128],
                  jnp.where(is1[96:128], a1[32:64], jnp.where(is2[96:128], a2[32:64], 0.0))),
    ], axis=0)
    vt = v.T.astype(BF16)
    lhs = jnp.concatenate([a.astype(BF16), qe], axis=1)
    rhs = jnp.concatenate([vt, st.astype(BF16)], axis=1)
    return _nt(lhs, rhs), st * decay + _nn(vt, kd)


def _hgrn_scan_kernel(q_ref, f_ref, v_ref, g_ref, lb_ref, on_ref, s0_ref, mall_ref, lvl_ref,
                      a_ref, s_ref, st_ref, *, hb, n_chunks, n_pad):
    t = pl.program_id(2)

    @pl.when(t == 0)
    def _():
        for h in range(hb):
            st_ref[h] = s0_ref[h].T

    lb_all = _lower_bound(lb_ref[...])
    row_valid = None
    if n_pad:
        row_valid = lax.broadcasted_iota(jnp.int32, (CHUNK, LANES), 0) >= n_pad
    heads = [slice(h * LANES, (h + 1) * LANES) for h in range(hb)]

    def body(c, carry):
        rows = pl.ds(pl.multiple_of(c * CHUNK, CHUNK), CHUNK)
        gates = [_hgrn_gates(q_ref[0, rows, cols], f_ref[0, rows, cols], lb_all[:, cols],
                             mall_ref[...], row_valid) for cols in heads]
        scores = [_hgrn_scores(*g) for g in gates]
        outs = [_hgrn_apply(*sc, v_ref[0, rows, cols], st_ref[h], lvl_ref[...])
                for h, (sc, cols) in enumerate(zip(scores, heads))]
        for h, ((o, st_new), cols) in enumerate(zip(outs, heads)):
            st_ref[h] = st_new
            a_ref[0, rows, cols] = _head_out(o, g_ref[0, rows, cols], on_ref[:, cols]).astype(BF16)
        return carry

    lax.fori_loop(0, n_chunks, body, 0)

    @pl.when(t == pl.num_programs(2) - 1)
    def _():
        for h in range(hb):
            s_ref[0, h] = st_ref[h].T


def _hgrn_scan(proj, lbraw, onorm, s0, *, tb, hb, n_pad=0):
    bsz, tlen, _ = proj.shape
    mall, lvl = _hgrn_consts()
    hcols = HG_F // (hb * LANES)
    sect = lambda s: (lambda b, h, t: (b, t, s * hcols + h))
    blk = (1, tb, hb * LANES)
    return pl.pallas_call(
        functools.partial(_hgrn_scan_kernel, hb=hb, n_chunks=tb // CHUNK, n_pad=n_pad),
        grid=(bsz, HG_HEADS // hb, tlen // tb),
        in_specs=[pl.BlockSpec(blk, sect(0)), pl.BlockSpec(blk, sect(1)),
                  pl.BlockSpec(blk, sect(2)), pl.BlockSpec(blk, sect(3)),
                  pl.BlockSpec((3, hb * LANES), lambda b, h, t: (0, h)),
                  pl.BlockSpec((1, hb * LANES), lambda b, h, t: (0, h)),
                  pl.BlockSpec((hb, HG_DK, HG_DK), lambda b, h, t: (h, 0, 0)),
                  pl.BlockSpec((CHUNK + 16, CHUNK), lambda b, h, t: (0, 0)),
                  pl.BlockSpec((CHUNK, CHUNK), lambda b, h, t: (0, 0))],
        out_specs=[pl.BlockSpec(blk, lambda b, h, t: (b, t, h)),
                   pl.BlockSpec((1, hb, HG_DK, HG_DK), lambda b, h, t: (b, h, 0, 0))],
        out_shape=[jax.ShapeDtypeStruct((bsz, tlen, D_MODEL), BF16),
                   jax.ShapeDtypeStruct((bsz, HG_HEADS, HG_DK, HG_DK), F32)],
        scratch_shapes=[pltpu.VMEM((hb, HG_DK, HG_DK), F32)],
        compiler_params=pltpu.CompilerParams(
            dimension_semantics=("parallel", "parallel", "arbitrary")),
        name="hgrn_scan",
    )(proj, proj, proj, proj, lbraw, onorm.reshape(1, D_MODEL), s0, mall, lvl)


def _hgrn_step_kernel(q_ref, f_ref, v_ref, g_ref, lb_ref, on_ref, s0_ref, a_ref, s_ref):
    lb = _lower_bound(lb_ref[...])[0]
    fp = f_ref[0]
    qraw = q_ref[0]
    v = v_ref[0]
    sig = _sigmoid(fp)
    f = lb + (1.0 - lb) * sig
    decay = jnp.exp(jnp.log(f))
    k = 1.0 - f
    q = _silu(qraw)

    def split(x):
        hi = x.astype(BF16).astype(F32)
        return hi, x - hi

    (kh, kl), (dh, dl), (qh, ql), (vh, vl) = split(k), split(decay), split(q), split(v)
    r = lax.broadcasted_iota(jnp.int32, (8, HG_DK), 0)
    ones_tail = jnp.concatenate([jnp.where((r == 3) | (r == 4), 1.0, 0.0),
                                 jnp.where((r == 5) | (r == 6), 1.0, 0.0)], axis=1).astype(BF16)
    v_hi_rows = (r == 0) | (r == 2)
    rows = []
    for h in range(HG_HEADS):
        row = lambda x: jnp.broadcast_to(x[h:h + 1, :], (8, HG_DK))
        lhs = jnp.where(r < 2, row(kh), jnp.where(r == 2, row(kl), jnp.where(
            r == 3, row(dh), jnp.where(r == 4, row(dl), jnp.where(
                r == 5, row(qh), jnp.where(r == 6, row(ql), 0.0))))))
        rhs_v = jnp.where(v_hi_rows, row(vh), jnp.where(r == 1, row(vl), 0.0))
        rhs = jnp.concatenate([rhs_v.astype(BF16), ones_tail], axis=1)
        out = lax.dot_general(lhs.astype(BF16), rhs, (((0,), (0,)), ((), ())),
                              preferred_element_type=F32)
        s_new = out[:, HG_DK:2 * HG_DK] * s0_ref[0, 0, h] + out[:, :HG_DK]
        s_ref[0, 0, h] = s_new
        rows.append(jnp.sum(out[:, 2 * HG_DK:] * s_new, axis=0, keepdims=True))
    o = jnp.concatenate(rows, axis=0)
    a_ref[0] = _head_out(o, g_ref[0], on_ref[...]).astype(BF16)


def _hgrn_step(q, f, v, g, lbraw, onorm, state):
    bsz = q.shape[0]
    vec = pl.BlockSpec((1, HG_HEADS, HG_DK), lambda b: (b, 0, 0))
    full = pl.BlockSpec((HG_HEADS, HG_DK), lambda b: (0, 0))
    lbspec = pl.BlockSpec((3, HG_HEADS, HG_DK), lambda b: (0, 0, 0))
    sspec = pl.BlockSpec((1, 1, HG_HEADS, HG_DK, HG_DK), lambda b: (0, b, 0, 0, 0))
    return pl.pallas_call(
        _hgrn_step_kernel,
        grid=(bsz,),
        in_specs=[vec, vec, vec, vec, lbspec, full, sspec],
        out_specs=[vec, sspec],
        out_shape=[jax.ShapeDtypeStruct((bsz, HG_HEADS, HG_DK), BF16),
                   jax.ShapeDtypeStruct(state.shape, state.dtype)],
        compiler_params=pltpu.CompilerParams(dimension_semantics=("parallel",)),
        name="hgrn_step",
    )(q, f, v, g, lbraw.reshape(3, HG_HEADS, HG_DK), onorm.reshape(HG_HEADS, HG_DK), state)


def _t5_bucket(dist):
    max_exact = REL_BUCKETS // 2
    d = jnp.maximum(dist, 1).astype(F32)
    large = max_exact + (jnp.log(d / max_exact) / math.log(REL_MAX_DIST / max_exact)
                         * (REL_BUCKETS - max_exact)).astype(jnp.int32)
    large = jnp.minimum(large, REL_BUCKETS - 1)
    return jnp.where(dist < max_exact, dist, large)


def _prompt_bias_kernel(w_ref, o_ref):
    band = 2 * WINDOW
    key = lax.broadcasted_iota(jnp.int32, (band, WINDOW), 0)
    for g in range(SW_GROUP):
        j, half = divmod(g, 2)
        row = jnp.broadcast_to(w_ref[g:g + 1, :], (band, 3 * WINDOW))
        toep = pltpu.roll(row, 0, 1, stride=1, stride_axis=0)[:, :WINDOW]
        dst = (slice(half * band, (half + 1) * band), slice(j * WINDOW, (j + 1) * WINDOW))
        o_ref[1, 0, dst[0], dst[1]] = toep
        o_ref[0, 0, dst[0], dst[1]] = jnp.where(key < WINDOW - N_META, NEG * LOG2E, toep)


def _prompt_bias(table):
    band = 2 * WINDOW
    i = jnp.arange(3 * WINDOW)
    dist = jnp.where(i < WINDOW, i + WINDOW, i - band)
    vals = table.astype(F32)[_t5_bucket(jnp.maximum(dist, 0))]
    vals = jnp.where(((dist >= 0) & (dist <= WINDOW))[:, None], vals, NEG) * LOG2E
    return pl.pallas_call(
        _prompt_bias_kernel,
        grid=(SW_KV,),
        in_specs=[pl.BlockSpec((SW_GROUP, 3 * WINDOW), lambda n: (n, 0))],
        out_specs=pl.BlockSpec((2, 1, 2 * band, 4 * WINDOW), lambda n: (0, n, 0, 0)),
        out_shape=jax.ShapeDtypeStruct((2, SW_KV, 2 * band, 4 * WINDOW), F32),
        compiler_params=pltpu.CompilerParams(dimension_semantics=("parallel",)),
        name="prompt_bias",
    )(vals.T)


def _expand_band(pair, u):
    lane = lax.broadcasted_iota(jnp.int32, pair.shape, 1)
    rolled = pltpu.roll(pair, SW_HD, axis=1)
    lo_src, hi_src = (pair, rolled) if u == 0 else (rolled, pair)
    top = jnp.where(lane < SW_HD, lo_src, 0.0)
    bot = jnp.where(lane >= SW_HD, hi_src, 0.0)
    return jnp.concatenate([top, bot], axis=0).astype(BF16)


def _swa_probs(s, sinks):
    band = 2 * WINDOW
    ps, ms = [], []
    for half in range(2):
        sh = s[half * band:(half + 1) * band]
        m = jnp.maximum(jnp.max(sh, axis=0, keepdims=True), sinks[half])
        ps.append(jnp.exp2(sh - m).astype(BF16))
        ms.append(m)
    return jnp.concatenate(ps, axis=0), ms


def _swa_values(pt, vxt, ms, sinks):
    ot = _nn(vxt, pt)
    outs = []
    for half in range(2):
        den = ot[2 * SW_HD + 8 * half:2 * SW_HD + 8 * half + 1] + jnp.exp2(sinks[half] - ms[half])
        outs.append(ot[SW_HD * half:SW_HD * (half + 1)] * (1.0 / den))
    return jnp.concatenate(outs, axis=0)


def _swa_prompt_kernel(sink_ref, q_ref, g_ref, k_ref, v_ref, km_ref, vm_ref, bias_ref,
                       o_ref, *, n_blocks):
    p = pl.program_id(1)
    qb = pl.program_id(2)
    n_pairs = SW_GROUP // 2
    band = 2 * WINDOW

    def sink_rows(u):
        return [jnp.concatenate(
            [jnp.full((1, WINDOW), sink_ref[(p * 2 + u) * SW_GROUP + 2 * j + half] * LOG2E, F32)
             for j in range(n_pairs)], axis=1) for half in range(2)]

    sinks = [sink_rows(u) for u in range(2)]
    orow = lax.broadcasted_iota(jnp.int32, (16, 2 * band), 0)
    ocol = lax.broadcasted_iota(jnp.int32, (16, 2 * band), 1)
    ones_rows = jnp.where((orow < 8) == (ocol < band), 1.0, 0.0).astype(BF16)
    zeros = jnp.zeros((SW_HD, band), BF16)

    def band_of(blk):
        tok = qb * n_blocks + blk
        cur = pl.ds(pl.multiple_of(tok * WINDOW, WINDOW), WINDOW)
        prev = pl.ds(pl.multiple_of(jnp.maximum(tok - 1, 0) * WINDOW, WINDOW), WINDOW)
        is_first = tok == 0
        kband = jnp.concatenate([jnp.where(is_first, km_ref[...], k_ref[0, prev, :]),
                                 k_ref[0, cur, :]], axis=0)
        vband = jnp.concatenate([jnp.where(is_first, vm_ref[...], v_ref[0, prev, :]),
                                 v_ref[0, cur, :]], axis=0)
        return (pl.ds(pl.multiple_of(blk * WINDOW, WINDOW), WINDOW), kband,
                vband.T.astype(BF16), jnp.where(is_first, 0, 1))

    def blocks(it, carry):
        bands = [band_of(it * SWA_BLOCKS_PER_ITER + i) for i in range(SWA_BLOCKS_PER_ITER)]
        units = [(bnd, u) for bnd in bands for u in range(2)]
        scores = []
        for (rows, kband, _, bias_sel), u in units:
            base = u * SW_GROUP * SW_HD
            q4 = jnp.concatenate(
                [q_ref[0, rows, base + j * LANES:base + (j + 1) * LANES] for j in range(n_pairs)],
                axis=0)
            q4 = (q4 * (SW_HD ** -0.5 * LOG2E)).astype(BF16)
            scores.append(_nt(_expand_band(kband, u), q4) + bias_ref[bias_sel, u])
        probs = [_swa_probs(s, sinks[u]) for s, (_, u) in zip(scores, units)]
        outs = []
        for (pt, ms), ((_, _, vbt, _), u) in zip(probs, units):
            vt = vbt[u * SW_HD:(u + 1) * SW_HD]
            vxt = jnp.concatenate([jnp.concatenate([vt, zeros], axis=1),
                                   jnp.concatenate([zeros, vt], axis=1), ones_rows], axis=0)
            outs.append(_swa_values(pt, vxt, ms, sinks[u]))
        for ot, ((rows, _, _, _), u) in zip(outs, units):
            base = u * SW_GROUP * SW_HD
            o = jnp.concatenate([ot[:, j * WINDOW:(j + 1) * WINDOW].T for j in range(n_pairs)],
                                axis=1)
            graw = g_ref[0, rows, base:base + SW_GROUP * SW_HD]
            o_ref[0, rows, base:base + SW_GROUP * SW_HD] = (o * _silu(graw)).astype(BF16)
        return carry

    lax.fori_loop(0, n_blocks // SWA_BLOCKS_PER_ITER, blocks, 0)


def _swa_prompt(proj, meta_k, meta_v, sinks, bias, *, tq):
    bsz, tlen, _ = proj.shape
    pw = 2 * SW_GROUP * SW_HD
    qblocks = SW_HEADS * SW_HD // pw
    kcol0 = SW_HEADS * SW_HD // LANES
    vcol0 = kcol0 + SW_KV * SW_HD // LANES
    gblk0 = (SW_HEADS * SW_HD + 2 * SW_KV * SW_HD) // pw
    return pl.pallas_call(
        functools.partial(_swa_prompt_kernel, n_blocks=tq // WINDOW),
        grid=(bsz, qblocks, tlen // tq),
        in_specs=[pl.BlockSpec(memory_space=pltpu.SMEM),
                  pl.BlockSpec((1, tq, pw), lambda b, p, t: (b, t, p)),
                  pl.BlockSpec((1, tq, pw), lambda b, p, t: (b, t, gblk0 + p)),
                  pl.BlockSpec((1, tlen, LANES), lambda b, p, t: (b, 0, kcol0 + p)),
                  pl.BlockSpec((1, tlen, LANES), lambda b, p, t: (b, 0, vcol0 + p)),
                  pl.BlockSpec((WINDOW, LANES), lambda b, p, t: (0, p)),
                  pl.BlockSpec((WINDOW, LANES), lambda b, p, t: (0, p)),
                  pl.BlockSpec((2, 2, 4 * WINDOW, 4 * WINDOW), lambda b, p, t: (0, p, 0, 0))],
        out_specs=pl.BlockSpec((1, tq, pw), lambda b, p, t: (b, t, p)),
        out_shape=jax.ShapeDtypeStruct((bsz, tlen, D_MODEL), BF16),
        compiler_params=pltpu.CompilerParams(
            dimension_semantics=("parallel", "parallel", "arbitrary"),
            vmem_limit_bytes=V7X_VMEM_LIMIT),
        name="swa_prompt",
    )(sinks, proj, proj, proj, proj, meta_k, meta_v, bias)


def _swa_sample_kernel(q_ref, g_ref, kn_ref, vn_ref, ck_ref, cv_ref, bc_ref, bn_ref, sink_ref,
                       o_ref, nk_ref, nv_ref):
    nkv = SW_KV * SW_HD
    q = q_ref[0] * (SW_HD ** -0.5)
    row = lax.broadcasted_iota(jnp.int32, (SW_HEADS, nkv), 0)
    col = lax.broadcasted_iota(jnp.int32, (SW_HEADS, nkv), 1)
    own = (row // SW_GROUP) == (col // SW_HD)
    qx = jnp.where(own, jnp.concatenate([q] * SW_KV, axis=1), 0.0)
    ck, cv = ck_ref[0], cv_ref[0]
    kn, vn = kn_ref[0], vn_ref[0]
    sink = sink_ref[...]

    s_c = _nt(qx.astype(BF16), ck.astype(BF16)) + bc_ref[...]
    s_n = jnp.sum(qx * kn, axis=-1, keepdims=True) + bn_ref[...]
    m = jnp.maximum(jnp.maximum(jnp.max(s_c, axis=-1, keepdims=True), s_n), sink)
    p_c = jnp.exp(s_c - m)
    p_n = jnp.exp(s_n - m)
    den = jnp.sum(p_c, axis=-1, keepdims=True) + p_n + jnp.exp(sink - m)
    o_all = _nn(p_c.astype(BF16), cv.astype(BF16))
    o_all = o_all + p_n * vn
    o_all = jnp.where(own, o_all, 0.0)
    o = o_all[:, 0:SW_HD]
    for n in range(1, SW_KV):
        o = o + o_all[:, n * SW_HD:(n + 1) * SW_HD]
    graw = g_ref[0]
    o_ref[0] = ((o / den) * _silu(graw)).astype(BF16)

    r = ck.shape[0]
    last = lax.broadcasted_iota(jnp.int32, ck.shape, 0) == r - 1
    nk_ref[0] = jnp.where(last, kn, pltpu.roll(ck, r - 1, axis=0))
    nv_ref[0] = jnp.where(last, vn, pltpu.roll(cv, r - 1, axis=0))


def _swa_sample(q, g, kn, vn, ck, cv, bias_c, bias_n, sinks):
    bsz, r, nkv = ck.shape
    head = pl.BlockSpec((1, SW_HEADS, SW_HD), lambda b: (b, 0, 0))
    new = pl.BlockSpec((1, 1, nkv), lambda b: (b, 0, 0))
    cache = pl.BlockSpec((1, r, nkv), lambda b: (b, 0, 0))
    return pl.pallas_call(
        _swa_sample_kernel,
        grid=(bsz,),
        in_specs=[head, head, new, new, cache, cache,
                  pl.BlockSpec((SW_HEADS, r), lambda b: (0, 0)),
                  pl.BlockSpec((SW_HEADS, 1), lambda b: (0, 0)),
                  pl.BlockSpec((SW_HEADS, 1), lambda b: (0, 0))],
        out_specs=[head, cache, cache],
        out_shape=[jax.ShapeDtypeStruct((bsz, SW_HEADS, SW_HD), BF16),
                   jax.ShapeDtypeStruct(ck.shape, ck.dtype),
                   jax.ShapeDtypeStruct(cv.shape, cv.dtype)],
        compiler_params=pltpu.CompilerParams(dimension_semantics=("parallel",)),
        name="swa_sample",
    )(q, g, kn, vn, ck, cv, bias_c, bias_n, sinks.reshape(SW_HEADS, 1))


def kernel(x_prompt, x_sample, state_hgrn, cache_k_win, cache_v_win, meta_tokens, rel_bias,
           hg_lower_bounds, hg_norm, hg_w_in, hg_onorm, hg_w_out,
           sw_norm, sw_w_in, sw_sinks, sw_w_out, final_norm):
    n_samp = x_sample.shape[0]
    samp = slice(N_META, N_META + n_samp)
    x_main = x_prompt.reshape(MAIN_ROWS, D_MODEL)
    x_small = jnp.concatenate(
        [meta_tokens.astype(F32), x_sample.reshape(n_samp, D_MODEL),
         jnp.zeros((N_SMALL - N_META - n_samp, D_MODEL), F32)], axis=0)

    h_main = _rmsnorm(x_main, hg_norm[0], BF16, 512)
    h_small = _rmsnorm(x_small, hg_norm[0], BF16, N_SMALL)
    p_main, p_small = _proj(h_main, h_small, hg_w_in[0], tm=512, tn=1024)

    meta_proj = jnp.pad(p_small[:N_META], ((CHUNK - N_META, 0), (0, 0)))[None]
    zero_state = jnp.zeros((HG_HEADS, HG_DK, HG_DK), F32)
    a_meta, s_meta = _hgrn_scan(meta_proj, hg_lower_bounds, hg_onorm[0], zero_state,
                                tb=CHUNK, hb=2, n_pad=CHUNK - N_META)
    a_main, s_prompt = _hgrn_scan(p_main.reshape(BATCH, SEQ, -1), hg_lower_bounds, hg_onorm[0],
                                  s_meta[0], tb=512, hb=8)
    sect = lambda s: p_small[samp, s * HG_F:(s + 1) * HG_F].reshape(n_samp, HG_HEADS, HG_DK)
    a_samp, s_sample = _hgrn_step(sect(0), sect(1), sect(2), sect(3), hg_lower_bounds,
                                  hg_onorm[0], state_hgrn)
    a_small = jnp.concatenate(
        [a_meta[0, CHUNK - N_META:], a_samp.reshape(n_samp, D_MODEL),
         jnp.zeros((N_SMALL - N_META - n_samp, D_MODEL), BF16)], axis=0)
    x1_main, x1_small = _proj(a_main.reshape(MAIN_ROWS, D_MODEL), a_small, hg_w_out[0],
                              x_main, x_small, tm=1024, tn=512)

    h_main = _rmsnorm(x1_main, sw_norm[0], BF16, 512)
    h_small = _rmsnorm(x1_small, sw_norm[0], BF16, N_SMALL)
    p_main, p_small = _proj(h_main, h_small, sw_w_in[0], tm=512, tn=1024)
    nq, nkv = SW_HEADS * SW_HD, SW_KV * SW_HD
    kv_tail = p_main.reshape(BATCH, SEQ, -1)[:, -WINDOW:, nq:nq + 2 * nkv]
    k_tail = kv_tail[:, :, :nkv].reshape(1, BATCH, WINDOW, SW_KV, SW_HD)
    v_tail = kv_tail[:, :, nkv:].reshape(1, BATCH, WINDOW, SW_KV, SW_HD)
    meta_kv = jnp.pad(p_small[:N_META, nq:nq + 2 * nkv], ((WINDOW - N_META, 0), (0, 0)))
    a_main = _swa_prompt(p_main.reshape(BATCH, SEQ, -1), meta_kv[:, :nkv], meta_kv[:, nkv:],
                         sw_sinks[0], _prompt_bias(rel_bias), tq=512)

    r = cache_k_win.shape[2]
    table = rel_bias.astype(F32)
    bias_c = table[_t5_bucket(r - jnp.arange(r))].T
    bias_n = table[_t5_bucket(jnp.zeros((1,), jnp.int32))].T
    q_s = p_small[samp, :nq].reshape(n_samp, SW_HEADS, SW_HD)
    g_s = p_small[samp, nq + 2 * nkv:].reshape(n_samp, SW_HEADS, SW_HD)
    a_samp, k_samp, v_samp = _swa_sample(
        q_s, g_s, p_small[samp, nq:nq + nkv].reshape(n_samp, 1, nkv),
        p_small[samp, nq + nkv:nq + 2 * nkv].reshape(n_samp, 1, nkv),
        cache_k_win[0].reshape(n_samp, r, nkv), cache_v_win[0].reshape(n_samp, r, nkv),
        bias_c, bias_n, sw_sinks[0])
    a_small = jnp.concatenate(
        [jnp.zeros((N_META, D_MODEL), BF16), a_samp.reshape(n_samp, D_MODEL),
         jnp.zeros((N_SMALL - N_META - n_samp, D_MODEL), BF16)], axis=0)
    x2_main, x2_small = _proj(a_main.reshape(MAIN_ROWS, D_MODEL), a_small, sw_w_out[0],
                              x1_main, x1_small, tm=1024, tn=512)

    y_prompt = _rmsnorm(x2_main, final_norm, F32, 512).reshape(BATCH, SEQ, D_MODEL)
    y_sample = _rmsnorm(x2_small, final_norm, F32, N_SMALL)[samp].reshape(n_samp, 1, D_MODEL)
    return (y_prompt, y_sample, s_prompt[None],
            k_tail.astype(cache_k_win.dtype), v_tail.astype(cache_v_win.dtype),
            s_sample,
            k_samp.reshape(1, n_samp, r, SW_KV, SW_HD), v_samp.reshape(1, n_samp, r, SW_KV, SW_HD))
```

```python
import functools
import math

import numpy as np
import jax
import jax.numpy as jnp
from jax import lax
from jax.experimental import pallas as pl
from jax.experimental.pallas import tpu as pltpu

D_MODEL = 4096
BATCH = 4
SEQ = 2048
DEC_BATCH = 32
N_META = 16
HG_HEADS = 32
HG_DK = 128
HG_F = HG_HEADS * HG_DK
SW_HEADS = 64
SW_KV = 8
SW_HD = 64
SW_GROUP = SW_HEADS // SW_KV
WINDOW = 128
REL_BUCKETS = 32
REL_MAX_DIST = 128
EPS = 1e-6
NEG = -1e30
LOG2E = math.log2(math.e)

LANES = 128
V7X_VMEM_LIMIT = 58 * 1024 * 1024

PROJ_TM = 1024
PROJ_TN = 1024
PROJ_RES_TM = 512
CHUNK = 128
SWA_BLOCKS_PER_ITER = 2
N_SMALL = 64
MAIN_ROWS = BATCH * SEQ

F32 = jnp.float32
BF16 = jnp.bfloat16


def _nt(a, b):
    return lax.dot_general(a, b, (((1,), (1,)), ((), ())), preferred_element_type=F32)


def _nn(a, b):
    return jnp.dot(a, b, preferred_element_type=F32)


def _sigmoid(x):
    return 1.0 / (1.0 + jnp.exp(-x))


def _silu(x):
    h = 0.5 * x
    return h + h * jnp.tanh(h)


def _rmsnorm_kernel(x_ref, g_ref, o_ref):
    x = x_ref[...]
    ms = jnp.mean(x * x, axis=-1, keepdims=True)
    o_ref[...] = (x * lax.rsqrt(ms + EPS) * g_ref[...]).astype(o_ref.dtype)


def _rmsnorm(x, g, out_dtype, tm):
    m, d = x.shape
    return pl.pallas_call(
        _rmsnorm_kernel,
        grid=(m // tm,),
        in_specs=[pl.BlockSpec((tm, d), lambda i: (i, 0)),
                  pl.BlockSpec((1, d), lambda i: (0, 0))],
        out_specs=pl.BlockSpec((tm, d), lambda i: (i, 0)),
        out_shape=jax.ShapeDtypeStruct((m, d), out_dtype),
        compiler_params=pltpu.CompilerParams(dimension_semantics=("parallel",),
                                             vmem_limit_bytes=V7X_VMEM_LIMIT),
        name="rmsnorm",
    )(x, g.reshape(1, d))


def _proj_kernel(a_ref, as_ref, w_hbm, *rest, tn, kc, cps, has_res):
    if has_res:
        r_ref, rs_ref, o_ref, os_ref, wb_ref, stage_ref, sem = rest
    else:
        o_ref, os_ref, wb_ref, stage_ref, sem = rest
    j, i = pl.program_id(0), pl.program_id(1)
    nj, ni = pl.num_programs(0), pl.num_programs(1)
    n_kc = w_hbm.shape[0] // kc
    step = j * ni + i
    cur = lax.rem(j, 2)
    par = lax.rem(step, 2)

    def aligned(x, m):
        return x * m if isinstance(x, int) else pl.multiple_of(x * m, m)

    def chunk_copy(tile, c, slot):
        return pltpu.make_async_copy(
            w_hbm.at[pl.ds(aligned(c, kc), kc), pl.ds(aligned(tile, tn), tn)],
            stage_ref.at[slot], sem.at[slot])

    def cast_chunk(slot, c, half):
        wb_ref[half, pl.ds(aligned(c, kc), kc), :] = stage_ref[slot].astype(BF16)

    def next_tile(jj):
        return jnp.minimum(jj + 1, nj - 1)

    @pl.when(step == 0)
    def _():
        chunk_copy(0, 0, 0).start()
        for c in range(n_kc):
            if c + 1 < n_kc:
                chunk_copy(0, c + 1, (c + 1) % 2).start()
            chunk_copy(0, c, c % 2).wait()
            cast_chunk(c % 2, c, 0)
        for u in range(cps):
            chunk_copy(next_tile(0), u, u).start()

    for u in range(cps):
        chunk_copy(next_tile(j), i * cps + u, par * cps + u).wait()

    @pl.when(step + 1 < nj * ni)
    def _():
        wrap = i + 1 == ni
        j2 = jnp.where(wrap, j + 1, j)
        i2 = jnp.where(wrap, 0, i + 1)
        for u in range(cps):
            chunk_copy(next_tile(j2), i2 * cps + u, (1 - par) * cps + u).start()

    @pl.when(i == 0)
    def _():
        small = _nn(as_ref[...], wb_ref[cur])
        if has_res:
            small = small + rs_ref[...]
        os_ref[...] = small

    for u in range(cps):
        cast_chunk(par * cps + u, i * cps + u, 1 - cur)
    out = _nn(a_ref[...], wb_ref[cur])
    if has_res:
        out = out + r_ref[...]
    o_ref[...] = out


def _proj(a, a_small, w, res=None, res_small=None, *, tm, tn, kc=256):
    m, k = a.shape
    n = w.shape[1]
    ms = a_small.shape[0]
    has_res = res is not None
    ni, n_kc = m // tm, k // kc
    assert m % tm == 0 and n % tn == 0 and k % kc == 0 and n_kc % ni == 0 and n // tn >= 2
    cps = n_kc // ni
    in_specs = [pl.BlockSpec((tm, k), lambda j, i: (i, 0)),
                pl.BlockSpec((ms, k), lambda j, i: (0, 0)),
                pl.BlockSpec(memory_space=pl.ANY)]
    args = [a, a_small, w]
    if has_res:
        in_specs += [pl.BlockSpec((tm, tn), lambda j, i: (i, j)),
                     pl.BlockSpec((ms, tn), lambda j, i: (0, j))]
        args += [res, res_small]
    return pl.pallas_call(
        functools.partial(_proj_kernel, tn=tn, kc=kc, cps=cps, has_res=has_res),
        grid=(n // tn, ni),
        in_specs=in_specs,
        out_specs=[pl.BlockSpec((tm, tn), lambda j, i: (i, j)),
                   pl.BlockSpec((ms, tn), lambda j, i: (0, j))],
        out_shape=[jax.ShapeDtypeStruct((m, n), F32),
                   jax.ShapeDtypeStruct((ms, n), F32)],
        scratch_shapes=[pltpu.VMEM((2, k, tn), BF16),
                        pltpu.VMEM((2 * cps, kc, tn), F32),
                        pltpu.SemaphoreType.DMA((2 * cps,))],
        compiler_params=pltpu.CompilerParams(
            dimension_semantics=("arbitrary", "arbitrary"),
            vmem_limit_bytes=V7X_VMEM_LIMIT),
        name="proj_res" if has_res else "proj",
    )(*args)


def _hgrn_consts():
    t = np.arange(CHUNK)[:, None]
    s = np.arange(CHUNK)[None, :]
    tri = s <= t
    ends = np.array([15, 47, 79, 111, 31, 95, 63, 127] + [-1] * 8)[:, None]
    mall = np.concatenate([tri, s <= ends], axis=0).astype(np.float32)
    lvl = np.full((CHUNK, CHUNK), 3, np.int32)
    lvl[(t >= 64) & (s < 64)] = 2
    lvl[(t // 64 == s // 64) & (t % 64 >= 32) & (s % 64 < 32)] = 1
    lvl[(t // 32 == s // 32) & (s <= t)] = 0
    return jnp.asarray(mall, BF16), jnp.asarray(lvl)


def _lower_bound(lbraw):
    mx = jnp.max(lbraw, axis=0, keepdims=True)
    e = jnp.exp(lbraw - mx)
    return e[0:1, :] / jnp.sum(e, axis=0, keepdims=True)


def _head_out(o, graw, onorm):
    ms = jnp.mean(o * o, axis=-1, keepdims=True)
    return (o * lax.rsqrt(ms + EPS) * onorm) * _silu(graw)


def _hgrn_gates(qraw, fp, lb, mall, row_valid):
    sig = _sigmoid(fp)
    f = lb + (1.0 - lb) * sig
    logf = jnp.log2(f)
    k = 1.0 - f
    if row_valid is not None:
        logf = jnp.where(row_valid, logf, 0.0)
        k = jnp.where(row_valid, k, 0.0)
    q = _silu(qraw)

    hi = logf.astype(BF16)
    mid = (logf - hi.astype(F32)).astype(BF16)
    cs = _nn(mall, jnp.concatenate([hi, mid], axis=1))
    return q, k, cs[:, :LANES] + cs[:, LANES:]


def _hgrn_scores(q, k, cs):
    b = cs[0:CHUNK]
    ref = cs[CHUNK:CHUNK + 8]
    bl = ref[7:8]
    cat = lambda parts: jnp.concatenate(parts, axis=0)
    d0 = cat([b[32 * i:32 * i + 32] - ref[i:i + 1] for i in range(4)])
    d1 = [b[64 * i:64 * i + 64] - ref[4 + i:5 + i] for i in range(2)]
    d2 = b - ref[6:7]
    zero32 = jnp.zeros((32, LANES), BF16)
    a0 = _nt((q * jnp.exp2(d0)).astype(BF16), (k * jnp.exp2(-d0)).astype(BF16))
    q1 = cat([q[32:64] * jnp.exp2(d1[0][32:64]), q[96:128] * jnp.exp2(d1[1][32:64])])
    k1 = [(k[64 * i:64 * i + 32] * jnp.exp2(-d1[i][0:32])).astype(BF16) for i in range(2)]
    a1 = _nt(q1.astype(BF16), cat([k1[0], zero32, k1[1], zero32]))
    a2 = _nt((q[64:128] * jnp.exp2(d2[64:128])).astype(BF16),
             cat([(k[0:64] * jnp.exp2(-d2[0:64])).astype(BF16), zero32, zero32]))
    qe = (q * jnp.exp2(b)).astype(BF16)
    kd = (k * jnp.exp2(bl - b)).astype(BF16)
    return a0, a1, a2, qe, kd, jnp.exp2(bl)


def _hgrn_apply(a0, a1, a2, qe, kd, decay, v, st, lvl):
    is0, is1, is2 = lvl == 0, lvl == 1, lvl == 2
    a = jnp.concatenate([
        jnp.where(is0[0:32], a0[0:32], 0.0),
        jnp.where(is0[32:64], a0[32:64], jnp.where(is1[32:64], a1[0:32], 0.0)),
        jnp.where(is0[64:96], a0[64:96], jnp.where(is2[64:96], a2[0:32], 0.0)),
        jnp.where(is0[96:128], a0[96:128],
                  jnp.where(is1[96:128], a1[32:64], jnp.where(is2[96:128], a2[32:64], 0.0))),
    ], axis=0)
    vt = v.T.astype(BF16)
    lhs = jnp.concatenate([a.astype(BF16), qe], axis=1)
    rhs = jnp.concatenate([vt, st.astype(BF16)], axis=1)
    return _nt(lhs, rhs), st * decay + _nn(vt, kd)


def _hgrn_scan_kernel(q_ref, f_ref, v_ref, g_ref, lb_ref, on_ref, s0_ref, mall_ref, lvl_ref,
                      a_ref, s_ref, st_ref, *, hb, n_chunks, n_pad):
    t = pl.program_id(2)

    @pl.when(t == 0)
    def _():
        for h in range(hb):
            st_ref[h] = s0_ref[h].T

    lb_all = _lower_bound(lb_ref[...])
    row_valid = None
    if n_pad:
        row_valid = lax.broadcasted_iota(jnp.int32, (CHUNK, LANES), 0) >= n_pad
    heads = [slice(h * LANES, (h + 1) * LANES) for h in range(hb)]

    def body(c, carry):
        rows = pl.ds(pl.multiple_of(c * CHUNK, CHUNK), CHUNK)
        gates = [_hgrn_gates(q_ref[0, rows, cols], f_ref[0, rows, cols], lb_all[:, cols],
                             mall_ref[...], row_valid) for cols in heads]
        scores = [_hgrn_scores(*g) for g in gates]
        outs = [_hgrn_apply(*sc, v_ref[0, rows, cols], st_ref[h], lvl_ref[...])
                for h, (sc, cols) in enumerate(zip(scores, heads))]
        for h, ((o, st_new), cols) in enumerate(zip(outs, heads)):
            st_ref[h] = st_new
            a_ref[0, rows, cols] = _head_out(o, g_ref[0, rows, cols], on_ref[:, cols]).astype(BF16)
        return carry

    lax.fori_loop(0, n_chunks, body, 0)

    @pl.when(t == pl.num_programs(2) - 1)
    def _():
        for h in range(hb):
            s_ref[0, h] = st_ref[h].T


def _hgrn_scan(proj, lbraw, onorm, s0, *, tb, hb, n_pad=0):
    bsz, tlen, _ = proj.shape
    mall, lvl = _hgrn_consts()
    hcols = HG_F // (hb * LANES)
    sect = lambda s: (lambda b, h, t: (b, t, s * hcols + h))
    blk = (1, tb, hb * LANES)
    return pl.pallas_call(
        functools.partial(_hgrn_scan_kernel, hb=hb, n_chunks=tb // CHUNK, n_pad=n_pad),
        grid=(bsz, HG_HEADS // hb, tlen // tb),
        in_specs=[pl.BlockSpec(blk, sect(0)), pl.BlockSpec(blk, sect(1)),
                  pl.BlockSpec(blk, sect(2)), pl.BlockSpec(blk, sect(3)),
                  pl.BlockSpec((3, hb * LANES), lambda b, h, t: (0, h)),
                  pl.BlockSpec((1, hb * LANES), lambda b, h, t: (0, h)),
                  pl.BlockSpec((hb, HG_DK, HG_DK), lambda b, h, t: (h, 0, 0)),
                  pl.BlockSpec((CHUNK + 16, CHUNK), lambda b, h, t: (0, 0)),
                  pl.BlockSpec((CHUNK, CHUNK), lambda b, h, t: (0, 0))],
        out_specs=[pl.BlockSpec(blk, lambda b, h, t: (b, t, h)),
                   pl.BlockSpec((1, hb, HG_DK, HG_DK), lambda b, h, t: (b, h, 0, 0))],
        out_shape=[jax.ShapeDtypeStruct((bsz, tlen, D_MODEL), BF16),
                   jax.ShapeDtypeStruct((bsz, HG_HEADS, HG_DK, HG_DK), F32)],
        scratch_shapes=[pltpu.VMEM((hb, HG_DK, HG_DK), F32)],
        compiler_params=pltpu.CompilerParams(
            dimension_semantics=("parallel", "parallel", "arbitrary")),
        name="hgrn_scan",
    )(proj, proj, proj, proj, lbraw, onorm.reshape(1, D_MODEL), s0, mall, lvl)


def _hgrn_step_kernel(q_ref, f_ref, v_ref, g_ref, lb_ref, on_ref, s0_ref, a_ref, s_ref):
    lb = _lower_bound(lb_ref[...])[0]
    fp = f_ref[0]
    qraw = q_ref[0]
    v = v_ref[0]
    sig = _sigmoid(fp)
    f = lb + (1.0 - lb) * sig
    decay = jnp.exp(jnp.log(f))
    k = 1.0 - f
    q = _silu(qraw)

    def split(x):
        hi = x.astype(BF16).astype(F32)
        return hi, x - hi

    (kh, kl), (dh, dl), (qh, ql), (vh, vl) = split(k), split(decay), split(q), split(v)
    r = lax.broadcasted_iota(jnp.int32, (8, HG_DK), 0)
    ones_tail = jnp.concatenate([jnp.where((r == 3) | (r == 4), 1.0, 0.0),
                                 jnp.where((r == 5) | (r == 6), 1.0, 0.0)], axis=1).astype(BF16)
    v_hi_rows = (r == 0) | (r == 2)
    rows = []
    for h in range(HG_HEADS):
        row = lambda x: jnp.broadcast_to(x[h:h + 1, :], (8, HG_DK))
        lhs = jnp.where(r < 2, row(kh), jnp.where(r == 2, row(kl), jnp.where(
            r == 3, row(dh), jnp.where(r == 4, row(dl), jnp.where(
                r == 5, row(qh), jnp.where(r == 6, row(ql), 0.0))))))
        rhs_v = jnp.where(v_hi_rows, row(vh), jnp.where(r == 1, row(vl), 0.0))
        rhs = jnp.concatenate([rhs_v.astype(BF16), ones_tail], axis=1)
        out = lax.dot_general(lhs.astype(BF16), rhs, (((0,), (0,)), ((), ())),
                              preferred_element_type=F32)
        s_new = out[:, HG_DK:2 * HG_DK] * s0_ref[0, 0, h] + out[:, :HG_DK]
        s_ref[0, 0, h] = s_new
        rows.append(jnp.sum(out[:, 2 * HG_DK:] * s_new, axis=0, keepdims=True))
    o = jnp.concatenate(rows, axis=0)
    a_ref[0] = _head_out(o, g_ref[0], on_ref[...]).astype(BF16)


def _hgrn_step(q, f, v, g, lbraw, onorm, state):
    bsz = q.shape[0]
    vec = pl.BlockSpec((1, HG_HEADS, HG_DK), lambda b: (b, 0, 0))
    full = pl.BlockSpec((HG_HEADS, HG_DK), lambda b: (0, 0))
    lbspec = pl.BlockSpec((3, HG_HEADS, HG_DK), lambda b: (0, 0, 0))
    sspec = pl.BlockSpec((1, 1, HG_HEADS, HG_DK, HG_DK), lambda b: (0, b, 0, 0, 0))
    return pl.pallas_call(
        _hgrn_step_kernel,
        grid=(bsz,),
        in_specs=[vec, vec, vec, vec, lbspec, full, sspec],
        out_specs=[vec, sspec],
        out_shape=[jax.ShapeDtypeStruct((bsz, HG_HEADS, HG_DK), BF16),
                   jax.ShapeDtypeStruct(state.shape, state.dtype)],
        compiler_params=pltpu.CompilerParams(dimension_semantics=("parallel",)),
        name="hgrn_step",
    )(q, f, v, g, lbraw.reshape(3, HG_HEADS, HG_DK), onorm.reshape(HG_HEADS, HG_DK), state)


def _t5_bucket(dist):
    max_exact = REL_BUCKETS // 2
    d = jnp.maximum(dist, 1).astype(F32)
    large = max_exact + (jnp.log(d / max_exact) / math.log(REL_MAX_DIST / max_exact)
                         * (REL_BUCKETS - max_exact)).astype(jnp.int32)
    large = jnp.minimum(large, REL_BUCKETS - 1)
    return jnp.where(dist < max_exact, dist, large)


def _prompt_bias_kernel(w_ref, o_ref):
    band = 2 * WINDOW
    key = lax.broadcasted_iota(jnp.int32, (band, WINDOW), 0)
    for g in range(SW_GROUP):
        j, half = divmod(g, 2)
        row = jnp.broadcast_to(w_ref[g:g + 1, :], (band, 3 * WINDOW))
        toep = pltpu.roll(row, 0, 1, stride=1, stride_axis=0)[:, :WINDOW]
        dst = (slice(half * band, (half + 1) * band), slice(j * WINDOW, (j + 1) * WINDOW))
        o_ref[1, 0, dst[0], dst[1]] = toep
        o_ref[0, 0, dst[0], dst[1]] = jnp.where(key < WINDOW - N_META, NEG * LOG2E, toep)


def _prompt_bias(table):
    band = 2 * WINDOW
    i = jnp.arange(3 * WINDOW)
    dist = jnp.where(i < WINDOW, i + WINDOW, i - band)
    vals = table.astype(F32)[_t5_bucket(jnp.maximum(dist, 0))]
    vals = jnp.where(((dist >= 0) & (dist <= WINDOW))[:, None], vals, NEG) * LOG2E
    return pl.pallas_call(
        _prompt_bias_kernel,
        grid=(SW_KV,),
        in_specs=[pl.BlockSpec((SW_GROUP, 3 * WINDOW), lambda n: (n, 0))],
        out_specs=pl.BlockSpec((2, 1, 2 * band, 4 * WINDOW), lambda n: (0, n, 0, 0)),
        out_shape=jax.ShapeDtypeStruct((2, SW_KV, 2 * band, 4 * WINDOW), F32),
        compiler_params=pltpu.CompilerParams(dimension_semantics=("parallel",)),
        name="prompt_bias",
    )(vals.T)


def _expand_band(pair, u):
    lane = lax.broadcasted_iota(jnp.int32, pair.shape, 1)
    rolled = pltpu.roll(pair, SW_HD, axis=1)
    lo_src, hi_src = (pair, rolled) if u == 0 else (rolled, pair)
    top = jnp.where(lane < SW_HD, lo_src, 0.0)
    bot = jnp.where(lane >= SW_HD, hi_src, 0.0)
    return jnp.concatenate([top, bot], axis=0).astype(BF16)


def _swa_probs(s, sinks):
    band = 2 * WINDOW
    ps, ms = [], []
    for half in range(2):
        sh = s[half * band:(half + 1) * band]
        m = jnp.maximum(jnp.max(sh, axis=0, keepdims=True), sinks[half])
        ps.append(jnp.exp2(sh - m).astype(BF16))
        ms.append(m)
    return jnp.concatenate(ps, axis=0), ms


def _swa_values(pt, vxt, ms, sinks):
    ot = _nn(vxt, pt)
    outs = []
    for half in range(2):
        den = ot[2 * SW_HD + 8 * half:2 * SW_HD + 8 * half + 1] + jnp.exp2(sinks[half] - ms[half])
        outs.append(ot[SW_HD * half:SW_HD * (half + 1)] * (1.0 / den))
    return jnp.concatenate(outs, axis=0)


def _swa_prompt_kernel(sink_ref, q_ref, g_ref, k_ref, v_ref, km_ref, vm_ref, bias_ref,
                       o_ref, *, n_blocks):
    p = pl.program_id(1)
    qb = pl.program_id(2)
    n_pairs = SW_GROUP // 2
    band = 2 * WINDOW

    def sink_rows(u):
        return [jnp.concatenate(
            [jnp.full((1, WINDOW), sink_ref[(p * 2 + u) * SW_GROUP + 2 * j + half] * LOG2E, F32)
             for j in range(n_pairs)], axis=1) for half in range(2)]

    sinks = [sink_rows(u) for u in range(2)]
    orow = lax.broadcasted_iota(jnp.int32, (16, 2 * band), 0)
    ocol = lax.broadcasted_iota(jnp.int32, (16, 2 * band), 1)
    ones_rows = jnp.where((orow < 8) == (ocol < band), 1.0, 0.0).astype(BF16)
    zeros = jnp.zeros((SW_HD, band), BF16)

    def band_of(blk):
        tok = qb * n_blocks + blk
        cur = pl.ds(pl.multiple_of(tok * WINDOW, WINDOW), WINDOW)
        prev = pl.ds(pl.multiple_of(jnp.maximum(tok - 1, 0) * WINDOW, WINDOW), WINDOW)
        is_first = tok == 0
        kband = jnp.concatenate([jnp.where(is_first, km_ref[...], k_ref[0, prev, :]),
                                 k_ref[0, cur, :]], axis=0)
        vband = jnp.concatenate([jnp.where(is_first, vm_ref[...], v_ref[0, prev, :]),
                                 v_ref[0, cur, :]], axis=0)
        return (pl.ds(pl.multiple_of(blk * WINDOW, WINDOW), WINDOW), kband,
                vband.T.astype(BF16), jnp.where(is_first, 0, 1))

    def blocks(it, carry):
        bands = [band_of(it * SWA_BLOCKS_PER_ITER + i) for i in range(SWA_BLOCKS_PER_ITER)]
        units = [(bnd, u) for bnd in bands for u in range(2)]
        scores = []
        for (rows, kband, _, bias_sel), u in units:
            base = u * SW_GROUP * SW_HD
            q4 = jnp.concatenate(
                [q_ref[0, rows, base + j * LANES:base + (j + 1) * LANES] for j in range(n_pairs)],
                axis=0)
            q4 = (q4 * (SW_HD ** -0.5 * LOG2E)).astype(BF16)
            scores.append(_nt(_expand_band(kband, u), q4) + bias_ref[bias_sel, u])
        probs = [_swa_probs(s, sinks[u]) for s, (_, u) in zip(scores, units)]
        outs = []
        for (pt, ms), ((_, _, vbt, _), u) in zip(probs, units):
            vt = vbt[u * SW_HD:(u + 1) * SW_HD]
            vxt = jnp.concatenate([jnp.concatenate([vt, zeros], axis=1),
                                   jnp.concatenate([zeros, vt], axis=1), ones_rows], axis=0)
            outs.append(_swa_values(pt, vxt, ms, sinks[u]))
        for ot, ((rows, _, _, _), u) in zip(outs, units):
            base = u * SW_GROUP * SW_HD
            o = jnp.concatenate([ot[:, j * WINDOW:(j + 1) * WINDOW].T for j in range(n_pairs)],
                                axis=1)
            graw = g_ref[0, rows, base:base + SW_GROUP * SW_HD]
            o_ref[0, rows, base:base + SW_GROUP * SW_HD] = (o * _silu(graw)).astype(BF16)
        return carry

    lax.fori_loop(0, n_blocks // SWA_BLOCKS_PER_ITER, blocks, 0)


def _swa_prompt(proj, meta_k, meta_v, sinks, bias, *, tq):
    bsz, tlen, _ = proj.shape
    pw = 2 * SW_GROUP * SW_HD
    qblocks = SW_HEADS * SW_HD // pw
    kcol0 = SW_HEADS * SW_HD // LANES
    vcol0 = kcol0 + SW_KV * SW_HD // LANES
    gblk0 = (SW_HEADS * SW_HD + 2 * SW_KV * SW_HD) // pw
    return pl.pallas_call(
        functools.partial(_swa_prompt_kernel, n_blocks=tq // WINDOW),
        grid=(bsz, qblocks, tlen // tq),
        in_specs=[pl.BlockSpec(memory_space=pltpu.SMEM),
                  pl.BlockSpec((1, tq, pw), lambda b, p, t: (b, t, p)),
                  pl.BlockSpec((1, tq, pw), lambda b, p, t: (b, t, gblk0 + p)),
                  pl.BlockSpec((1, tlen, LANES), lambda b, p, t: (b, 0, kcol0 + p)),
                  pl.BlockSpec((1, tlen, LANES), lambda b, p, t: (b, 0, vcol0 + p)),
                  pl.BlockSpec((WINDOW, LANES), lambda b, p, t: (0, p)),
                  pl.BlockSpec((WINDOW, LANES), lambda b, p, t: (0, p)),
                  pl.BlockSpec((2, 2, 4 * WINDOW, 4 * WINDOW), lambda b, p, t: (0, p, 0, 0))],
        out_specs=pl.BlockSpec((1, tq, pw), lambda b, p, t: (b, t, p)),
        out_shape=jax.ShapeDtypeStruct((bsz, tlen, D_MODEL), BF16),
        compiler_params=pltpu.CompilerParams(
            dimension_semantics=("parallel", "parallel", "arbitrary"),
            vmem_limit_bytes=V7X_VMEM_LIMIT),
        name="swa_prompt",
    )(sinks, proj, proj, proj, proj, meta_k, meta_v, bias)


def _swa_sample_kernel(q_ref, g_ref, kn_ref, vn_ref, ck_ref, cv_ref, bc_ref, bn_ref, sink_ref,
                       o_ref, nk_ref, nv_ref):
    nkv = SW_KV * SW_HD
    q = q_ref[0] * (SW_HD ** -0.5)
    row = lax.broadcasted_iota(jnp.int32, (SW_HEADS, nkv), 0)
    col = lax.broadcasted_iota(jnp.int32, (SW_HEADS, nkv), 1)
    own = (row // SW_GROUP) == (col // SW_HD)
    qx = jnp.where(own, jnp.concatenate([q] * SW_KV, axis=1), 0.0)
    ck, cv = ck_ref[0], cv_ref[0]
    kn, vn = kn_ref[0], vn_ref[0]
    sink = sink_ref[...]

    s_c = _nt(qx.astype(BF16), ck.astype(BF16)) + bc_ref[...]
    s_n = jnp.sum(qx * kn, axis=-1, keepdims=True) + bn_ref[...]
    m = jnp.maximum(jnp.maximum(jnp.max(s_c, axis=-1, keepdims=True), s_n), sink)
    p_c = jnp.exp(s_c - m)
    p_n = jnp.exp(s_n - m)
    den = jnp.sum(p_c, axis=-1, keepdims=True) + p_n + jnp.exp(sink - m)
    o_all = _nn(p_c.astype(BF16), cv.astype(BF16))
    o_all = o_all + p_n * vn
    o_all = jnp.where(own, o_all, 0.0)
    o = o_all[:, 0:SW_HD]
    for n in range(1, SW_KV):
        o = o + o_all[:, n * SW_HD:(n + 1) * SW_HD]
    graw = g_ref[0]
    o_ref[0] = ((o / den) * _silu(graw)).astype(BF16)

    r = ck.shape[0]
    last = lax.broadcasted_iota(jnp.int32, ck.shape, 0) == r - 1
    nk_ref[0] = jnp.where(last, kn, pltpu.roll(ck, r - 1, axis=0))
    nv_ref[0] = jnp.where(last, vn, pltpu.roll(cv, r - 1, axis=0))


def _swa_sample(q, g, kn, vn, ck, cv, bias_c, bias_n, sinks):
    bsz, r, nkv = ck.shape
    head = pl.BlockSpec((1, SW_HEADS, SW_HD), lambda b: (b, 0, 0))
    new = pl.BlockSpec((1, 1, nkv), lambda b: (b, 0, 0))
    cache = pl.BlockSpec((1, r, nkv), lambda b: (b, 0, 0))
    return pl.pallas_call(
        _swa_sample_kernel,
        grid=(bsz,),
        in_specs=[head, head, new, new, cache, cache,
                  pl.BlockSpec((SW_HEADS, r), lambda b: (0, 0)),
                  pl.BlockSpec((SW_HEADS, 1), lambda b: (0, 0)),
                  pl.BlockSpec((SW_HEADS, 1), lambda b: (0, 0))],
        out_specs=[head, cache, cache],
        out_shape=[jax.ShapeDtypeStruct((bsz, SW_HEADS, SW_HD), BF16),
                   jax.ShapeDtypeStruct(ck.shape, ck.dtype),
                   jax.ShapeDtypeStruct(cv.shape, cv.dtype)],
        compiler_params=pltpu.CompilerParams(dimension_semantics=("parallel",)),
        name="swa_sample",
    )(q, g, kn, vn, ck, cv, bias_c, bias_n, sinks.reshape(SW_HEADS, 1))


def kernel(x_prompt, x_sample, state_hgrn, cache_k_win, cache_v_win, meta_tokens, rel_bias,
           hg_lower_bounds, hg_norm, hg_w_in, hg_onorm, hg_w_out,
           sw_norm, sw_w_in, sw_sinks, sw_w_out, final_norm):
    n_samp = x_sample.shape[0]
    samp = slice(N_META, N_META + n_samp)
    x_main = x_prompt.reshape(MAIN_ROWS, D_MODEL)
    x_small = jnp.concatenate(
        [meta_tokens.astype(F32), x_sample.reshape(n_samp, D_MODEL),
         jnp.zeros((N_SMALL - N_META - n_samp, D_MODEL), F32)], axis=0)

    h_main = _rmsnorm(x_main, hg_norm[0], BF16, 512)
    h_small = _rmsnorm(x_small, hg_norm[0], BF16, N_SMALL)
    p_main, p_small = _proj(h_main, h_small, hg_w_in[0], tm=PROJ_TM, tn=PROJ_TN)

    meta_proj = jnp.pad(p_small[:N_META], ((CHUNK - N_META, 0), (0, 0)))[None]
    zero_state = jnp.zeros((HG_HEADS, HG_DK, HG_DK), F32)
    a_meta, s_meta = _hgrn_scan(meta_proj, hg_lower_bounds, hg_onorm[0], zero_state,
                                tb=CHUNK, hb=2, n_pad=CHUNK - N_META)
    a_main, s_prompt = _hgrn_scan(p_main.reshape(BATCH, SEQ, -1), hg_lower_bounds, hg_onorm[0],
                                  s_meta[0], tb=512, hb=8)
    sect = lambda s: p_small[samp, s * HG_F:(s + 1) * HG_F].reshape(n_samp, HG_HEADS, HG_DK)
    a_samp, s_sample = _hgrn_step(sect(0), sect(1), sect(2), sect(3), hg_lower_bounds,
                                  hg_onorm[0], state_hgrn)
    a_small = jnp.concatenate(
        [a_meta[0, CHUNK - N_META:], a_samp.reshape(n_samp, D_MODEL),
         jnp.zeros((N_SMALL - N_META - n_samp, D_MODEL), BF16)], axis=0)
    x1_main, x1_small = _proj(a_main.reshape(MAIN_ROWS, D_MODEL), a_small, hg_w_out[0],
                              x_main, x_small, tm=PROJ_RES_TM, tn=PROJ_TN)

    h_main = _rmsnorm(x1_main, sw_norm[0], BF16, 512)
    h_small = _rmsnorm(x1_small, sw_norm[0], BF16, N_SMALL)
    p_main, p_small = _proj(h_main, h_small, sw_w_in[0], tm=PROJ_TM, tn=PROJ_TN)
    nq, nkv = SW_HEADS * SW_HD, SW_KV * SW_HD
    kv_tail = p_main.reshape(BATCH, SEQ, -1)[:, -WINDOW:, nq:nq + 2 * nkv]
    k_tail = kv_tail[:, :, :nkv].reshape(1, BATCH, WINDOW, SW_KV, SW_HD)
    v_tail = kv_tail[:, :, nkv:].reshape(1, BATCH, WINDOW, SW_KV, SW_HD)
    meta_kv = jnp.pad(p_small[:N_META, nq:nq + 2 * nkv], ((WINDOW - N_META, 0), (0, 0)))
    a_main = _swa_prompt(p_main.reshape(BATCH, SEQ, -1), meta_kv[:, :nkv], meta_kv[:, nkv:],
                         sw_sinks[0], _prompt_bias(rel_bias), tq=512)

    r = cache_k_win.shape[2]
    table = rel_bias.astype(F32)
    bias_c = table[_t5_bucket(r - jnp.arange(r))].T
    bias_n = table[_t5_bucket(jnp.zeros((1,), jnp.int32))].T
    q_s = p_small[samp, :nq].reshape(n_samp, SW_HEADS, SW_HD)
    g_s = p_small[samp, nq + 2 * nkv:].reshape(n_samp, SW_HEADS, SW_HD)
    a_samp, k_samp, v_samp = _swa_sample(
        q_s, g_s, p_small[samp, nq:nq + nkv].reshape(n_samp, 1, nkv),
        p_small[samp, nq + nkv:nq + 2 * nkv].reshape(n_samp, 1, nkv),
        cache_k_win[0].reshape(n_samp, r, nkv), cache_v_win[0].reshape(n_samp, r, nkv),
        bias_c, bias_n, sw_sinks[0])
    a_small = jnp.concatenate(
        [jnp.zeros((N_META, D_MODEL), BF16), a_samp.reshape(n_samp, D_MODEL),
         jnp.zeros((N_SMALL - N_META - n_samp, D_MODEL), BF16)], axis=0)
    x2_main, x2_small = _proj(a_main.reshape(MAIN_ROWS, D_MODEL), a_small, sw_w_out[0],
                              x1_main, x1_small, tm=PROJ_RES_TM, tn=PROJ_TN)

    y_prompt = _rmsnorm(x2_main, final_norm, F32, 512).reshape(BATCH, SEQ, D_MODEL)
    y_sample = _rmsnorm(x2_small, final_norm, F32, N_SMALL)[samp].reshape(n_samp, 1, D_MODEL)
    return (y_prompt, y_sample, s_prompt[None],
            k_tail.astype(cache_k_win.dtype), v_tail.astype(cache_v_win.dtype),
            s_sample,
            k_samp.reshape(1, n_samp, r, SW_KV, SW_HD), v_samp.reshape(1, n_samp, r, SW_KV, SW_HD))
```

```python
import functools
import math

import numpy as np
import jax
import jax.numpy as jnp
from jax import lax
from jax.experimental import pallas as pl
from jax.experimental.pallas import tpu as pltpu

D_MODEL = 4096
BATCH = 4
SEQ = 2048
DEC_BATCH = 32
N_META = 16
HG_HEADS = 32
HG_DK = 128
HG_F = HG_HEADS * HG_DK
SW_HEADS = 64
SW_KV = 8
SW_HD = 64
SW_GROUP = SW_HEADS // SW_KV
WINDOW = 128
REL_BUCKETS = 32
REL_MAX_DIST = 128
EPS = 1e-6
NEG = -1e30
LOG2E = math.log2(math.e)

LANES = 128
V7X_VMEM_LIMIT = 58 * 1024 * 1024

PROJ_TM = 1024
PROJ_TN = 1024
PROJ_RES_TM = 512
CHUNK = 128
SWA_BLOCKS_PER_ITER = 2
SAMPLES_PER_STEP = 4
N_SMALL = 64
MAIN_ROWS = BATCH * SEQ

F32 = jnp.float32
BF16 = jnp.bfloat16


def _nt(a, b):
    return lax.dot_general(a, b, (((1,), (1,)), ((), ())), preferred_element_type=F32)


def _nn(a, b):
    return jnp.dot(a, b, preferred_element_type=F32)


def _sigmoid(x):
    return 1.0 / (1.0 + jnp.exp(-x))


def _silu(x):
    h = 0.5 * x
    return h + h * jnp.tanh(h)


def _rmsnorm_kernel(x_ref, g_ref, o_ref):
    x = x_ref[...]
    ms = jnp.mean(x * x, axis=-1, keepdims=True)
    o_ref[...] = (x * lax.rsqrt(ms + EPS) * g_ref[...]).astype(o_ref.dtype)


def _rmsnorm(x, g, out_dtype, tm):
    m, d = x.shape
    return pl.pallas_call(
        _rmsnorm_kernel,
        grid=(m // tm,),
        in_specs=[pl.BlockSpec((tm, d), lambda i: (i, 0)),
                  pl.BlockSpec((1, d), lambda i: (0, 0))],
        out_specs=pl.BlockSpec((tm, d), lambda i: (i, 0)),
        out_shape=jax.ShapeDtypeStruct((m, d), out_dtype),
        compiler_params=pltpu.CompilerParams(dimension_semantics=("parallel",),
                                             vmem_limit_bytes=V7X_VMEM_LIMIT),
        name="rmsnorm",
    )(x, g.reshape(1, d))


def _proj_kernel(a_ref, as_ref, w_hbm, *rest, tn, kc, cps, has_res):
    if has_res:
        r_ref, rs_ref, o_ref, os_ref, wb_ref, stage_ref, sem = rest
    else:
        o_ref, os_ref, wb_ref, stage_ref, sem = rest
    j, i = pl.program_id(0), pl.program_id(1)
    nj, ni = pl.num_programs(0), pl.num_programs(1)
    n_kc = w_hbm.shape[0] // kc
    step = j * ni + i
    cur = lax.rem(j, 2)
    par = lax.rem(step, 2)

    def aligned(x, m):
        return x * m if isinstance(x, int) else pl.multiple_of(x * m, m)

    def chunk_copy(tile, c, slot):
        return pltpu.make_async_copy(
            w_hbm.at[pl.ds(aligned(c, kc), kc), pl.ds(aligned(tile, tn), tn)],
            stage_ref.at[slot], sem.at[slot])

    def cast_chunk(slot, c, half):
        wb_ref[half, pl.ds(aligned(c, kc), kc), :] = stage_ref[slot].astype(BF16)

    def next_tile(jj):
        return jnp.minimum(jj + 1, nj - 1)

    @pl.when(step == 0)
    def _():
        chunk_copy(0, 0, 0).start()
        for c in range(n_kc):
            if c + 1 < n_kc:
                chunk_copy(0, c + 1, (c + 1) % 2).start()
            chunk_copy(0, c, c % 2).wait()
            cast_chunk(c % 2, c, 0)
        for u in range(cps):
            chunk_copy(next_tile(0), u, u).start()

    for u in range(cps):
        chunk_copy(next_tile(j), i * cps + u, par * cps + u).wait()

    @pl.when(step + 1 < nj * ni)
    def _():
        wrap = i + 1 == ni
        j2 = jnp.where(wrap, j + 1, j)
        i2 = jnp.where(wrap, 0, i + 1)
        for u in range(cps):
            chunk_copy(next_tile(j2), i2 * cps + u, (1 - par) * cps + u).start()

    @pl.when(i == 0)
    def _():
        small = _nn(as_ref[...], wb_ref[cur])
        if has_res:
            small = small + rs_ref[...]
        os_ref[...] = small

    for u in range(cps):
        cast_chunk(par * cps + u, i * cps + u, 1 - cur)
    out = _nn(a_ref[...], wb_ref[cur])
    if has_res:
        out = out + r_ref[...]
    o_ref[...] = out


def _proj(a, a_small, w, res=None, res_small=None, *, tm, tn, kc=256):
    m, k = a.shape
    n = w.shape[1]
    ms = a_small.shape[0]
    has_res = res is not None
    ni, n_kc = m // tm, k // kc
    assert m % tm == 0 and n % tn == 0 and k % kc == 0 and n_kc % ni == 0 and n // tn >= 2
    cps = n_kc // ni
    in_specs = [pl.BlockSpec((tm, k), lambda j, i: (i, 0)),
                pl.BlockSpec((ms, k), lambda j, i: (0, 0)),
                pl.BlockSpec(memory_space=pl.ANY)]
    args = [a, a_small, w]
    if has_res:
        in_specs += [pl.BlockSpec((tm, tn), lambda j, i: (i, j)),
                     pl.BlockSpec((ms, tn), lambda j, i: (0, j))]
        args += [res, res_small]
    return pl.pallas_call(
        functools.partial(_proj_kernel, tn=tn, kc=kc, cps=cps, has_res=has_res),
        grid=(n // tn, ni),
        in_specs=in_specs,
        out_specs=[pl.BlockSpec((tm, tn), lambda j, i: (i, j)),
                   pl.BlockSpec((ms, tn), lambda j, i: (0, j))],
        out_shape=[jax.ShapeDtypeStruct((m, n), F32),
                   jax.ShapeDtypeStruct((ms, n), F32)],
        scratch_shapes=[pltpu.VMEM((2, k, tn), BF16),
                        pltpu.VMEM((2 * cps, kc, tn), F32),
                        pltpu.SemaphoreType.DMA((2 * cps,))],
        compiler_params=pltpu.CompilerParams(
            dimension_semantics=("arbitrary", "arbitrary"),
            vmem_limit_bytes=V7X_VMEM_LIMIT),
        name="proj_res" if has_res else "proj",
    )(*args)


def _hgrn_consts():
    t = np.arange(CHUNK)[:, None]
    s = np.arange(CHUNK)[None, :]
    tri = s <= t
    ends = np.array([15, 47, 79, 111, 31, 95, 63, 127] + [-1] * 8)[:, None]
    mall = np.concatenate([tri, s <= ends], axis=0).astype(np.float32)
    lvl = np.full((CHUNK, CHUNK), 3, np.int32)
    lvl[(t >= 64) & (s < 64)] = 2
    lvl[(t // 64 == s // 64) & (t % 64 >= 32) & (s % 64 < 32)] = 1
    lvl[(t // 32 == s // 32) & (s <= t)] = 0
    return jnp.asarray(mall, BF16), jnp.asarray(lvl)


def _lower_bound(lbraw):
    mx = jnp.max(lbraw, axis=0, keepdims=True)
    e = jnp.exp(lbraw - mx)
    return e[0:1, :] / jnp.sum(e, axis=0, keepdims=True)


def _head_out(o, graw, onorm):
    ms = jnp.mean(o * o, axis=-1, keepdims=True)
    return (o * lax.rsqrt(ms + EPS) * onorm) * _silu(graw)


def _hgrn_gates(qraw, fp, lb, mall, row_valid):
    sig = _sigmoid(fp)
    f = lb + (1.0 - lb) * sig
    logf = jnp.log2(f)
    k = 1.0 - f
    if row_valid is not None:
        logf = jnp.where(row_valid, logf, 0.0)
        k = jnp.where(row_valid, k, 0.0)
    q = _silu(qraw)

    hi = logf.astype(BF16)
    mid = (logf - hi.astype(F32)).astype(BF16)
    cs = _nn(mall, jnp.concatenate([hi, mid], axis=1))
    return q, k, cs[:, :LANES] + cs[:, LANES:]


def _hgrn_scores(q, k, cs):
    b = cs[0:CHUNK]
    ref = cs[CHUNK:CHUNK + 8]
    bl = ref[7:8]
    cat = lambda parts: jnp.concatenate(parts, axis=0)
    d0 = cat([b[32 * i:32 * i + 32] - ref[i:i + 1] for i in range(4)])
    d1 = [b[64 * i:64 * i + 64] - ref[4 + i:5 + i] for i in range(2)]
    d2 = b - ref[6:7]
    zero32 = jnp.zeros((32, LANES), BF16)
    a0 = _nt((q * jnp.exp2(d0)).astype(BF16), (k * jnp.exp2(-d0)).astype(BF16))
    q1 = cat([q[32:64] * jnp.exp2(d1[0][32:64]), q[96:128] * jnp.exp2(d1[1][32:64])])
    k1 = [(k[64 * i:64 * i + 32] * jnp.exp2(-d1[i][0:32])).astype(BF16) for i in range(2)]
    a1 = _nt(q1.astype(BF16), cat([k1[0], zero32, k1[1], zero32]))
    a2 = _nt((q[64:128] * jnp.exp2(d2[64:128])).astype(BF16),
             cat([(k[0:64] * jnp.exp2(-d2[0:64])).astype(BF16), zero32, zero32]))
    qe = (q * jnp.exp2(b)).astype(BF16)
    kd = (k * jnp.exp2(bl - b)).astype(BF16)
    return a0, a1, a2, qe, kd, jnp.exp2(bl)


def _hgrn_apply(a0, a1, a2, qe, kd, decay, v, st, lvl):
    is0, is1, is2 = lvl == 0, lvl == 1, lvl == 2
    a = jnp.concatenate([
        jnp.where(is0[0:32], a0[0:32], 0.0),
        jnp.where(is0[32:64], a0[32:64], jnp.where(is1[32:64], a1[0:32], 0.0)),
        jnp.where(is0[64:96], a0[64:96], jnp.where(is2[64:96], a2[0:32], 0.0)),
        jnp.where(is0[96:128], a0[96:128],
                  jnp.where(is1[96:128], a1[32:64], jnp.where(is2[96:128], a2[32:64], 0.0))),
    ], axis=0)
    vt = v.T.astype(BF16)
    lhs = jnp.concatenate([a.astype(BF16), qe], axis=1)
    rhs = jnp.concatenate([vt, st.astype(BF16)], axis=1)
    return _nt(lhs, rhs), st * decay + _nn(vt, kd)


def _hgrn_scan_kernel(q_ref, f_ref, v_ref, g_ref, lb_ref, on_ref, s0_ref, mall_ref, lvl_ref,
                      a_ref, s_ref, st_ref, *, hb, n_chunks, n_pad):
    t = pl.program_id(2)

    @pl.when(t == 0)
    def _():
        for h in range(hb):
            st_ref[h] = s0_ref[h].T

    lb_all = _lower_bound(lb_ref[...])
    row_valid = None
    if n_pad:
        row_valid = lax.broadcasted_iota(jnp.int32, (CHUNK, LANES), 0) >= n_pad
    heads = [slice(h * LANES, (h + 1) * LANES) for h in range(hb)]

    def body(c, carry):
        rows = pl.ds(pl.multiple_of(c * CHUNK, CHUNK), CHUNK)
        gates = [_hgrn_gates(q_ref[0, rows, cols], f_ref[0, rows, cols], lb_all[:, cols],
                             mall_ref[...], row_valid) for cols in heads]
        scores = [_hgrn_scores(*g) for g in gates]
        outs = [_hgrn_apply(*sc, v_ref[0, rows, cols], st_ref[h], lvl_ref[...])
                for h, (sc, cols) in enumerate(zip(scores, heads))]
        for h, ((o, st_new), cols) in enumerate(zip(outs, heads)):
            st_ref[h] = st_new
            a_ref[0, rows, cols] = _head_out(o, g_ref[0, rows, cols], on_ref[:, cols]).astype(BF16)
        return carry

    lax.fori_loop(0, n_chunks, body, 0)

    @pl.when(t == pl.num_programs(2) - 1)
    def _():
        for h in range(hb):
            s_ref[0, h] = st_ref[h].T


def _hgrn_scan(proj, lbraw, onorm, s0, *, tb, hb, n_pad=0):
    bsz, tlen, _ = proj.shape
    mall, lvl = _hgrn_consts()
    hcols = HG_F // (hb * LANES)
    sect = lambda s: (lambda b, h, t: (b, t, s * hcols + h))
    blk = (1, tb, hb * LANES)
    return pl.pallas_call(
        functools.partial(_hgrn_scan_kernel, hb=hb, n_chunks=tb // CHUNK, n_pad=n_pad),
        grid=(bsz, HG_HEADS // hb, tlen // tb),
        in_specs=[pl.BlockSpec(blk, sect(0)), pl.BlockSpec(blk, sect(1)),
                  pl.BlockSpec(blk, sect(2)), pl.BlockSpec(blk, sect(3)),
                  pl.BlockSpec((3, hb * LANES), lambda b, h, t: (0, h)),
                  pl.BlockSpec((1, hb * LANES), lambda b, h, t: (0, h)),
                  pl.BlockSpec((hb, HG_DK, HG_DK), lambda b, h, t: (h, 0, 0)),
                  pl.BlockSpec((CHUNK + 16, CHUNK), lambda b, h, t: (0, 0)),
                  pl.BlockSpec((CHUNK, CHUNK), lambda b, h, t: (0, 0))],
        out_specs=[pl.BlockSpec(blk, lambda b, h, t: (b, t, h)),
                   pl.BlockSpec((1, hb, HG_DK, HG_DK), lambda b, h, t: (b, h, 0, 0))],
        out_shape=[jax.ShapeDtypeStruct((bsz, tlen, D_MODEL), BF16),
                   jax.ShapeDtypeStruct((bsz, HG_HEADS, HG_DK, HG_DK), F32)],
        scratch_shapes=[pltpu.VMEM((hb, HG_DK, HG_DK), F32)],
        compiler_params=pltpu.CompilerParams(
            dimension_semantics=("parallel", "parallel", "arbitrary")),
        name="hgrn_scan",
    )(proj, proj, proj, proj, lbraw, onorm.reshape(1, D_MODEL), s0, mall, lvl)


def _hgrn_step_kernel(q_ref, f_ref, v_ref, g_ref, lb_ref, on_ref, s0_ref, a_ref, s_ref, *, nb):
    lb = _lower_bound(lb_ref[...])[0]

    def split(x):
        hi = x.astype(BF16).astype(F32)
        return hi, x - hi

    r = lax.broadcasted_iota(jnp.int32, (8, HG_DK), 0)
    ones_tail = jnp.concatenate([jnp.where((r == 3) | (r == 4), 1.0, 0.0),
                                 jnp.where((r == 5) | (r == 6), 1.0, 0.0)], axis=1).astype(BF16)
    v_hi_rows = (r == 0) | (r == 2)
    for bi in range(nb):
        sig = _sigmoid(f_ref[bi])
        f = lb + (1.0 - lb) * sig
        decay = jnp.exp(jnp.log(f))
        k = 1.0 - f
        q = _silu(q_ref[bi])
        (kh, kl), (dh, dl), (qh, ql), (vh, vl) = split(k), split(decay), split(q), split(v_ref[bi])
        rows = []
        for h in range(HG_HEADS):
            row = lambda x: jnp.broadcast_to(x[h:h + 1, :], (8, HG_DK))
            lhs = jnp.where(r < 2, row(kh), jnp.where(r == 2, row(kl), jnp.where(
                r == 3, row(dh), jnp.where(r == 4, row(dl), jnp.where(
                    r == 5, row(qh), jnp.where(r == 6, row(ql), 0.0))))))
            rhs_v = jnp.where(v_hi_rows, row(vh), jnp.where(r == 1, row(vl), 0.0))
            rhs = jnp.concatenate([rhs_v.astype(BF16), ones_tail], axis=1)
            out = lax.dot_general(lhs.astype(BF16), rhs, (((0,), (0,)), ((), ())),
                                  preferred_element_type=F32)
            s_new = out[:, HG_DK:2 * HG_DK] * s0_ref[0, bi, h] + out[:, :HG_DK]
            s_ref[0, bi, h] = s_new
            rows.append(jnp.sum(out[:, 2 * HG_DK:] * s_new, axis=0, keepdims=True))
        o = jnp.concatenate(rows, axis=0)
        a_ref[bi] = _head_out(o, g_ref[bi], on_ref[...]).astype(BF16)


def _hgrn_step(q, f, v, g, lbraw, onorm, state, *, nb):
    bsz = q.shape[0]
    vec = pl.BlockSpec((nb, HG_HEADS, HG_DK), lambda b: (b, 0, 0))
    full = pl.BlockSpec((HG_HEADS, HG_DK), lambda b: (0, 0))
    lbspec = pl.BlockSpec((3, HG_HEADS, HG_DK), lambda b: (0, 0, 0))
    sspec = pl.BlockSpec((1, nb, HG_HEADS, HG_DK, HG_DK), lambda b: (0, b, 0, 0, 0))
    return pl.pallas_call(
        functools.partial(_hgrn_step_kernel, nb=nb),
        grid=(bsz // nb,),
        in_specs=[vec, vec, vec, vec, lbspec, full, sspec],
        out_specs=[vec, sspec],
        out_shape=[jax.ShapeDtypeStruct((bsz, HG_HEADS, HG_DK), BF16),
                   jax.ShapeDtypeStruct(state.shape, state.dtype)],
        compiler_params=pltpu.CompilerParams(dimension_semantics=("parallel",),
                                             vmem_limit_bytes=V7X_VMEM_LIMIT),
        name="hgrn_step",
    )(q, f, v, g, lbraw.reshape(3, HG_HEADS, HG_DK), onorm.reshape(HG_HEADS, HG_DK), state)


def _t5_bucket(dist):
    max_exact = REL_BUCKETS // 2
    d = jnp.maximum(dist, 1).astype(F32)
    large = max_exact + (jnp.log(d / max_exact) / math.log(REL_MAX_DIST / max_exact)
                         * (REL_BUCKETS - max_exact)).astype(jnp.int32)
    large = jnp.minimum(large, REL_BUCKETS - 1)
    return jnp.where(dist < max_exact, dist, large)


def _prompt_bias_kernel(w_ref, o_ref):
    band = 2 * WINDOW
    key = lax.broadcasted_iota(jnp.int32, (band, WINDOW), 0)
    for g in range(SW_GROUP):
        j, half = divmod(g, 2)
        row = jnp.broadcast_to(w_ref[g:g + 1, :], (band, 3 * WINDOW))
        toep = pltpu.roll(row, 0, 1, stride=1, stride_axis=0)[:, :WINDOW]
        dst = (slice(half * band, (half + 1) * band), slice(j * WINDOW, (j + 1) * WINDOW))
        o_ref[1, 0, dst[0], dst[1]] = toep
        o_ref[0, 0, dst[0], dst[1]] = jnp.where(key < WINDOW - N_META, NEG * LOG2E, toep)


def _prompt_bias(table):
    band = 2 * WINDOW
    i = jnp.arange(3 * WINDOW)
    dist = jnp.where(i < WINDOW, i + WINDOW, i - band)
    vals = table.astype(F32)[_t5_bucket(jnp.maximum(dist, 0))]
    vals = jnp.where(((dist >= 0) & (dist <= WINDOW))[:, None], vals, NEG) * LOG2E
    return pl.pallas_call(
        _prompt_bias_kernel,
        grid=(SW_KV,),
        in_specs=[pl.BlockSpec((SW_GROUP, 3 * WINDOW), lambda n: (n, 0))],
        out_specs=pl.BlockSpec((2, 1, 2 * band, 4 * WINDOW), lambda n: (0, n, 0, 0)),
        out_shape=jax.ShapeDtypeStruct((2, SW_KV, 2 * band, 4 * WINDOW), F32),
        compiler_params=pltpu.CompilerParams(dimension_semantics=("parallel",)),
        name="prompt_bias",
    )(vals.T)


def _expand_band(pair, u):
    lane = lax.broadcasted_iota(jnp.int32, pair.shape, 1)
    rolled = pltpu.roll(pair, SW_HD, axis=1)
    lo_src, hi_src = (pair, rolled) if u == 0 else (rolled, pair)
    top = jnp.where(lane < SW_HD, lo_src, 0.0)
    bot = jnp.where(lane >= SW_HD, hi_src, 0.0)
    return jnp.concatenate([top, bot], axis=0).astype(BF16)


def _swa_probs(s, sinks):
    band = 2 * WINDOW
    ps, ms = [], []
    for half in range(2):
        sh = s[half * band:(half + 1) * band]
        m = jnp.maximum(jnp.max(sh, axis=0, keepdims=True), sinks[half])
        ps.append(jnp.exp2(sh - m).astype(BF16))
        ms.append(m)
    return jnp.concatenate(ps, axis=0), ms


def _swa_values(pt, vxt, ms, sinks):
    ot = _nn(vxt, pt)
    outs = []
    for half in range(2):
        den = ot[2 * SW_HD + 8 * half:2 * SW_HD + 8 * half + 1] + jnp.exp2(sinks[half] - ms[half])
        outs.append(ot[SW_HD * half:SW_HD * (half + 1)] * (1.0 / den))
    return jnp.concatenate(outs, axis=0)


def _swa_prompt_kernel(sink_ref, q_ref, g_ref, k_ref, v_ref, km_ref, vm_ref, bias_ref,
                       o_ref, *, n_blocks):
    p = pl.program_id(1)
    qb = pl.program_id(2)
    n_pairs = SW_GROUP // 2
    band = 2 * WINDOW

    def sink_rows(u):
        return [jnp.concatenate(
            [jnp.full((1, WINDOW), sink_ref[(p * 2 + u) * SW_GROUP + 2 * j + half] * LOG2E, F32)
             for j in range(n_pairs)], axis=1) for half in range(2)]

    sinks = [sink_rows(u) for u in range(2)]
    orow = lax.broadcasted_iota(jnp.int32, (16, 2 * band), 0)
    ocol = lax.broadcasted_iota(jnp.int32, (16, 2 * band), 1)
    ones_rows = jnp.where((orow < 8) == (ocol < band), 1.0, 0.0).astype(BF16)
    zeros = jnp.zeros((SW_HD, band), BF16)

    def band_of(blk):
        tok = qb * n_blocks + blk
        cur = pl.ds(pl.multiple_of(tok * WINDOW, WINDOW), WINDOW)
        prev = pl.ds(pl.multiple_of(jnp.maximum(tok - 1, 0) * WINDOW, WINDOW), WINDOW)
        is_first = tok == 0
        kband = jnp.concatenate([jnp.where(is_first, km_ref[...], k_ref[0, prev, :]),
                                 k_ref[0, cur, :]], axis=0)
        vband = jnp.concatenate([jnp.where(is_first, vm_ref[...], v_ref[0, prev, :]),
                                 v_ref[0, cur, :]], axis=0)
        return (pl.ds(pl.multiple_of(blk * WINDOW, WINDOW), WINDOW), kband,
                vband.T.astype(BF16), jnp.where(is_first, 0, 1))

    def blocks(it, carry):
        bands = [band_of(it * SWA_BLOCKS_PER_ITER + i) for i in range(SWA_BLOCKS_PER_ITER)]
        units = [(bnd, u) for bnd in bands for u in range(2)]
        scores = []
        for (rows, kband, _, bias_sel), u in units:
            base = u * SW_GROUP * SW_HD
            q4 = jnp.concatenate(
                [q_ref[0, rows, base + j * LANES:base + (j + 1) * LANES] for j in range(n_pairs)],
                axis=0)
            q4 = (q4 * (SW_HD ** -0.5 * LOG2E)).astype(BF16)
            scores.append(_nt(_expand_band(kband, u), q4) + bias_ref[bias_sel, u])
        probs = [_swa_probs(s, sinks[u]) for s, (_, u) in zip(scores, units)]
        outs = []
        for (pt, ms), ((_, _, vbt, _), u) in zip(probs, units):
            vt = vbt[u * SW_HD:(u + 1) * SW_HD]
            vxt = jnp.concatenate([jnp.concatenate([vt, zeros], axis=1),
                                   jnp.concatenate([zeros, vt], axis=1), ones_rows], axis=0)
            outs.append(_swa_values(pt, vxt, ms, sinks[u]))
        for ot, ((rows, _, _, _), u) in zip(outs, units):
            base = u * SW_GROUP * SW_HD
            o = jnp.concatenate([ot[:, j * WINDOW:(j + 1) * WINDOW].T for j in range(n_pairs)],
                                axis=1)
            graw = g_ref[0, rows, base:base + SW_GROUP * SW_HD]
            o_ref[0, rows, base:base + SW_GROUP * SW_HD] = (o * _silu(graw)).astype(BF16)
        return carry

    lax.fori_loop(0, n_blocks // SWA_BLOCKS_PER_ITER, blocks, 0)


def _swa_prompt(proj, meta_k, meta_v, sinks, bias, *, tq):
    bsz, tlen, _ = proj.shape
    pw = 2 * SW_GROUP * SW_HD
    qblocks = SW_HEADS * SW_HD // pw
    kcol0 = SW_HEADS * SW_HD // LANES
    vcol0 = kcol0 + SW_KV * SW_HD // LANES
    gblk0 = (SW_HEADS * SW_HD + 2 * SW_KV * SW_HD) // pw
    return pl.pallas_call(
        functools.partial(_swa_prompt_kernel, n_blocks=tq // WINDOW),
        grid=(bsz, qblocks, tlen // tq),
        in_specs=[pl.BlockSpec(memory_space=pltpu.SMEM),
                  pl.BlockSpec((1, tq, pw), lambda b, p, t: (b, t, p)),
                  pl.BlockSpec((1, tq, pw), lambda b, p, t: (b, t, gblk0 + p)),
                  pl.BlockSpec((1, tlen, LANES), lambda b, p, t: (b, 0, kcol0 + p)),
                  pl.BlockSpec((1, tlen, LANES), lambda b, p, t: (b, 0, vcol0 + p)),
                  pl.BlockSpec((WINDOW, LANES), lambda b, p, t: (0, p)),
                  pl.BlockSpec((WINDOW, LANES), lambda b, p, t: (0, p)),
                  pl.BlockSpec((2, 2, 4 * WINDOW, 4 * WINDOW), lambda b, p, t: (0, p, 0, 0))],
        out_specs=pl.BlockSpec((1, tq, pw), lambda b, p, t: (b, t, p)),
        out_shape=jax.ShapeDtypeStruct((bsz, tlen, D_MODEL), BF16),
        compiler_params=pltpu.CompilerParams(
            dimension_semantics=("parallel", "parallel", "arbitrary"),
            vmem_limit_bytes=V7X_VMEM_LIMIT),
        name="swa_prompt",
    )(sinks, proj, proj, proj, proj, meta_k, meta_v, bias)


def _swa_sample_kernel(q_ref, g_ref, kn_ref, vn_ref, ck_ref, cv_ref, bc_ref, bn_ref, sink_ref,
                       o_ref, nk_ref, nv_ref, *, nb):
    nkv = SW_KV * SW_HD
    r = ck_ref.shape[1]
    row = lax.broadcasted_iota(jnp.int32, (SW_HEADS, nkv), 0)
    col = lax.broadcasted_iota(jnp.int32, (SW_HEADS, nkv), 1)
    own = (row // SW_GROUP) == (col // SW_HD)
    last = lax.broadcasted_iota(jnp.int32, (r, nkv), 0) == r - 1
    sink = sink_ref[...]
    for bi in range(nb):
        q = q_ref[bi] * (SW_HD ** -0.5)
        qx = jnp.where(own, jnp.concatenate([q] * SW_KV, axis=1), 0.0)
        ck, cv = ck_ref[bi], cv_ref[bi]
        kn, vn = kn_ref[bi], vn_ref[bi]

        s_c = _nt(qx.astype(BF16), ck.astype(BF16)) + bc_ref[...]
        s_n = jnp.sum(qx * kn, axis=-1, keepdims=True) + bn_ref[...]
        m = jnp.maximum(jnp.maximum(jnp.max(s_c, axis=-1, keepdims=True), s_n), sink)
        p_c = jnp.exp(s_c - m)
        p_n = jnp.exp(s_n - m)
        den = jnp.sum(p_c, axis=-1, keepdims=True) + p_n + jnp.exp(sink - m)
        o_all = _nn(p_c.astype(BF16), cv.astype(BF16))
        o_all = o_all + p_n * vn
        o_all = jnp.where(own, o_all, 0.0)
        o = o_all[:, 0:SW_HD]
        for n in range(1, SW_KV):
            o = o + o_all[:, n * SW_HD:(n + 1) * SW_HD]
        o_ref[bi] = ((o / den) * _silu(g_ref[bi])).astype(BF16)

        nk_ref[bi] = jnp.where(last, kn, pltpu.roll(ck, r - 1, axis=0))
        nv_ref[bi] = jnp.where(last, vn, pltpu.roll(cv, r - 1, axis=0))


def _swa_sample(q, g, kn, vn, ck, cv, bias_c, bias_n, sinks, *, nb):
    bsz, r, nkv = ck.shape
    head = pl.BlockSpec((nb, SW_HEADS, SW_HD), lambda b: (b, 0, 0))
    new = pl.BlockSpec((nb, 1, nkv), lambda b: (b, 0, 0))
    cache = pl.BlockSpec((nb, r, nkv), lambda b: (b, 0, 0))
    return pl.pallas_call(
        functools.partial(_swa_sample_kernel, nb=nb),
        grid=(bsz // nb,),
        in_specs=[head, head, new, new, cache, cache,
                  pl.BlockSpec((SW_HEADS, r), lambda b: (0, 0)),
                  pl.BlockSpec((SW_HEADS, 1), lambda b: (0, 0)),
                  pl.BlockSpec((SW_HEADS, 1), lambda b: (0, 0))],
        out_specs=[head, cache, cache],
        out_shape=[jax.ShapeDtypeStruct((bsz, SW_HEADS, SW_HD), BF16),
                   jax.ShapeDtypeStruct(ck.shape, ck.dtype),
                   jax.ShapeDtypeStruct(cv.shape, cv.dtype)],
        compiler_params=pltpu.CompilerParams(dimension_semantics=("parallel",)),
        name="swa_sample",
    )(q, g, kn, vn, ck, cv, bias_c, bias_n, sinks.reshape(SW_HEADS, 1))


def kernel(x_prompt, x_sample, state_hgrn, cache_k_win, cache_v_win, meta_tokens, rel_bias,
           hg_lower_bounds, hg_norm, hg_w_in, hg_onorm, hg_w_out,
           sw_norm, sw_w_in, sw_sinks, sw_w_out, final_norm):
    n_samp = x_sample.shape[0]
    samp = slice(N_META, N_META + n_samp)
    x_main = x_prompt.reshape(MAIN_ROWS, D_MODEL)
    x_small = jnp.concatenate(
        [meta_tokens.astype(F32), x_sample.reshape(n_samp, D_MODEL),
         jnp.zeros((N_SMALL - N_META - n_samp, D_MODEL), F32)], axis=0)

    h_main = _rmsnorm(x_main, hg_norm[0], BF16, 512)
    h_small = _rmsnorm(x_small, hg_norm[0], BF16, N_SMALL)
    p_main, p_small = _proj(h_main, h_small, hg_w_in[0], tm=PROJ_TM, tn=PROJ_TN)

    meta_proj = jnp.pad(p_small[:N_META], ((CHUNK - N_META, 0), (0, 0)))[None]
    zero_state = jnp.zeros((HG_HEADS, HG_DK, HG_DK), F32)
    a_meta, s_meta = _hgrn_scan(meta_proj, hg_lower_bounds, hg_onorm[0], zero_state,
                                tb=CHUNK, hb=2, n_pad=CHUNK - N_META)
    a_main, s_prompt = _hgrn_scan(p_main.reshape(BATCH, SEQ, -1), hg_lower_bounds, hg_onorm[0],
                                  s_meta[0], tb=512, hb=8)
    sect = lambda s: p_small[samp, s * HG_F:(s + 1) * HG_F].reshape(n_samp, HG_HEADS, HG_DK)
    a_samp, s_sample = _hgrn_step(sect(0), sect(1), sect(2), sect(3), hg_lower_bounds,
                                  hg_onorm[0], state_hgrn, nb=SAMPLES_PER_STEP)
    a_small = jnp.concatenate(
        [a_meta[0, CHUNK - N_META:], a_samp.reshape(n_samp, D_MODEL),
         jnp.zeros((N_SMALL - N_META - n_samp, D_MODEL), BF16)], axis=0)
    x1_main, x1_small = _proj(a_main.reshape(MAIN_ROWS, D_MODEL), a_small, hg_w_out[0],
                              x_main, x_small, tm=PROJ_RES_TM, tn=PROJ_TN)

    h_main = _rmsnorm(x1_main, sw_norm[0], BF16, 512)
    h_small = _rmsnorm(x1_small, sw_norm[0], BF16, N_SMALL)
    p_main, p_small = _proj(h_main, h_small, sw_w_in[0], tm=PROJ_TM, tn=PROJ_TN)
    nq, nkv = SW_HEADS * SW_HD, SW_KV * SW_HD
    kv_tail = p_main.reshape(BATCH, SEQ, -1)[:, -WINDOW:, nq:nq + 2 * nkv]
    k_tail = kv_tail[:, :, :nkv].reshape(1, BATCH, WINDOW, SW_KV, SW_HD)
    v_tail = kv_tail[:, :, nkv:].reshape(1, BATCH, WINDOW, SW_KV, SW_HD)
    meta_kv = jnp.pad(p_small[:N_META, nq:nq + 2 * nkv], ((WINDOW - N_META, 0), (0, 0)))
    a_main = _swa_prompt(p_main.reshape(BATCH, SEQ, -1), meta_kv[:, :nkv], meta_kv[:, nkv:],
                         sw_sinks[0], _prompt_bias(rel_bias), tq=512)

    r = cache_k_win.shape[2]
    table = rel_bias.astype(F32)
    bias_c = table[_t5_bucket(r - jnp.arange(r))].T
    bias_n = table[_t5_bucket(jnp.zeros((1,), jnp.int32))].T
    q_s = p_small[samp, :nq].reshape(n_samp, SW_HEADS, SW_HD)
    g_s = p_small[samp, nq + 2 * nkv:].reshape(n_samp, SW_HEADS, SW_HD)
    a_samp, k_samp, v_samp = _swa_sample(
        q_s, g_s, p_small[samp, nq:nq + nkv].reshape(n_samp, 1, nkv),
        p_small[samp, nq + nkv:nq + 2 * nkv].reshape(n_samp, 1, nkv),
        cache_k_win[0].reshape(n_samp, r, nkv), cache_v_win[0].reshape(n_samp, r, nkv),
        bias_c, bias_n, sw_sinks[0], nb=SAMPLES_PER_STEP)
    a_small = jnp.concatenate(
        [jnp.zeros((N_META, D_MODEL), BF16), a_samp.reshape(n_samp, D_MODEL),
         jnp.zeros((N_SMALL - N_META - n_samp, D_MODEL), BF16)], axis=0)
    x2_main, x2_small = _proj(a_main.reshape(MAIN_ROWS, D_MODEL), a_small, sw_w_out[0],
                              x1_main, x1_small, tm=PROJ_RES_TM, tn=PROJ_TN)

    y_prompt = _rmsnorm(x2_main, final_norm, F32, 512).reshape(BATCH, SEQ, D_MODEL)
    y_sample = _rmsnorm(x2_small, final_norm, F32, N_SMALL)[samp].reshape(n_samp, 1, D_MODEL)
    return (y_prompt, y_sample, s_prompt[None],
            k_tail.astype(cache_k_win.dtype), v_tail.astype(cache_v_win.dtype),
            s_sample,
            k_samp.reshape(1, n_samp, r, SW_KV, SW_HD), v_samp.reshape(1, n_samp, r, SW_KV, SW_HD))
```

```python
import functools
import math

import numpy as np
import jax
import jax.numpy as jnp
from jax import lax
from jax.experimental import pallas as pl
from jax.experimental.pallas import tpu as pltpu

D_MODEL = 4096
BATCH = 4
SEQ = 2048
DEC_BATCH = 32
N_META = 16
HG_HEADS = 32
HG_DK = 128
HG_F = HG_HEADS * HG_DK
SW_HEADS = 64
SW_KV = 8
SW_HD = 64
SW_GROUP = SW_HEADS // SW_KV
WINDOW = 128
REL_BUCKETS = 32
REL_MAX_DIST = 128
EPS = 1e-6
NEG = -1e30
LOG2E = math.log2(math.e)

LANES = 128
V7X_VMEM_LIMIT = 58 * 1024 * 1024

PROJ_TM = 1024
PROJ_TN = 1024
PROJ_RES_TM = 512
CHUNK = 128
SWA_BLOCKS_PER_ITER = 2
SAMPLES_PER_STEP = 4
N_SMALL = 64
MAIN_ROWS = BATCH * SEQ

F32 = jnp.float32
BF16 = jnp.bfloat16


def _nt(a, b):
    return lax.dot_general(a, b, (((1,), (1,)), ((), ())), preferred_element_type=F32)


def _nn(a, b):
    return jnp.dot(a, b, preferred_element_type=F32)


def _sigmoid(x):
    return 1.0 / (1.0 + jnp.exp(-x))


def _silu(x):
    h = 0.5 * x
    return h + h * jnp.tanh(h)


def _rmsnorm_kernel(x_ref, g_ref, o_ref):
    x = x_ref[...]
    ms = jnp.mean(x * x, axis=-1, keepdims=True)
    o_ref[...] = (x * lax.rsqrt(ms + EPS) * g_ref[...]).astype(o_ref.dtype)


def _rmsnorm(x, g, out_dtype, tm):
    m, d = x.shape
    return pl.pallas_call(
        _rmsnorm_kernel,
        grid=(m // tm,),
        in_specs=[pl.BlockSpec((tm, d), lambda i: (i, 0)),
                  pl.BlockSpec((1, d), lambda i: (0, 0))],
        out_specs=pl.BlockSpec((tm, d), lambda i: (i, 0)),
        out_shape=jax.ShapeDtypeStruct((m, d), out_dtype),
        compiler_params=pltpu.CompilerParams(dimension_semantics=("parallel",),
                                             vmem_limit_bytes=V7X_VMEM_LIMIT),
        name="rmsnorm",
    )(x, g.reshape(1, d))


def _proj_kernel(*refs, tn, kc, cps, has_res, has_scale, emit_stats):
    refs = list(refs)
    a_ref, as_ref, w_hbm = refs[:3]
    del refs[:3]
    if has_scale:
        gw_ref, sq_in_ref, sqs_in_ref = refs[:3]
        del refs[:3]
    if has_res:
        r_ref, rs_ref = refs[:2]
        del refs[:2]
    o_ref, os_ref = refs[:2]
    del refs[:2]
    if emit_stats:
        ob_ref, obs_ref, sq_ref, sqs_ref = refs[:4]
        del refs[:4]
    wb_ref, stage_ref, sem = refs
    j, i = pl.program_id(0), pl.program_id(1)
    nj, ni = pl.num_programs(0), pl.num_programs(1)
    kdim = w_hbm.shape[0]
    n_kc = kdim // kc
    step = j * ni + i
    cur = lax.rem(j, 2)
    par = lax.rem(step, 2)

    def aligned(x, m):
        return x * m if isinstance(x, int) else pl.multiple_of(x * m, m)

    def chunk_copy(tile, c, slot):
        return pltpu.make_async_copy(
            w_hbm.at[pl.ds(aligned(c, kc), kc), pl.ds(aligned(tile, tn), tn)],
            stage_ref.at[slot], sem.at[slot])

    def cast_chunk(slot, c, half):
        rows = pl.ds(aligned(c, kc), kc)
        w = stage_ref[slot]
        if has_scale:
            w = w * jnp.concatenate([gw_ref[rows, :]] * (tn // LANES), axis=1)
        wb_ref[half, rows, :] = w.astype(BF16)

    def next_tile(jj):
        return jnp.minimum(jj + 1, nj - 1)

    def finish(out, res_ref, sq_in, o, ob, sq):
        if has_scale:
            out = out * lax.rsqrt(jnp.sum(sq_in[...], axis=0) * (1.0 / kdim) + EPS)
        if has_res:
            out = out + res_ref[...]
        o[...] = out
        if emit_stats:
            ob[...] = out.astype(BF16)
            sq[0] = jnp.sum(out * out, axis=1, keepdims=True)

    @pl.when(step == 0)
    def _():
        chunk_copy(0, 0, 0).start()
        for c in range(n_kc):
            if c + 1 < n_kc:
                chunk_copy(0, c + 1, (c + 1) % 2).start()
            chunk_copy(0, c, c % 2).wait()
            cast_chunk(c % 2, c, 0)
        for u in range(cps):
            chunk_copy(next_tile(0), u, u).start()

    for u in range(cps):
        chunk_copy(next_tile(j), i * cps + u, par * cps + u).wait()

    @pl.when(step + 1 < nj * ni)
    def _():
        wrap = i + 1 == ni
        j2 = jnp.where(wrap, j + 1, j)
        i2 = jnp.where(wrap, 0, i + 1)
        for u in range(cps):
            chunk_copy(next_tile(j2), i2 * cps + u, (1 - par) * cps + u).start()

    @pl.when(i == 0)
    def _():
        finish(_nn(as_ref[...], wb_ref[cur]), rs_ref if has_res else None,
               sqs_in_ref if has_scale else None, os_ref,
               obs_ref if emit_stats else None, sqs_ref if emit_stats else None)

    for u in range(cps):
        cast_chunk(par * cps + u, i * cps + u, 1 - cur)
    finish(_nn(a_ref[...], wb_ref[cur]), r_ref if has_res else None,
           sq_in_ref if has_scale else None, o_ref,
           ob_ref if emit_stats else None, sq_ref if emit_stats else None)


def _proj(a, a_small, w, res=None, res_small=None, scale=None, *, tm, tn, kc=256,
          emit_stats=False):
    m, k = a.shape
    n = w.shape[1]
    ms = a_small.shape[0]
    has_res, has_scale = res is not None, scale is not None
    ni, nj, n_kc = m // tm, n // tn, k // kc
    assert m % tm == 0 and n % tn == 0 and k % kc == 0 and n_kc % ni == 0 and nj >= 2
    cps = n_kc // ni
    in_specs = [pl.BlockSpec((tm, k), lambda j, i: (i, 0)),
                pl.BlockSpec((ms, k), lambda j, i: (0, 0)),
                pl.BlockSpec(memory_space=pl.ANY)]
    args = [a, a_small, w]
    if has_scale:
        gain, sq, sq_small = scale
        in_specs += [pl.BlockSpec((k, LANES), lambda j, i: (0, 0)),
                     pl.BlockSpec((sq.shape[0], tm, 1), lambda j, i: (0, i, 0)),
                     pl.BlockSpec((sq.shape[0], ms, 1), lambda j, i: (0, 0, 0))]
        args += [jnp.broadcast_to(gain.astype(F32)[:, None], (k, LANES)), sq, sq_small]
    if has_res:
        in_specs += [pl.BlockSpec((tm, tn), lambda j, i: (i, j)),
                     pl.BlockSpec((ms, tn), lambda j, i: (0, j))]
        args += [res, res_small]
    out_specs = [pl.BlockSpec((tm, tn), lambda j, i: (i, j)),
                 pl.BlockSpec((ms, tn), lambda j, i: (0, j))]
    out_shape = [jax.ShapeDtypeStruct((m, n), F32), jax.ShapeDtypeStruct((ms, n), F32)]
    if emit_stats:
        out_specs += [pl.BlockSpec((tm, tn), lambda j, i: (i, j)),
                      pl.BlockSpec((ms, tn), lambda j, i: (0, j)),
                      pl.BlockSpec((1, tm, 1), lambda j, i: (j, i, 0)),
                      pl.BlockSpec((1, ms, 1), lambda j, i: (j, 0, 0))]
        out_shape += [jax.ShapeDtypeStruct((m, n), BF16), jax.ShapeDtypeStruct((ms, n), BF16),
                      jax.ShapeDtypeStruct((nj, m, 1), F32), jax.ShapeDtypeStruct((nj, ms, 1), F32)]
    return pl.pallas_call(
        functools.partial(_proj_kernel, tn=tn, kc=kc, cps=cps, has_res=has_res,
                          has_scale=has_scale, emit_stats=emit_stats),
        grid=(nj, ni),
        in_specs=in_specs,
        out_specs=out_specs,
        out_shape=out_shape,
        scratch_shapes=[pltpu.VMEM((2, k, tn), BF16),
                        pltpu.VMEM((2 * cps, kc, tn), F32),
                        pltpu.SemaphoreType.DMA((2 * cps,))],
        compiler_params=pltpu.CompilerParams(
            dimension_semantics=("arbitrary", "arbitrary"),
            vmem_limit_bytes=V7X_VMEM_LIMIT),
        name="proj_res" if has_res else "proj",
    )(*args)


def _hgrn_consts():
    t = np.arange(CHUNK)[:, None]
    s = np.arange(CHUNK)[None, :]
    tri = s <= t
    ends = np.array([15, 47, 79, 111, 31, 95, 63, 127] + [-1] * 8)[:, None]
    mall = np.concatenate([tri, s <= ends], axis=0).astype(np.float32)
    lvl = np.full((CHUNK, CHUNK), 3, np.int32)
    lvl[(t >= 64) & (s < 64)] = 2
    lvl[(t // 64 == s // 64) & (t % 64 >= 32) & (s % 64 < 32)] = 1
    lvl[(t // 32 == s // 32) & (s <= t)] = 0
    return jnp.asarray(mall, BF16), jnp.asarray(lvl)


def _lower_bound(lbraw):
    mx = jnp.max(lbraw, axis=0, keepdims=True)
    e = jnp.exp(lbraw - mx)
    return e[0:1, :] / jnp.sum(e, axis=0, keepdims=True)


def _head_out(o, graw, onorm):
    ms = jnp.mean(o * o, axis=-1, keepdims=True)
    return (o * lax.rsqrt(ms + EPS) * onorm) * _silu(graw)


def _hgrn_gates(qraw, fp, lb, mall, row_valid):
    sig = _sigmoid(fp)
    f = lb + (1.0 - lb) * sig
    logf = jnp.log2(f)
    k = 1.0 - f
    if row_valid is not None:
        logf = jnp.where(row_valid, logf, 0.0)
        k = jnp.where(row_valid, k, 0.0)
    q = _silu(qraw)

    hi = logf.astype(BF16)
    mid = (logf - hi.astype(F32)).astype(BF16)
    cs = _nn(mall, jnp.concatenate([hi, mid], axis=1))
    return q, k, cs[:, :LANES] + cs[:, LANES:]


def _hgrn_scores(q, k, cs):
    b = cs[0:CHUNK]
    ref = cs[CHUNK:CHUNK + 8]
    bl = ref[7:8]
    cat = lambda parts: jnp.concatenate(parts, axis=0)
    d0 = cat([b[32 * i:32 * i + 32] - ref[i:i + 1] for i in range(4)])
    d1 = [b[64 * i:64 * i + 64] - ref[4 + i:5 + i] for i in range(2)]
    d2 = b - ref[6:7]
    zero32 = jnp.zeros((32, LANES), BF16)
    a0 = _nt((q * jnp.exp2(d0)).astype(BF16), (k * jnp.exp2(-d0)).astype(BF16))
    q1 = cat([q[32:64] * jnp.exp2(d1[0][32:64]), q[96:128] * jnp.exp2(d1[1][32:64])])
    k1 = [(k[64 * i:64 * i + 32] * jnp.exp2(-d1[i][0:32])).astype(BF16) for i in range(2)]
    a1 = _nt(q1.astype(BF16), cat([k1[0], zero32, k1[1], zero32]))
    a2 = _nt((q[64:128] * jnp.exp2(d2[64:128])).astype(BF16),
             cat([(k[0:64] * jnp.exp2(-d2[0:64])).astype(BF16), zero32, zero32]))
    qe = (q * jnp.exp2(b)).astype(BF16)
    kd = (k * jnp.exp2(bl - b)).astype(BF16)
    return a0, a1, a2, qe, kd, jnp.exp2(bl)


def _hgrn_apply(a0, a1, a2, qe, kd, decay, v, st, lvl):
    is0, is1, is2 = lvl == 0, lvl == 1, lvl == 2
    a = jnp.concatenate([
        jnp.where(is0[0:32], a0[0:32], 0.0),
        jnp.where(is0[32:64], a0[32:64], jnp.where(is1[32:64], a1[0:32], 0.0)),
        jnp.where(is0[64:96], a0[64:96], jnp.where(is2[64:96], a2[0:32], 0.0)),
        jnp.where(is0[96:128], a0[96:128],
                  jnp.where(is1[96:128], a1[32:64], jnp.where(is2[96:128], a2[32:64], 0.0))),
    ], axis=0)
    vt = v.T.astype(BF16)
    lhs = jnp.concatenate([a.astype(BF16), qe], axis=1)
    rhs = jnp.concatenate([vt, st.astype(BF16)], axis=1)
    return _nt(lhs, rhs), st * decay + _nn(vt, kd)


def _hgrn_scan_kernel(q_ref, f_ref, v_ref, g_ref, lb_ref, on_ref, s0_ref, mall_ref, lvl_ref,
                      a_ref, s_ref, st_ref, *, hb, n_chunks, n_pad):
    t = pl.program_id(2)

    @pl.when(t == 0)
    def _():
        for h in range(hb):
            st_ref[h] = s0_ref[h].T

    lb_all = _lower_bound(lb_ref[...])
    row_valid = None
    if n_pad:
        row_valid = lax.broadcasted_iota(jnp.int32, (CHUNK, LANES), 0) >= n_pad
    heads = [slice(h * LANES, (h + 1) * LANES) for h in range(hb)]

    def body(c, carry):
        rows = pl.ds(pl.multiple_of(c * CHUNK, CHUNK), CHUNK)
        gates = [_hgrn_gates(q_ref[0, rows, cols], f_ref[0, rows, cols], lb_all[:, cols],
                             mall_ref[...], row_valid) for cols in heads]
        scores = [_hgrn_scores(*g) for g in gates]
        outs = [_hgrn_apply(*sc, v_ref[0, rows, cols], st_ref[h], lvl_ref[...])
                for h, (sc, cols) in enumerate(zip(scores, heads))]
        for h, ((o, st_new), cols) in enumerate(zip(outs, heads)):
            st_ref[h] = st_new
            a_ref[0, rows, cols] = _head_out(o, g_ref[0, rows, cols], on_ref[:, cols]).astype(BF16)
        return carry

    lax.fori_loop(0, n_chunks, body, 0)

    @pl.when(t == pl.num_programs(2) - 1)
    def _():
        for h in range(hb):
            s_ref[0, h] = st_ref[h].T


def _hgrn_scan(proj, lbraw, onorm, s0, *, tb, hb, n_pad=0):
    bsz, tlen, _ = proj.shape
    mall, lvl = _hgrn_consts()
    hcols = HG_F // (hb * LANES)
    sect = lambda s: (lambda b, h, t: (b, t, s * hcols + h))
    blk = (1, tb, hb * LANES)
    return pl.pallas_call(
        functools.partial(_hgrn_scan_kernel, hb=hb, n_chunks=tb // CHUNK, n_pad=n_pad),
        grid=(bsz, HG_HEADS // hb, tlen // tb),
        in_specs=[pl.BlockSpec(blk, sect(0)), pl.BlockSpec(blk, sect(1)),
                  pl.BlockSpec(blk, sect(2)), pl.BlockSpec(blk, sect(3)),
                  pl.BlockSpec((3, hb * LANES), lambda b, h, t: (0, h)),
                  pl.BlockSpec((1, hb * LANES), lambda b, h, t: (0, h)),
                  pl.BlockSpec((hb, HG_DK, HG_DK), lambda b, h, t: (h, 0, 0)),
                  pl.BlockSpec((CHUNK + 16, CHUNK), lambda b, h, t: (0, 0)),
                  pl.BlockSpec((CHUNK, CHUNK), lambda b, h, t: (0, 0))],
        out_specs=[pl.BlockSpec(blk, lambda b, h, t: (b, t, h)),
                   pl.BlockSpec((1, hb, HG_DK, HG_DK), lambda b, h, t: (b, h, 0, 0))],
        out_shape=[jax.ShapeDtypeStruct((bsz, tlen, D_MODEL), BF16),
                   jax.ShapeDtypeStruct((bsz, HG_HEADS, HG_DK, HG_DK), F32)],
        scratch_shapes=[pltpu.VMEM((hb, HG_DK, HG_DK), F32)],
        compiler_params=pltpu.CompilerParams(
            dimension_semantics=("parallel", "parallel", "arbitrary")),
        name="hgrn_scan",
    )(proj, proj, proj, proj, lbraw, onorm.reshape(1, D_MODEL), s0, mall, lvl)


def _hgrn_step_kernel(q_ref, f_ref, v_ref, g_ref, lb_ref, on_ref, s0_ref, a_ref, s_ref, *, nb):
    lb = _lower_bound(lb_ref[...])[0]

    def split(x):
        hi = x.astype(BF16).astype(F32)
        return hi, x - hi

    r = lax.broadcasted_iota(jnp.int32, (8, HG_DK), 0)
    ones_tail = jnp.concatenate([jnp.where((r == 3) | (r == 4), 1.0, 0.0),
                                 jnp.where((r == 5) | (r == 6), 1.0, 0.0)], axis=1).astype(BF16)
    v_hi_rows = (r == 0) | (r == 2)
    for bi in range(nb):
        sig = _sigmoid(f_ref[bi])
        f = lb + (1.0 - lb) * sig
        decay = jnp.exp(jnp.log(f))
        k = 1.0 - f
        q = _silu(q_ref[bi])
        (kh, kl), (dh, dl), (qh, ql), (vh, vl) = split(k), split(decay), split(q), split(v_ref[bi])
        rows = []
        for h in range(HG_HEADS):
            row = lambda x: jnp.broadcast_to(x[h:h + 1, :], (8, HG_DK))
            lhs = jnp.where(r < 2, row(kh), jnp.where(r == 2, row(kl), jnp.where(
                r == 3, row(dh), jnp.where(r == 4, row(dl), jnp.where(
                    r == 5, row(qh), jnp.where(r == 6, row(ql), 0.0))))))
            rhs_v = jnp.where(v_hi_rows, row(vh), jnp.where(r == 1, row(vl), 0.0))
            rhs = jnp.concatenate([rhs_v.astype(BF16), ones_tail], axis=1)
            out = lax.dot_general(lhs.astype(BF16), rhs, (((0,), (0,)), ((), ())),
                                  preferred_element_type=F32)
            s_new = out[:, HG_DK:2 * HG_DK] * s0_ref[0, bi, h] + out[:, :HG_DK]
            s_ref[0, bi, h] = s_new
            rows.append(jnp.sum(out[:, 2 * HG_DK:] * s_new, axis=0, keepdims=True))
        o = jnp.concatenate(rows, axis=0)
        a_ref[bi] = _head_out(o, g_ref[bi], on_ref[...]).astype(BF16)


def _hgrn_step(q, f, v, g, lbraw, onorm, state, *, nb):
    bsz = q.shape[0]
    vec = pl.BlockSpec((nb, HG_HEADS, HG_DK), lambda b: (b, 0, 0))
    full = pl.BlockSpec((HG_HEADS, HG_DK), lambda b: (0, 0))
    lbspec = pl.BlockSpec((3, HG_HEADS, HG_DK), lambda b: (0, 0, 0))
    sspec = pl.BlockSpec((1, nb, HG_HEADS, HG_DK, HG_DK), lambda b: (0, b, 0, 0, 0))
    return pl.pallas_call(
        functools.partial(_hgrn_step_kernel, nb=nb),
        grid=(bsz // nb,),
        in_specs=[vec, vec, vec, vec, lbspec, full, sspec],
        out_specs=[vec, sspec],
        out_shape=[jax.ShapeDtypeStruct((bsz, HG_HEADS, HG_DK), BF16),
                   jax.ShapeDtypeStruct(state.shape, state.dtype)],
        compiler_params=pltpu.CompilerParams(dimension_semantics=("parallel",),
                                             vmem_limit_bytes=V7X_VMEM_LIMIT),
        name="hgrn_step",
    )(q, f, v, g, lbraw.reshape(3, HG_HEADS, HG_DK), onorm.reshape(HG_HEADS, HG_DK), state)


def _t5_bucket(dist):
    max_exact = REL_BUCKETS // 2
    d = jnp.maximum(dist, 1).astype(F32)
    large = max_exact + (jnp.log(d / max_exact) / math.log(REL_MAX_DIST / max_exact)
                         * (REL_BUCKETS - max_exact)).astype(jnp.int32)
    large = jnp.minimum(large, REL_BUCKETS - 1)
    return jnp.where(dist < max_exact, dist, large)


def _prompt_bias_kernel(w_ref, o_ref):
    band = 2 * WINDOW
    key = lax.broadcasted_iota(jnp.int32, (band, WINDOW), 0)
    for g in range(SW_GROUP):
        j, half = divmod(g, 2)
        row = jnp.broadcast_to(w_ref[g:g + 1, :], (band, 3 * WINDOW))
        toep = pltpu.roll(row, 0, 1, stride=1, stride_axis=0)[:, :WINDOW]
        dst = (slice(half * band, (half + 1) * band), slice(j * WINDOW, (j + 1) * WINDOW))
        o_ref[1, 0, dst[0], dst[1]] = toep
        o_ref[0, 0, dst[0], dst[1]] = jnp.where(key < WINDOW - N_META, NEG * LOG2E, toep)


def _prompt_bias(table):
    band = 2 * WINDOW
    i = jnp.arange(3 * WINDOW)
    dist = jnp.where(i < WINDOW, i + WINDOW, i - band)
    vals = table.astype(F32)[_t5_bucket(jnp.maximum(dist, 0))]
    vals = jnp.where(((dist >= 0) & (dist <= WINDOW))[:, None], vals, NEG) * LOG2E
    return pl.pallas_call(
        _prompt_bias_kernel,
        grid=(SW_KV,),
        in_specs=[pl.BlockSpec((SW_GROUP, 3 * WINDOW), lambda n: (n, 0))],
        out_specs=pl.BlockSpec((2, 1, 2 * band, 4 * WINDOW), lambda n: (0, n, 0, 0)),
        out_shape=jax.ShapeDtypeStruct((2, SW_KV, 2 * band, 4 * WINDOW), F32),
        compiler_params=pltpu.CompilerParams(dimension_semantics=("parallel",)),
        name="prompt_bias",
    )(vals.T)


def _expand_band(pair, u):
    lane = lax.broadcasted_iota(jnp.int32, pair.shape, 1)
    rolled = pltpu.roll(pair, SW_HD, axis=1)
    lo_src, hi_src = (pair, rolled) if u == 0 else (rolled, pair)
    top = jnp.where(lane < SW_HD, lo_src, 0.0)
    bot = jnp.where(lane >= SW_HD, hi_src, 0.0)
    return jnp.concatenate([top, bot], axis=0).astype(BF16)


def _swa_probs(s, sinks):
    band = 2 * WINDOW
    ps, ms = [], []
    for half in range(2):
        sh = s[half * band:(half + 1) * band]
        m = jnp.maximum(jnp.max(sh, axis=0, keepdims=True), sinks[half])
        ps.append(jnp.exp2(sh - m).astype(BF16))
        ms.append(m)
    return jnp.concatenate(ps, axis=0), ms


def _swa_values(pt, vxt, ms, sinks):
    ot = _nn(vxt, pt)
    outs = []
    for half in range(2):
        den = ot[2 * SW_HD + 8 * half:2 * SW_HD + 8 * half + 1] + jnp.exp2(sinks[half] - ms[half])
        outs.append(ot[SW_HD * half:SW_HD * (half + 1)] * (1.0 / den))
    return jnp.concatenate(outs, axis=0)


def _swa_prompt_kernel(sink_ref, q_ref, g_ref, k_ref, v_ref, km_ref, vm_ref, bias_ref,
                       o_ref, *, n_blocks):
    p = pl.program_id(1)
    qb = pl.program_id(2)
    n_pairs = SW_GROUP // 2
    band = 2 * WINDOW

    def sink_rows(u):
        return [jnp.concatenate(
            [jnp.full((1, WINDOW), sink_ref[(p * 2 + u) * SW_GROUP + 2 * j + half] * LOG2E, F32)
             for j in range(n_pairs)], axis=1) for half in range(2)]

    sinks = [sink_rows(u) for u in range(2)]
    orow = lax.broadcasted_iota(jnp.int32, (16, 2 * band), 0)
    ocol = lax.broadcasted_iota(jnp.int32, (16, 2 * band), 1)
    ones_rows = jnp.where((orow < 8) == (ocol < band), 1.0, 0.0).astype(BF16)
    zeros = jnp.zeros((SW_HD, band), BF16)

    def band_of(blk):
        tok = qb * n_blocks + blk
        cur = pl.ds(pl.multiple_of(tok * WINDOW, WINDOW), WINDOW)
        prev = pl.ds(pl.multiple_of(jnp.maximum(tok - 1, 0) * WINDOW, WINDOW), WINDOW)
        is_first = tok == 0
        kband = jnp.concatenate([jnp.where(is_first, km_ref[...], k_ref[0, prev, :]),
                                 k_ref[0, cur, :]], axis=0)
        vband = jnp.concatenate([jnp.where(is_first, vm_ref[...], v_ref[0, prev, :]),
                                 v_ref[0, cur, :]], axis=0)
        return (pl.ds(pl.multiple_of(blk * WINDOW, WINDOW), WINDOW), kband,
                vband.T.astype(BF16), jnp.where(is_first, 0, 1))

    def blocks(it, carry):
        bands = [band_of(it * SWA_BLOCKS_PER_ITER + i) for i in range(SWA_BLOCKS_PER_ITER)]
        units = [(bnd, u) for bnd in bands for u in range(2)]
        scores = []
        for (rows, kband, _, bias_sel), u in units:
            base = u * SW_GROUP * SW_HD
            q4 = jnp.concatenate(
                [q_ref[0, rows, base + j * LANES:base + (j + 1) * LANES] for j in range(n_pairs)],
                axis=0)
            q4 = (q4 * (SW_HD ** -0.5 * LOG2E)).astype(BF16)
            scores.append(_nt(_expand_band(kband, u), q4) + bias_ref[bias_sel, u])
        probs = [_swa_probs(s, sinks[u]) for s, (_, u) in zip(scores, units)]
        outs = []
        for (pt, ms), ((_, _, vbt, _), u) in zip(probs, units):
            vt = vbt[u * SW_HD:(u + 1) * SW_HD]
            vxt = jnp.concatenate([jnp.concatenate([vt, zeros], axis=1),
                                   jnp.concatenate([zeros, vt], axis=1), ones_rows], axis=0)
            outs.append(_swa_values(pt, vxt, ms, sinks[u]))
        for ot, ((rows, _, _, _), u) in zip(outs, units):
            base = u * SW_GROUP * SW_HD
            o = jnp.concatenate([ot[:, j * WINDOW:(j + 1) * WINDOW].T for j in range(n_pairs)],
                                axis=1)
            graw = g_ref[0, rows, base:base + SW_GROUP * SW_HD]
            o_ref[0, rows, base:base + SW_GROUP * SW_HD] = (o * _silu(graw)).astype(BF16)
        return carry

    lax.fori_loop(0, n_blocks // SWA_BLOCKS_PER_ITER, blocks, 0)


def _swa_prompt(proj, meta_k, meta_v, sinks, bias, *, tq):
    bsz, tlen, _ = proj.shape
    pw = 2 * SW_GROUP * SW_HD
    qblocks = SW_HEADS * SW_HD // pw
    kcol0 = SW_HEADS * SW_HD // LANES
    vcol0 = kcol0 + SW_KV * SW_HD // LANES
    gblk0 = (SW_HEADS * SW_HD + 2 * SW_KV * SW_HD) // pw
    return pl.pallas_call(
        functools.partial(_swa_prompt_kernel, n_blocks=tq // WINDOW),
        grid=(bsz, qblocks, tlen // tq),
        in_specs=[pl.BlockSpec(memory_space=pltpu.SMEM),
                  pl.BlockSpec((1, tq, pw), lambda b, p, t: (b, t, p)),
                  pl.BlockSpec((1, tq, pw), lambda b, p, t: (b, t, gblk0 + p)),
                  pl.BlockSpec((1, tlen, LANES), lambda b, p, t: (b, 0, kcol0 + p)),
                  pl.BlockSpec((1, tlen, LANES), lambda b, p, t: (b, 0, vcol0 + p)),
                  pl.BlockSpec((WINDOW, LANES), lambda b, p, t: (0, p)),
                  pl.BlockSpec((WINDOW, LANES), lambda b, p, t: (0, p)),
                  pl.BlockSpec((2, 2, 4 * WINDOW, 4 * WINDOW), lambda b, p, t: (0, p, 0, 0))],
        out_specs=pl.BlockSpec((1, tq, pw), lambda b, p, t: (b, t, p)),
        out_shape=jax.ShapeDtypeStruct((bsz, tlen, D_MODEL), BF16),
        compiler_params=pltpu.CompilerParams(
            dimension_semantics=("parallel", "parallel", "arbitrary"),
            vmem_limit_bytes=V7X_VMEM_LIMIT),
        name="swa_prompt",
    )(sinks, proj, proj, proj, proj, meta_k, meta_v, bias)


def _swa_sample_kernel(q_ref, g_ref, kn_ref, vn_ref, ck_ref, cv_ref, bc_ref, bn_ref, sink_ref,
                       o_ref, nk_ref, nv_ref, *, nb):
    nkv = SW_KV * SW_HD
    r = ck_ref.shape[1]
    row = lax.broadcasted_iota(jnp.int32, (SW_HEADS, nkv), 0)
    col = lax.broadcasted_iota(jnp.int32, (SW_HEADS, nkv), 1)
    own = (row // SW_GROUP) == (col // SW_HD)
    last = lax.broadcasted_iota(jnp.int32, (r, nkv), 0) == r - 1
    sink = sink_ref[...]
    for bi in range(nb):
        q = q_ref[bi] * (SW_HD ** -0.5)
        qx = jnp.where(own, jnp.concatenate([q] * SW_KV, axis=1), 0.0)
        ck, cv = ck_ref[bi], cv_ref[bi]
        kn, vn = kn_ref[bi], vn_ref[bi]

        s_c = _nt(qx.astype(BF16), ck.astype(BF16)) + bc_ref[...]
        s_n = jnp.sum(qx * kn, axis=-1, keepdims=True) + bn_ref[...]
        m = jnp.maximum(jnp.maximum(jnp.max(s_c, axis=-1, keepdims=True), s_n), sink)
        p_c = jnp.exp(s_c - m)
        p_n = jnp.exp(s_n - m)
        den = jnp.sum(p_c, axis=-1, keepdims=True) + p_n + jnp.exp(sink - m)
        o_all = _nn(p_c.astype(BF16), cv.astype(BF16))
        o_all = o_all + p_n * vn
        o_all = jnp.where(own, o_all, 0.0)
        o = o_all[:, 0:SW_HD]
        for n in range(1, SW_KV):
            o = o + o_all[:, n * SW_HD:(n + 1) * SW_HD]
        o_ref[bi] = ((o / den) * _silu(g_ref[bi])).astype(BF16)

        nk_ref[bi] = jnp.where(last, kn, pltpu.roll(ck, r - 1, axis=0))
        nv_ref[bi] = jnp.where(last, vn, pltpu.roll(cv, r - 1, axis=0))


def _swa_sample(q, g, kn, vn, ck, cv, bias_c, bias_n, sinks, *, nb):
    bsz, r, nkv = ck.shape
    head = pl.BlockSpec((nb, SW_HEADS, SW_HD), lambda b: (b, 0, 0))
    new = pl.BlockSpec((nb, 1, nkv), lambda b: (b, 0, 0))
    cache = pl.BlockSpec((nb, r, nkv), lambda b: (b, 0, 0))
    return pl.pallas_call(
        functools.partial(_swa_sample_kernel, nb=nb),
        grid=(bsz // nb,),
        in_specs=[head, head, new, new, cache, cache,
                  pl.BlockSpec((SW_HEADS, r), lambda b: (0, 0)),
                  pl.BlockSpec((SW_HEADS, 1), lambda b: (0, 0)),
                  pl.BlockSpec((SW_HEADS, 1), lambda b: (0, 0))],
        out_specs=[head, cache, cache],
        out_shape=[jax.ShapeDtypeStruct((bsz, SW_HEADS, SW_HD), BF16),
                   jax.ShapeDtypeStruct(ck.shape, ck.dtype),
                   jax.ShapeDtypeStruct(cv.shape, cv.dtype)],
        compiler_params=pltpu.CompilerParams(dimension_semantics=("parallel",)),
        name="swa_sample",
    )(q, g, kn, vn, ck, cv, bias_c, bias_n, sinks.reshape(SW_HEADS, 1))


def kernel(x_prompt, x_sample, state_hgrn, cache_k_win, cache_v_win, meta_tokens, rel_bias,
           hg_lower_bounds, hg_norm, hg_w_in, hg_onorm, hg_w_out,
           sw_norm, sw_w_in, sw_sinks, sw_w_out, final_norm):
    n_samp = x_sample.shape[0]
    samp = slice(N_META, N_META + n_samp)
    x_main = x_prompt.reshape(MAIN_ROWS, D_MODEL)
    x_small = jnp.concatenate(
        [meta_tokens.astype(F32), x_sample.reshape(n_samp, D_MODEL),
         jnp.zeros((N_SMALL - N_META - n_samp, D_MODEL), F32)], axis=0)

    h_main = _rmsnorm(x_main, hg_norm[0], BF16, 512)
    h_small = _rmsnorm(x_small, hg_norm[0], BF16, N_SMALL)
    p_main, p_small = _proj(h_main, h_small, hg_w_in[0], tm=PROJ_TM, tn=PROJ_TN)

    meta_proj = jnp.pad(p_small[:N_META], ((CHUNK - N_META, 0), (0, 0)))[None]
    zero_state = jnp.zeros((HG_HEADS, HG_DK, HG_DK), F32)
    a_meta, s_meta = _hgrn_scan(meta_proj, hg_lower_bounds, hg_onorm[0], zero_state,
                                tb=CHUNK, hb=2, n_pad=CHUNK - N_META)
    a_main, s_prompt = _hgrn_scan(p_main.reshape(BATCH, SEQ, -1), hg_lower_bounds, hg_onorm[0],
                                  s_meta[0], tb=512, hb=8)
    sect = lambda s: p_small[samp, s * HG_F:(s + 1) * HG_F].reshape(n_samp, HG_HEADS, HG_DK)
    a_samp, s_sample = _hgrn_step(sect(0), sect(1), sect(2), sect(3), hg_lower_bounds,
                                  hg_onorm[0], state_hgrn, nb=SAMPLES_PER_STEP)
    a_small = jnp.concatenate(
        [a_meta[0, CHUNK - N_META:], a_samp.reshape(n_samp, D_MODEL),
         jnp.zeros((N_SMALL - N_META - n_samp, D_MODEL), BF16)], axis=0)
    x1_main, x1_small, x1b_main, x1b_small, sq_main, sq_small = _proj(
        a_main.reshape(MAIN_ROWS, D_MODEL), a_small, hg_w_out[0], x_main, x_small,
        tm=PROJ_RES_TM, tn=PROJ_TN, emit_stats=True)

    p_main, p_small = _proj(x1b_main, x1b_small, sw_w_in[0],
                            scale=(sw_norm[0], jnp.sum(sq_main, axis=0, keepdims=True),
                                   jnp.sum(sq_small, axis=0, keepdims=True)),
                            tm=PROJ_TM, tn=PROJ_TN)
    nq, nkv = SW_HEADS * SW_HD, SW_KV * SW_HD
    kv_tail = p_main.reshape(BATCH, SEQ, -1)[:, -WINDOW:, nq:nq + 2 * nkv]
    k_tail = kv_tail[:, :, :nkv].reshape(1, BATCH, WINDOW, SW_KV, SW_HD)
    v_tail = kv_tail[:, :, nkv:].reshape(1, BATCH, WINDOW, SW_KV, SW_HD)
    meta_kv = jnp.pad(p_small[:N_META, nq:nq + 2 * nkv], ((WINDOW - N_META, 0), (0, 0)))
    a_main = _swa_prompt(p_main.reshape(BATCH, SEQ, -1), meta_kv[:, :nkv], meta_kv[:, nkv:],
                         sw_sinks[0], _prompt_bias(rel_bias), tq=512)

    r = cache_k_win.shape[2]
    table = rel_bias.astype(F32)
    bias_c = table[_t5_bucket(r - jnp.arange(r))].T
    bias_n = table[_t5_bucket(jnp.zeros((1,), jnp.int32))].T
    q_s = p_small[samp, :nq].reshape(n_samp, SW_HEADS, SW_HD)
    g_s = p_small[samp, nq + 2 * nkv:].reshape(n_samp, SW_HEADS, SW_HD)
    a_samp, k_samp, v_samp = _swa_sample(
        q_s, g_s, p_small[samp, nq:nq + nkv].reshape(n_samp, 1, nkv),
        p_small[samp, nq + nkv:nq + 2 * nkv].reshape(n_samp, 1, nkv),
        cache_k_win[0].reshape(n_samp, r, nkv), cache_v_win[0].reshape(n_samp, r, nkv),
        bias_c, bias_n, sw_sinks[0], nb=SAMPLES_PER_STEP)
    a_small = jnp.concatenate(
        [jnp.zeros((N_META, D_MODEL), BF16), a_samp.reshape(n_samp, D_MODEL),
         jnp.zeros((N_SMALL - N_META - n_samp, D_MODEL), BF16)], axis=0)
    x2_main, x2_small = _proj(a_main.reshape(MAIN_ROWS, D_MODEL), a_small, sw_w_out[0],
                              x1_main, x1_small, tm=PROJ_RES_TM, tn=PROJ_TN)

    y_prompt = _rmsnorm(x2_main, final_norm, F32, 512).reshape(BATCH, SEQ, D_MODEL)
    y_sample = _rmsnorm(x2_small, final_norm, F32, N_SMALL)[samp].reshape(n_samp, 1, D_MODEL)
    return (y_prompt, y_sample, s_prompt[None],
            k_tail.astype(cache_k_win.dtype), v_tail.astype(cache_v_win.dtype),
            s_sample,
            k_samp.reshape(1, n_samp, r, SW_KV, SW_HD), v_samp.reshape(1, n_samp, r, SW_KV, SW_HD))
```

```python
import functools
import math

import numpy as np
import jax
import jax.numpy as jnp
from jax import lax
from jax.experimental import pallas as pl
from jax.experimental.pallas import tpu as pltpu

D_MODEL = 4096
BATCH = 4
SEQ = 2048
DEC_BATCH = 32
N_META = 16
HG_HEADS = 32
HG_DK = 128
HG_F = HG_HEADS * HG_DK
SW_HEADS = 64
SW_KV = 8
SW_HD = 64
SW_GROUP = SW_HEADS // SW_KV
WINDOW = 128
REL_BUCKETS = 32
REL_MAX_DIST = 128
EPS = 1e-6
NEG = -1e30
LOG2E = math.log2(math.e)

LANES = 128
V7X_VMEM_LIMIT = 58 * 1024 * 1024

PROJ_TM = 1024
PROJ_TN = 1024
PROJ_RES_TM = 512
CHUNK = 128
SWA_BLOCKS_PER_ITER = 4
SAMPLES_PER_STEP = 4
N_SMALL = 64
MAIN_ROWS = BATCH * SEQ

F32 = jnp.float32
BF16 = jnp.bfloat16


def _nt(a, b):
    return lax.dot_general(a, b, (((1,), (1,)), ((), ())), preferred_element_type=F32)


def _nn(a, b):
    return jnp.dot(a, b, preferred_element_type=F32)


def _sigmoid(x):
    return 1.0 / (1.0 + jnp.exp(-x))


def _silu(x):
    h = 0.5 * x
    return h + h * jnp.tanh(h)


def _rmsnorm_kernel(x_ref, g_ref, o_ref):
    x = x_ref[...]
    ms = jnp.mean(x * x, axis=-1, keepdims=True)
    o_ref[...] = (x * lax.rsqrt(ms + EPS) * g_ref[...]).astype(o_ref.dtype)


def _rmsnorm(x, g, out_dtype, tm):
    m, d = x.shape
    return pl.pallas_call(
        _rmsnorm_kernel,
        grid=(m // tm,),
        in_specs=[pl.BlockSpec((tm, d), lambda i: (i, 0)),
                  pl.BlockSpec((1, d), lambda i: (0, 0))],
        out_specs=pl.BlockSpec((tm, d), lambda i: (i, 0)),
        out_shape=jax.ShapeDtypeStruct((m, d), out_dtype),
        compiler_params=pltpu.CompilerParams(dimension_semantics=("parallel",),
                                             vmem_limit_bytes=V7X_VMEM_LIMIT),
        name="rmsnorm",
    )(x, g.reshape(1, d))


def _proj_kernel(*refs, tn, kc, cps, has_res, has_scale, emit_stats):
    refs = list(refs)
    a_ref, as_ref, w_hbm = refs[:3]
    del refs[:3]
    if has_scale:
        gw_ref, sq_in_ref, sqs_in_ref = refs[:3]
        del refs[:3]
    if has_res:
        r_ref, rs_ref = refs[:2]
        del refs[:2]
    o_ref, os_ref = refs[:2]
    del refs[:2]
    if emit_stats:
        ob_ref, obs_ref, sq_ref, sqs_ref = refs[:4]
        del refs[:4]
    wb_ref, stage_ref, sem = refs
    j, i = pl.program_id(0), pl.program_id(1)
    nj, ni = pl.num_programs(0), pl.num_programs(1)
    kdim = w_hbm.shape[0]
    n_kc = kdim // kc
    step = j * ni + i
    cur = lax.rem(j, 2)
    par = lax.rem(step, 2)

    def aligned(x, m):
        return x * m if isinstance(x, int) else pl.multiple_of(x * m, m)

    def chunk_copy(tile, c, slot):
        return pltpu.make_async_copy(
            w_hbm.at[pl.ds(aligned(c, kc), kc), pl.ds(aligned(tile, tn), tn)],
            stage_ref.at[slot], sem.at[slot])

    def cast_chunk(slot, c, half):
        rows = pl.ds(aligned(c, kc), kc)
        w = stage_ref[slot]
        if has_scale:
            w = w * jnp.concatenate([gw_ref[rows, :]] * (tn // LANES), axis=1)
        wb_ref[half, rows, :] = w.astype(BF16)

    def next_tile(jj):
        return jnp.minimum(jj + 1, nj - 1)

    def finish(out, res_ref, sq_in, o, ob, sq):
        if has_scale:
            out = out * lax.rsqrt(jnp.sum(sq_in[...], axis=0) * (1.0 / kdim) + EPS)
        if has_res:
            out = out + res_ref[...]
        o[...] = out
        if emit_stats:
            ob[...] = out.astype(BF16)
            sq[0] = jnp.sum(out * out, axis=1, keepdims=True)

    @pl.when(step == 0)
    def _():
        chunk_copy(0, 0, 0).start()
        for c in range(n_kc):
            if c + 1 < n_kc:
                chunk_copy(0, c + 1, (c + 1) % 2).start()
            chunk_copy(0, c, c % 2).wait()
            cast_chunk(c % 2, c, 0)
        for u in range(cps):
            chunk_copy(next_tile(0), u, u).start()

    for u in range(cps):
        chunk_copy(next_tile(j), i * cps + u, par * cps + u).wait()

    @pl.when(step + 1 < nj * ni)
    def _():
        wrap = i + 1 == ni
        j2 = jnp.where(wrap, j + 1, j)
        i2 = jnp.where(wrap, 0, i + 1)
        for u in range(cps):
            chunk_copy(next_tile(j2), i2 * cps + u, (1 - par) * cps + u).start()

    @pl.when(i == 0)
    def _():
        finish(_nn(as_ref[...], wb_ref[cur]), rs_ref if has_res else None,
               sqs_in_ref if has_scale else None, os_ref,
               obs_ref if emit_stats else None, sqs_ref if emit_stats else None)

    for u in range(cps):
        cast_chunk(par * cps + u, i * cps + u, 1 - cur)
    finish(_nn(a_ref[...], wb_ref[cur]), r_ref if has_res else None,
           sq_in_ref if has_scale else None, o_ref,
           ob_ref if emit_stats else None, sq_ref if emit_stats else None)


def _proj(a, a_small, w, res=None, res_small=None, scale=None, *, tm, tn, kc=256,
          emit_stats=False):
    m, k = a.shape
    n = w.shape[1]
    ms = a_small.shape[0]
    has_res, has_scale = res is not None, scale is not None
    ni, nj, n_kc = m // tm, n // tn, k // kc
    assert m % tm == 0 and n % tn == 0 and k % kc == 0 and n_kc % ni == 0 and nj >= 2
    cps = n_kc // ni
    in_specs = [pl.BlockSpec((tm, k), lambda j, i: (i, 0)),
                pl.BlockSpec((ms, k), lambda j, i: (0, 0)),
                pl.BlockSpec(memory_space=pl.ANY)]
    args = [a, a_small, w]
    if has_scale:
        gain, sq, sq_small = scale
        in_specs += [pl.BlockSpec((k, LANES), lambda j, i: (0, 0)),
                     pl.BlockSpec((sq.shape[0], tm, 1), lambda j, i: (0, i, 0)),
                     pl.BlockSpec((sq.shape[0], ms, 1), lambda j, i: (0, 0, 0))]
        args += [jnp.broadcast_to(gain.astype(F32)[:, None], (k, LANES)), sq, sq_small]
    if has_res:
        in_specs += [pl.BlockSpec((tm, tn), lambda j, i: (i, j)),
                     pl.BlockSpec((ms, tn), lambda j, i: (0, j))]
        args += [res, res_small]
    out_specs = [pl.BlockSpec((tm, tn), lambda j, i: (i, j)),
                 pl.BlockSpec((ms, tn), lambda j, i: (0, j))]
    out_shape = [jax.ShapeDtypeStruct((m, n), F32), jax.ShapeDtypeStruct((ms, n), F32)]
    if emit_stats:
        out_specs += [pl.BlockSpec((tm, tn), lambda j, i: (i, j)),
                      pl.BlockSpec((ms, tn), lambda j, i: (0, j)),
                      pl.BlockSpec((1, tm, 1), lambda j, i: (j, i, 0)),
                      pl.BlockSpec((1, ms, 1), lambda j, i: (j, 0, 0))]
        out_shape += [jax.ShapeDtypeStruct((m, n), BF16), jax.ShapeDtypeStruct((ms, n), BF16),
                      jax.ShapeDtypeStruct((nj, m, 1), F32), jax.ShapeDtypeStruct((nj, ms, 1), F32)]
    return pl.pallas_call(
        functools.partial(_proj_kernel, tn=tn, kc=kc, cps=cps, has_res=has_res,
                          has_scale=has_scale, emit_stats=emit_stats),
        grid=(nj, ni),
        in_specs=in_specs,
        out_specs=out_specs,
        out_shape=out_shape,
        scratch_shapes=[pltpu.VMEM((2, k, tn), BF16),
                        pltpu.VMEM((2 * cps, kc, tn), F32),
                        pltpu.SemaphoreType.DMA((2 * cps,))],
        compiler_params=pltpu.CompilerParams(
            dimension_semantics=("arbitrary", "arbitrary"),
            vmem_limit_bytes=V7X_VMEM_LIMIT),
        name="proj_res" if has_res else "proj",
    )(*args)


def _hgrn_consts():
    t = np.arange(CHUNK)[:, None]
    s = np.arange(CHUNK)[None, :]
    tri = s <= t
    ends = np.array([15, 47, 79, 111, 31, 95, 63, 127] + [-1] * 8)[:, None]
    mall = np.concatenate([tri, s <= ends], axis=0).astype(np.float32)
    lvl = np.full((CHUNK, CHUNK), 3, np.int32)
    lvl[(t >= 64) & (s < 64)] = 2
    lvl[(t // 64 == s // 64) & (t % 64 >= 32) & (s % 64 < 32)] = 1
    lvl[(t // 32 == s // 32) & (s <= t)] = 0
    return jnp.asarray(mall, BF16), jnp.asarray(lvl)


def _lower_bound(lbraw):
    mx = jnp.max(lbraw, axis=0, keepdims=True)
    e = jnp.exp(lbraw - mx)
    return e[0:1, :] / jnp.sum(e, axis=0, keepdims=True)


def _head_out(o, graw, onorm):
    ms = jnp.mean(o * o, axis=-1, keepdims=True)
    return (o * lax.rsqrt(ms + EPS) * onorm) * _silu(graw)


def _hgrn_gates(qraw, fp, lb, mall, row_valid):
    sig = _sigmoid(fp)
    f = lb + (1.0 - lb) * sig
    logf = jnp.log2(f)
    k = 1.0 - f
    if row_valid is not None:
        logf = jnp.where(row_valid, logf, 0.0)
        k = jnp.where(row_valid, k, 0.0)
    q = _silu(qraw)

    hi = logf.astype(BF16)
    mid = (logf - hi.astype(F32)).astype(BF16)
    cs = _nn(mall, jnp.concatenate([hi, mid], axis=1))
    return q, k, cs[:, :LANES] + cs[:, LANES:]


def _hgrn_scores(q, k, cs):
    b = cs[0:CHUNK]
    ref = cs[CHUNK:CHUNK + 8]
    bl = ref[7:8]
    cat = lambda parts: jnp.concatenate(parts, axis=0)
    d0 = cat([b[32 * i:32 * i + 32] - ref[i:i + 1] for i in range(4)])
    d1 = [b[64 * i:64 * i + 64] - ref[4 + i:5 + i] for i in range(2)]
    d2 = b - ref[6:7]
    zero32 = jnp.zeros((32, LANES), BF16)
    a0 = _nt((q * jnp.exp2(d0)).astype(BF16), (k * jnp.exp2(-d0)).astype(BF16))
    q1 = cat([q[32:64] * jnp.exp2(d1[0][32:64]), q[96:128] * jnp.exp2(d1[1][32:64])])
    k1 = [(k[64 * i:64 * i + 32] * jnp.exp2(-d1[i][0:32])).astype(BF16) for i in range(2)]
    a1 = _nt(q1.astype(BF16), cat([k1[0], zero32, k1[1], zero32]))
    a2 = _nt((q[64:128] * jnp.exp2(d2[64:128])).astype(BF16),
             cat([(k[0:64] * jnp.exp2(-d2[0:64])).astype(BF16), zero32, zero32]))
    qe = (q * jnp.exp2(b)).astype(BF16)
    kd = (k * jnp.exp2(bl - b)).astype(BF16)
    return a0, a1, a2, qe, kd, jnp.exp2(bl)


def _hgrn_apply(a0, a1, a2, qe, kd, decay, v, st, lvl):
    is0, is1, is2 = lvl == 0, lvl == 1, lvl == 2
    a = jnp.concatenate([
        jnp.where(is0[0:32], a0[0:32], 0.0),
        jnp.where(is0[32:64], a0[32:64], jnp.where(is1[32:64], a1[0:32], 0.0)),
        jnp.where(is0[64:96], a0[64:96], jnp.where(is2[64:96], a2[0:32], 0.0)),
        jnp.where(is0[96:128], a0[96:128],
                  jnp.where(is1[96:128], a1[32:64], jnp.where(is2[96:128], a2[32:64], 0.0))),
    ], axis=0)
    vt = v.T.astype(BF16)
    lhs = jnp.concatenate([a.astype(BF16), qe], axis=1)
    rhs = jnp.concatenate([vt, st.astype(BF16)], axis=1)
    return _nt(lhs, rhs), st * decay + _nn(vt, kd)


def _hgrn_scan_kernel(q_ref, f_ref, v_ref, g_ref, lb_ref, on_ref, s0_ref, mall_ref, lvl_ref,
                      a_ref, s_ref, st_ref, *, hb, n_chunks, n_pad):
    t = pl.program_id(2)

    @pl.when(t == 0)
    def _():
        for h in range(hb):
            st_ref[h] = s0_ref[h].T

    lb_all = _lower_bound(lb_ref[...])
    row_valid = None
    if n_pad:
        row_valid = lax.broadcasted_iota(jnp.int32, (CHUNK, LANES), 0) >= n_pad
    heads = [slice(h * LANES, (h + 1) * LANES) for h in range(hb)]

    def body(c, carry):
        rows = pl.ds(pl.multiple_of(c * CHUNK, CHUNK), CHUNK)
        gates = [_hgrn_gates(q_ref[0, rows, cols], f_ref[0, rows, cols], lb_all[:, cols],
                             mall_ref[...], row_valid) for cols in heads]
        scores = [_hgrn_scores(*g) for g in gates]
        outs = [_hgrn_apply(*sc, v_ref[0, rows, cols], st_ref[h], lvl_ref[...])
                for h, (sc, cols) in enumerate(zip(scores, heads))]
        for h, ((o, st_new), cols) in enumerate(zip(outs, heads)):
            st_ref[h] = st_new
            a_ref[0, rows, cols] = _head_out(o, g_ref[0, rows, cols], on_ref[:, cols]).astype(BF16)
        return carry

    lax.fori_loop(0, n_chunks, body, 0)

    @pl.when(t == pl.num_programs(2) - 1)
    def _():
        for h in range(hb):
            s_ref[0, h] = st_ref[h].T


def _hgrn_scan(proj, lbraw, onorm, s0, *, tb, hb, n_pad=0):
    bsz, tlen, _ = proj.shape
    mall, lvl = _hgrn_consts()
    hcols = HG_F // (hb * LANES)
    sect = lambda s: (lambda b, h, t: (b, t, s * hcols + h))
    blk = (1, tb, hb * LANES)
    return pl.pallas_call(
        functools.partial(_hgrn_scan_kernel, hb=hb, n_chunks=tb // CHUNK, n_pad=n_pad),
        grid=(bsz, HG_HEADS // hb, tlen // tb),
        in_specs=[pl.BlockSpec(blk, sect(0)), pl.BlockSpec(blk, sect(1)),
                  pl.BlockSpec(blk, sect(2)), pl.BlockSpec(blk, sect(3)),
                  pl.BlockSpec((3, hb * LANES), lambda b, h, t: (0, h)),
                  pl.BlockSpec((1, hb * LANES), lambda b, h, t: (0, h)),
                  pl.BlockSpec((hb, HG_DK, HG_DK), lambda b, h, t: (h, 0, 0)),
                  pl.BlockSpec((CHUNK + 16, CHUNK), lambda b, h, t: (0, 0)),
                  pl.BlockSpec((CHUNK, CHUNK), lambda b, h, t: (0, 0))],
        out_specs=[pl.BlockSpec(blk, lambda b, h, t: (b, t, h)),
                   pl.BlockSpec((1, hb, HG_DK, HG_DK), lambda b, h, t: (b, h, 0, 0))],
        out_shape=[jax.ShapeDtypeStruct((bsz, tlen, D_MODEL), BF16),
                   jax.ShapeDtypeStruct((bsz, HG_HEADS, HG_DK, HG_DK), F32)],
        scratch_shapes=[pltpu.VMEM((hb, HG_DK, HG_DK), F32)],
        compiler_params=pltpu.CompilerParams(
            dimension_semantics=("parallel", "parallel", "arbitrary"),
            vmem_limit_bytes=V7X_VMEM_LIMIT),
        name="hgrn_scan",
    )(proj, proj, proj, proj, lbraw, onorm.reshape(1, D_MODEL), s0, mall, lvl)


def _hgrn_step_kernel(q_ref, f_ref, v_ref, g_ref, lb_ref, on_ref, s0_ref, a_ref, s_ref, *, nb):
    lb = _lower_bound(lb_ref[...])[0]

    def split(x):
        hi = x.astype(BF16).astype(F32)
        return hi, x - hi

    r = lax.broadcasted_iota(jnp.int32, (8, HG_DK), 0)
    ones_tail = jnp.concatenate([jnp.where((r == 3) | (r == 4), 1.0, 0.0),
                                 jnp.where((r == 5) | (r == 6), 1.0, 0.0)], axis=1).astype(BF16)
    v_hi_rows = (r == 0) | (r == 2)
    for bi in range(nb):
        sig = _sigmoid(f_ref[bi])
        f = lb + (1.0 - lb) * sig
        decay = jnp.exp(jnp.log(f))
        k = 1.0 - f
        q = _silu(q_ref[bi])
        (kh, kl), (dh, dl), (qh, ql), (vh, vl) = split(k), split(decay), split(q), split(v_ref[bi])
        rows = []
        for h in range(HG_HEADS):
            row = lambda x: jnp.broadcast_to(x[h:h + 1, :], (8, HG_DK))
            lhs = jnp.where(r < 2, row(kh), jnp.where(r == 2, row(kl), jnp.where(
                r == 3, row(dh), jnp.where(r == 4, row(dl), jnp.where(
                    r == 5, row(qh), jnp.where(r == 6, row(ql), 0.0))))))
            rhs_v = jnp.where(v_hi_rows, row(vh), jnp.where(r == 1, row(vl), 0.0))
            rhs = jnp.concatenate([rhs_v.astype(BF16), ones_tail], axis=1)
            out = lax.dot_general(lhs.astype(BF16), rhs, (((0,), (0,)), ((), ())),
                                  preferred_element_type=F32)
            s_new = out[:, HG_DK:2 * HG_DK] * s0_ref[0, bi, h] + out[:, :HG_DK]
            s_ref[0, bi, h] = s_new
            rows.append(jnp.sum(out[:, 2 * HG_DK:] * s_new, axis=0, keepdims=True))
        o = jnp.concatenate(rows, axis=0)
        a_ref[bi] = _head_out(o, g_ref[bi], on_ref[...]).astype(BF16)


def _hgrn_step(q, f, v, g, lbraw, onorm, state, *, nb):
    bsz = q.shape[0]
    vec = pl.BlockSpec((nb, HG_HEADS, HG_DK), lambda b: (b, 0, 0))
    full = pl.BlockSpec((HG_HEADS, HG_DK), lambda b: (0, 0))
    lbspec = pl.BlockSpec((3, HG_HEADS, HG_DK), lambda b: (0, 0, 0))
    sspec = pl.BlockSpec((1, nb, HG_HEADS, HG_DK, HG_DK), lambda b: (0, b, 0, 0, 0))
    return pl.pallas_call(
        functools.partial(_hgrn_step_kernel, nb=nb),
        grid=(bsz // nb,),
        in_specs=[vec, vec, vec, vec, lbspec, full, sspec],
        out_specs=[vec, sspec],
        out_shape=[jax.ShapeDtypeStruct((bsz, HG_HEADS, HG_DK), BF16),
                   jax.ShapeDtypeStruct(state.shape, state.dtype)],
        compiler_params=pltpu.CompilerParams(dimension_semantics=("parallel",),
                                             vmem_limit_bytes=V7X_VMEM_LIMIT),
        name="hgrn_step",
    )(q, f, v, g, lbraw.reshape(3, HG_HEADS, HG_DK), onorm.reshape(HG_HEADS, HG_DK), state)


def _t5_bucket(dist):
    max_exact = REL_BUCKETS // 2
    d = jnp.maximum(dist, 1).astype(F32)
    large = max_exact + (jnp.log(d / max_exact) / math.log(REL_MAX_DIST / max_exact)
                         * (REL_BUCKETS - max_exact)).astype(jnp.int32)
    large = jnp.minimum(large, REL_BUCKETS - 1)
    return jnp.where(dist < max_exact, dist, large)


def _prompt_bias_kernel(w_ref, o_ref):
    band = 2 * WINDOW
    key = lax.broadcasted_iota(jnp.int32, (band, WINDOW), 0)
    for g in range(SW_GROUP):
        j, half = divmod(g, 2)
        row = jnp.broadcast_to(w_ref[g:g + 1, :], (band, 3 * WINDOW))
        toep = pltpu.roll(row, 0, 1, stride=1, stride_axis=0)[:, :WINDOW]
        dst = (slice(half * band, (half + 1) * band), slice(j * WINDOW, (j + 1) * WINDOW))
        o_ref[1, 0, dst[0], dst[1]] = toep
        o_ref[0, 0, dst[0], dst[1]] = jnp.where(key < WINDOW - N_META, NEG * LOG2E, toep)


def _prompt_bias(table):
    band = 2 * WINDOW
    i = jnp.arange(3 * WINDOW)
    dist = jnp.where(i < WINDOW, i + WINDOW, i - band)
    vals = table.astype(F32)[_t5_bucket(jnp.maximum(dist, 0))]
    vals = jnp.where(((dist >= 0) & (dist <= WINDOW))[:, None], vals, NEG) * LOG2E
    return pl.pallas_call(
        _prompt_bias_kernel,
        grid=(SW_KV,),
        in_specs=[pl.BlockSpec((SW_GROUP, 3 * WINDOW), lambda n: (n, 0))],
        out_specs=pl.BlockSpec((2, 1, 2 * band, 4 * WINDOW), lambda n: (0, n, 0, 0)),
        out_shape=jax.ShapeDtypeStruct((2, SW_KV, 2 * band, 4 * WINDOW), F32),
        compiler_params=pltpu.CompilerParams(dimension_semantics=("parallel",)),
        name="prompt_bias",
    )(vals.T)


def _expand_band(pair, u):
    lane = lax.broadcasted_iota(jnp.int32, pair.shape, 1)
    rolled = pltpu.roll(pair, SW_HD, axis=1)
    lo_src, hi_src = (pair, rolled) if u == 0 else (rolled, pair)
    top = jnp.where(lane < SW_HD, lo_src, 0.0)
    bot = jnp.where(lane >= SW_HD, hi_src, 0.0)
    return jnp.concatenate([top, bot], axis=0).astype(BF16)


def _swa_probs(s, sinks):
    band = 2 * WINDOW
    ps, ms = [], []
    for half in range(2):
        sh = s[half * band:(half + 1) * band]
        m = jnp.maximum(jnp.max(sh, axis=0, keepdims=True), sinks[half])
        ps.append(jnp.exp2(sh - m).astype(BF16))
        ms.append(m)
    return jnp.concatenate(ps, axis=0), ms


def _swa_values(pt, vxt, ms, sinks):
    ot = _nn(vxt, pt)
    outs = []
    for half in range(2):
        den = ot[2 * SW_HD + 8 * half:2 * SW_HD + 8 * half + 1] + jnp.exp2(sinks[half] - ms[half])
        outs.append(ot[SW_HD * half:SW_HD * (half + 1)] * (1.0 / den))
    return jnp.concatenate(outs, axis=0)


def _swa_prompt_kernel(sink_ref, q_ref, g_ref, k_ref, v_ref, km_ref, vm_ref, bias_ref,
                       o_ref, *, n_blocks):
    p = pl.program_id(1)
    qb = pl.program_id(2)
    n_pairs = SW_GROUP // 2
    band = 2 * WINDOW

    def sink_rows(u):
        return [jnp.concatenate(
            [jnp.full((1, WINDOW), sink_ref[(p * 2 + u) * SW_GROUP + 2 * j + half] * LOG2E, F32)
             for j in range(n_pairs)], axis=1) for half in range(2)]

    sinks = [sink_rows(u) for u in range(2)]
    orow = lax.broadcasted_iota(jnp.int32, (16, 2 * band), 0)
    ocol = lax.broadcasted_iota(jnp.int32, (16, 2 * band), 1)
    ones_rows = jnp.where((orow < 8) == (ocol < band), 1.0, 0.0).astype(BF16)
    zeros = jnp.zeros((SW_HD, band), BF16)

    def band_of(blk):
        tok = qb * n_blocks + blk
        cur = pl.ds(pl.multiple_of(tok * WINDOW, WINDOW), WINDOW)
        prev = pl.ds(pl.multiple_of(jnp.maximum(tok - 1, 0) * WINDOW, WINDOW), WINDOW)
        is_first = tok == 0
        kband = jnp.concatenate([jnp.where(is_first, km_ref[...], k_ref[0, prev, :]),
                                 k_ref[0, cur, :]], axis=0)
        vband = jnp.concatenate([jnp.where(is_first, vm_ref[...], v_ref[0, prev, :]),
                                 v_ref[0, cur, :]], axis=0)
        return (pl.ds(pl.multiple_of(blk * WINDOW, WINDOW), WINDOW), kband,
                vband.T.astype(BF16), jnp.where(is_first, 0, 1))

    def blocks(it, carry):
        bands = [band_of(it * SWA_BLOCKS_PER_ITER + i) for i in range(SWA_BLOCKS_PER_ITER)]
        units = [(bnd, u) for bnd in bands for u in range(2)]
        scores = []
        for (rows, kband, _, bias_sel), u in units:
            base = u * SW_GROUP * SW_HD
            q4 = jnp.concatenate(
                [q_ref[0, rows, base + j * LANES:base + (j + 1) * LANES] for j in range(n_pairs)],
                axis=0)
            q4 = (q4 * (SW_HD ** -0.5 * LOG2E)).astype(BF16)
            scores.append(_nt(_expand_band(kband, u), q4) + bias_ref[bias_sel, u])
        probs = [_swa_probs(s, sinks[u]) for s, (_, u) in zip(scores, units)]
        outs = []
        for (pt, ms), ((_, _, vbt, _), u) in zip(probs, units):
            vt = vbt[u * SW_HD:(u + 1) * SW_HD]
            vxt = jnp.concatenate([jnp.concatenate([vt, zeros], axis=1),
                                   jnp.concatenate([zeros, vt], axis=1), ones_rows], axis=0)
            outs.append(_swa_values(pt, vxt, ms, sinks[u]))
        for ot, ((rows, _, _, _), u) in zip(outs, units):
            base = u * SW_GROUP * SW_HD
            o = jnp.concatenate([ot[:, j * WINDOW:(j + 1) * WINDOW].T for j in range(n_pairs)],
                                axis=1)
            graw = g_ref[0, rows, base:base + SW_GROUP * SW_HD]
            o_ref[0, rows, base:base + SW_GROUP * SW_HD] = (o * _silu(graw)).astype(BF16)
        return carry

    lax.fori_loop(0, n_blocks // SWA_BLOCKS_PER_ITER, blocks, 0)


def _swa_prompt(proj, meta_k, meta_v, sinks, bias, *, tq):
    bsz, tlen, _ = proj.shape
    pw = 2 * SW_GROUP * SW_HD
    qblocks = SW_HEADS * SW_HD // pw
    kcol0 = SW_HEADS * SW_HD // LANES
    vcol0 = kcol0 + SW_KV * SW_HD // LANES
    gblk0 = (SW_HEADS * SW_HD + 2 * SW_KV * SW_HD) // pw
    return pl.pallas_call(
        functools.partial(_swa_prompt_kernel, n_blocks=tq // WINDOW),
        grid=(bsz, qblocks, tlen // tq),
        in_specs=[pl.BlockSpec(memory_space=pltpu.SMEM),
                  pl.BlockSpec((1, tq, pw), lambda b, p, t: (b, t, p)),
                  pl.BlockSpec((1, tq, pw), lambda b, p, t: (b, t, gblk0 + p)),
                  pl.BlockSpec((1, tlen, LANES), lambda b, p, t: (b, 0, kcol0 + p)),
                  pl.BlockSpec((1, tlen, LANES), lambda b, p, t: (b, 0, vcol0 + p)),
                  pl.BlockSpec((WINDOW, LANES), lambda b, p, t: (0, p)),
                  pl.BlockSpec((WINDOW, LANES), lambda b, p, t: (0, p)),
                  pl.BlockSpec((2, 2, 4 * WINDOW, 4 * WINDOW), lambda b, p, t: (0, p, 0, 0))],
        out_specs=pl.BlockSpec((1, tq, pw), lambda b, p, t: (b, t, p)),
        out_shape=jax.ShapeDtypeStruct((bsz, tlen, D_MODEL), BF16),
        compiler_params=pltpu.CompilerParams(
            dimension_semantics=("parallel", "parallel", "arbitrary"),
            vmem_limit_bytes=V7X_VMEM_LIMIT),
        name="swa_prompt",
    )(sinks, proj, proj, proj, proj, meta_k, meta_v, bias)


def _swa_sample_kernel(q_ref, g_ref, kn_ref, vn_ref, ck_ref, cv_ref, bc_ref, bn_ref, sink_ref,
                       o_ref, nk_ref, nv_ref, *, nb):
    nkv = SW_KV * SW_HD
    r = ck_ref.shape[1]
    row = lax.broadcasted_iota(jnp.int32, (SW_HEADS, nkv), 0)
    col = lax.broadcasted_iota(jnp.int32, (SW_HEADS, nkv), 1)
    own = (row // SW_GROUP) == (col // SW_HD)
    last = lax.broadcasted_iota(jnp.int32, (r, nkv), 0) == r - 1
    sink = sink_ref[...]
    for bi in range(nb):
        q = q_ref[bi] * (SW_HD ** -0.5)
        qx = jnp.where(own, jnp.concatenate([q] * SW_KV, axis=1), 0.0)
        ck, cv = ck_ref[bi], cv_ref[bi]
        kn, vn = kn_ref[bi], vn_ref[bi]

        s_c = _nt(qx.astype(BF16), ck.astype(BF16)) + bc_ref[...]
        s_n = jnp.sum(qx * kn, axis=-1, keepdims=True) + bn_ref[...]
        m = jnp.maximum(jnp.maximum(jnp.max(s_c, axis=-1, keepdims=True), s_n), sink)
        p_c = jnp.exp(s_c - m)
        p_n = jnp.exp(s_n - m)
        den = jnp.sum(p_c, axis=-1, keepdims=True) + p_n + jnp.exp(sink - m)
        o_all = _nn(p_c.astype(BF16), cv.astype(BF16))
        o_all = o_all + p_n * vn
        o_all = jnp.where(own, o_all, 0.0)
        o = o_all[:, 0:SW_HD]
        for n in range(1, SW_KV):
            o = o + o_all[:, n * SW_HD:(n + 1) * SW_HD]
        o_ref[bi] = ((o / den) * _silu(g_ref[bi])).astype(BF16)

        nk_ref[bi] = jnp.where(last, kn, pltpu.roll(ck, r - 1, axis=0))
        nv_ref[bi] = jnp.where(last, vn, pltpu.roll(cv, r - 1, axis=0))


def _swa_sample(q, g, kn, vn, ck, cv, bias_c, bias_n, sinks, *, nb):
    bsz, r, nkv = ck.shape
    head = pl.BlockSpec((nb, SW_HEADS, SW_HD), lambda b: (b, 0, 0))
    new = pl.BlockSpec((nb, 1, nkv), lambda b: (b, 0, 0))
    cache = pl.BlockSpec((nb, r, nkv), lambda b: (b, 0, 0))
    return pl.pallas_call(
        functools.partial(_swa_sample_kernel, nb=nb),
        grid=(bsz // nb,),
        in_specs=[head, head, new, new, cache, cache,
                  pl.BlockSpec((SW_HEADS, r), lambda b: (0, 0)),
                  pl.BlockSpec((SW_HEADS, 1), lambda b: (0, 0)),
                  pl.BlockSpec((SW_HEADS, 1), lambda b: (0, 0))],
        out_specs=[head, cache, cache],
        out_shape=[jax.ShapeDtypeStruct((bsz, SW_HEADS, SW_HD), BF16),
                   jax.ShapeDtypeStruct(ck.shape, ck.dtype),
                   jax.ShapeDtypeStruct(cv.shape, cv.dtype)],
        compiler_params=pltpu.CompilerParams(dimension_semantics=("parallel",)),
        name="swa_sample",
    )(q, g, kn, vn, ck, cv, bias_c, bias_n, sinks.reshape(SW_HEADS, 1))


def kernel(x_prompt, x_sample, state_hgrn, cache_k_win, cache_v_win, meta_tokens, rel_bias,
           hg_lower_bounds, hg_norm, hg_w_in, hg_onorm, hg_w_out,
           sw_norm, sw_w_in, sw_sinks, sw_w_out, final_norm):
    n_samp = x_sample.shape[0]
    samp = slice(N_META, N_META + n_samp)
    x_main = x_prompt.reshape(MAIN_ROWS, D_MODEL)
    x_small = jnp.concatenate(
        [meta_tokens.astype(F32), x_sample.reshape(n_samp, D_MODEL),
         jnp.zeros((N_SMALL - N_META - n_samp, D_MODEL), F32)], axis=0)

    h_main = _rmsnorm(x_main, hg_norm[0], BF16, 512)
    h_small = _rmsnorm(x_small, hg_norm[0], BF16, N_SMALL)
    p_main, p_small = _proj(h_main, h_small, hg_w_in[0], tm=PROJ_TM, tn=PROJ_TN)

    meta_proj = jnp.pad(p_small[:N_META], ((CHUNK - N_META, 0), (0, 0)))[None]
    zero_state = jnp.zeros((HG_HEADS, HG_DK, HG_DK), F32)
    a_meta, s_meta = _hgrn_scan(meta_proj, hg_lower_bounds, hg_onorm[0], zero_state,
                                tb=CHUNK, hb=8, n_pad=CHUNK - N_META)
    a_main, s_prompt = _hgrn_scan(p_main.reshape(BATCH, SEQ, -1), hg_lower_bounds, hg_onorm[0],
                                  s_meta[0], tb=1024, hb=8)
    sect = lambda s: p_small[samp, s * HG_F:(s + 1) * HG_F].reshape(n_samp, HG_HEADS, HG_DK)
    a_samp, s_sample = _hgrn_step(sect(0), sect(1), sect(2), sect(3), hg_lower_bounds,
                                  hg_onorm[0], state_hgrn, nb=SAMPLES_PER_STEP)
    a_small = jnp.concatenate(
        [a_meta[0, CHUNK - N_META:], a_samp.reshape(n_samp, D_MODEL),
         jnp.zeros((N_SMALL - N_META - n_samp, D_MODEL), BF16)], axis=0)
    x1_main, x1_small, x1b_main, x1b_small, sq_main, sq_small = _proj(
        a_main.reshape(MAIN_ROWS, D_MODEL), a_small, hg_w_out[0], x_main, x_small,
        tm=PROJ_RES_TM, tn=PROJ_TN, emit_stats=True)

    p_main, p_small = _proj(x1b_main, x1b_small, sw_w_in[0],
                            scale=(sw_norm[0], jnp.sum(sq_main, axis=0, keepdims=True),
                                   jnp.sum(sq_small, axis=0, keepdims=True)),
                            tm=PROJ_TM, tn=PROJ_TN)
    nq, nkv = SW_HEADS * SW_HD, SW_KV * SW_HD
    kv_tail = p_main.reshape(BATCH, SEQ, -1)[:, -WINDOW:, nq:nq + 2 * nkv]
    k_tail = kv_tail[:, :, :nkv].reshape(1, BATCH, WINDOW, SW_KV, SW_HD)
    v_tail = kv_tail[:, :, nkv:].reshape(1, BATCH, WINDOW, SW_KV, SW_HD)
    meta_kv = jnp.pad(p_small[:N_META, nq:nq + 2 * nkv], ((WINDOW - N_META, 0), (0, 0)))
    a_main = _swa_prompt(p_main.reshape(BATCH, SEQ, -1), meta_kv[:, :nkv], meta_kv[:, nkv:],
                         sw_sinks[0], _prompt_bias(rel_bias), tq=512)

    r = cache_k_win.shape[2]
    table = rel_bias.astype(F32)
    bias_c = table[_t5_bucket(r - jnp.arange(r))].T
    bias_n = table[_t5_bucket(jnp.zeros((1,), jnp.int32))].T
    q_s = p_small[samp, :nq].reshape(n_samp, SW_HEADS, SW_HD)
    g_s = p_small[samp, nq + 2 * nkv:].reshape(n_samp, SW_HEADS, SW_HD)
    a_samp, k_samp, v_samp = _swa_sample(
        q_s, g_s, p_small[samp, nq:nq + nkv].reshape(n_samp, 1, nkv),
        p_small[samp, nq + nkv:nq + 2 * nkv].reshape(n_samp, 1, nkv),
        cache_k_win[0].reshape(n_samp, r, nkv), cache_v_win[0].reshape(n_samp, r, nkv),
        bias_c, bias_n, sw_sinks[0], nb=SAMPLES_PER_STEP)
    a_small = jnp.concatenate(
        [jnp.zeros((N_META, D_MODEL), BF16), a_samp.reshape(n_samp, D_MODEL),
         jnp.zeros((N_SMALL - N_META - n_samp, D_MODEL), BF16)], axis=0)
    x2_main, x2_small = _proj(a_main.reshape(MAIN_ROWS, D_MODEL), a_small, sw_w_out[0],
                              x1_main, x1_small, tm=PROJ_RES_TM, tn=PROJ_TN)

    y_prompt = _rmsnorm(x2_main, final_norm, F32, 512).reshape(BATCH, SEQ, D_MODEL)
    y_sample = _rmsnorm(x2_small, final_norm, F32, N_SMALL)[samp].reshape(n_samp, 1, D_MODEL)
    return (y_prompt, y_sample, s_prompt[None],
            k_tail.astype(cache_k_win.dtype), v_tail.astype(cache_v_win.dtype),
            s_sample,
            k_samp.reshape(1, n_samp, r, SW_KV, SW_HD), v_samp.reshape(1, n_samp, r, SW_KV, SW_HD))
```

```python
import functools
import math

import numpy as np
import jax
import jax.numpy as jnp
from jax import lax
from jax.experimental import pallas as pl
from jax.experimental.pallas import tpu as pltpu

D_MODEL = 4096
BATCH = 4
SEQ = 2048
DEC_BATCH = 32
N_META = 16
HG_HEADS = 32
HG_DK = 128
HG_F = HG_HEADS * HG_DK
SW_HEADS = 64
SW_KV = 8
SW_HD = 64
SW_GROUP = SW_HEADS // SW_KV
WINDOW = 128
REL_BUCKETS = 32
REL_MAX_DIST = 128
EPS = 1e-6
NEG = -1e30
LOG2E = math.log2(math.e)

LANES = 128
V7X_VMEM_LIMIT = 58 * 1024 * 1024

PROJ_TM = 1024
PROJ_TN = 1024
PROJ_RES_TM = 512
CHUNK = 128
SWA_BLOCKS_PER_ITER = 4
SAMPLES_PER_STEP = 4
N_SMALL = 64
MAIN_ROWS = BATCH * SEQ

F32 = jnp.float32
BF16 = jnp.bfloat16


def _nt(a, b):
    return lax.dot_general(a, b, (((1,), (1,)), ((), ())), preferred_element_type=F32)


def _nn(a, b):
    return jnp.dot(a, b, preferred_element_type=F32)


def _sigmoid(x):
    return 1.0 / (1.0 + jnp.exp(-x))


def _silu(x):
    h = 0.5 * x
    return h + h * jnp.tanh(h)


def _rmsnorm_kernel(x_ref, g_ref, o_ref):
    x = x_ref[...]
    ms = jnp.mean(x * x, axis=-1, keepdims=True)
    o_ref[...] = (x * lax.rsqrt(ms + EPS) * g_ref[...]).astype(o_ref.dtype)


def _rmsnorm(x, g, out_dtype, tm):
    m, d = x.shape
    return pl.pallas_call(
        _rmsnorm_kernel,
        grid=(m // tm,),
        in_specs=[pl.BlockSpec((tm, d), lambda i: (i, 0)),
                  pl.BlockSpec((1, d), lambda i: (0, 0))],
        out_specs=pl.BlockSpec((tm, d), lambda i: (i, 0)),
        out_shape=jax.ShapeDtypeStruct((m, d), out_dtype),
        compiler_params=pltpu.CompilerParams(dimension_semantics=("parallel",),
                                             vmem_limit_bytes=V7X_VMEM_LIMIT),
        name="rmsnorm",
    )(x, g.reshape(1, d))


def _proj_kernel(*refs, tn, kc, cps, has_res, has_scale, emit_stats):
    refs = list(refs)
    a_ref, as_ref, w_hbm = refs[:3]
    del refs[:3]
    if has_scale:
        gw_ref, sq_in_ref, sqs_in_ref = refs[:3]
        del refs[:3]
    if has_res:
        r_ref, rs_ref = refs[:2]
        del refs[:2]
    o_ref, os_ref = refs[:2]
    del refs[:2]
    if emit_stats:
        ob_ref, obs_ref, sq_ref, sqs_ref = refs[:4]
        del refs[:4]
    wb_ref, stage_ref, sem = refs
    j, i = pl.program_id(0), pl.program_id(1)
    nj, ni = pl.num_programs(0), pl.num_programs(1)
    kdim = w_hbm.shape[0]
    n_kc = kdim // kc
    step = j * ni + i
    cur = lax.rem(j, 2)
    par = lax.rem(step, 2)

    def aligned(x, m):
        return x * m if isinstance(x, int) else pl.multiple_of(x * m, m)

    def chunk_copy(tile, c, slot):
        return pltpu.make_async_copy(
            w_hbm.at[pl.ds(aligned(c, kc), kc), pl.ds(aligned(tile, tn), tn)],
            stage_ref.at[slot], sem.at[slot])

    def cast_chunk(slot, c, half):
        rows = pl.ds(aligned(c, kc), kc)
        w = stage_ref[slot]
        if has_scale:
            w = w * jnp.concatenate([gw_ref[rows, :]] * (tn // LANES), axis=1)
        wb_ref[half, rows, :] = w.astype(BF16)

    def next_tile(jj):
        return jnp.minimum(jj + 1, nj - 1)

    def finish(out, res_ref, sq_in, o, ob, sq):
        if has_scale:
            out = out * lax.rsqrt(jnp.sum(sq_in[...], axis=0) * (1.0 / kdim) + EPS)
        if has_res:
            out = out + res_ref[...]
        o[...] = out
        if emit_stats:
            ob[...] = out.astype(BF16)
            sq[0] = jnp.sum(out * out, axis=1, keepdims=True)

    @pl.when(step == 0)
    def _():
        chunk_copy(0, 0, 0).start()
        for c in range(n_kc):
            if c + 1 < n_kc:
                chunk_copy(0, c + 1, (c + 1) % 2).start()
            chunk_copy(0, c, c % 2).wait()
            cast_chunk(c % 2, c, 0)
        for u in range(cps):
            chunk_copy(next_tile(0), u, u).start()

    for u in range(cps):
        chunk_copy(next_tile(j), i * cps + u, par * cps + u).wait()

    @pl.when(step + 1 < nj * ni)
    def _():
        wrap = i + 1 == ni
        j2 = jnp.where(wrap, j + 1, j)
        i2 = jnp.where(wrap, 0, i + 1)
        for u in range(cps):
            chunk_copy(next_tile(j2), i2 * cps + u, (1 - par) * cps + u).start()

    @pl.when(i == 0)
    def _():
        finish(_nn(as_ref[...], wb_ref[cur]), rs_ref if has_res else None,
               sqs_in_ref if has_scale else None, os_ref,
               obs_ref if emit_stats else None, sqs_ref if emit_stats else None)

    for u in range(cps):
        cast_chunk(par * cps + u, i * cps + u, 1 - cur)
    finish(_nn(a_ref[...], wb_ref[cur]), r_ref if has_res else None,
           sq_in_ref if has_scale else None, o_ref,
           ob_ref if emit_stats else None, sq_ref if emit_stats else None)


def _proj(a, a_small, w, res=None, res_small=None, scale=None, *, tm, tn, kc=256,
          emit_stats=False):
    m, k = a.shape
    n = w.shape[1]
    ms = a_small.shape[0]
    has_res, has_scale = res is not None, scale is not None
    ni, nj, n_kc = m // tm, n // tn, k // kc
    assert m % tm == 0 and n % tn == 0 and k % kc == 0 and n_kc % ni == 0 and nj >= 2
    cps = n_kc // ni
    in_specs = [pl.BlockSpec((tm, k), lambda j, i: (i, 0)),
                pl.BlockSpec((ms, k), lambda j, i: (0, 0)),
                pl.BlockSpec(memory_space=pl.ANY)]
    args = [a, a_small, w]
    if has_scale:
        gain, sq, sq_small = scale
        in_specs += [pl.BlockSpec((k, LANES), lambda j, i: (0, 0)),
                     pl.BlockSpec((sq.shape[0], tm, 1), lambda j, i: (0, i, 0)),
                     pl.BlockSpec((sq.shape[0], ms, 1), lambda j, i: (0, 0, 0))]
        args += [jnp.broadcast_to(gain.astype(F32)[:, None], (k, LANES)), sq, sq_small]
    if has_res:
        in_specs += [pl.BlockSpec((tm, tn), lambda j, i: (i, j)),
                     pl.BlockSpec((ms, tn), lambda j, i: (0, j))]
        args += [res, res_small]
    out_specs = [pl.BlockSpec((tm, tn), lambda j, i: (i, j)),
                 pl.BlockSpec((ms, tn), lambda j, i: (0, j))]
    out_shape = [jax.ShapeDtypeStruct((m, n), F32), jax.ShapeDtypeStruct((ms, n), F32)]
    if emit_stats:
        out_specs += [pl.BlockSpec((tm, tn), lambda j, i: (i, j)),
                      pl.BlockSpec((ms, tn), lambda j, i: (0, j)),
                      pl.BlockSpec((1, tm, 1), lambda j, i: (j, i, 0)),
                      pl.BlockSpec((1, ms, 1), lambda j, i: (j, 0, 0))]
        out_shape += [jax.ShapeDtypeStruct((m, n), BF16), jax.ShapeDtypeStruct((ms, n), BF16),
                      jax.ShapeDtypeStruct((nj, m, 1), F32), jax.ShapeDtypeStruct((nj, ms, 1), F32)]
    return pl.pallas_call(
        functools.partial(_proj_kernel, tn=tn, kc=kc, cps=cps, has_res=has_res,
                          has_scale=has_scale, emit_stats=emit_stats),
        grid=(nj, ni),
        in_specs=in_specs,
        out_specs=out_specs,
        out_shape=out_shape,
        scratch_shapes=[pltpu.VMEM((2, k, tn), BF16),
                        pltpu.VMEM((2 * cps, kc, tn), F32),
                        pltpu.SemaphoreType.DMA((2 * cps,))],
        compiler_params=pltpu.CompilerParams(
            dimension_semantics=("arbitrary", "arbitrary"),
            vmem_limit_bytes=V7X_VMEM_LIMIT),
        name="proj_res" if has_res else "proj",
    )(*args)


def _hgrn_consts():
    t = np.arange(CHUNK)[:, None]
    s = np.arange(CHUNK)[None, :]
    tri = s <= t
    ends = np.array([15, 47, 79, 111, 31, 95, 63, 127] + [-1] * 8)[:, None]
    mall = np.concatenate([tri, s <= ends], axis=0).astype(np.float32)
    lvl = np.full((CHUNK, CHUNK), 3, np.int32)
    lvl[(t >= 64) & (s < 64)] = 2
    lvl[(t // 64 == s // 64) & (t % 64 >= 32) & (s % 64 < 32)] = 1
    lvl[(t // 32 == s // 32) & (s <= t)] = 0
    return jnp.asarray(mall, BF16), jnp.asarray(lvl)


def _lower_bound(lbraw):
    mx = jnp.max(lbraw, axis=0, keepdims=True)
    e = jnp.exp(lbraw - mx)
    return e[0:1, :] / jnp.sum(e, axis=0, keepdims=True)


def _head_out(o, graw, onorm):
    ms = jnp.mean(o * o, axis=-1, keepdims=True)
    return (o * lax.rsqrt(ms + EPS) * onorm) * _silu(graw)


def _hgrn_gates(qraw, fp, lb, mall, row_valid):
    sig = _sigmoid(fp)
    f = lb + (1.0 - lb) * sig
    logf = jnp.log2(f)
    k = 1.0 - f
    if row_valid is not None:
        logf = jnp.where(row_valid, logf, 0.0)
        k = jnp.where(row_valid, k, 0.0)
    q = _silu(qraw)

    hi = logf.astype(BF16)
    mid = (logf - hi.astype(F32)).astype(BF16)
    cs = _nn(mall, jnp.concatenate([hi, mid], axis=1))
    return q, k, cs[:, :LANES] + cs[:, LANES:]


def _hgrn_scores(q, k, cs):
    b = cs[0:CHUNK]
    ref = cs[CHUNK:CHUNK + 8]
    bl = ref[7:8]
    cat = lambda parts: jnp.concatenate(parts, axis=0)
    d0 = cat([b[32 * i:32 * i + 32] - ref[i:i + 1] for i in range(4)])
    d1 = [b[64 * i:64 * i + 64] - ref[4 + i:5 + i] for i in range(2)]
    d2 = b - ref[6:7]
    zero32 = jnp.zeros((32, LANES), BF16)
    a0 = _nt((q * jnp.exp2(d0)).astype(BF16), (k * jnp.exp2(-d0)).astype(BF16))
    q1 = cat([q[32:64] * jnp.exp2(d1[0][32:64]), q[96:128] * jnp.exp2(d1[1][32:64])])
    k1 = [(k[64 * i:64 * i + 32] * jnp.exp2(-d1[i][0:32])).astype(BF16) for i in range(2)]
    a1 = _nt(q1.astype(BF16), cat([k1[0], zero32, k1[1], zero32]))
    a2 = _nt((q[64:128] * jnp.exp2(d2[64:128])).astype(BF16),
             cat([(k[0:64] * jnp.exp2(-d2[0:64])).astype(BF16), zero32, zero32]))
    qe = (q * jnp.exp2(b)).astype(BF16)
    kd = (k * jnp.exp2(bl - b)).astype(BF16)
    return a0, a1, a2, qe, kd, jnp.exp2(bl)


def _hgrn_apply(a0, a1, a2, qe, kd, decay, v, st, lvl):
    is0, is1, is2 = lvl == 0, lvl == 1, lvl == 2
    a = jnp.concatenate([
        jnp.where(is0[0:32], a0[0:32], 0.0),
        jnp.where(is0[32:64], a0[32:64], jnp.where(is1[32:64], a1[0:32], 0.0)),
        jnp.where(is0[64:96], a0[64:96], jnp.where(is2[64:96], a2[0:32], 0.0)),
        jnp.where(is0[96:128], a0[96:128],
                  jnp.where(is1[96:128], a1[32:64], jnp.where(is2[96:128], a2[32:64], 0.0))),
    ], axis=0)
    vt = v.T.astype(BF16)
    lhs = jnp.concatenate([a.astype(BF16), qe], axis=1)
    rhs = jnp.concatenate([vt, st.astype(BF16)], axis=1)
    return _nt(lhs, rhs), st * decay + _nn(vt, kd)


def _hgrn_scan_kernel(q_ref, f_ref, v_ref, g_ref, lb_ref, on_ref, s0_ref, mall_ref, lvl_ref,
                      a_ref, s_ref, st_ref, *, hb, n_chunks, n_pad):
    t = pl.program_id(2)

    @pl.when(t == 0)
    def _():
        for h in range(hb):
            st_ref[h] = s0_ref[h].T

    lb_all = _lower_bound(lb_ref[...])
    row_valid = None
    if n_pad:
        row_valid = lax.broadcasted_iota(jnp.int32, (CHUNK, LANES), 0) >= n_pad
    heads = [slice(h * LANES, (h + 1) * LANES) for h in range(hb)]

    def body(c, carry):
        rows = pl.ds(pl.multiple_of(c * CHUNK, CHUNK), CHUNK)
        gates = [_hgrn_gates(q_ref[0, rows, cols], f_ref[0, rows, cols], lb_all[:, cols],
                             mall_ref[...], row_valid) for cols in heads]
        scores = [_hgrn_scores(*g) for g in gates]
        outs = [_hgrn_apply(*sc, v_ref[0, rows, cols], st_ref[h], lvl_ref[...])
                for h, (sc, cols) in enumerate(zip(scores, heads))]
        for h, ((o, st_new), cols) in enumerate(zip(outs, heads)):
            st_ref[h] = st_new
            a_ref[0, rows, cols] = _head_out(o, g_ref[0, rows, cols], on_ref[:, cols]).astype(BF16)
        return carry

    lax.fori_loop(0, n_chunks, body, 0)

    @pl.when(t == pl.num_programs(2) - 1)
    def _():
        for h in range(hb):
            s_ref[0, h] = st_ref[h].T


def _hgrn_scan(proj, lbraw, onorm, s0, *, tb, hb, n_pad=0):
    bsz, tlen, _ = proj.shape
    mall, lvl = _hgrn_consts()
    hcols = HG_F // (hb * LANES)
    sect = lambda s: (lambda b, h, t: (b, t, s * hcols + h))
    blk = (1, tb, hb * LANES)
    return pl.pallas_call(
        functools.partial(_hgrn_scan_kernel, hb=hb, n_chunks=tb // CHUNK, n_pad=n_pad),
        grid=(bsz, HG_HEADS // hb, tlen // tb),
        in_specs=[pl.BlockSpec(blk, sect(0)), pl.BlockSpec(blk, sect(1)),
                  pl.BlockSpec(blk, sect(2)), pl.BlockSpec(blk, sect(3)),
                  pl.BlockSpec((3, hb * LANES), lambda b, h, t: (0, h)),
                  pl.BlockSpec((1, hb * LANES), lambda b, h, t: (0, h)),
                  pl.BlockSpec((hb, HG_DK, HG_DK), lambda b, h, t: (h, 0, 0)),
                  pl.BlockSpec((CHUNK + 16, CHUNK), lambda b, h, t: (0, 0)),
                  pl.BlockSpec((CHUNK, CHUNK), lambda b, h, t: (0, 0))],
        out_specs=[pl.BlockSpec(blk, lambda b, h, t: (b, t, h)),
                   pl.BlockSpec((1, hb, HG_DK, HG_DK), lambda b, h, t: (b, h, 0, 0))],
        out_shape=[jax.ShapeDtypeStruct((bsz, tlen, D_MODEL), BF16),
                   jax.ShapeDtypeStruct((bsz, HG_HEADS, HG_DK, HG_DK), F32)],
        scratch_shapes=[pltpu.VMEM((hb, HG_DK, HG_DK), F32)],
        compiler_params=pltpu.CompilerParams(
            dimension_semantics=("parallel", "parallel", "arbitrary"),
            vmem_limit_bytes=V7X_VMEM_LIMIT),
        name="hgrn_scan",
    )(proj, proj, proj, proj, lbraw, onorm.reshape(1, D_MODEL), s0, mall, lvl)


def _hgrn_step_kernel(q_ref, f_ref, v_ref, g_ref, lb_ref, on_ref, s0_ref, a_ref, s_ref, *, nb):
    lb = _lower_bound(lb_ref[...])[0]

    def split(x):
        hi = x.astype(BF16).astype(F32)
        return hi, x - hi

    r = lax.broadcasted_iota(jnp.int32, (8, HG_DK), 0)
    ones_tail = jnp.concatenate([jnp.where((r == 3) | (r == 4), 1.0, 0.0),
                                 jnp.where((r == 5) | (r == 6), 1.0, 0.0)], axis=1).astype(BF16)
    v_hi_rows = (r == 0) | (r == 2)
    for bi in range(nb):
        sig = _sigmoid(f_ref[bi])
        f = lb + (1.0 - lb) * sig
        decay = jnp.exp(jnp.log(f))
        k = 1.0 - f
        q = _silu(q_ref[bi])
        (kh, kl), (dh, dl), (qh, ql), (vh, vl) = split(k), split(decay), split(q), split(v_ref[bi])
        rows = []
        for h in range(HG_HEADS):
            row = lambda x: jnp.broadcast_to(x[h:h + 1, :], (8, HG_DK))
            lhs = jnp.where(r < 2, row(kh), jnp.where(r == 2, row(kl), jnp.where(
                r == 3, row(dh), jnp.where(r == 4, row(dl), jnp.where(
                    r == 5, row(qh), jnp.where(r == 6, row(ql), 0.0))))))
            rhs_v = jnp.where(v_hi_rows, row(vh), jnp.where(r == 1, row(vl), 0.0))
            rhs = jnp.concatenate([rhs_v.astype(BF16), ones_tail], axis=1)
            out = lax.dot_general(lhs.astype(BF16), rhs, (((0,), (0,)), ((), ())),
                                  preferred_element_type=F32)
            s_new = out[:, HG_DK:2 * HG_DK] * s0_ref[0, bi, h] + out[:, :HG_DK]
            s_ref[0, bi, h] = s_new
            rows.append(jnp.sum(out[:, 2 * HG_DK:] * s_new, axis=0, keepdims=True))
        o = jnp.concatenate(rows, axis=0)
        a_ref[bi] = _head_out(o, g_ref[bi], on_ref[...]).astype(BF16)


def _hgrn_step(q, f, v, g, lbraw, onorm, state, *, nb):
    bsz = q.shape[0]
    vec = pl.BlockSpec((nb, HG_HEADS, HG_DK), lambda b: (b, 0, 0))
    full = pl.BlockSpec((HG_HEADS, HG_DK), lambda b: (0, 0))
    lbspec = pl.BlockSpec((3, HG_HEADS, HG_DK), lambda b: (0, 0, 0))
    sspec = pl.BlockSpec((1, nb, HG_HEADS, HG_DK, HG_DK), lambda b: (0, b, 0, 0, 0))
    return pl.pallas_call(
        functools.partial(_hgrn_step_kernel, nb=nb),
        grid=(bsz // nb,),
        in_specs=[vec, vec, vec, vec, lbspec, full, sspec],
        out_specs=[vec, sspec],
        out_shape=[jax.ShapeDtypeStruct((bsz, HG_HEADS, HG_DK), BF16),
                   jax.ShapeDtypeStruct(state.shape, state.dtype)],
        compiler_params=pltpu.CompilerParams(dimension_semantics=("parallel",),
                                             vmem_limit_bytes=V7X_VMEM_LIMIT),
        name="hgrn_step",
    )(q, f, v, g, lbraw.reshape(3, HG_HEADS, HG_DK), onorm.reshape(HG_HEADS, HG_DK), state)


def _t5_bucket(dist):
    max_exact = REL_BUCKETS // 2
    d = jnp.maximum(dist, 1).astype(F32)
    large = max_exact + (jnp.log(d / max_exact) / math.log(REL_MAX_DIST / max_exact)
                         * (REL_BUCKETS - max_exact)).astype(jnp.int32)
    large = jnp.minimum(large, REL_BUCKETS - 1)
    return jnp.where(dist < max_exact, dist, large)


def _prompt_bias_kernel(w_ref, o_ref):
    band = 2 * WINDOW
    key = lax.broadcasted_iota(jnp.int32, (band, WINDOW), 0)
    for g in range(SW_GROUP):
        j, half = divmod(g, 2)
        row = jnp.broadcast_to(w_ref[g:g + 1, :], (band, 3 * WINDOW))
        toep = pltpu.roll(row, 0, 1, stride=1, stride_axis=0)[:, :WINDOW]
        dst = (slice(half * band, (half + 1) * band), slice(j * WINDOW, (j + 1) * WINDOW))
        o_ref[1, 0, dst[0], dst[1]] = toep
        o_ref[0, 0, dst[0], dst[1]] = jnp.where(key < WINDOW - N_META, NEG * LOG2E, toep)


def _prompt_bias(table):
    band = 2 * WINDOW
    i = jnp.arange(3 * WINDOW)
    dist = jnp.where(i < WINDOW, i + WINDOW, i - band)
    vals = table.astype(F32)[_t5_bucket(jnp.maximum(dist, 0))]
    vals = jnp.where(((dist >= 0) & (dist <= WINDOW))[:, None], vals, NEG) * LOG2E
    return pl.pallas_call(
        _prompt_bias_kernel,
        grid=(SW_KV,),
        in_specs=[pl.BlockSpec((SW_GROUP, 3 * WINDOW), lambda n: (n, 0))],
        out_specs=pl.BlockSpec((2, 1, 2 * band, 4 * WINDOW), lambda n: (0, n, 0, 0)),
        out_shape=jax.ShapeDtypeStruct((2, SW_KV, 2 * band, 4 * WINDOW), F32),
        compiler_params=pltpu.CompilerParams(dimension_semantics=("parallel",)),
        name="prompt_bias",
    )(vals.T)


def _expand_band(pair, u):
    lane = lax.broadcasted_iota(jnp.int32, pair.shape, 1)
    rolled = pltpu.roll(pair, SW_HD, axis=1)
    lo_src, hi_src = (pair, rolled) if u == 0 else (rolled, pair)
    top = jnp.where(lane < SW_HD, lo_src, 0.0)
    bot = jnp.where(lane >= SW_HD, hi_src, 0.0)
    return jnp.concatenate([top, bot], axis=0).astype(BF16)


def _swa_probs(s, sinks):
    band = 2 * WINDOW
    ps, ms = [], []
    for half in range(2):
        sh = s[half * band:(half + 1) * band]
        m = jnp.maximum(jnp.max(sh, axis=0, keepdims=True), sinks[half])
        ps.append(jnp.exp2(sh - m).astype(BF16))
        ms.append(m)
    return jnp.concatenate(ps, axis=0), ms


def _swa_values(pt, vxt, ms, sinks):
    ot = _nn(vxt, pt)
    outs = []
    for half in range(2):
        den = ot[2 * SW_HD + 8 * half:2 * SW_HD + 8 * half + 1] + jnp.exp2(sinks[half] - ms[half])
        outs.append(ot[SW_HD * half:SW_HD * (half + 1)] * (1.0 / den))
    return jnp.concatenate(outs, axis=0)


def _swa_prompt_kernel(sink_ref, q_ref, g_ref, k_ref, v_ref, km_ref, vm_ref, bias_ref,
                       o_ref, *, n_blocks):
    p = pl.program_id(0)
    qb = pl.program_id(2)
    n_pairs = SW_GROUP // 2
    band = 2 * WINDOW

    def sink_rows(u):
        return [jnp.concatenate(
            [jnp.full((1, WINDOW), sink_ref[(p * 2 + u) * SW_GROUP + 2 * j + half] * LOG2E, F32)
             for j in range(n_pairs)], axis=1) for half in range(2)]

    sinks = [sink_rows(u) for u in range(2)]
    orow = lax.broadcasted_iota(jnp.int32, (16, 2 * band), 0)
    ocol = lax.broadcasted_iota(jnp.int32, (16, 2 * band), 1)
    ones_rows = jnp.where((orow < 8) == (ocol < band), 1.0, 0.0).astype(BF16)
    zeros = jnp.zeros((SW_HD, band), BF16)

    def band_of(blk):
        tok = qb * n_blocks + blk
        cur = pl.ds(pl.multiple_of(tok * WINDOW, WINDOW), WINDOW)
        prev = pl.ds(pl.multiple_of(jnp.maximum(tok - 1, 0) * WINDOW, WINDOW), WINDOW)
        is_first = tok == 0
        kband = jnp.concatenate([jnp.where(is_first, km_ref[...], k_ref[0, prev, :]),
                                 k_ref[0, cur, :]], axis=0)
        vband = jnp.concatenate([jnp.where(is_first, vm_ref[...], v_ref[0, prev, :]),
                                 v_ref[0, cur, :]], axis=0)
        return (pl.ds(pl.multiple_of(blk * WINDOW, WINDOW), WINDOW), kband,
                vband.T.astype(BF16), jnp.where(is_first, 0, 1))

    def blocks(it, carry):
        bands = [band_of(it * SWA_BLOCKS_PER_ITER + i) for i in range(SWA_BLOCKS_PER_ITER)]
        units = [(bnd, u) for bnd in bands for u in range(2)]
        scores = []
        for (rows, kband, _, bias_sel), u in units:
            base = u * SW_GROUP * SW_HD
            q4 = jnp.concatenate(
                [q_ref[0, rows, base + j * LANES:base + (j + 1) * LANES] for j in range(n_pairs)],
                axis=0)
            q4 = (q4 * (SW_HD ** -0.5 * LOG2E)).astype(BF16)
            scores.append(_nt(_expand_band(kband, u), q4) + bias_ref[bias_sel, u])
        probs = [_swa_probs(s, sinks[u]) for s, (_, u) in zip(scores, units)]
        outs = []
        for (pt, ms), ((_, _, vbt, _), u) in zip(probs, units):
            vt = vbt[u * SW_HD:(u + 1) * SW_HD]
            vxt = jnp.concatenate([jnp.concatenate([vt, zeros], axis=1),
                                   jnp.concatenate([zeros, vt], axis=1), ones_rows], axis=0)
            outs.append(_swa_values(pt, vxt, ms, sinks[u]))
        for ot, ((rows, _, _, _), u) in zip(outs, units):
            base = u * SW_GROUP * SW_HD
            o = jnp.concatenate([ot[:, j * WINDOW:(j + 1) * WINDOW].T for j in range(n_pairs)],
                                axis=1)
            graw = g_ref[0, rows, base:base + SW_GROUP * SW_HD]
            o_ref[0, rows, base:base + SW_GROUP * SW_HD] = (o * _silu(graw)).astype(BF16)
        return carry

    lax.fori_loop(0, n_blocks // SWA_BLOCKS_PER_ITER, blocks, 0)


def _swa_prompt(proj, meta_k, meta_v, sinks, bias, *, tq):
    bsz, tlen, _ = proj.shape
    pw = 2 * SW_GROUP * SW_HD
    qblocks = SW_HEADS * SW_HD // pw
    kcol0 = SW_HEADS * SW_HD // LANES
    vcol0 = kcol0 + SW_KV * SW_HD // LANES
    gblk0 = (SW_HEADS * SW_HD + 2 * SW_KV * SW_HD) // pw
    return pl.pallas_call(
        functools.partial(_swa_prompt_kernel, n_blocks=tq // WINDOW),
        grid=(qblocks, bsz, tlen // tq),
        in_specs=[pl.BlockSpec(memory_space=pltpu.SMEM),
                  pl.BlockSpec((1, tq, pw), lambda p, b, t: (b, t, p)),
                  pl.BlockSpec((1, tq, pw), lambda p, b, t: (b, t, gblk0 + p)),
                  pl.BlockSpec((1, tlen, LANES), lambda p, b, t: (b, 0, kcol0 + p)),
                  pl.BlockSpec((1, tlen, LANES), lambda p, b, t: (b, 0, vcol0 + p)),
                  pl.BlockSpec((WINDOW, LANES), lambda p, b, t: (0, p)),
                  pl.BlockSpec((WINDOW, LANES), lambda p, b, t: (0, p)),
                  pl.BlockSpec((2, 2, 4 * WINDOW, 4 * WINDOW), lambda p, b, t: (0, p, 0, 0))],
        out_specs=pl.BlockSpec((1, tq, pw), lambda p, b, t: (b, t, p)),
        out_shape=jax.ShapeDtypeStruct((bsz, tlen, D_MODEL), BF16),
        compiler_params=pltpu.CompilerParams(
            dimension_semantics=("parallel", "parallel", "arbitrary"),
            vmem_limit_bytes=V7X_VMEM_LIMIT),
        name="swa_prompt",
    )(sinks, proj, proj, proj, proj, meta_k, meta_v, bias)


def _swa_sample_kernel(q_ref, g_ref, kn_ref, vn_ref, ck_ref, cv_ref, bc_ref, bn_ref, sink_ref,
                       o_ref, nk_ref, nv_ref, *, nb):
    nkv = SW_KV * SW_HD
    r = ck_ref.shape[1]
    row = lax.broadcasted_iota(jnp.int32, (SW_HEADS, nkv), 0)
    col = lax.broadcasted_iota(jnp.int32, (SW_HEADS, nkv), 1)
    own = (row // SW_GROUP) == (col // SW_HD)
    last = lax.broadcasted_iota(jnp.int32, (r, nkv), 0) == r - 1
    sink = sink_ref[...]
    for bi in range(nb):
        q = q_ref[bi] * (SW_HD ** -0.5)
        qx = jnp.where(own, jnp.concatenate([q] * SW_KV, axis=1), 0.0)
        ck, cv = ck_ref[bi], cv_ref[bi]
        kn, vn = kn_ref[bi], vn_ref[bi]

        s_c = _nt(qx.astype(BF16), ck.astype(BF16)) + bc_ref[...]
        s_n = jnp.sum(qx * kn, axis=-1, keepdims=True) + bn_ref[...]
        m = jnp.maximum(jnp.maximum(jnp.max(s_c, axis=-1, keepdims=True), s_n), sink)
        p_c = jnp.exp(s_c - m)
        p_n = jnp.exp(s_n - m)
        den = jnp.sum(p_c, axis=-1, keepdims=True) + p_n + jnp.exp(sink - m)
        o_all = _nn(p_c.astype(BF16), cv.astype(BF16))
        o_all = o_all + p_n * vn
        o_all = jnp.where(own, o_all, 0.0)
        o = o_all[:, 0:SW_HD]
        for n in range(1, SW_KV):
            o = o + o_all[:, n * SW_HD:(n + 1) * SW_HD]
        o_ref[bi] = ((o / den) * _silu(g_ref[bi])).astype(BF16)

        nk_ref[bi] = jnp.where(last, kn, pltpu.roll(ck, r - 1, axis=0))
        nv_ref[bi] = jnp.where(last, vn, pltpu.roll(cv, r - 1, axis=0))


def _swa_sample(q, g, kn, vn, ck, cv, bias_c, bias_n, sinks, *, nb):
    bsz, r, nkv = ck.shape
    head = pl.BlockSpec((nb, SW_HEADS, SW_HD), lambda b: (b, 0, 0))
    new = pl.BlockSpec((nb, 1, nkv), lambda b: (b, 0, 0))
    cache = pl.BlockSpec((nb, r, nkv), lambda b: (b, 0, 0))
    return pl.pallas_call(
        functools.partial(_swa_sample_kernel, nb=nb),
        grid=(bsz // nb,),
        in_specs=[head, head, new, new, cache, cache,
                  pl.BlockSpec((SW_HEADS, r), lambda b: (0, 0)),
                  pl.BlockSpec((SW_HEADS, 1), lambda b: (0, 0)),
                  pl.BlockSpec((SW_HEADS, 1), lambda b: (0, 0))],
        out_specs=[head, cache, cache],
        out_shape=[jax.ShapeDtypeStruct((bsz, SW_HEADS, SW_HD), BF16),
                   jax.ShapeDtypeStruct(ck.shape, ck.dtype),
                   jax.ShapeDtypeStruct(cv.shape, cv.dtype)],
        compiler_params=pltpu.CompilerParams(dimension_semantics=("parallel",)),
        name="swa_sample",
    )(q, g, kn, vn, ck, cv, bias_c, bias_n, sinks.reshape(SW_HEADS, 1))


def kernel(x_prompt, x_sample, state_hgrn, cache_k_win, cache_v_win, meta_tokens, rel_bias,
           hg_lower_bounds, hg_norm, hg_w_in, hg_onorm, hg_w_out,
           sw_norm, sw_w_in, sw_sinks, sw_w_out, final_norm):
    n_samp = x_sample.shape[0]
    samp = slice(N_META, N_META + n_samp)
    x_main = x_prompt.reshape(MAIN_ROWS, D_MODEL)
    x_small = jnp.concatenate(
        [meta_tokens.astype(F32), x_sample.reshape(n_samp, D_MODEL),
         jnp.zeros((N_SMALL - N_META - n_samp, D_MODEL), F32)], axis=0)

    h_main = _rmsnorm(x_main, hg_norm[0], BF16, 512)
    h_small = _rmsnorm(x_small, hg_norm[0], BF16, N_SMALL)
    p_main, p_small = _proj(h_main, h_small, hg_w_in[0], tm=PROJ_TM, tn=PROJ_TN)

    meta_proj = jnp.pad(p_small[:N_META], ((CHUNK - N_META, 0), (0, 0)))[None]
    zero_state = jnp.zeros((HG_HEADS, HG_DK, HG_DK), F32)
    a_meta, s_meta = _hgrn_scan(meta_proj, hg_lower_bounds, hg_onorm[0], zero_state,
                                tb=CHUNK, hb=8, n_pad=CHUNK - N_META)
    a_main, s_prompt = _hgrn_scan(p_main.reshape(BATCH, SEQ, -1), hg_lower_bounds, hg_onorm[0],
                                  s_meta[0], tb=1024, hb=8)
    sect = lambda s: p_small[samp, s * HG_F:(s + 1) * HG_F].reshape(n_samp, HG_HEADS, HG_DK)
    a_samp, s_sample = _hgrn_step(sect(0), sect(1), sect(2), sect(3), hg_lower_bounds,
                                  hg_onorm[0], state_hgrn, nb=SAMPLES_PER_STEP)
    a_small = jnp.concatenate(
        [a_meta[0, CHUNK - N_META:], a_samp.reshape(n_samp, D_MODEL),
         jnp.zeros((N_SMALL - N_META - n_samp, D_MODEL), BF16)], axis=0)
    x1_main, x1_small, x1b_main, x1b_small, sq_main, sq_small = _proj(
        a_main.reshape(MAIN_ROWS, D_MODEL), a_small, hg_w_out[0], x_main, x_small,
        tm=PROJ_RES_TM, tn=PROJ_TN, emit_stats=True)

    p_main, p_small = _proj(x1b_main, x1b_small, sw_w_in[0],
                            scale=(sw_norm[0], jnp.sum(sq_main, axis=0, keepdims=True),
                                   jnp.sum(sq_small, axis=0, keepdims=True)),
                            tm=PROJ_TM, tn=PROJ_TN)
    nq, nkv = SW_HEADS * SW_HD, SW_KV * SW_HD
    kv_tail = p_main.reshape(BATCH, SEQ, -1)[:, -WINDOW:, nq:nq + 2 * nkv]
    k_tail = kv_tail[:, :, :nkv].reshape(1, BATCH, WINDOW, SW_KV, SW_HD)
    v_tail = kv_tail[:, :, nkv:].reshape(1, BATCH, WINDOW, SW_KV, SW_HD)
    meta_kv = jnp.pad(p_small[:N_META, nq:nq + 2 * nkv], ((WINDOW - N_META, 0), (0, 0)))
    a_main = _swa_prompt(p_main.reshape(BATCH, SEQ, -1), meta_kv[:, :nkv], meta_kv[:, nkv:],
                         sw_sinks[0], _prompt_bias(rel_bias), tq=1024)

    r = cache_k_win.shape[2]
    table = rel_bias.astype(F32)
    bias_c = table[_t5_bucket(r - jnp.arange(r))].T
    bias_n = table[_t5_bucket(jnp.zeros((1,), jnp.int32))].T
    q_s = p_small[samp, :nq].reshape(n_samp, SW_HEADS, SW_HD)
    g_s = p_small[samp, nq + 2 * nkv:].reshape(n_samp, SW_HEADS, SW_HD)
    a_samp, k_samp, v_samp = _swa_sample(
        q_s, g_s, p_small[samp, nq:nq + nkv].reshape(n_samp, 1, nkv),
        p_small[samp, nq + nkv:nq + 2 * nkv].reshape(n_samp, 1, nkv),
        cache_k_win[0].reshape(n_samp, r, nkv), cache_v_win[0].reshape(n_samp, r, nkv),
        bias_c, bias_n, sw_sinks[0], nb=SAMPLES_PER_STEP)
    a_small = jnp.concatenate(
        [jnp.zeros((N_META, D_MODEL), BF16), a_samp.reshape(n_samp, D_MODEL),
         jnp.zeros((N_SMALL - N_META - n_samp, D_MODEL), BF16)], axis=0)
    x2_main, x2_small = _proj(a_main.reshape(MAIN_ROWS, D_MODEL), a_small, sw_w_out[0],
                              x1_main, x1_small, tm=PROJ_RES_TM, tn=PROJ_TN)

    y_prompt = _rmsnorm(x2_main, final_norm, F32, 512).reshape(BATCH, SEQ, D_MODEL)
    y_sample = _rmsnorm(x2_small, final_norm, F32, N_SMALL)[samp].reshape(n_samp, 1, D_MODEL)
    return (y_prompt, y_sample, s_prompt[None],
            k_tail.astype(cache_k_win.dtype), v_tail.astype(cache_v_win.dtype),
            s_sample,
            k_samp.reshape(1, n_samp, r, SW_KV, SW_HD), v_samp.reshape(1, n_samp, r, SW_KV, SW_HD))
```

```python
import functools
import math

import numpy as np
import jax
import jax.numpy as jnp
from jax import lax
from jax.experimental import pallas as pl
from jax.experimental.pallas import tpu as pltpu

D_MODEL = 4096
BATCH = 4
SEQ = 2048
DEC_BATCH = 32
N_META = 16
HG_HEADS = 32
HG_DK = 128
HG_F = HG_HEADS * HG_DK
SW_HEADS = 64
SW_KV = 8
SW_HD = 64
SW_GROUP = SW_HEADS // SW_KV
WINDOW = 128
REL_BUCKETS = 32
REL_MAX_DIST = 128
EPS = 1e-6
NEG = -1e30
LOG2E = math.log2(math.e)

LANES = 128
V7X_VMEM_LIMIT = 58 * 1024 * 1024

PROJ_TM = 1024
PROJ_TN = 1024
PROJ_RES_TM = 512
CHUNK = 128
SWA_BLOCKS_PER_ITER = 4
SAMPLES_PER_STEP = 4
N_SMALL = 64
MAIN_ROWS = BATCH * SEQ

F32 = jnp.float32
BF16 = jnp.bfloat16


def _nt(a, b):
    return lax.dot_general(a, b, (((1,), (1,)), ((), ())), preferred_element_type=F32)


def _nn(a, b):
    return jnp.dot(a, b, preferred_element_type=F32)


def _sigmoid(x):
    return 1.0 / (1.0 + jnp.exp(-x))


def _silu(x):
    h = 0.5 * x
    return h + h * jnp.tanh(h)


def _rmsnorm_kernel(x_ref, g_ref, o_ref):
    x = x_ref[...]
    ms = jnp.mean(x * x, axis=-1, keepdims=True)
    o_ref[...] = (x * lax.rsqrt(ms + EPS) * g_ref[...]).astype(o_ref.dtype)


def _rmsnorm(x, g, out_dtype, tm):
    m, d = x.shape
    return pl.pallas_call(
        _rmsnorm_kernel,
        grid=(m // tm,),
        in_specs=[pl.BlockSpec((tm, d), lambda i: (i, 0)),
                  pl.BlockSpec((1, d), lambda i: (0, 0))],
        out_specs=pl.BlockSpec((tm, d), lambda i: (i, 0)),
        out_shape=jax.ShapeDtypeStruct((m, d), out_dtype),
        compiler_params=pltpu.CompilerParams(dimension_semantics=("parallel",),
                                             vmem_limit_bytes=V7X_VMEM_LIMIT),
        name="rmsnorm",
    )(x, g.reshape(1, d))


def _proj_kernel(*refs, tn, kc, cps, has_res, has_scale, emit_stats, sections):
    refs = list(refs)
    a_ref, as_ref, w_hbm = refs[:3]
    del refs[:3]
    if sections == "hgrn":
        lb_ref = refs.pop(0)
    if has_scale:
        gw_ref, sq_in_ref, sqs_in_ref = refs[:3]
        del refs[:3]
    if has_res:
        r_ref, rs_ref = refs[:2]
        del refs[:2]
    o_ref, os_ref = refs[:2]
    del refs[:2]
    if emit_stats:
        ob_ref, obs_ref, sq_ref, sqs_ref = refs[:4]
        del refs[:4]
    wb_ref, stage_ref, sem = refs
    j, i = pl.program_id(0), pl.program_id(1)
    nj, ni = pl.num_programs(0), pl.num_programs(1)
    kdim = w_hbm.shape[0]
    n_kc = kdim // kc
    step = j * ni + i
    cur = lax.rem(j, 2)
    par = lax.rem(step, 2)

    def aligned(x, m):
        return x * m if isinstance(x, int) else pl.multiple_of(x * m, m)

    def chunk_copy(tile, c, slot):
        return pltpu.make_async_copy(
            w_hbm.at[pl.ds(aligned(c, kc), kc), pl.ds(aligned(tile, tn), tn)],
            stage_ref.at[slot], sem.at[slot])

    def cast_chunk(slot, c, half):
        rows = pl.ds(aligned(c, kc), kc)
        w = stage_ref[slot]
        if has_scale:
            w = w * jnp.concatenate([gw_ref[rows, :]] * (tn // LANES), axis=1)
        wb_ref[half, rows, :] = w.astype(BF16)

    def next_tile(jj):
        return jnp.minimum(jj + 1, nj - 1)

    def forget_gate(x):
        lb = _lower_bound(lb_ref[...])
        return lb + (1.0 - lb) * _sigmoid(x)

    def identity(x):
        return x

    if sections == "hgrn":
        quarter = nj // 4
        gates = [_silu, forget_gate, identity]
        gate_id = jnp.where((j < quarter) | (j >= 3 * quarter), 0,
                            jnp.where(j < 2 * quarter, 1, 2))
    elif sections == "swa":
        q_tiles = SW_HEADS * SW_HD // tn
        gates = [lambda x: x * (SW_HD ** -0.5 * LOG2E), identity, _silu]
        gate_id = jnp.where(j < q_tiles, 0, jnp.where(j == q_tiles, 1, 2))
    else:
        gates, gate_id = [identity], None

    def per_section(body):
        if gate_id is None:
            body(gates[0])
        else:
            lax.cond(gate_id == 0, lambda: body(gates[0]),
                     lambda: lax.cond(gate_id == 1, lambda: body(gates[1]),
                                      lambda: body(gates[2])))

    def finish(out, gate, res_ref, sq_in, o, ob, sq):
        if has_scale:
            out = out * lax.rsqrt(jnp.sum(sq_in[...], axis=0) * (1.0 / kdim) + EPS)
        out = gate(out)
        if has_res:
            out = out + res_ref[...]
        o[...] = out
        if emit_stats:
            ob[...] = out.astype(BF16)
            sq[0] = jnp.sum(out * out, axis=1, keepdims=True)

    @pl.when(step == 0)
    def _():
        chunk_copy(0, 0, 0).start()
        for c in range(n_kc):
            if c + 1 < n_kc:
                chunk_copy(0, c + 1, (c + 1) % 2).start()
            chunk_copy(0, c, c % 2).wait()
            cast_chunk(c % 2, c, 0)
        for u in range(cps):
            chunk_copy(next_tile(0), u, u).start()

    for u in range(cps):
        chunk_copy(next_tile(j), i * cps + u, par * cps + u).wait()

    @pl.when(step + 1 < nj * ni)
    def _():
        wrap = i + 1 == ni
        j2 = jnp.where(wrap, j + 1, j)
        i2 = jnp.where(wrap, 0, i + 1)
        for u in range(cps):
            chunk_copy(next_tile(j2), i2 * cps + u, (1 - par) * cps + u).start()

    @pl.when(i == 0)
    def _():
        per_section(lambda gate: finish(
            _nn(as_ref[...], wb_ref[cur]), gate, rs_ref if has_res else None,
            sqs_in_ref if has_scale else None, os_ref,
            obs_ref if emit_stats else None, sqs_ref if emit_stats else None))

    def main(gate):
        for u in range(cps):
            cast_chunk(par * cps + u, i * cps + u, 1 - cur)
        finish(_nn(a_ref[...], wb_ref[cur]), gate, r_ref if has_res else None,
               sq_in_ref if has_scale else None, o_ref,
               ob_ref if emit_stats else None, sq_ref if emit_stats else None)

    per_section(main)


def _proj(a, a_small, w, res=None, res_small=None, scale=None, *, tm, tn, kc=256,
          emit_stats=False, sections=None, lbraw=None):
    m, k = a.shape
    n = w.shape[1]
    ms = a_small.shape[0]
    has_res, has_scale = res is not None, scale is not None
    ni, nj, n_kc = m // tm, n // tn, k // kc
    assert m % tm == 0 and n % tn == 0 and k % kc == 0 and n_kc % ni == 0 and nj >= 2
    cps = n_kc // ni
    in_specs = [pl.BlockSpec((tm, k), lambda j, i: (i, 0)),
                pl.BlockSpec((ms, k), lambda j, i: (0, 0)),
                pl.BlockSpec(memory_space=pl.ANY)]
    args = [a, a_small, w]
    if sections == "hgrn":
        quarter = nj // 4
        assert nj % 4 == 0 and lbraw.shape == (3, n // 4)
        in_specs += [pl.BlockSpec((3, tn), lambda j, i: (0, jnp.clip(j - quarter, 0, quarter - 1)))]
        args += [lbraw]
    if sections == "swa":
        assert tn == 2 * SW_KV * SW_HD and (SW_HEADS * SW_HD) % tn == 0
    if has_scale:
        gain, sq, sq_small = scale
        in_specs += [pl.BlockSpec((k, LANES), lambda j, i: (0, 0)),
                     pl.BlockSpec((sq.shape[0], tm, 1), lambda j, i: (0, i, 0)),
                     pl.BlockSpec((sq.shape[0], ms, 1), lambda j, i: (0, 0, 0))]
        args += [jnp.broadcast_to(gain.astype(F32)[:, None], (k, LANES)), sq, sq_small]
    if has_res:
        in_specs += [pl.BlockSpec((tm, tn), lambda j, i: (i, j)),
                     pl.BlockSpec((ms, tn), lambda j, i: (0, j))]
        args += [res, res_small]
    out_specs = [pl.BlockSpec((tm, tn), lambda j, i: (i, j)),
                 pl.BlockSpec((ms, tn), lambda j, i: (0, j))]
    out_shape = [jax.ShapeDtypeStruct((m, n), F32), jax.ShapeDtypeStruct((ms, n), F32)]
    if emit_stats:
        out_specs += [pl.BlockSpec((tm, tn), lambda j, i: (i, j)),
                      pl.BlockSpec((ms, tn), lambda j, i: (0, j)),
                      pl.BlockSpec((1, tm, 1), lambda j, i: (j, i, 0)),
                      pl.BlockSpec((1, ms, 1), lambda j, i: (j, 0, 0))]
        out_shape += [jax.ShapeDtypeStruct((m, n), BF16), jax.ShapeDtypeStruct((ms, n), BF16),
                      jax.ShapeDtypeStruct((nj, m, 1), F32), jax.ShapeDtypeStruct((nj, ms, 1), F32)]
    return pl.pallas_call(
        functools.partial(_proj_kernel, tn=tn, kc=kc, cps=cps, has_res=has_res,
                          has_scale=has_scale, emit_stats=emit_stats, sections=sections),
        grid=(nj, ni),
        in_specs=in_specs,
        out_specs=out_specs,
        out_shape=out_shape,
        scratch_shapes=[pltpu.VMEM((2, k, tn), BF16),
                        pltpu.VMEM((2 * cps, kc, tn), F32),
                        pltpu.SemaphoreType.DMA((2 * cps,))],
        compiler_params=pltpu.CompilerParams(
            dimension_semantics=("arbitrary", "arbitrary"),
            vmem_limit_bytes=V7X_VMEM_LIMIT),
        name="proj_res" if has_res else "proj",
    )(*args)


def _hgrn_consts():
    t = np.arange(CHUNK)[:, None]
    s = np.arange(CHUNK)[None, :]
    tri = s <= t
    ends = np.array([15, 47, 79, 111, 31, 95, 63, 127] + [-1] * 8)[:, None]
    mall = np.concatenate([tri, s <= ends], axis=0).astype(np.float32)
    lvl = np.full((CHUNK, CHUNK), 3, np.int32)
    lvl[(t >= 64) & (s < 64)] = 2
    lvl[(t // 64 == s // 64) & (t % 64 >= 32) & (s % 64 < 32)] = 1
    lvl[(t // 32 == s // 32) & (s <= t)] = 0
    return jnp.asarray(mall, BF16), jnp.asarray(lvl)


def _lower_bound(lbraw):
    mx = jnp.max(lbraw, axis=0, keepdims=True)
    e = jnp.exp(lbraw - mx)
    return e[0:1, :] / jnp.sum(e, axis=0, keepdims=True)


def _head_out(o, gate, onorm):
    ms = jnp.mean(o * o, axis=-1, keepdims=True)
    return (o * lax.rsqrt(ms + EPS) * onorm) * gate


def _hgrn_gates(q, f, mall, row_valid):
    logf = jnp.log2(f)
    k = 1.0 - f
    if row_valid is not None:
        logf = jnp.where(row_valid, logf, 0.0)
        k = jnp.where(row_valid, k, 0.0)

    hi = logf.astype(BF16)
    mid = (logf - hi.astype(F32)).astype(BF16)
    cs = _nn(mall, jnp.concatenate([hi, mid], axis=1))
    return q, k, cs[:, :LANES] + cs[:, LANES:]


def _hgrn_scores(q, k, cs):
    b = cs[0:CHUNK]
    ref = cs[CHUNK:CHUNK + 8]
    bl = ref[7:8]
    cat = lambda parts: jnp.concatenate(parts, axis=0)
    d0 = cat([b[32 * i:32 * i + 32] - ref[i:i + 1] for i in range(4)])
    d1 = [b[64 * i:64 * i + 64] - ref[4 + i:5 + i] for i in range(2)]
    d2 = b - ref[6:7]
    zero32 = jnp.zeros((32, LANES), BF16)
    a0 = _nt((q * jnp.exp2(d0)).astype(BF16), (k * jnp.exp2(-d0)).astype(BF16))
    q1 = cat([q[32:64] * jnp.exp2(d1[0][32:64]), q[96:128] * jnp.exp2(d1[1][32:64])])
    k1 = [(k[64 * i:64 * i + 32] * jnp.exp2(-d1[i][0:32])).astype(BF16) for i in range(2)]
    a1 = _nt(q1.astype(BF16), cat([k1[0], zero32, k1[1], zero32]))
    a2 = _nt((q[64:128] * jnp.exp2(d2[64:128])).astype(BF16),
             cat([(k[0:64] * jnp.exp2(-d2[0:64])).astype(BF16), zero32, zero32]))
    qe = (q * jnp.exp2(b)).astype(BF16)
    kd = (k * jnp.exp2(bl - b)).astype(BF16)
    return a0, a1, a2, qe, kd, jnp.exp2(bl)


def _hgrn_apply(a0, a1, a2, qe, kd, decay, v, st, lvl):
    is0, is1, is2 = lvl == 0, lvl == 1, lvl == 2
    a = jnp.concatenate([
        jnp.where(is0[0:32], a0[0:32], 0.0),
        jnp.where(is0[32:64], a0[32:64], jnp.where(is1[32:64], a1[0:32], 0.0)),
        jnp.where(is0[64:96], a0[64:96], jnp.where(is2[64:96], a2[0:32], 0.0)),
        jnp.where(is0[96:128], a0[96:128],
                  jnp.where(is1[96:128], a1[32:64], jnp.where(is2[96:128], a2[32:64], 0.0))),
    ], axis=0)
    vt = v.T.astype(BF16)
    lhs = jnp.concatenate([a.astype(BF16), qe], axis=1)
    rhs = jnp.concatenate([vt, st.astype(BF16)], axis=1)
    return _nt(lhs, rhs), st * decay + _nn(vt, kd)


def _hgrn_scan_kernel(q_ref, f_ref, v_ref, g_ref, on_ref, s0_ref, mall_ref, lvl_ref,
                      a_ref, s_ref, st_ref, *, hb, n_chunks, n_pad):
    t = pl.program_id(2)

    @pl.when(t == 0)
    def _():
        for h in range(hb):
            st_ref[h] = s0_ref[h].T

    row_valid = None
    if n_pad:
        row_valid = lax.broadcasted_iota(jnp.int32, (CHUNK, LANES), 0) >= n_pad
    heads = [slice(h * LANES, (h + 1) * LANES) for h in range(hb)]

    def body(c, carry):
        rows = pl.ds(pl.multiple_of(c * CHUNK, CHUNK), CHUNK)
        gates = [_hgrn_gates(q_ref[0, rows, cols], f_ref[0, rows, cols], mall_ref[...], row_valid)
                 for cols in heads]
        scores = [_hgrn_scores(*g) for g in gates]
        outs = [_hgrn_apply(*sc, v_ref[0, rows, cols], st_ref[h], lvl_ref[...])
                for h, (sc, cols) in enumerate(zip(scores, heads))]
        for h, ((o, st_new), cols) in enumerate(zip(outs, heads)):
            st_ref[h] = st_new
            a_ref[0, rows, cols] = _head_out(o, g_ref[0, rows, cols], on_ref[:, cols]).astype(BF16)
        return carry

    lax.fori_loop(0, n_chunks, body, 0)

    @pl.when(t == pl.num_programs(2) - 1)
    def _():
        for h in range(hb):
            s_ref[0, h] = st_ref[h].T


def _hgrn_scan(proj, onorm, s0, *, tb, hb, n_pad=0):
    bsz, tlen, _ = proj.shape
    mall, lvl = _hgrn_consts()
    hcols = HG_F // (hb * LANES)
    sect = lambda s: (lambda b, h, t: (b, t, s * hcols + h))
    blk = (1, tb, hb * LANES)
    return pl.pallas_call(
        functools.partial(_hgrn_scan_kernel, hb=hb, n_chunks=tb // CHUNK, n_pad=n_pad),
        grid=(bsz, HG_HEADS // hb, tlen // tb),
        in_specs=[pl.BlockSpec(blk, sect(0)), pl.BlockSpec(blk, sect(1)),
                  pl.BlockSpec(blk, sect(2)), pl.BlockSpec(blk, sect(3)),
                  pl.BlockSpec((1, hb * LANES), lambda b, h, t: (0, h)),
                  pl.BlockSpec((hb, HG_DK, HG_DK), lambda b, h, t: (h, 0, 0)),
                  pl.BlockSpec((CHUNK + 16, CHUNK), lambda b, h, t: (0, 0)),
                  pl.BlockSpec((CHUNK, CHUNK), lambda b, h, t: (0, 0))],
        out_specs=[pl.BlockSpec(blk, lambda b, h, t: (b, t, h)),
                   pl.BlockSpec((1, hb, HG_DK, HG_DK), lambda b, h, t: (b, h, 0, 0))],
        out_shape=[jax.ShapeDtypeStruct((bsz, tlen, D_MODEL), BF16),
                   jax.ShapeDtypeStruct((bsz, HG_HEADS, HG_DK, HG_DK), F32)],
        scratch_shapes=[pltpu.VMEM((hb, HG_DK, HG_DK), F32)],
        compiler_params=pltpu.CompilerParams(
            dimension_semantics=("parallel", "parallel", "arbitrary"),
            vmem_limit_bytes=V7X_VMEM_LIMIT),
        name="hgrn_scan",
    )(proj, proj, proj, proj, onorm.reshape(1, D_MODEL), s0, mall, lvl)


def _hgrn_step_kernel(q_ref, f_ref, v_ref, g_ref, on_ref, s0_ref, a_ref, s_ref, *, nb):
    def split(x):
        hi = x.astype(BF16).astype(F32)
        return hi, x - hi

    r = lax.broadcasted_iota(jnp.int32, (8, HG_DK), 0)
    ones_tail = jnp.concatenate([jnp.where((r == 3) | (r == 4), 1.0, 0.0),
                                 jnp.where((r == 5) | (r == 6), 1.0, 0.0)], axis=1).astype(BF16)
    k_hi_rows, k_lo_rows = r < 2, (r == 2) | (r == 7)
    v_hi_rows, v_lo_rows = (r == 0) | (r == 2), (r == 1) | (r == 7)
    for bi in range(nb):
        f = f_ref[bi]
        decay = jnp.exp(jnp.log(f))
        k = 1.0 - f
        (kh, kl), (dh, dl), (qh, ql), (vh, vl) = (split(k), split(decay), split(q_ref[bi]),
                                                  split(v_ref[bi]))
        rows = []
        for h in range(HG_HEADS):
            row = lambda x: jnp.broadcast_to(x[h:h + 1, :], (8, HG_DK))
            lhs = jnp.where(k_hi_rows, row(kh), jnp.where(k_lo_rows, row(kl), jnp.where(
                r == 3, row(dh), jnp.where(r == 4, row(dl), jnp.where(
                    r == 5, row(qh), row(ql))))))
            rhs_v = jnp.where(v_hi_rows, row(vh), jnp.where(v_lo_rows, row(vl), 0.0))
            rhs = jnp.concatenate([rhs_v.astype(BF16), ones_tail], axis=1)
            out = lax.dot_general(lhs.astype(BF16), rhs, (((0,), (0,)), ((), ())),
                                  preferred_element_type=F32)
            s_new = out[:, HG_DK:2 * HG_DK] * s0_ref[0, bi, h] + out[:, :HG_DK]
            s_ref[0, bi, h] = s_new
            rows.append(jnp.sum(out[:, 2 * HG_DK:] * s_new, axis=0, keepdims=True))
        o = jnp.concatenate(rows, axis=0)
        a_ref[bi] = _head_out(o, g_ref[bi], on_ref[...]).astype(BF16)


def _hgrn_step(q, f, v, g, onorm, state, *, nb):
    bsz = q.shape[0]
    vec = pl.BlockSpec((nb, HG_HEADS, HG_DK), lambda b: (b, 0, 0))
    full = pl.BlockSpec((HG_HEADS, HG_DK), lambda b: (0, 0))
    sspec = pl.BlockSpec((1, nb, HG_HEADS, HG_DK, HG_DK), lambda b: (0, b, 0, 0, 0))
    return pl.pallas_call(
        functools.partial(_hgrn_step_kernel, nb=nb),
        grid=(bsz // nb,),
        in_specs=[vec, vec, vec, vec, full, sspec],
        out_specs=[vec, sspec],
        out_shape=[jax.ShapeDtypeStruct((bsz, HG_HEADS, HG_DK), BF16),
                   jax.ShapeDtypeStruct(state.shape, state.dtype)],
        compiler_params=pltpu.CompilerParams(dimension_semantics=("parallel",),
                                             vmem_limit_bytes=V7X_VMEM_LIMIT),
        name="hgrn_step",
    )(q, f, v, g, onorm.reshape(HG_HEADS, HG_DK), state)


def _t5_bucket(dist):
    max_exact = REL_BUCKETS // 2
    d = jnp.maximum(dist, 1).astype(F32)
    large = max_exact + (jnp.log(d / max_exact) / math.log(REL_MAX_DIST / max_exact)
                         * (REL_BUCKETS - max_exact)).astype(jnp.int32)
    large = jnp.minimum(large, REL_BUCKETS - 1)
    return jnp.where(dist < max_exact, dist, large)


def _prompt_bias_kernel(w_ref, o_ref):
    band = 2 * WINDOW
    key = lax.broadcasted_iota(jnp.int32, (band, WINDOW), 0)
    for g in range(SW_GROUP):
        j, half = divmod(g, 2)
        row = jnp.broadcast_to(w_ref[g:g + 1, :], (band, 3 * WINDOW))
        toep = pltpu.roll(row, 0, 1, stride=1, stride_axis=0)[:, :WINDOW]
        dst = (slice(half * band, (half + 1) * band), slice(j * WINDOW, (j + 1) * WINDOW))
        o_ref[1, 0, dst[0], dst[1]] = toep
        o_ref[0, 0, dst[0], dst[1]] = jnp.where(key < WINDOW - N_META, NEG * LOG2E, toep)


def _prompt_bias(table):
    band = 2 * WINDOW
    i = jnp.arange(3 * WINDOW)
    dist = jnp.where(i < WINDOW, i + WINDOW, i - band)
    vals = table.astype(F32)[_t5_bucket(jnp.maximum(dist, 0))]
    vals = jnp.where(((dist >= 0) & (dist <= WINDOW))[:, None], vals, NEG) * LOG2E
    return pl.pallas_call(
        _prompt_bias_kernel,
        grid=(SW_KV,),
        in_specs=[pl.BlockSpec((SW_GROUP, 3 * WINDOW), lambda n: (n, 0))],
        out_specs=pl.BlockSpec((2, 1, 2 * band, 4 * WINDOW), lambda n: (0, n, 0, 0)),
        out_shape=jax.ShapeDtypeStruct((2, SW_KV, 2 * band, 4 * WINDOW), F32),
        compiler_params=pltpu.CompilerParams(dimension_semantics=("parallel",)),
        name="prompt_bias",
    )(vals.T)


def _expand_band(pair, u):
    lane = lax.broadcasted_iota(jnp.int32, pair.shape, 1)
    rolled = pltpu.roll(pair, SW_HD, axis=1)
    lo_src, hi_src = (pair, rolled) if u == 0 else (rolled, pair)
    top = jnp.where(lane < SW_HD, lo_src, 0.0)
    bot = jnp.where(lane >= SW_HD, hi_src, 0.0)
    return jnp.concatenate([top, bot], axis=0).astype(BF16)


def _swa_probs(s, sinks):
    band = 2 * WINDOW
    ps, ms = [], []
    for half in range(2):
        sh = s[half * band:(half + 1) * band]
        m = jnp.maximum(jnp.max(sh, axis=0, keepdims=True), sinks[half])
        ps.append(jnp.exp2(sh - m).astype(BF16))
        ms.append(m)
    return jnp.concatenate(ps, axis=0), ms


def _swa_values(pt, vxt, ms, sinks):
    ot = _nn(vxt, pt)
    outs = []
    for half in range(2):
        den = ot[2 * SW_HD + 8 * half:2 * SW_HD + 8 * half + 1] + jnp.exp2(sinks[half] - ms[half])
        outs.append(ot[SW_HD * half:SW_HD * (half + 1)] * (1.0 / den))
    return jnp.concatenate(outs, axis=0)


def _swa_prompt_kernel(sink_ref, q_ref, g_ref, k_ref, v_ref, km_ref, vm_ref, bias_ref,
                       o_ref, *, n_blocks):
    p = pl.program_id(0)
    qb = pl.program_id(2)
    n_pairs = SW_GROUP // 2
    band = 2 * WINDOW

    def sink_rows(u):
        return [jnp.concatenate(
            [jnp.full((1, WINDOW), sink_ref[(p * 2 + u) * SW_GROUP + 2 * j + half] * LOG2E, F32)
             for j in range(n_pairs)], axis=1) for half in range(2)]

    sinks = [sink_rows(u) for u in range(2)]
    orow = lax.broadcasted_iota(jnp.int32, (16, 2 * band), 0)
    ocol = lax.broadcasted_iota(jnp.int32, (16, 2 * band), 1)
    ones_rows = jnp.where((orow < 8) == (ocol < band), 1.0, 0.0).astype(BF16)
    zeros = jnp.zeros((SW_HD, band), BF16)

    def band_of(blk):
        tok = qb * n_blocks + blk
        cur = pl.ds(pl.multiple_of(tok * WINDOW, WINDOW), WINDOW)
        prev = pl.ds(pl.multiple_of(jnp.maximum(tok - 1, 0) * WINDOW, WINDOW), WINDOW)
        is_first = tok == 0
        kband = jnp.concatenate([jnp.where(is_first, km_ref[...], k_ref[0, prev, :]),
                                 k_ref[0, cur, :]], axis=0)
        vband = jnp.concatenate([jnp.where(is_first, vm_ref[...], v_ref[0, prev, :]),
                                 v_ref[0, cur, :]], axis=0)
        return (pl.ds(pl.multiple_of(blk * WINDOW, WINDOW), WINDOW), kband,
                vband.T.astype(BF16), jnp.where(is_first, 0, 1))

    def blocks(it, carry):
        bands = [band_of(it * SWA_BLOCKS_PER_ITER + i) for i in range(SWA_BLOCKS_PER_ITER)]
        units = [(bnd, u) for bnd in bands for u in range(2)]
        scores = []
        for (rows, kband, _, bias_sel), u in units:
            base = u * SW_GROUP * SW_HD
            q4 = jnp.concatenate(
                [q_ref[0, rows, base + j * LANES:base + (j + 1) * LANES] for j in range(n_pairs)],
                axis=0).astype(BF16)
            scores.append(_nt(_expand_band(kband, u), q4) + bias_ref[bias_sel, u])
        probs = [_swa_probs(s, sinks[u]) for s, (_, u) in zip(scores, units)]
        outs = []
        for (pt, ms), ((_, _, vbt, _), u) in zip(probs, units):
            vt = vbt[u * SW_HD:(u + 1) * SW_HD]
            vxt = jnp.concatenate([jnp.concatenate([vt, zeros], axis=1),
                                   jnp.concatenate([zeros, vt], axis=1), ones_rows], axis=0)
            outs.append(_swa_values(pt, vxt, ms, sinks[u]))
        for ot, ((rows, _, _, _), u) in zip(outs, units):
            base = u * SW_GROUP * SW_HD
            o = jnp.concatenate([ot[:, j * WINDOW:(j + 1) * WINDOW].T for j in range(n_pairs)],
                                axis=1)
            gate = g_ref[0, rows, base:base + SW_GROUP * SW_HD]
            o_ref[0, rows, base:base + SW_GROUP * SW_HD] = (o * gate).astype(BF16)
        return carry

    lax.fori_loop(0, n_blocks // SWA_BLOCKS_PER_ITER, blocks, 0)


def _swa_prompt(proj, meta_k, meta_v, sinks, bias, *, tq):
    bsz, tlen, _ = proj.shape
    pw = 2 * SW_GROUP * SW_HD
    qblocks = SW_HEADS * SW_HD // pw
    kcol0 = SW_HEADS * SW_HD // LANES
    vcol0 = kcol0 + SW_KV * SW_HD // LANES
    gblk0 = (SW_HEADS * SW_HD + 2 * SW_KV * SW_HD) // pw
    return pl.pallas_call(
        functools.partial(_swa_prompt_kernel, n_blocks=tq // WINDOW),
        grid=(qblocks, bsz, tlen // tq),
        in_specs=[pl.BlockSpec(memory_space=pltpu.SMEM),
                  pl.BlockSpec((1, tq, pw), lambda p, b, t: (b, t, p)),
                  pl.BlockSpec((1, tq, pw), lambda p, b, t: (b, t, gblk0 + p)),
                  pl.BlockSpec((1, tlen, LANES), lambda p, b, t: (b, 0, kcol0 + p)),
                  pl.BlockSpec((1, tlen, LANES), lambda p, b, t: (b, 0, vcol0 + p)),
                  pl.BlockSpec((WINDOW, LANES), lambda p, b, t: (0, p)),
                  pl.BlockSpec((WINDOW, LANES), lambda p, b, t: (0, p)),
                  pl.BlockSpec((2, 2, 4 * WINDOW, 4 * WINDOW), lambda p, b, t: (0, p, 0, 0))],
        out_specs=pl.BlockSpec((1, tq, pw), lambda p, b, t: (b, t, p)),
        out_shape=jax.ShapeDtypeStruct((bsz, tlen, D_MODEL), BF16),
        compiler_params=pltpu.CompilerParams(
            dimension_semantics=("parallel", "parallel", "arbitrary"),
            vmem_limit_bytes=V7X_VMEM_LIMIT),
        name="swa_prompt",
    )(sinks, proj, proj, proj, proj, meta_k, meta_v, bias)


def _swa_sample_kernel(q_ref, g_ref, kn_ref, vn_ref, ck_ref, cv_ref, bc_ref, bn_ref, sink_ref,
                       o_ref, nk_ref, nv_ref, *, nb):
    nkv = SW_KV * SW_HD
    r = ck_ref.shape[1]
    row = lax.broadcasted_iota(jnp.int32, (SW_HEADS, nkv), 0)
    col = lax.broadcasted_iota(jnp.int32, (SW_HEADS, nkv), 1)
    own = (row // SW_GROUP) == (col // SW_HD)
    last = lax.broadcasted_iota(jnp.int32, (r, nkv), 0) == r - 1
    sink = sink_ref[...] * LOG2E
    for bi in range(nb):
        q = q_ref[bi]
        qx = jnp.where(own, jnp.concatenate([q] * SW_KV, axis=1), 0.0)
        ck, cv = ck_ref[bi], cv_ref[bi]
        kn, vn = kn_ref[bi], vn_ref[bi]

        s_c = _nt(qx.astype(BF16), ck.astype(BF16)) + bc_ref[...]
        s_n = jnp.sum(qx * kn, axis=-1, keepdims=True) + bn_ref[...]
        m = jnp.maximum(jnp.maximum(jnp.max(s_c, axis=-1, keepdims=True), s_n), sink)
        p_c = jnp.exp2(s_c - m)
        p_n = jnp.exp2(s_n - m)
        den = jnp.sum(p_c, axis=-1, keepdims=True) + p_n + jnp.exp2(sink - m)
        o_all = _nn(p_c.astype(BF16), cv.astype(BF16))
        o_all = o_all + p_n * vn
        o_all = jnp.where(own, o_all, 0.0)
        o = o_all[:, 0:SW_HD]
        for n in range(1, SW_KV):
            o = o + o_all[:, n * SW_HD:(n + 1) * SW_HD]
        o_ref[bi] = ((o / den) * g_ref[bi]).astype(BF16)

        nk_ref[bi] = jnp.where(last, kn, pltpu.roll(ck, r - 1, axis=0))
        nv_ref[bi] = jnp.where(last, vn, pltpu.roll(cv, r - 1, axis=0))


def _swa_sample(q, g, kn, vn, ck, cv, bias_c, bias_n, sinks, *, nb):
    bsz, r, nkv = ck.shape
    head = pl.BlockSpec((nb, SW_HEADS, SW_HD), lambda b: (b, 0, 0))
    new = pl.BlockSpec((nb, 1, nkv), lambda b: (b, 0, 0))
    cache = pl.BlockSpec((nb, r, nkv), lambda b: (b, 0, 0))
    return pl.pallas_call(
        functools.partial(_swa_sample_kernel, nb=nb),
        grid=(bsz // nb,),
        in_specs=[head, head, new, new, cache, cache,
                  pl.BlockSpec((SW_HEADS, r), lambda b: (0, 0)),
                  pl.BlockSpec((SW_HEADS, 1), lambda b: (0, 0)),
                  pl.BlockSpec((SW_HEADS, 1), lambda b: (0, 0))],
        out_specs=[head, cache, cache],
        out_shape=[jax.ShapeDtypeStruct((bsz, SW_HEADS, SW_HD), BF16),
                   jax.ShapeDtypeStruct(ck.shape, ck.dtype),
                   jax.ShapeDtypeStruct(cv.shape, cv.dtype)],
        compiler_params=pltpu.CompilerParams(dimension_semantics=("parallel",)),
        name="swa_sample",
    )(q, g, kn, vn, ck, cv, bias_c, bias_n, sinks.reshape(SW_HEADS, 1))


def kernel(x_prompt, x_sample, state_hgrn, cache_k_win, cache_v_win, meta_tokens, rel_bias,
           hg_lower_bounds, hg_norm, hg_w_in, hg_onorm, hg_w_out,
           sw_norm, sw_w_in, sw_sinks, sw_w_out, final_norm):
    n_samp = x_sample.shape[0]
    samp = slice(N_META, N_META + n_samp)
    x_main = x_prompt.reshape(MAIN_ROWS, D_MODEL)
    x_small = jnp.concatenate(
        [meta_tokens.astype(F32), x_sample.reshape(n_samp, D_MODEL),
         jnp.zeros((N_SMALL - N_META - n_samp, D_MODEL), F32)], axis=0)

    h_main = _rmsnorm(x_main, hg_norm[0], BF16, 512)
    h_small = _rmsnorm(x_small, hg_norm[0], BF16, N_SMALL)
    p_main, p_small = _proj(h_main, h_small, hg_w_in[0], tm=PROJ_TM, tn=PROJ_TN,
                            sections="hgrn", lbraw=hg_lower_bounds.astype(F32))

    meta_proj = jnp.pad(p_small[:N_META], ((CHUNK - N_META, 0), (0, 0)))[None]
    zero_state = jnp.zeros((HG_HEADS, HG_DK, HG_DK), F32)
    a_meta, s_meta = _hgrn_scan(meta_proj, hg_onorm[0], zero_state,
                                tb=CHUNK, hb=8, n_pad=CHUNK - N_META)
    a_main, s_prompt = _hgrn_scan(p_main.reshape(BATCH, SEQ, -1), hg_onorm[0],
                                  s_meta[0], tb=1024, hb=8)
    sect = lambda s: p_small[samp, s * HG_F:(s + 1) * HG_F].reshape(n_samp, HG_HEADS, HG_DK)
    a_samp, s_sample = _hgrn_step(sect(0), sect(1), sect(2), sect(3),
                                  hg_onorm[0], state_hgrn, nb=SAMPLES_PER_STEP)
    a_small = jnp.concatenate(
        [a_meta[0, CHUNK - N_META:], a_samp.reshape(n_samp, D_MODEL),
         jnp.zeros((N_SMALL - N_META - n_samp, D_MODEL), BF16)], axis=0)
    x1_main, x1_small, x1b_main, x1b_small, sq_main, sq_small = _proj(
        a_main.reshape(MAIN_ROWS, D_MODEL), a_small, hg_w_out[0], x_main, x_small,
        tm=PROJ_RES_TM, tn=PROJ_TN, emit_stats=True)

    p_main, p_small = _proj(x1b_main, x1b_small, sw_w_in[0],
                            scale=(sw_norm[0], jnp.sum(sq_main, axis=0, keepdims=True),
                                   jnp.sum(sq_small, axis=0, keepdims=True)),
                            tm=PROJ_TM, tn=PROJ_TN, sections="swa")
    nq, nkv = SW_HEADS * SW_HD, SW_KV * SW_HD
    kv_tail = p_main.reshape(BATCH, SEQ, -1)[:, -WINDOW:, nq:nq + 2 * nkv]
    k_tail = kv_tail[:, :, :nkv].reshape(1, BATCH, WINDOW, SW_KV, SW_HD)
    v_tail = kv_tail[:, :, nkv:].reshape(1, BATCH, WINDOW, SW_KV, SW_HD)
    meta_kv = jnp.pad(p_small[:N_META, nq:nq + 2 * nkv], ((WINDOW - N_META, 0), (0, 0)))
    a_main = _swa_prompt(p_main.reshape(BATCH, SEQ, -1), meta_kv[:, :nkv], meta_kv[:, nkv:],
                         sw_sinks[0], _prompt_bias(rel_bias), tq=1024)

    r = cache_k_win.shape[2]
    table = rel_bias.astype(F32)
    bias_c = table[_t5_bucket(r - jnp.arange(r))].T * LOG2E
    bias_n = table[_t5_bucket(jnp.zeros((1,), jnp.int32))].T * LOG2E
    q_s = p_small[samp, :nq].reshape(n_samp, SW_HEADS, SW_HD)
    g_s = p_small[samp, nq + 2 * nkv:].reshape(n_samp, SW_HEADS, SW_HD)
    a_samp, k_samp, v_samp = _swa_sample(
        q_s, g_s, p_small[samp, nq:nq + nkv].reshape(n_samp, 1, nkv),
        p_small[samp, nq + nkv:nq + 2 * nkv].reshape(n_samp, 1, nkv),
        cache_k_win[0].reshape(n_samp, r, nkv), cache_v_win[0].reshape(n_samp, r, nkv),
        bias_c, bias_n, sw_sinks[0], nb=SAMPLES_PER_STEP)
    a_small = jnp.concatenate(
        [jnp.zeros((N_META, D_MODEL), BF16), a_samp.reshape(n_samp, D_MODEL),
         jnp.zeros((N_SMALL - N_META - n_samp, D_MODEL), BF16)], axis=0)
    x2_main, x2_small = _proj(a_main.reshape(MAIN_ROWS, D_MODEL), a_small, sw_w_out[0],
                              x1_main, x1_small, tm=PROJ_RES_TM, tn=PROJ_TN)

    y_prompt = _rmsnorm(x2_main, final_norm, F32, 512).reshape(BATCH, SEQ, D_MODEL)
    y_sample = _rmsnorm(x2_small, final_norm, F32, N_SMALL)[samp].reshape(n_samp, 1, D_MODEL)
    return (y_prompt, y_sample, s_prompt[None],
            k_tail.astype(cache_k_win.dtype), v_tail.astype(cache_v_win.dtype),
            s_sample,
            k_samp.reshape(1, n_samp, r, SW_KV, SW_HD), v_samp.reshape(1, n_samp, r, SW_KV, SW_HD))
```

```python
import functools
import math

import numpy as np
import jax
import jax.numpy as jnp
from jax import lax
from jax.experimental import pallas as pl
from jax.experimental.pallas import tpu as pltpu

D_MODEL = 4096
BATCH = 4
SEQ = 2048
DEC_BATCH = 32
N_META = 16
HG_HEADS = 32
HG_DK = 128
HG_F = HG_HEADS * HG_DK
SW_HEADS = 64
SW_KV = 8
SW_HD = 64
SW_GROUP = SW_HEADS // SW_KV
WINDOW = 128
REL_BUCKETS = 32
REL_MAX_DIST = 128
EPS = 1e-6
NEG = -1e30
LOG2E = math.log2(math.e)

LANES = 128
V7X_VMEM_LIMIT = 58 * 1024 * 1024

PROJ_TM = 1024
PROJ_TN = 1024
PROJ_RES_TM = 512
CHUNK = 128
SWA_BLOCKS_PER_ITER = 4
SAMPLES_PER_STEP = 4
N_SMALL = 64
MAIN_ROWS = BATCH * SEQ

F32 = jnp.float32
BF16 = jnp.bfloat16


def _nt(a, b):
    return lax.dot_general(a, b, (((1,), (1,)), ((), ())), preferred_element_type=F32)


def _nn(a, b):
    return jnp.dot(a, b, preferred_element_type=F32)


def _sigmoid(x):
    return 1.0 / (1.0 + jnp.exp(-x))


def _silu(x):
    h = 0.5 * x
    return h + h * jnp.tanh(h)


def _rmsnorm_kernel(x_ref, g_ref, o_ref):
    x = x_ref[...]
    ms = jnp.mean(x * x, axis=-1, keepdims=True)
    o_ref[...] = (x * lax.rsqrt(ms + EPS) * g_ref[...]).astype(o_ref.dtype)


def _rmsnorm(x, g, out_dtype, tm):
    m, d = x.shape
    return pl.pallas_call(
        _rmsnorm_kernel,
        grid=(m // tm,),
        in_specs=[pl.BlockSpec((tm, d), lambda i: (i, 0)),
                  pl.BlockSpec((1, d), lambda i: (0, 0))],
        out_specs=pl.BlockSpec((tm, d), lambda i: (i, 0)),
        out_shape=jax.ShapeDtypeStruct((m, d), out_dtype),
        compiler_params=pltpu.CompilerParams(dimension_semantics=("parallel",),
                                             vmem_limit_bytes=V7X_VMEM_LIMIT),
        name="rmsnorm",
    )(x, g.reshape(1, d))


def _proj_kernel(*refs, tn, kc, cps, col0, nj_total, single, raw_in, has_head, has_res, has_scale,
                 emit_stats, sections):
    refs = list(refs)
    a_ref, as_ref, w_hbm = refs[:3]
    del refs[:3]
    if sections == "hgrn":
        lb_ref = refs.pop(0)
    if has_scale:
        gw_ref = refs.pop(0)
        if not raw_in:
            sq_in_ref, sqs_in_ref = refs[:2]
            del refs[:2]
    if has_res:
        r_ref, rs_ref = refs[:2]
        del refs[:2]
    if has_head:
        head_ref, heads_ref = refs[:2]
        del refs[:2]
    o_ref, os_ref = refs[:2]
    del refs[:2]
    if emit_stats:
        ob_ref, obs_ref, sq_ref, sqs_ref = refs[:4]
        del refs[:4]
    if raw_in:
        hb_ref, hbs_ref, sqx_ref, sqxs_ref = refs[:4]
        del refs[:4]
    wb_ref, stage_ref, sem = refs
    j, i = pl.program_id(0), pl.program_id(1)
    nj, ni = pl.num_programs(0), pl.num_programs(1)
    kdim = w_hbm.shape[0]
    n_kc = kdim // kc
    step = j * ni + i
    cur = 0 if single else lax.rem(j, 2)
    par = lax.rem(step, 2)
    jg = j + col0

    def aligned(x, m):
        return x * m if isinstance(x, int) else pl.multiple_of(x * m, m)

    def chunk_copy(tile, c, slot):
        return pltpu.make_async_copy(
            w_hbm.at[pl.ds(aligned(c, kc), kc), pl.ds(aligned(tile, tn), tn)],
            stage_ref.at[slot], sem.at[slot])

    def cast_chunk(slot, c, half):
        rows = pl.ds(aligned(c, kc), kc)
        w = stage_ref[slot]
        if has_scale:
            w = w * jnp.concatenate([gw_ref[rows, :]] * (tn // LANES), axis=1)
        wb_ref[half, rows, :] = w.astype(BF16)

    def next_tile(jj):
        return jnp.minimum(jj + 1, nj - 1) + col0

    def forget_gate(x):
        lb = _lower_bound(lb_ref[...])
        return lb + (1.0 - lb) * _sigmoid(x)

    def identity(x):
        return x

    if sections == "hgrn":
        quarter = nj_total // 4
        gates = [_silu, forget_gate, identity]
        gate_id = jnp.where((jg < quarter) | (jg >= 3 * quarter), 0,
                            jnp.where(jg < 2 * quarter, 1, 2))
    elif sections == "swa":
        q_tiles = SW_HEADS * SW_HD // tn
        gates = [lambda x: x * (SW_HD ** -0.5 * LOG2E), identity, _silu]
        gate_id = jnp.where(jg < q_tiles, 0, jnp.where(jg == q_tiles, 1, 2))
    else:
        gates, gate_id = [identity], None

    def per_section(body):
        if gate_id is None:
            body(gates[0])
        else:
            lax.cond(gate_id == 0, lambda: body(gates[0]),
                     lambda: lax.cond(gate_id == 1, lambda: body(gates[1]),
                                      lambda: body(gates[2])))

    def operand(ref, sq_in, hb, sqx):
        if raw_in:
            x = ref[...]
            ssq = jnp.sum(x * x, axis=1, keepdims=True)
            xb = x.astype(BF16)
            hb[...] = xb
            sqx[0] = ssq
            return xb, lax.rsqrt(ssq * (1.0 / kdim) + EPS)
        if has_scale:
            return ref[...], lax.rsqrt(jnp.sum(sq_in[...], axis=0) * (1.0 / kdim) + EPS)
        return ref[...], None

    def finish(x_and_factor, gate, res_ref, o, ob, sq):
        xb, factor = x_and_factor
        out = _nn(xb, wb_ref[cur])
        if has_scale:
            out = out * factor
        out = gate(out)
        if has_res:
            out = out + res_ref[...]
        o[...] = out
        if emit_stats:
            ob[...] = out.astype(BF16)
            sq[0] = jnp.sum(out * out, axis=1, keepdims=True)

    @pl.when(step == 0)
    def _():
        if not has_head:
            chunk_copy(col0, 0, 0).start()
            for c in range(n_kc):
                if c + 1 < n_kc:
                    chunk_copy(col0, c + 1, (c + 1) % 2).start()
                chunk_copy(col0, c, c % 2).wait()
                cast_chunk(c % 2, c, 0)
        if not single:
            for u in range(cps):
                chunk_copy(next_tile(0), u, u).start()

    if not single:
        for u in range(cps):
            chunk_copy(next_tile(j), i * cps + u, par * cps + u).wait()

        @pl.when(step + 1 < nj * ni)
        def _():
            wrap = i + 1 == ni
            j2 = jnp.where(wrap, j + 1, j)
            i2 = jnp.where(wrap, 0, i + 1)
            for u in range(cps):
                chunk_copy(next_tile(j2), i2 * cps + u, (1 - par) * cps + u).start()

    def small_rows():
        per_section(lambda gate: finish(
            operand(as_ref, sqs_in_ref if has_scale and not raw_in else None,
                    hbs_ref if raw_in else None, sqxs_ref if raw_in else None),
            gate, rs_ref if has_res else None, os_ref,
            obs_ref if emit_stats else None, sqs_ref if emit_stats else None))

    def cast_next():
        if not single:
            for u in range(cps):
                cast_chunk(par * cps + u, i * cps + u, 1 - cur)

    def main(gate):
        cast_next()
        finish(operand(a_ref, sq_in_ref if has_scale and not raw_in else None,
                       hb_ref if raw_in else None, sqx_ref if raw_in else None),
               gate, r_ref if has_res else None, o_ref,
               ob_ref if emit_stats else None, sq_ref if emit_stats else None)

    def copy_head():
        cast_next()
        tm = o_ref.shape[0]
        pltpu.sync_copy(head_ref.at[pl.ds(pl.multiple_of(i * tm, tm), tm), :], o_ref)

        @pl.when(i == 0)
        def _():
            pltpu.sync_copy(heads_ref, os_ref)

    if has_head:
        @pl.when((i == 0) & (j > 0))
        def _():
            small_rows()
        lax.cond(j == 0, copy_head, lambda: per_section(main))
    else:
        @pl.when(i == 0)
        def _():
            small_rows()
        per_section(main)


def _proj(a, a_small, w, res=None, res_small=None, scale=None, *, tm, tn, kc=256,
          emit_stats=False, sections=None, lbraw=None, head=None):
    m, k = a.shape
    n = w.shape[1]
    ms = a_small.shape[0]
    has_res, has_scale = res is not None, scale is not None
    raw_in = has_scale and scale[1] is None
    col0 = 0
    nj_total = n // tn
    nj = 1 if raw_in else nj_total
    n_out = nj * tn
    single = nj == 1
    ni, n_kc = m // tm, k // kc
    assert m % tm == 0 and n % tn == 0 and k % kc == 0 and n_kc % ni == 0
    cps = n_kc // ni
    in_specs = [pl.BlockSpec((tm, k), lambda j, i: (i, 0)),
                pl.BlockSpec((ms, k), lambda j, i: (0, 0)),
                pl.BlockSpec(memory_space=pl.ANY)]
    args = [a, a_small, w]
    if sections == "hgrn":
        quarter = nj_total // 4
        assert nj_total % 4 == 0 and lbraw.shape == (3, n // 4)
        in_specs += [pl.BlockSpec(
            (3, tn), lambda j, i: (0, jnp.clip(j + col0 - quarter, 0, quarter - 1)))]
        args += [lbraw]
    if sections == "swa":
        assert tn == 2 * SW_KV * SW_HD and (SW_HEADS * SW_HD) % tn == 0
    if has_scale:
        gain, sq, sq_small = scale
        in_specs += [pl.BlockSpec((k, LANES), lambda j, i: (0, 0))]
        args += [jnp.broadcast_to(gain.astype(F32)[:, None], (k, LANES))]
        if not raw_in:
            in_specs += [pl.BlockSpec((sq.shape[0], tm, 1), lambda j, i: (0, i, 0)),
                         pl.BlockSpec((sq.shape[0], ms, 1), lambda j, i: (0, 0, 0))]
            args += [sq, sq_small]
    if has_res:
        in_specs += [pl.BlockSpec((tm, tn), lambda j, i: (i, j + col0)),
                     pl.BlockSpec((ms, tn), lambda j, i: (0, j + col0))]
        args += [res, res_small]
    if head is not None:
        in_specs += [pl.BlockSpec(memory_space=pl.ANY), pl.BlockSpec(memory_space=pl.ANY)]
        args += list(head)
    out_specs = [pl.BlockSpec((tm, tn), lambda j, i: (i, j + col0)),
                 pl.BlockSpec((ms, tn), lambda j, i: (0, j + col0))]
    out_shape = [jax.ShapeDtypeStruct((m, n_out), F32), jax.ShapeDtypeStruct((ms, n_out), F32)]
    if emit_stats:
        out_specs += [pl.BlockSpec((tm, tn), lambda j, i: (i, j + col0)),
                      pl.BlockSpec((ms, tn), lambda j, i: (0, j + col0)),
                      pl.BlockSpec((1, tm, 1), lambda j, i: (j, i, 0)),
                      pl.BlockSpec((1, ms, 1), lambda j, i: (j, 0, 0))]
        out_shape += [jax.ShapeDtypeStruct((m, n), BF16), jax.ShapeDtypeStruct((ms, n), BF16),
                      jax.ShapeDtypeStruct((nj, m, 1), F32), jax.ShapeDtypeStruct((nj, ms, 1), F32)]
    if raw_in:
        assert single
        out_specs += [pl.BlockSpec((tm, k), lambda j, i: (i, 0)),
                      pl.BlockSpec((ms, k), lambda j, i: (0, 0)),
                      pl.BlockSpec((1, tm, 1), lambda j, i: (0, i, 0)),
                      pl.BlockSpec((1, ms, 1), lambda j, i: (0, 0, 0))]
        out_shape += [jax.ShapeDtypeStruct((m, k), BF16), jax.ShapeDtypeStruct((ms, k), BF16),
                      jax.ShapeDtypeStruct((1, m, 1), F32), jax.ShapeDtypeStruct((1, ms, 1), F32)]
    slots = 2 if single else 2 * cps
    return pl.pallas_call(
        functools.partial(_proj_kernel, tn=tn, kc=kc, cps=cps, col0=col0, nj_total=nj_total,
                          single=single, raw_in=raw_in, has_head=head is not None,
                          has_res=has_res, has_scale=has_scale, emit_stats=emit_stats,
                          sections=sections),
        grid=(nj, ni),
        in_specs=in_specs,
        out_specs=out_specs,
        out_shape=out_shape,
        scratch_shapes=[pltpu.VMEM((1 if single else 2, k, tn), BF16),
                        pltpu.VMEM((slots, kc, tn), F32),
                        pltpu.SemaphoreType.DMA((slots,))],
        compiler_params=pltpu.CompilerParams(
            dimension_semantics=("arbitrary", "arbitrary"),
            vmem_limit_bytes=V7X_VMEM_LIMIT),
        name="proj_res" if has_res else "proj",
    )(*args)


def _hgrn_consts():
    t = np.arange(CHUNK)[:, None]
    s = np.arange(CHUNK)[None, :]
    tri = s <= t
    ends = np.array([15, 47, 79, 111, 31, 95, 63, 127] + [-1] * 8)[:, None]
    mall = np.concatenate([tri, s <= ends], axis=0).astype(np.float32)
    lvl = np.full((CHUNK, CHUNK), 3, np.int32)
    lvl[(t >= 64) & (s < 64)] = 2
    lvl[(t // 64 == s // 64) & (t % 64 >= 32) & (s % 64 < 32)] = 1
    lvl[(t // 32 == s // 32) & (s <= t)] = 0
    return jnp.asarray(mall, BF16), jnp.asarray(lvl)


def _lower_bound(lbraw):
    mx = jnp.max(lbraw, axis=0, keepdims=True)
    e = jnp.exp(lbraw - mx)
    return e[0:1, :] / jnp.sum(e, axis=0, keepdims=True)


def _head_out(o, gate, onorm):
    ms = jnp.mean(o * o, axis=-1, keepdims=True)
    return (o * lax.rsqrt(ms + EPS) * onorm) * gate


def _hgrn_gates(q, f, mall, row_valid):
    logf = jnp.log2(f)
    k = 1.0 - f
    if row_valid is not None:
        logf = jnp.where(row_valid, logf, 0.0)
        k = jnp.where(row_valid, k, 0.0)

    hi = logf.astype(BF16)
    mid = (logf - hi.astype(F32)).astype(BF16)
    cs = _nn(mall, jnp.concatenate([hi, mid], axis=1))
    return q, k, cs[:, :LANES] + cs[:, LANES:]


def _hgrn_scores(q, k, cs):
    b = cs[0:CHUNK]
    ref = cs[CHUNK:CHUNK + 8]
    bl = ref[7:8]
    cat = lambda parts: jnp.concatenate(parts, axis=0)
    d0 = cat([b[32 * i:32 * i + 32] - ref[i:i + 1] for i in range(4)])
    d1 = [b[64 * i:64 * i + 64] - ref[4 + i:5 + i] for i in range(2)]
    d2 = b - ref[6:7]
    zero32 = jnp.zeros((32, LANES), BF16)
    a0 = _nt((q * jnp.exp2(d0)).astype(BF16), (k * jnp.exp2(-d0)).astype(BF16))
    q1 = cat([q[32:64] * jnp.exp2(d1[0][32:64]), q[96:128] * jnp.exp2(d1[1][32:64])])
    k1 = [(k[64 * i:64 * i + 32] * jnp.exp2(-d1[i][0:32])).astype(BF16) for i in range(2)]
    a1 = _nt(q1.astype(BF16), cat([k1[0], zero32, k1[1], zero32]))
    a2 = _nt((q[64:128] * jnp.exp2(d2[64:128])).astype(BF16),
             cat([(k[0:64] * jnp.exp2(-d2[0:64])).astype(BF16), zero32, zero32]))
    qe = (q * jnp.exp2(b)).astype(BF16)
    kd = (k * jnp.exp2(bl - b)).astype(BF16)
    return a0, a1, a2, qe, kd, jnp.exp2(bl)


def _hgrn_apply(a0, a1, a2, qe, kd, decay, v, st, lvl):
    is0, is1, is2 = lvl == 0, lvl == 1, lvl == 2
    a = jnp.concatenate([
        jnp.where(is0[0:32], a0[0:32], 0.0),
        jnp.where(is0[32:64], a0[32:64], jnp.where(is1[32:64], a1[0:32], 0.0)),
        jnp.where(is0[64:96], a0[64:96], jnp.where(is2[64:96], a2[0:32], 0.0)),
        jnp.where(is0[96:128], a0[96:128],
                  jnp.where(is1[96:128], a1[32:64], jnp.where(is2[96:128], a2[32:64], 0.0))),
    ], axis=0)
    vt = v.T.astype(BF16)
    lhs = jnp.concatenate([a.astype(BF16), qe], axis=1)
    rhs = jnp.concatenate([vt, st.astype(BF16)], axis=1)
    return _nt(lhs, rhs), st * decay + _nn(vt, kd)


def _hgrn_scan_kernel(q_ref, f_ref, v_ref, g_ref, on_ref, s0_ref, mall_ref, lvl_ref,
                      a_ref, s_ref, st_ref, *, hb, n_chunks, n_pad):
    t = pl.program_id(2)

    @pl.when(t == 0)
    def _():
        for h in range(hb):
            st_ref[h] = s0_ref[h].T

    row_valid = None
    if n_pad:
        row_valid = lax.broadcasted_iota(jnp.int32, (CHUNK, LANES), 0) >= n_pad
    heads = [slice(h * LANES, (h + 1) * LANES) for h in range(hb)]

    def body(c, carry):
        rows = pl.ds(pl.multiple_of(c * CHUNK, CHUNK), CHUNK)
        gates = [_hgrn_gates(q_ref[0, rows, cols], f_ref[0, rows, cols], mall_ref[...], row_valid)
                 for cols in heads]
        scores = [_hgrn_scores(*g) for g in gates]
        outs = [_hgrn_apply(*sc, v_ref[0, rows, cols], st_ref[h], lvl_ref[...])
                for h, (sc, cols) in enumerate(zip(scores, heads))]
        for h, ((o, st_new), cols) in enumerate(zip(outs, heads)):
            st_ref[h] = st_new
            a_ref[0, rows, cols] = _head_out(o, g_ref[0, rows, cols], on_ref[:, cols]).astype(BF16)
        return carry

    lax.fori_loop(0, n_chunks, body, 0)

    @pl.when(t == pl.num_programs(2) - 1)
    def _():
        for h in range(hb):
            s_ref[0, h] = st_ref[h].T


def _hgrn_scan(proj, onorm, s0, *, tb, hb, n_pad=0):
    bsz, tlen, _ = proj.shape
    mall, lvl = _hgrn_consts()
    hcols = HG_F // (hb * LANES)
    sect = lambda s: (lambda b, h, t: (b, t, s * hcols + h))
    blk = (1, tb, hb * LANES)
    return pl.pallas_call(
        functools.partial(_hgrn_scan_kernel, hb=hb, n_chunks=tb // CHUNK, n_pad=n_pad),
        grid=(bsz, HG_HEADS // hb, tlen // tb),
        in_specs=[pl.BlockSpec(blk, sect(0)), pl.BlockSpec(blk, sect(1)),
                  pl.BlockSpec(blk, sect(2)), pl.BlockSpec(blk, sect(3)),
                  pl.BlockSpec((1, hb * LANES), lambda b, h, t: (0, h)),
                  pl.BlockSpec((hb, HG_DK, HG_DK), lambda b, h, t: (h, 0, 0)),
                  pl.BlockSpec((CHUNK + 16, CHUNK), lambda b, h, t: (0, 0)),
                  pl.BlockSpec((CHUNK, CHUNK), lambda b, h, t: (0, 0))],
        out_specs=[pl.BlockSpec(blk, lambda b, h, t: (b, t, h)),
                   pl.BlockSpec((1, hb, HG_DK, HG_DK), lambda b, h, t: (b, h, 0, 0))],
        out_shape=[jax.ShapeDtypeStruct((bsz, tlen, D_MODEL), BF16),
                   jax.ShapeDtypeStruct((bsz, HG_HEADS, HG_DK, HG_DK), F32)],
        scratch_shapes=[pltpu.VMEM((hb, HG_DK, HG_DK), F32)],
        compiler_params=pltpu.CompilerParams(
            dimension_semantics=("parallel", "parallel", "arbitrary"),
            vmem_limit_bytes=V7X_VMEM_LIMIT),
        name="hgrn_scan",
    )(proj, proj, proj, proj, onorm.reshape(1, D_MODEL), s0, mall, lvl)


def _hgrn_step_kernel(q_ref, f_ref, v_ref, g_ref, on_ref, s0_ref, a_ref, s_ref, *, nb):
    def split(x):
        hi = x.astype(BF16).astype(F32)
        return hi, x - hi

    r = lax.broadcasted_iota(jnp.int32, (8, HG_DK), 0)
    ones_tail = jnp.concatenate([jnp.where((r == 3) | (r == 4), 1.0, 0.0),
                                 jnp.where((r == 5) | (r == 6), 1.0, 0.0)], axis=1).astype(BF16)
    k_hi_rows, k_lo_rows = r < 2, (r == 2) | (r == 7)
    v_hi_rows, v_lo_rows = (r == 0) | (r == 2), (r == 1) | (r == 7)
    for bi in range(nb):
        f = f_ref[bi]
        decay = jnp.exp(jnp.log(f))
        k = 1.0 - f
        (kh, kl), (dh, dl), (qh, ql), (vh, vl) = (split(k), split(decay), split(q_ref[bi]),
                                                  split(v_ref[bi]))
        rows = []
        for h in range(HG_HEADS):
            row = lambda x: jnp.broadcast_to(x[h:h + 1, :], (8, HG_DK))
            lhs = jnp.where(k_hi_rows, row(kh), jnp.where(k_lo_rows, row(kl), jnp.where(
                r == 3, row(dh), jnp.where(r == 4, row(dl), jnp.where(
                    r == 5, row(qh), row(ql))))))
            rhs_v = jnp.where(v_hi_rows, row(vh), jnp.where(v_lo_rows, row(vl), 0.0))
            rhs = jnp.concatenate([rhs_v.astype(BF16), ones_tail], axis=1)
            out = lax.dot_general(lhs.astype(BF16), rhs, (((0,), (0,)), ((), ())),
                                  preferred_element_type=F32)
            s_new = out[:, HG_DK:2 * HG_DK] * s0_ref[0, bi, h] + out[:, :HG_DK]
            s_ref[0, bi, h] = s_new
            rows.append(jnp.sum(out[:, 2 * HG_DK:] * s_new, axis=0, keepdims=True))
        o = jnp.concatenate(rows, axis=0)
        a_ref[bi] = _head_out(o, g_ref[bi], on_ref[...]).astype(BF16)


def _hgrn_step(q, f, v, g, onorm, state, *, nb):
    bsz = q.shape[0]
    vec = pl.BlockSpec((nb, HG_HEADS, HG_DK), lambda b: (b, 0, 0))
    full = pl.BlockSpec((HG_HEADS, HG_DK), lambda b: (0, 0))
    sspec = pl.BlockSpec((1, nb, HG_HEADS, HG_DK, HG_DK), lambda b: (0, b, 0, 0, 0))
    return pl.pallas_call(
        functools.partial(_hgrn_step_kernel, nb=nb),
        grid=(bsz // nb,),
        in_specs=[vec, vec, vec, vec, full, sspec],
        out_specs=[vec, sspec],
        out_shape=[jax.ShapeDtypeStruct((bsz, HG_HEADS, HG_DK), BF16),
                   jax.ShapeDtypeStruct(state.shape, state.dtype)],
        compiler_params=pltpu.CompilerParams(dimension_semantics=("parallel",),
                                             vmem_limit_bytes=V7X_VMEM_LIMIT),
        name="hgrn_step",
    )(q, f, v, g, onorm.reshape(HG_HEADS, HG_DK), state)


def _t5_bucket(dist):
    max_exact = REL_BUCKETS // 2
    d = jnp.maximum(dist, 1).astype(F32)
    large = max_exact + (jnp.log(d / max_exact) / math.log(REL_MAX_DIST / max_exact)
                         * (REL_BUCKETS - max_exact)).astype(jnp.int32)
    large = jnp.minimum(large, REL_BUCKETS - 1)
    return jnp.where(dist < max_exact, dist, large)


def _prompt_bias_kernel(w_ref, o_ref):
    band = 2 * WINDOW
    key = lax.broadcasted_iota(jnp.int32, (band, WINDOW), 0)
    for g in range(SW_GROUP):
        j, half = divmod(g, 2)
        row = jnp.broadcast_to(w_ref[g:g + 1, :], (band, 3 * WINDOW))
        toep = pltpu.roll(row, 0, 1, stride=1, stride_axis=0)[:, :WINDOW]
        dst = (slice(half * band, (half + 1) * band), slice(j * WINDOW, (j + 1) * WINDOW))
        o_ref[1, 0, dst[0], dst[1]] = toep
        o_ref[0, 0, dst[0], dst[1]] = jnp.where(key < WINDOW - N_META, NEG * LOG2E, toep)


def _prompt_bias(table):
    band = 2 * WINDOW
    i = jnp.arange(3 * WINDOW)
    dist = jnp.where(i < WINDOW, i + WINDOW, i - band)
    vals = table.astype(F32)[_t5_bucket(jnp.maximum(dist, 0))]
    vals = jnp.where(((dist >= 0) & (dist <= WINDOW))[:, None], vals, NEG) * LOG2E
    return pl.pallas_call(
        _prompt_bias_kernel,
        grid=(SW_KV,),
        in_specs=[pl.BlockSpec((SW_GROUP, 3 * WINDOW), lambda n: (n, 0))],
        out_specs=pl.BlockSpec((2, 1, 2 * band, 4 * WINDOW), lambda n: (0, n, 0, 0)),
        out_shape=jax.ShapeDtypeStruct((2, SW_KV, 2 * band, 4 * WINDOW), F32),
        compiler_params=pltpu.CompilerParams(dimension_semantics=("parallel",)),
        name="prompt_bias",
    )(vals.T)


def _expand_band(pair, u):
    lane = lax.broadcasted_iota(jnp.int32, pair.shape, 1)
    rolled = pltpu.roll(pair, SW_HD, axis=1)
    lo_src, hi_src = (pair, rolled) if u == 0 else (rolled, pair)
    top = jnp.where(lane < SW_HD, lo_src, 0.0)
    bot = jnp.where(lane >= SW_HD, hi_src, 0.0)
    return jnp.concatenate([top, bot], axis=0).astype(BF16)


def _swa_probs(s, sinks):
    band = 2 * WINDOW
    ps, ms = [], []
    for half in range(2):
        sh = s[half * band:(half + 1) * band]
        m = jnp.maximum(jnp.max(sh, axis=0, keepdims=True), sinks[half])
        ps.append(jnp.exp2(sh - m).astype(BF16))
        ms.append(m)
    return jnp.concatenate(ps, axis=0), ms


def _swa_values(pt, vxt, ms, sinks):
    ot = _nn(vxt, pt)
    outs = []
    for half in range(2):
        den = ot[2 * SW_HD + 8 * half:2 * SW_HD + 8 * half + 1] + jnp.exp2(sinks[half] - ms[half])
        outs.append(ot[SW_HD * half:SW_HD * (half + 1)] * (1.0 / den))
    return jnp.concatenate(outs, axis=0)


def _swa_prompt_kernel(sink_ref, q_ref, g_ref, k_ref, v_ref, km_ref, vm_ref, bias_ref,
                       o_ref, *, n_blocks):
    p = pl.program_id(0)
    qb = pl.program_id(2)
    n_pairs = SW_GROUP // 2
    band = 2 * WINDOW

    def sink_rows(u):
        return [jnp.concatenate(
            [jnp.full((1, WINDOW), sink_ref[(p * 2 + u) * SW_GROUP + 2 * j + half] * LOG2E, F32)
             for j in range(n_pairs)], axis=1) for half in range(2)]

    sinks = [sink_rows(u) for u in range(2)]
    orow = lax.broadcasted_iota(jnp.int32, (16, 2 * band), 0)
    ocol = lax.broadcasted_iota(jnp.int32, (16, 2 * band), 1)
    ones_rows = jnp.where((orow < 8) == (ocol < band), 1.0, 0.0).astype(BF16)
    zeros = jnp.zeros((SW_HD, band), BF16)

    def band_of(blk):
        tok = qb * n_blocks + blk
        cur = pl.ds(pl.multiple_of(tok * WINDOW, WINDOW), WINDOW)
        prev = pl.ds(pl.multiple_of(jnp.maximum(tok - 1, 0) * WINDOW, WINDOW), WINDOW)
        is_first = tok == 0
        kband = jnp.concatenate([jnp.where(is_first, km_ref[...], k_ref[0, prev, :]),
                                 k_ref[0, cur, :]], axis=0)
        vband = jnp.concatenate([jnp.where(is_first, vm_ref[...], v_ref[0, prev, :]),
                                 v_ref[0, cur, :]], axis=0)
        return (pl.ds(pl.multiple_of(blk * WINDOW, WINDOW), WINDOW), kband,
                vband.T.astype(BF16), jnp.where(is_first, 0, 1))

    def blocks(it, carry):
        bands = [band_of(it * SWA_BLOCKS_PER_ITER + i) for i in range(SWA_BLOCKS_PER_ITER)]
        units = [(bnd, u) for bnd in bands for u in range(2)]
        scores = []
        for (rows, kband, _, bias_sel), u in units:
            base = u * SW_GROUP * SW_HD
            q4 = jnp.concatenate(
                [q_ref[0, rows, base + j * LANES:base + (j + 1) * LANES] for j in range(n_pairs)],
                axis=0).astype(BF16)
            scores.append(_nt(_expand_band(kband, u), q4) + bias_ref[bias_sel, u])
        probs = [_swa_probs(s, sinks[u]) for s, (_, u) in zip(scores, units)]
        outs = []
        for (pt, ms), ((_, _, vbt, _), u) in zip(probs, units):
            vt = vbt[u * SW_HD:(u + 1) * SW_HD]
            vxt = jnp.concatenate([jnp.concatenate([vt, zeros], axis=1),
                                   jnp.concatenate([zeros, vt], axis=1), ones_rows], axis=0)
            outs.append(_swa_values(pt, vxt, ms, sinks[u]))
        for ot, ((rows, _, _, _), u) in zip(outs, units):
            base = u * SW_GROUP * SW_HD
            o = jnp.concatenate([ot[:, j * WINDOW:(j + 1) * WINDOW].T for j in range(n_pairs)],
                                axis=1)
            gate = g_ref[0, rows, base:base + SW_GROUP * SW_HD]
            o_ref[0, rows, base:base + SW_GROUP * SW_HD] = (o * gate).astype(BF16)
        return carry

    lax.fori_loop(0, n_blocks // SWA_BLOCKS_PER_ITER, blocks, 0)


def _swa_prompt(proj, meta_k, meta_v, sinks, bias, *, tq):
    bsz, tlen, _ = proj.shape
    pw = 2 * SW_GROUP * SW_HD
    qblocks = SW_HEADS * SW_HD // pw
    kcol0 = SW_HEADS * SW_HD // LANES
    vcol0 = kcol0 + SW_KV * SW_HD // LANES
    gblk0 = (SW_HEADS * SW_HD + 2 * SW_KV * SW_HD) // pw
    return pl.pallas_call(
        functools.partial(_swa_prompt_kernel, n_blocks=tq // WINDOW),
        grid=(qblocks, bsz, tlen // tq),
        in_specs=[pl.BlockSpec(memory_space=pltpu.SMEM),
                  pl.BlockSpec((1, tq, pw), lambda p, b, t: (b, t, p)),
                  pl.BlockSpec((1, tq, pw), lambda p, b, t: (b, t, gblk0 + p)),
                  pl.BlockSpec((1, tlen, LANES), lambda p, b, t: (b, 0, kcol0 + p)),
                  pl.BlockSpec((1, tlen, LANES), lambda p, b, t: (b, 0, vcol0 + p)),
                  pl.BlockSpec((WINDOW, LANES), lambda p, b, t: (0, p)),
                  pl.BlockSpec((WINDOW, LANES), lambda p, b, t: (0, p)),
                  pl.BlockSpec((2, 2, 4 * WINDOW, 4 * WINDOW), lambda p, b, t: (0, p, 0, 0))],
        out_specs=pl.BlockSpec((1, tq, pw), lambda p, b, t: (b, t, p)),
        out_shape=jax.ShapeDtypeStruct((bsz, tlen, D_MODEL), BF16),
        compiler_params=pltpu.CompilerParams(
            dimension_semantics=("parallel", "parallel", "arbitrary"),
            vmem_limit_bytes=V7X_VMEM_LIMIT),
        name="swa_prompt",
    )(sinks, proj, proj, proj, proj, meta_k, meta_v, bias)


def _swa_sample_kernel(q_ref, g_ref, kn_ref, vn_ref, ck_ref, cv_ref, bc_ref, bn_ref, sink_ref,
                       o_ref, nk_ref, nv_ref, *, nb):
    nkv = SW_KV * SW_HD
    r = ck_ref.shape[1]
    row = lax.broadcasted_iota(jnp.int32, (SW_HEADS, nkv), 0)
    col = lax.broadcasted_iota(jnp.int32, (SW_HEADS, nkv), 1)
    own = (row // SW_GROUP) == (col // SW_HD)
    last = lax.broadcasted_iota(jnp.int32, (r, nkv), 0) == r - 1
    sink = sink_ref[...] * LOG2E
    for bi in range(nb):
        q = q_ref[bi]
        qx = jnp.where(own, jnp.concatenate([q] * SW_KV, axis=1), 0.0)
        ck, cv = ck_ref[bi], cv_ref[bi]
        kn, vn = kn_ref[bi], vn_ref[bi]

        s_c = _nt(qx.astype(BF16), ck.astype(BF16)) + bc_ref[...]
        s_n = jnp.sum(qx * kn, axis=-1, keepdims=True) + bn_ref[...]
        m = jnp.maximum(jnp.maximum(jnp.max(s_c, axis=-1, keepdims=True), s_n), sink)
        p_c = jnp.exp2(s_c - m)
        p_n = jnp.exp2(s_n - m)
        den = jnp.sum(p_c, axis=-1, keepdims=True) + p_n + jnp.exp2(sink - m)
        o_all = _nn(p_c.astype(BF16), cv.astype(BF16))
        o_all = o_all + p_n * vn
        o_all = jnp.where(own, o_all, 0.0)
        o = o_all[:, 0:SW_HD]
        for n in range(1, SW_KV):
            o = o + o_all[:, n * SW_HD:(n + 1) * SW_HD]
        o_ref[bi] = ((o / den) * g_ref[bi]).astype(BF16)

        nk_ref[bi] = jnp.where(last, kn, pltpu.roll(ck, r - 1, axis=0))
        nv_ref[bi] = jnp.where(last, vn, pltpu.roll(cv, r - 1, axis=0))


def _swa_sample(q, g, kn, vn, ck, cv, bias_c, bias_n, sinks, *, nb):
    bsz, r, nkv = ck.shape
    head = pl.BlockSpec((nb, SW_HEADS, SW_HD), lambda b: (b, 0, 0))
    new = pl.BlockSpec((nb, 1, nkv), lambda b: (b, 0, 0))
    cache = pl.BlockSpec((nb, r, nkv), lambda b: (b, 0, 0))
    return pl.pallas_call(
        functools.partial(_swa_sample_kernel, nb=nb),
        grid=(bsz // nb,),
        in_specs=[head, head, new, new, cache, cache,
                  pl.BlockSpec((SW_HEADS, r), lambda b: (0, 0)),
                  pl.BlockSpec((SW_HEADS, 1), lambda b: (0, 0)),
                  pl.BlockSpec((SW_HEADS, 1), lambda b: (0, 0))],
        out_specs=[head, cache, cache],
        out_shape=[jax.ShapeDtypeStruct((bsz, SW_HEADS, SW_HD), BF16),
                   jax.ShapeDtypeStruct(ck.shape, ck.dtype),
                   jax.ShapeDtypeStruct(cv.shape, cv.dtype)],
        compiler_params=pltpu.CompilerParams(dimension_semantics=("parallel",)),
        name="swa_sample",
    )(q, g, kn, vn, ck, cv, bias_c, bias_n, sinks.reshape(SW_HEADS, 1))


def kernel(x_prompt, x_sample, state_hgrn, cache_k_win, cache_v_win, meta_tokens, rel_bias,
           hg_lower_bounds, hg_norm, hg_w_in, hg_onorm, hg_w_out,
           sw_norm, sw_w_in, sw_sinks, sw_w_out, final_norm):
    n_samp = x_sample.shape[0]
    samp = slice(N_META, N_META + n_samp)
    x_main = x_prompt.reshape(MAIN_ROWS, D_MODEL)
    x_small = jnp.concatenate(
        [meta_tokens.astype(F32), x_sample.reshape(n_samp, D_MODEL),
         jnp.zeros((N_SMALL - N_META - n_samp, D_MODEL), F32)], axis=0)

    lbraw = hg_lower_bounds.astype(F32)
    p_head, p_head_small, xb_main, xb_small, sqx_main, sqx_small = _proj(
        x_main, x_small, hg_w_in[0], scale=(hg_norm[0], None, None), tm=PROJ_RES_TM, tn=PROJ_TN,
        sections="hgrn", lbraw=lbraw)
    p_main, p_small = _proj(xb_main, xb_small, hg_w_in[0],
                            scale=(hg_norm[0], sqx_main, sqx_small), tm=PROJ_TM, tn=PROJ_TN,
                            sections="hgrn", lbraw=lbraw, head=(p_head, p_head_small))

    meta_proj = jnp.pad(p_small[:N_META], ((CHUNK - N_META, 0), (0, 0)))[None]
    zero_state = jnp.zeros((HG_HEADS, HG_DK, HG_DK), F32)
    a_meta, s_meta = _hgrn_scan(meta_proj, hg_onorm[0], zero_state,
                                tb=CHUNK, hb=8, n_pad=CHUNK - N_META)
    a_main, s_prompt = _hgrn_scan(p_main.reshape(BATCH, SEQ, -1), hg_onorm[0],
                                  s_meta[0], tb=1024, hb=8)
    sect = lambda s: p_small[samp, s * HG_F:(s + 1) * HG_F].reshape(n_samp, HG_HEADS, HG_DK)
    a_samp, s_sample = _hgrn_step(sect(0), sect(1), sect(2), sect(3),
                                  hg_onorm[0], state_hgrn, nb=SAMPLES_PER_STEP)
    a_small = jnp.concatenate(
        [a_meta[0, CHUNK - N_META:], a_samp.reshape(n_samp, D_MODEL),
         jnp.zeros((N_SMALL - N_META - n_samp, D_MODEL), BF16)], axis=0)
    x1_main, x1_small, x1b_main, x1b_small, sq_main, sq_small = _proj(
        a_main.reshape(MAIN_ROWS, D_MODEL), a_small, hg_w_out[0], x_main, x_small,
        tm=PROJ_RES_TM, tn=PROJ_TN, emit_stats=True)

    p_main, p_small = _proj(x1b_main, x1b_small, sw_w_in[0],
                            scale=(sw_norm[0], jnp.sum(sq_main, axis=0, keepdims=True),
                                   jnp.sum(sq_small, axis=0, keepdims=True)),
                            tm=PROJ_TM, tn=PROJ_TN, sections="swa")
    nq, nkv = SW_HEADS * SW_HD, SW_KV * SW_HD
    kv_tail = p_main.reshape(BATCH, SEQ, -1)[:, -WINDOW:, nq:nq + 2 * nkv]
    k_tail = kv_tail[:, :, :nkv].reshape(1, BATCH, WINDOW, SW_KV, SW_HD)
    v_tail = kv_tail[:, :, nkv:].reshape(1, BATCH, WINDOW, SW_KV, SW_HD)
    meta_kv = jnp.pad(p_small[:N_META, nq:nq + 2 * nkv], ((WINDOW - N_META, 0), (0, 0)))
    a_main = _swa_prompt(p_main.reshape(BATCH, SEQ, -1), meta_kv[:, :nkv], meta_kv[:, nkv:],
                         sw_sinks[0], _prompt_bias(rel_bias), tq=1024)

    r = cache_k_win.shape[2]
    table = rel_bias.astype(F32)
    bias_c = table[_t5_bucket(r - jnp.arange(r))].T * LOG2E
    bias_n = table[_t5_bucket(jnp.zeros((1,), jnp.int32))].T * LOG2E
    q_s = p_small[samp, :nq].reshape(n_samp, SW_HEADS, SW_HD)
    g_s = p_small[samp, nq + 2 * nkv:].reshape(n_samp, SW_HEADS, SW_HD)
    a_samp, k_samp, v_samp = _swa_sample(
        q_s, g_s, p_small[samp, nq:nq + nkv].reshape(n_samp, 1, nkv),
        p_small[samp, nq + nkv:nq + 2 * nkv].reshape(n_samp, 1, nkv),
        cache_k_win[0].reshape(n_samp, r, nkv), cache_v_win[0].reshape(n_samp, r, nkv),
        bias_c, bias_n, sw_sinks[0], nb=SAMPLES_PER_STEP)
    a_small = jnp.concatenate(
        [jnp.zeros((N_META, D_MODEL), BF16), a_samp.reshape(n_samp, D_MODEL),
         jnp.zeros((N_SMALL - N_META - n_samp, D_MODEL), BF16)], axis=0)
    x2_main, x2_small = _proj(a_main.reshape(MAIN_ROWS, D_MODEL), a_small, sw_w_out[0],
                              x1_main, x1_small, tm=PROJ_RES_TM, tn=PROJ_TN)

    y_prompt = _rmsnorm(x2_main, final_norm, F32, 512).reshape(BATCH, SEQ, D_MODEL)
    y_sample = _rmsnorm(x2_small, final_norm, F32, N_SMALL)[samp].reshape(n_samp, 1, D_MODEL)
    return (y_prompt, y_sample, s_prompt[None],
            k_tail.astype(cache_k_win.dtype), v_tail.astype(cache_v_win.dtype),
            s_sample,
            k_samp.reshape(1, n_samp, r, SW_KV, SW_HD), v_samp.reshape(1, n_samp, r, SW_KV, SW_HD))
```

```python
import functools
import math

import numpy as np
import jax
import jax.numpy as jnp
from jax import lax
from jax.experimental import pallas as pl
from jax.experimental.pallas import tpu as pltpu

D_MODEL = 4096
BATCH = 4
SEQ = 2048
DEC_BATCH = 32
N_META = 16
HG_HEADS = 32
HG_DK = 128
HG_F = HG_HEADS * HG_DK
SW_HEADS = 64
SW_KV = 8
SW_HD = 64
SW_GROUP = SW_HEADS // SW_KV
WINDOW = 128
REL_BUCKETS = 32
REL_MAX_DIST = 128
EPS = 1e-6
NEG = -1e30
LOG2E = math.log2(math.e)

LANES = 128
V7X_VMEM_LIMIT = 58 * 1024 * 1024

PROJ_TM = 1024
PROJ_TN = 1024
PROJ_RES_TM = 512
CHUNK = 128
SWA_BLOCKS_PER_ITER = 4
SAMPLES_PER_STEP = 4
N_SMALL = 64
MAIN_ROWS = BATCH * SEQ

F32 = jnp.float32
BF16 = jnp.bfloat16


def _nt(a, b):
    return lax.dot_general(a, b, (((1,), (1,)), ((), ())), preferred_element_type=F32)


def _nn(a, b):
    return jnp.dot(a, b, preferred_element_type=F32)


def _sigmoid(x):
    return 1.0 / (1.0 + jnp.exp(-x))


def _silu(x):
    h = 0.5 * x
    return h + h * jnp.tanh(h)


def _rmsnorm_kernel(x_ref, g_ref, o_ref):
    x = x_ref[...]
    ms = jnp.mean(x * x, axis=-1, keepdims=True)
    o_ref[...] = (x * lax.rsqrt(ms + EPS) * g_ref[...]).astype(o_ref.dtype)


def _rmsnorm(x, g, out_dtype, tm):
    m, d = x.shape
    return pl.pallas_call(
        _rmsnorm_kernel,
        grid=(m // tm,),
        in_specs=[pl.BlockSpec((tm, d), lambda i: (i, 0)),
                  pl.BlockSpec((1, d), lambda i: (0, 0))],
        out_specs=pl.BlockSpec((tm, d), lambda i: (i, 0)),
        out_shape=jax.ShapeDtypeStruct((m, d), out_dtype),
        compiler_params=pltpu.CompilerParams(dimension_semantics=("parallel",),
                                             vmem_limit_bytes=V7X_VMEM_LIMIT),
        name="rmsnorm",
    )(x, g.reshape(1, d))


def _proj_kernel(*refs, tn, kc, cps, has_res, has_scale, emit_stats, sections):
    refs = list(refs)
    a_ref, as_ref, w_hbm = refs[:3]
    del refs[:3]
    if sections == "hgrn":
        lb_ref = refs.pop(0)
    if has_scale:
        gw_ref, sq_in_ref, sqs_in_ref = refs[:3]
        del refs[:3]
    if has_res:
        r_ref, rs_ref = refs[:2]
        del refs[:2]
    o_ref, os_ref = refs[:2]
    del refs[:2]
    if emit_stats:
        ob_ref, obs_ref, sq_ref, sqs_ref = refs[:4]
        del refs[:4]
    wb_ref, stage_ref, sem = refs
    j, i = pl.program_id(0), pl.program_id(1)
    nj, ni = pl.num_programs(0), pl.num_programs(1)
    kdim = w_hbm.shape[0]
    n_kc = kdim // kc
    step = j * ni + i
    cur = lax.rem(j, 2)
    par = lax.rem(step, 2)

    def aligned(x, m):
        return x * m if isinstance(x, int) else pl.multiple_of(x * m, m)

    def chunk_copy(tile, c, slot):
        return pltpu.make_async_copy(
            w_hbm.at[pl.ds(aligned(c, kc), kc), pl.ds(aligned(tile, tn), tn)],
            stage_ref.at[slot], sem.at[slot])

    def cast_chunk(slot, c, half):
        rows = pl.ds(aligned(c, kc), kc)
        w = stage_ref[slot]
        if has_scale:
            w = w * jnp.concatenate([gw_ref[rows, :]] * (tn // LANES), axis=1)
        wb_ref[half, rows, :] = w.astype(BF16)

    def next_tile(jj):
        return jnp.minimum(jj + 1, nj - 1)

    def forget_gate(x):
        lb = _lower_bound(lb_ref[...])
        return lb + (1.0 - lb) * _sigmoid(x)

    def identity(x):
        return x

    if sections == "hgrn":
        quarter = nj // 4
        gates = [_silu, forget_gate, identity]
        gate_id = jnp.where((j < quarter) | (j >= 3 * quarter), 0,
                            jnp.where(j < 2 * quarter, 1, 2))
    elif sections == "swa":
        q_tiles = SW_HEADS * SW_HD // tn
        gates = [lambda x: x * (SW_HD ** -0.5 * LOG2E), identity, _silu]
        gate_id = jnp.where(j < q_tiles, 0, jnp.where(j == q_tiles, 1, 2))
    else:
        gates, gate_id = [identity], None

    def per_section(body):
        if gate_id is None:
            body(gates[0])
        else:
            lax.cond(gate_id == 0, lambda: body(gates[0]),
                     lambda: lax.cond(gate_id == 1, lambda: body(gates[1]),
                                      lambda: body(gates[2])))

    def finish(out, gate, res_ref, sq_in, o, ob, sq):
        if has_scale:
            out = out * lax.rsqrt(jnp.sum(sq_in[...], axis=0) * (1.0 / kdim) + EPS)
        out = gate(out)
        if has_res:
            out = out + res_ref[...]
        o[...] = out
        if emit_stats:
            ob[...] = out.astype(BF16)
            sq[0] = jnp.sum(out * out, axis=1, keepdims=True)

    @pl.when(step == 0)
    def _():
        chunk_copy(0, 0, 0).start()
        for c in range(n_kc):
            if c + 1 < n_kc:
                chunk_copy(0, c + 1, (c + 1) % 2).start()
            chunk_copy(0, c, c % 2).wait()
            cast_chunk(c % 2, c, 0)
        for u in range(cps):
            chunk_copy(next_tile(0), u, u).start()

    for u in range(cps):
        chunk_copy(next_tile(j), i * cps + u, par * cps + u).wait()

    @pl.when(step + 1 < nj * ni)
    def _():
        wrap = i + 1 == ni
        j2 = jnp.where(wrap, j + 1, j)
        i2 = jnp.where(wrap, 0, i + 1)
        for u in range(cps):
            chunk_copy(next_tile(j2), i2 * cps + u, (1 - par) * cps + u).start()

    @pl.when(i == 0)
    def _():
        per_section(lambda gate: finish(
            _nn(as_ref[...], wb_ref[cur]), gate, rs_ref if has_res else None,
            sqs_in_ref if has_scale else None, os_ref,
            obs_ref if emit_stats else None, sqs_ref if emit_stats else None))

    def main(gate):
        for u in range(cps):
            cast_chunk(par * cps + u, i * cps + u, 1 - cur)
        finish(_nn(a_ref[...], wb_ref[cur]), gate, r_ref if has_res else None,
               sq_in_ref if has_scale else None, o_ref,
               ob_ref if emit_stats else None, sq_ref if emit_stats else None)

    per_section(main)


def _proj(a, a_small, w, res=None, res_small=None, scale=None, *, tm, tn, kc=256,
          emit_stats=False, sections=None, lbraw=None):
    m, k = a.shape
    n = w.shape[1]
    ms = a_small.shape[0]
    has_res, has_scale = res is not None, scale is not None
    ni, nj, n_kc = m // tm, n // tn, k // kc
    assert m % tm == 0 and n % tn == 0 and k % kc == 0 and n_kc % ni == 0 and nj >= 2
    cps = n_kc // ni
    in_specs = [pl.BlockSpec((tm, k), lambda j, i: (i, 0)),
                pl.BlockSpec((ms, k), lambda j, i: (0, 0)),
                pl.BlockSpec(memory_space=pl.ANY)]
    args = [a, a_small, w]
    if sections == "hgrn":
        quarter = nj // 4
        assert nj % 4 == 0 and lbraw.shape == (3, n // 4)
        in_specs += [pl.BlockSpec((3, tn), lambda j, i: (0, jnp.clip(j - quarter, 0, quarter - 1)))]
        args += [lbraw]
    if sections == "swa":
        assert tn == 2 * SW_KV * SW_HD and (SW_HEADS * SW_HD) % tn == 0
    if has_scale:
        gain, sq, sq_small = scale
        in_specs += [pl.BlockSpec((k, LANES), lambda j, i: (0, 0)),
                     pl.BlockSpec((sq.shape[0], tm, 1), lambda j, i: (0, i, 0)),
                     pl.BlockSpec((sq.shape[0], ms, 1), lambda j, i: (0, 0, 0))]
        args += [jnp.broadcast_to(gain.astype(F32)[:, None], (k, LANES)), sq, sq_small]
    if has_res:
        in_specs += [pl.BlockSpec((tm, tn), lambda j, i: (i, j)),
                     pl.BlockSpec((ms, tn), lambda j, i: (0, j))]
        args += [res, res_small]
    out_specs = [pl.BlockSpec((tm, tn), lambda j, i: (i, j)),
                 pl.BlockSpec((ms, tn), lambda j, i: (0, j))]
    out_shape = [jax.ShapeDtypeStruct((m, n), F32), jax.ShapeDtypeStruct((ms, n), F32)]
    if emit_stats:
        out_specs += [pl.BlockSpec((tm, tn), lambda j, i: (i, j)),
                      pl.BlockSpec((ms, tn), lambda j, i: (0, j)),
                      pl.BlockSpec((1, tm, 1), lambda j, i: (j, i, 0)),
                      pl.BlockSpec((1, ms, 1), lambda j, i: (j, 0, 0))]
        out_shape += [jax.ShapeDtypeStruct((m, n), BF16), jax.ShapeDtypeStruct((ms, n), BF16),
                      jax.ShapeDtypeStruct((nj, m, 1), F32), jax.ShapeDtypeStruct((nj, ms, 1), F32)]
    return pl.pallas_call(
        functools.partial(_proj_kernel, tn=tn, kc=kc, cps=cps, has_res=has_res,
                          has_scale=has_scale, emit_stats=emit_stats, sections=sections),
        grid=(nj, ni),
        in_specs=in_specs,
        out_specs=out_specs,
        out_shape=out_shape,
        scratch_shapes=[pltpu.VMEM((2, k, tn), BF16),
                        pltpu.VMEM((2 * cps, kc, tn), F32),
                        pltpu.SemaphoreType.DMA((2 * cps,))],
        compiler_params=pltpu.CompilerParams(
            dimension_semantics=("arbitrary", "arbitrary"),
            vmem_limit_bytes=V7X_VMEM_LIMIT),
        name="proj_res" if has_res else "proj",
    )(*args)


def _hgrn_consts():
    t = np.arange(CHUNK)[:, None]
    s = np.arange(CHUNK)[None, :]
    tri = s <= t
    ends = np.array([15, 47, 79, 111, 31, 95, 63, 127] + [-1] * 8)[:, None]
    mall = np.concatenate([tri, s <= ends], axis=0).astype(np.float32)
    lvl = np.full((CHUNK, CHUNK), 3, np.int32)
    lvl[(t >= 64) & (s < 64)] = 2
    lvl[(t // 64 == s // 64) & (t % 64 >= 32) & (s % 64 < 32)] = 1
    lvl[(t // 32 == s // 32) & (s <= t)] = 0
    return jnp.asarray(mall, BF16), jnp.asarray(lvl)


def _lower_bound(lbraw):
    mx = jnp.max(lbraw, axis=0, keepdims=True)
    e = jnp.exp(lbraw - mx)
    return e[0:1, :] / jnp.sum(e, axis=0, keepdims=True)


def _head_out(o, gate, onorm):
    ms = jnp.mean(o * o, axis=-1, keepdims=True)
    return (o * lax.rsqrt(ms + EPS) * onorm) * gate


def _hgrn_gates(q, f, mall, row_valid):
    logf = jnp.log2(f)
    k = 1.0 - f
    if row_valid is not None:
        logf = jnp.where(row_valid, logf, 0.0)
        k = jnp.where(row_valid, k, 0.0)

    hi = logf.astype(BF16)
    mid = (logf - hi.astype(F32)).astype(BF16)
    cs = _nn(mall, jnp.concatenate([hi, mid], axis=1))
    return q, k, cs[:, :LANES] + cs[:, LANES:]


def _hgrn_scores(q, k, cs):
    b = cs[0:CHUNK]
    ref = cs[CHUNK:CHUNK + 8]
    bl = ref[7:8]
    cat = lambda parts: jnp.concatenate(parts, axis=0)
    d0 = cat([b[32 * i:32 * i + 32] - ref[i:i + 1] for i in range(4)])
    d1 = [b[64 * i:64 * i + 64] - ref[4 + i:5 + i] for i in range(2)]
    d2 = b - ref[6:7]
    zero32 = jnp.zeros((32, LANES), BF16)
    a0 = _nt((q * jnp.exp2(d0)).astype(BF16), (k * jnp.exp2(-d0)).astype(BF16))
    q1 = cat([q[32:64] * jnp.exp2(d1[0][32:64]), q[96:128] * jnp.exp2(d1[1][32:64])])
    k1 = [(k[64 * i:64 * i + 32] * jnp.exp2(-d1[i][0:32])).astype(BF16) for i in range(2)]
    a1 = _nt(q1.astype(BF16), cat([k1[0], zero32, k1[1], zero32]))
    a2 = _nt((q[64:128] * jnp.exp2(d2[64:128])).astype(BF16),
             cat([(k[0:64] * jnp.exp2(-d2[0:64])).astype(BF16), zero32, zero32]))
    qe = (q * jnp.exp2(b)).astype(BF16)
    kd = (k * jnp.exp2(bl - b)).astype(BF16)
    return a0, a1, a2, qe, kd, jnp.exp2(bl)


def _hgrn_apply(a0, a1, a2, qe, kd, decay, v, st, lvl):
    is0, is1, is2 = lvl == 0, lvl == 1, lvl == 2
    a = jnp.concatenate([
        jnp.where(is0[0:32], a0[0:32], 0.0),
        jnp.where(is0[32:64], a0[32:64], jnp.where(is1[32:64], a1[0:32], 0.0)),
        jnp.where(is0[64:96], a0[64:96], jnp.where(is2[64:96], a2[0:32], 0.0)),
        jnp.where(is0[96:128], a0[96:128],
                  jnp.where(is1[96:128], a1[32:64], jnp.where(is2[96:128], a2[32:64], 0.0))),
    ], axis=0)
    vt = v.T.astype(BF16)
    lhs = jnp.concatenate([a.astype(BF16), qe], axis=1)
    rhs = jnp.concatenate([vt, st.astype(BF16)], axis=1)
    return _nt(lhs, rhs), st * decay + _nn(vt, kd)


def _hgrn_scan_kernel(q_ref, f_ref, v_ref, g_ref, on_ref, s0_ref, mall_ref, lvl_ref,
                      a_ref, s_ref, st_ref, *, hb, n_chunks, n_pad):
    t = pl.program_id(2)

    @pl.when(t == 0)
    def _():
        for h in range(hb):
            st_ref[h] = s0_ref[h].T

    row_valid = None
    if n_pad:
        row_valid = lax.broadcasted_iota(jnp.int32, (CHUNK, LANES), 0) >= n_pad
    heads = [slice(h * LANES, (h + 1) * LANES) for h in range(hb)]

    def body(c, carry):
        rows = pl.ds(pl.multiple_of(c * CHUNK, CHUNK), CHUNK)
        gates = [_hgrn_gates(q_ref[0, rows, cols], f_ref[0, rows, cols], mall_ref[...], row_valid)
                 for cols in heads]
        scores = [_hgrn_scores(*g) for g in gates]
        outs = [_hgrn_apply(*sc, v_ref[0, rows, cols], st_ref[h], lvl_ref[...])
                for h, (sc, cols) in enumerate(zip(scores, heads))]
        for h, ((o, st_new), cols) in enumerate(zip(outs, heads)):
            st_ref[h] = st_new
            a_ref[0, rows, cols] = _head_out(o, g_ref[0, rows, cols], on_ref[:, cols]).astype(BF16)
        return carry

    lax.fori_loop(0, n_chunks, body, 0)

    @pl.when(t == pl.num_programs(2) - 1)
    def _():
        for h in range(hb):
            s_ref[0, h] = st_ref[h].T


def _hgrn_scan(proj, onorm, s0, *, tb, hb, n_pad=0):
    bsz, tlen, _ = proj.shape
    mall, lvl = _hgrn_consts()
    hcols = HG_F // (hb * LANES)
    sect = lambda s: (lambda b, h, t: (b, t, s * hcols + h))
    blk = (1, tb, hb * LANES)
    return pl.pallas_call(
        functools.partial(_hgrn_scan_kernel, hb=hb, n_chunks=tb // CHUNK, n_pad=n_pad),
        grid=(bsz, HG_HEADS // hb, tlen // tb),
        in_specs=[pl.BlockSpec(blk, sect(0)), pl.BlockSpec(blk, sect(1)),
                  pl.BlockSpec(blk, sect(2)), pl.BlockSpec(blk, sect(3)),
                  pl.BlockSpec((1, hb * LANES), lambda b, h, t: (0, h)),
                  pl.BlockSpec((hb, HG_DK, HG_DK), lambda b, h, t: (h, 0, 0)),
                  pl.BlockSpec((CHUNK + 16, CHUNK), lambda b, h, t: (0, 0)),
                  pl.BlockSpec((CHUNK, CHUNK), lambda b, h, t: (0, 0))],
        out_specs=[pl.BlockSpec(blk, lambda b, h, t: (b, t, h)),
                   pl.BlockSpec((1, hb, HG_DK, HG_DK), lambda b, h, t: (b, h, 0, 0))],
        out_shape=[jax.ShapeDtypeStruct((bsz, tlen, D_MODEL), BF16),
                   jax.ShapeDtypeStruct((bsz, HG_HEADS, HG_DK, HG_DK), F32)],
        scratch_shapes=[pltpu.VMEM((hb, HG_DK, HG_DK), F32)],
        compiler_params=pltpu.CompilerParams(
            dimension_semantics=("parallel", "parallel", "arbitrary"),
            vmem_limit_bytes=V7X_VMEM_LIMIT),
        name="hgrn_scan",
    )(proj, proj, proj, proj, onorm.reshape(1, D_MODEL), s0, mall, lvl)


def _hgrn_step_kernel(q_ref, f_ref, v_ref, g_ref, on_ref, s0_ref, a_ref, s_ref, *, nb):
    def split(x):
        hi = x.astype(BF16).astype(F32)
        return hi, x - hi

    r = lax.broadcasted_iota(jnp.int32, (8, HG_DK), 0)
    ones_tail = jnp.concatenate([jnp.where((r == 3) | (r == 4), 1.0, 0.0),
                                 jnp.where((r == 5) | (r == 6), 1.0, 0.0)], axis=1).astype(BF16)
    k_hi_rows, k_lo_rows = r < 2, (r == 2) | (r == 7)
    v_hi_rows, v_lo_rows = (r == 0) | (r == 2), (r == 1) | (r == 7)
    for bi in range(nb):
        f = f_ref[bi]
        decay = jnp.exp(jnp.log(f))
        k = 1.0 - f
        (kh, kl), (dh, dl), (qh, ql), (vh, vl) = (split(k), split(decay), split(q_ref[bi]),
                                                  split(v_ref[bi]))
        rows = []
        for h in range(HG_HEADS):
            row = lambda x: jnp.broadcast_to(x[h:h + 1, :], (8, HG_DK))
            lhs = jnp.where(k_hi_rows, row(kh), jnp.where(k_lo_rows, row(kl), jnp.where(
                r == 3, row(dh), jnp.where(r == 4, row(dl), jnp.where(
                    r == 5, row(qh), row(ql))))))
            rhs_v = jnp.where(v_hi_rows, row(vh), jnp.where(v_lo_rows, row(vl), 0.0))
            rhs = jnp.concatenate([rhs_v.astype(BF16), ones_tail], axis=1)
            out = lax.dot_general(lhs.astype(BF16), rhs, (((0,), (0,)), ((), ())),
                                  preferred_element_type=F32)
            s_new = out[:, HG_DK:2 * HG_DK] * s0_ref[0, bi, h] + out[:, :HG_DK]
            s_ref[0, bi, h] = s_new
            rows.append(jnp.sum(out[:, 2 * HG_DK:] * s_new, axis=0, keepdims=True))
        o = jnp.concatenate(rows, axis=0)
        a_ref[bi] = _head_out(o, g_ref[bi], on_ref[...]).astype(BF16)


def _hgrn_step(q, f, v, g, onorm, state, *, nb):
    bsz = q.shape[0]
    vec = pl.BlockSpec((nb, HG_HEADS, HG_DK), lambda b: (b, 0, 0))
    full = pl.BlockSpec((HG_HEADS, HG_DK), lambda b: (0, 0))
    sspec = pl.BlockSpec((1, nb, HG_HEADS, HG_DK, HG_DK), lambda b: (0, b, 0, 0, 0))
    return pl.pallas_call(
        functools.partial(_hgrn_step_kernel, nb=nb),
        grid=(bsz // nb,),
        in_specs=[vec, vec, vec, vec, full, sspec],
        out_specs=[vec, sspec],
        out_shape=[jax.ShapeDtypeStruct((bsz, HG_HEADS, HG_DK), BF16),
                   jax.ShapeDtypeStruct(state.shape, state.dtype)],
        compiler_params=pltpu.CompilerParams(dimension_semantics=("parallel",),
                                             vmem_limit_bytes=V7X_VMEM_LIMIT),
        name="hgrn_step",
    )(q, f, v, g, onorm.reshape(HG_HEADS, HG_DK), state)


def _t5_bucket(dist):
    max_exact = REL_BUCKETS // 2
    d = jnp.maximum(dist, 1).astype(F32)
    large = max_exact + (jnp.log(d / max_exact) / math.log(REL_MAX_DIST / max_exact)
                         * (REL_BUCKETS - max_exact)).astype(jnp.int32)
    large = jnp.minimum(large, REL_BUCKETS - 1)
    return jnp.where(dist < max_exact, dist, large)


def _prompt_bias_kernel(w_ref, o_ref):
    band = 2 * WINDOW
    hq = WINDOW // 2
    key = lax.broadcasted_iota(jnp.int32, (band, WINDOW), 0)
    lane = lax.broadcasted_iota(jnp.int32, (band, WINDOW), 1)

    def toeplitz(g):
        row = jnp.broadcast_to(w_ref[g:g + 1, :], (band, 3 * WINDOW))
        return pltpu.roll(row, 0, 1, stride=1, stride_axis=0)[:, :WINDOW]

    for half in range(2):
        for m in range(SW_GROUP // 4):
            ta, tb = toeplitz(2 * (2 * m) + half), toeplitz(2 * (2 * m + 1) + half)
            blocks = [jnp.where(lane < hq, ta, pltpu.roll(tb, hq, axis=1)),
                      jnp.where(lane < hq, pltpu.roll(ta, hq, axis=1), tb)]
            for qh in range(2):
                c = qh * (SW_GROUP // 4) + m
                dst = (slice(half * band, (half + 1) * band), slice(c * WINDOW, (c + 1) * WINDOW))
                o_ref[1, 0, dst[0], dst[1]] = blocks[qh]
                o_ref[0, 0, dst[0], dst[1]] = jnp.where(key < WINDOW - N_META, NEG * LOG2E,
                                                        blocks[qh])


def _prompt_bias(table):
    band = 2 * WINDOW
    i = jnp.arange(3 * WINDOW)
    dist = jnp.where(i < WINDOW, i + WINDOW, i - band)
    vals = table.astype(F32)[_t5_bucket(jnp.maximum(dist, 0))]
    vals = jnp.where(((dist >= 0) & (dist <= WINDOW))[:, None], vals, NEG) * LOG2E
    return pl.pallas_call(
        _prompt_bias_kernel,
        grid=(SW_KV,),
        in_specs=[pl.BlockSpec((SW_GROUP, 3 * WINDOW), lambda n: (n, 0))],
        out_specs=pl.BlockSpec((2, 1, 2 * band, 4 * WINDOW), lambda n: (0, n, 0, 0)),
        out_shape=jax.ShapeDtypeStruct((2, SW_KV, 2 * band, 4 * WINDOW), F32),
        compiler_params=pltpu.CompilerParams(dimension_semantics=("parallel",)),
        name="prompt_bias",
    )(vals.T)


def _expand_band(pair, u):
    lane = lax.broadcasted_iota(jnp.int32, pair.shape, 1)
    rolled = pltpu.roll(pair, SW_HD, axis=1)
    lo_src, hi_src = (pair, rolled) if u == 0 else (rolled, pair)
    top = jnp.where(lane < SW_HD, lo_src, 0.0)
    bot = jnp.where(lane >= SW_HD, hi_src, 0.0)
    return jnp.concatenate([top, bot], axis=0).astype(BF16)


def _swa_probs(s, bias, sinks):
    band, hq = 2 * WINDOW, WINDOW // 2
    live_rows = band - hq
    n_half = s[0].shape[1]
    dead = jnp.zeros((hq, n_half), BF16)
    ps, ms = [], []
    for half in range(2):
        cols, mcols = [], []
        for qh in range(2):
            rows = slice(half * band + qh * hq, half * band + qh * hq + live_rows)
            lanes = slice(qh * n_half, (qh + 1) * n_half)
            sv = s[qh][half * live_rows:(half + 1) * live_rows] + bias[rows, lanes]
            m = jnp.maximum(jnp.max(sv, axis=0, keepdims=True), sinks[half][:, lanes])
            live = jnp.exp2(sv - m).astype(BF16)
            cols.append(jnp.concatenate([live, dead] if qh == 0 else [dead, live], axis=0))
            mcols.append(m)
        ps.append(jnp.concatenate(cols, axis=1))
        ms.append(jnp.concatenate(mcols, axis=1))
    return jnp.concatenate(ps, axis=0), ms


def _swa_values(pt, vxt, ms, sinks):
    ot = _nn(vxt, pt)
    outs = []
    for half in range(2):
        den = ot[2 * SW_HD + 8 * half:2 * SW_HD + 8 * half + 1] + jnp.exp2(sinks[half] - ms[half])
        outs.append(ot[SW_HD * half:SW_HD * (half + 1)] * (1.0 / den))
    return jnp.concatenate(outs, axis=0)


def _swa_prompt_kernel(sink_ref, q_ref, g_ref, k_ref, v_ref, km_ref, vm_ref, bias_ref,
                       o_ref, *, n_blocks):
    p = pl.program_id(0)
    qb = pl.program_id(2)
    n_pairs = SW_GROUP // 2
    band = 2 * WINDOW

    hq = WINDOW // 2

    def sink_rows(u):
        return [jnp.concatenate(
            [jnp.full((1, hq), sink_ref[(p * 2 + u) * SW_GROUP + 2 * j + half] * LOG2E, F32)
             for _ in range(2) for j in range(n_pairs)], axis=1) for half in range(2)]

    sinks = [sink_rows(u) for u in range(2)]
    orow = lax.broadcasted_iota(jnp.int32, (16, 2 * band), 0)
    ocol = lax.broadcasted_iota(jnp.int32, (16, 2 * band), 1)
    ones_rows = jnp.where((orow < 8) == (ocol < band), 1.0, 0.0).astype(BF16)
    zeros = jnp.zeros((SW_HD, band), BF16)

    def band_of(blk):
        tok = qb * n_blocks + blk
        cur = pl.ds(pl.multiple_of(tok * WINDOW, WINDOW), WINDOW)
        prev = pl.ds(pl.multiple_of(jnp.maximum(tok - 1, 0) * WINDOW, WINDOW), WINDOW)
        is_first = tok == 0
        kband = jnp.concatenate([jnp.where(is_first, km_ref[...], k_ref[0, prev, :]),
                                 k_ref[0, cur, :]], axis=0)
        vband = jnp.concatenate([jnp.where(is_first, vm_ref[...], v_ref[0, prev, :]),
                                 v_ref[0, cur, :]], axis=0)
        return (pl.multiple_of(blk * WINDOW, WINDOW), kband,
                vband.T.astype(BF16), jnp.where(is_first, 0, 1))

    def blocks(it, carry):
        bands = [band_of(it * SWA_BLOCKS_PER_ITER + i) for i in range(SWA_BLOCKS_PER_ITER)]
        units = [(bnd, u) for bnd in bands for u in range(2)]
        scores = []
        for (row0, kband, _, _), u in units:
            base = u * SW_GROUP * SW_HD
            q4 = jnp.concatenate(
                [q_ref[0, pl.ds(pl.multiple_of(row0 + qh * hq, hq), hq),
                       base + j * LANES:base + (j + 1) * LANES]
                 for qh in range(2) for j in range(n_pairs)],
                axis=0).astype(BF16)
            kx = _expand_band(kband, u)
            live = band - hq
            scores.append([
                _nt(jnp.concatenate([kx[qh * hq:qh * hq + live],
                                     kx[band + qh * hq:band + qh * hq + live]], axis=0),
                    q4[qh * n_pairs * hq:(qh + 1) * n_pairs * hq]) for qh in range(2)])
        probs = [_swa_probs(s, bias_ref[bias_sel, u], sinks[u])
                 for s, ((_, _, _, bias_sel), u) in zip(scores, units)]
        outs = []
        for (pt, ms), ((_, _, vbt, _), u) in zip(probs, units):
            vt = vbt[u * SW_HD:(u + 1) * SW_HD]
            vxt = jnp.concatenate([jnp.concatenate([vt, zeros], axis=1),
                                   jnp.concatenate([zeros, vt], axis=1), ones_rows], axis=0)
            outs.append(_swa_values(pt, vxt, ms, sinks[u]))
        for ot, ((row0, _, _, _), u) in zip(outs, units):
            base = u * SW_GROUP * SW_HD
            rows = pl.ds(row0, WINDOW)
            tr = [ot[:, c * LANES:(c + 1) * LANES].T for c in range(n_pairs)]
            o = jnp.concatenate(
                [jnp.concatenate([tr[qh * 2 + j // 2][(j % 2) * hq:(j % 2 + 1) * hq]
                                  for j in range(n_pairs)], axis=1) for qh in range(2)],
                axis=0)
            gate = g_ref[0, rows, base:base + SW_GROUP * SW_HD]
            o_ref[0, rows, base:base + SW_GROUP * SW_HD] = (o * gate).astype(BF16)
        return carry

    lax.fori_loop(0, n_blocks // SWA_BLOCKS_PER_ITER, blocks, 0)


def _swa_prompt(proj, meta_k, meta_v, sinks, bias, *, tq):
    bsz, tlen, _ = proj.shape
    pw = 2 * SW_GROUP * SW_HD
    qblocks = SW_HEADS * SW_HD // pw
    kcol0 = SW_HEADS * SW_HD // LANES
    vcol0 = kcol0 + SW_KV * SW_HD // LANES
    gblk0 = (SW_HEADS * SW_HD + 2 * SW_KV * SW_HD) // pw
    return pl.pallas_call(
        functools.partial(_swa_prompt_kernel, n_blocks=tq // WINDOW),
        grid=(qblocks, bsz, tlen // tq),
        in_specs=[pl.BlockSpec(memory_space=pltpu.SMEM),
                  pl.BlockSpec((1, tq, pw), lambda p, b, t: (b, t, p)),
                  pl.BlockSpec((1, tq, pw), lambda p, b, t: (b, t, gblk0 + p)),
                  pl.BlockSpec((1, tlen, LANES), lambda p, b, t: (b, 0, kcol0 + p)),
                  pl.BlockSpec((1, tlen, LANES), lambda p, b, t: (b, 0, vcol0 + p)),
                  pl.BlockSpec((WINDOW, LANES), lambda p, b, t: (0, p)),
                  pl.BlockSpec((WINDOW, LANES), lambda p, b, t: (0, p)),
                  pl.BlockSpec((2, 2, 4 * WINDOW, 4 * WINDOW), lambda p, b, t: (0, p, 0, 0))],
        out_specs=pl.BlockSpec((1, tq, pw), lambda p, b, t: (b, t, p)),
        out_shape=jax.ShapeDtypeStruct((bsz, tlen, D_MODEL), BF16),
        compiler_params=pltpu.CompilerParams(
            dimension_semantics=("parallel", "parallel", "arbitrary"),
            vmem_limit_bytes=V7X_VMEM_LIMIT),
        name="swa_prompt",
    )(sinks, proj, proj, proj, proj, meta_k, meta_v, bias)


def _swa_sample_kernel(q_ref, g_ref, kn_ref, vn_ref, ck_ref, cv_ref, bc_ref, bn_ref, sink_ref,
                       o_ref, nk_ref, nv_ref, *, nb):
    nkv = SW_KV * SW_HD
    r = ck_ref.shape[1]
    row = lax.broadcasted_iota(jnp.int32, (SW_HEADS, nkv), 0)
    col = lax.broadcasted_iota(jnp.int32, (SW_HEADS, nkv), 1)
    own = (row // SW_GROUP) == (col // SW_HD)
    last = lax.broadcasted_iota(jnp.int32, (r, nkv), 0) == r - 1
    sink = sink_ref[...] * LOG2E
    for bi in range(nb):
        q = q_ref[bi]
        qx = jnp.where(own, jnp.concatenate([q] * SW_KV, axis=1), 0.0)
        ck, cv = ck_ref[bi], cv_ref[bi]
        kn, vn = kn_ref[bi], vn_ref[bi]

        s_c = _nt(qx.astype(BF16), ck.astype(BF16)) + bc_ref[...]
        s_n = jnp.sum(qx * kn, axis=-1, keepdims=True) + bn_ref[...]
        m = jnp.maximum(jnp.maximum(jnp.max(s_c, axis=-1, keepdims=True), s_n), sink)
        p_c = jnp.exp2(s_c - m)
        p_n = jnp.exp2(s_n - m)
        den = jnp.sum(p_c, axis=-1, keepdims=True) + p_n + jnp.exp2(sink - m)
        o_all = _nn(p_c.astype(BF16), cv.astype(BF16))
        o_all = o_all + p_n * vn
        o_all = jnp.where(own, o_all, 0.0)
        o = o_all[:, 0:SW_HD]
        for n in range(1, SW_KV):
            o = o + o_all[:, n * SW_HD:(n + 1) * SW_HD]
        o_ref[bi] = ((o / den) * g_ref[bi]).astype(BF16)

        nk_ref[bi] = jnp.where(last, kn, pltpu.roll(ck, r - 1, axis=0))
        nv_ref[bi] = jnp.where(last, vn, pltpu.roll(cv, r - 1, axis=0))


def _swa_sample(q, g, kn, vn, ck, cv, bias_c, bias_n, sinks, *, nb):
    bsz, r, nkv = ck.shape
    head = pl.BlockSpec((nb, SW_HEADS, SW_HD), lambda b: (b, 0, 0))
    new = pl.BlockSpec((nb, 1, nkv), lambda b: (b, 0, 0))
    cache = pl.BlockSpec((nb, r, nkv), lambda b: (b, 0, 0))
    return pl.pallas_call(
        functools.partial(_swa_sample_kernel, nb=nb),
        grid=(bsz // nb,),
        in_specs=[head, head, new, new, cache, cache,
                  pl.BlockSpec((SW_HEADS, r), lambda b: (0, 0)),
                  pl.BlockSpec((SW_HEADS, 1), lambda b: (0, 0)),
                  pl.BlockSpec((SW_HEADS, 1), lambda b: (0, 0))],
        out_specs=[head, cache, cache],
        out_shape=[jax.ShapeDtypeStruct((bsz, SW_HEADS, SW_HD), BF16),
                   jax.ShapeDtypeStruct(ck.shape, ck.dtype),
                   jax.ShapeDtypeStruct(cv.shape, cv.dtype)],
        compiler_params=pltpu.CompilerParams(dimension_semantics=("parallel",)),
        name="swa_sample",
    )(q, g, kn, vn, ck, cv, bias_c, bias_n, sinks.reshape(SW_HEADS, 1))


def kernel(x_prompt, x_sample, state_hgrn, cache_k_win, cache_v_win, meta_tokens, rel_bias,
           hg_lower_bounds, hg_norm, hg_w_in, hg_onorm, hg_w_out,
           sw_norm, sw_w_in, sw_sinks, sw_w_out, final_norm):
    n_samp = x_sample.shape[0]
    samp = slice(N_META, N_META + n_samp)
    x_main = x_prompt.reshape(MAIN_ROWS, D_MODEL)
    x_small = jnp.concatenate(
        [meta_tokens.astype(F32), x_sample.reshape(n_samp, D_MODEL),
         jnp.zeros((N_SMALL - N_META - n_samp, D_MODEL), F32)], axis=0)

    h_main = _rmsnorm(x_main, hg_norm[0], BF16, 512)
    h_small = _rmsnorm(x_small, hg_norm[0], BF16, N_SMALL)
    p_main, p_small = _proj(h_main, h_small, hg_w_in[0], tm=PROJ_TM, tn=PROJ_TN,
                            sections="hgrn", lbraw=hg_lower_bounds.astype(F32))

    meta_proj = jnp.pad(p_small[:N_META], ((CHUNK - N_META, 0), (0, 0)))[None]
    zero_state = jnp.zeros((HG_HEADS, HG_DK, HG_DK), F32)
    a_meta, s_meta = _hgrn_scan(meta_proj, hg_onorm[0], zero_state,
                                tb=CHUNK, hb=8, n_pad=CHUNK - N_META)
    a_main, s_prompt = _hgrn_scan(p_main.reshape(BATCH, SEQ, -1), hg_onorm[0],
                                  s_meta[0], tb=1024, hb=8)
    sect = lambda s: p_small[samp, s * HG_F:(s + 1) * HG_F].reshape(n_samp, HG_HEADS, HG_DK)
    a_samp, s_sample = _hgrn_step(sect(0), sect(1), sect(2), sect(3),
                                  hg_onorm[0], state_hgrn, nb=SAMPLES_PER_STEP)
    a_small = jnp.concatenate(
        [a_meta[0, CHUNK - N_META:], a_samp.reshape(n_samp, D_MODEL),
         jnp.zeros((N_SMALL - N_META - n_samp, D_MODEL), BF16)], axis=0)
    x1_main, x1_small, x1b_main, x1b_small, sq_main, sq_small = _proj(
        a_main.reshape(MAIN_ROWS, D_MODEL), a_small, hg_w_out[0], x_main, x_small,
        tm=PROJ_RES_TM, tn=PROJ_TN, emit_stats=True)

    p_main, p_small = _proj(x1b_main, x1b_small, sw_w_in[0],
                            scale=(sw_norm[0], jnp.sum(sq_main, axis=0, keepdims=True),
                                   jnp.sum(sq_small, axis=0, keepdims=True)),
                            tm=PROJ_TM, tn=PROJ_TN, sections="swa")
    nq, nkv = SW_HEADS * SW_HD, SW_KV * SW_HD
    kv_tail = p_main.reshape(BATCH, SEQ, -1)[:, -WINDOW:, nq:nq + 2 * nkv]
    k_tail = kv_tail[:, :, :nkv].reshape(1, BATCH, WINDOW, SW_KV, SW_HD)
    v_tail = kv_tail[:, :, nkv:].reshape(1, BATCH, WINDOW, SW_KV, SW_HD)
    meta_kv = jnp.pad(p_small[:N_META, nq:nq + 2 * nkv], ((WINDOW - N_META, 0), (0, 0)))
    a_main = _swa_prompt(p_main.reshape(BATCH, SEQ, -1), meta_kv[:, :nkv], meta_kv[:, nkv:],
                         sw_sinks[0], _prompt_bias(rel_bias), tq=1024)

    r = cache_k_win.shape[2]
    table = rel_bias.astype(F32)
    bias_c = table[_t5_bucket(r - jnp.arange(r))].T * LOG2E
    bias_n = table[_t5_bucket(jnp.zeros((1,), jnp.int32))].T * LOG2E
    q_s = p_small[samp, :nq].reshape(n_samp, SW_HEADS, SW_HD)
    g_s = p_small[samp, nq + 2 * nkv:].reshape(n_samp, SW_HEADS, SW_HD)
    a_samp, k_samp, v_samp = _swa_sample(
        q_s, g_s, p_small[samp, nq:nq + nkv].reshape(n_samp, 1, nkv),
        p_small[samp, nq + nkv:nq + 2 * nkv].reshape(n_samp, 1, nkv),
        cache_k_win[0].reshape(n_samp, r, nkv), cache_v_win[0].reshape(n_samp, r, nkv),
        bias_c, bias_n, sw_sinks[0], nb=SAMPLES_PER_STEP)
    a_small = jnp.concatenate(
        [jnp.zeros((N_META, D_MODEL), BF16), a_samp.reshape(n_samp, D_MODEL),
         jnp.zeros((N_SMALL - N_META - n_samp, D_MODEL), BF16)], axis=0)
    x2_main, x2_small = _proj(a_main.reshape(MAIN_ROWS, D_MODEL), a_small, sw_w_out[0],
                              x1_main, x1_small, tm=PROJ_RES_TM, tn=PROJ_TN)

    y_prompt = _rmsnorm(x2_main, final_norm, F32, 512).reshape(BATCH, SEQ, D_MODEL)
    y_sample = _rmsnorm(x2_small, final_norm, F32, N_SMALL)[samp].reshape(n_samp, 1, D_MODEL)
    return (y_prompt, y_sample, s_prompt[None],
            k_tail.astype(cache_k_win.dtype), v_tail.astype(cache_v_win.dtype),
            s_sample,
            k_samp.reshape(1, n_samp, r, SW_KV, SW_HD), v_samp.reshape(1, n_samp, r, SW_KV, SW_HD))
```

```python
import functools
import math

import numpy as np
import jax
import jax.numpy as jnp
from jax import lax
from jax.experimental import pallas as pl
from jax.experimental.pallas import tpu as pltpu

D_MODEL = 4096
BATCH = 4
SEQ = 2048
DEC_BATCH = 32
N_META = 16
HG_HEADS = 32
HG_DK = 128
HG_F = HG_HEADS * HG_DK
SW_HEADS = 64
SW_KV = 8
SW_HD = 64
SW_GROUP = SW_HEADS // SW_KV
WINDOW = 128
REL_BUCKETS = 32
REL_MAX_DIST = 128
EPS = 1e-6
NEG = -1e30
LOG2E = math.log2(math.e)

LANES = 128
V7X_VMEM_LIMIT = 58 * 1024 * 1024

PROJ_TM = 1024
PROJ_TN = 1024
PROJ_RES_TM = 512
CHUNK = 128
SWA_BLOCKS_PER_ITER = 4
SAMPLES_PER_STEP = 4
N_SMALL = 64
MAIN_ROWS = BATCH * SEQ

F32 = jnp.float32
BF16 = jnp.bfloat16


def _nt(a, b):
    return lax.dot_general(a, b, (((1,), (1,)), ((), ())), preferred_element_type=F32)


def _nn(a, b):
    return jnp.dot(a, b, preferred_element_type=F32)


def _sigmoid(x):
    return 1.0 / (1.0 + jnp.exp(-x))


def _silu(x):
    h = 0.5 * x
    return h + h * jnp.tanh(h)


def _rmsnorm_kernel(x_ref, g_ref, o_ref):
    x = x_ref[...]
    ms = jnp.mean(x * x, axis=-1, keepdims=True)
    o_ref[...] = (x * lax.rsqrt(ms + EPS) * g_ref[...]).astype(o_ref.dtype)


def _rmsnorm(x, g, out_dtype, tm):
    m, d = x.shape
    return pl.pallas_call(
        _rmsnorm_kernel,
        grid=(m // tm,),
        in_specs=[pl.BlockSpec((tm, d), lambda i: (i, 0)),
                  pl.BlockSpec((1, d), lambda i: (0, 0))],
        out_specs=pl.BlockSpec((tm, d), lambda i: (i, 0)),
        out_shape=jax.ShapeDtypeStruct((m, d), out_dtype),
        compiler_params=pltpu.CompilerParams(dimension_semantics=("parallel",),
                                             vmem_limit_bytes=V7X_VMEM_LIMIT),
        name="rmsnorm",
    )(x, g.reshape(1, d))


def _proj_kernel(*refs, tn, kc, cps, has_res, has_scale, emit_stats, sections):
    refs = list(refs)
    a_ref, as_ref, w_hbm = refs[:3]
    del refs[:3]
    if sections == "hgrn":
        lb_ref = refs.pop(0)
    if has_scale:
        gw_ref, sq_in_ref, sqs_in_ref = refs[:3]
        del refs[:3]
    if has_res:
        r_ref, rs_ref = refs[:2]
        del refs[:2]
    o_ref, os_ref = refs[:2]
    del refs[:2]
    if emit_stats:
        ob_ref, obs_ref, sq_ref, sqs_ref = refs[:4]
        del refs[:4]
    wb_ref, stage_ref, sem = refs
    j, i = pl.program_id(0), pl.program_id(1)
    nj, ni = pl.num_programs(0), pl.num_programs(1)
    kdim = w_hbm.shape[0]
    n_kc = kdim // kc
    step = j * ni + i
    cur = lax.rem(j, 2)
    par = lax.rem(step, 2)

    def aligned(x, m):
        return x * m if isinstance(x, int) else pl.multiple_of(x * m, m)

    def chunk_copy(tile, c, slot):
        return pltpu.make_async_copy(
            w_hbm.at[pl.ds(aligned(c, kc), kc), pl.ds(aligned(tile, tn), tn)],
            stage_ref.at[slot], sem.at[slot])

    def cast_chunk(slot, c, half):
        rows = pl.ds(aligned(c, kc), kc)
        w = stage_ref[slot]
        if has_scale:
            w = w * jnp.concatenate([gw_ref[rows, :]] * (tn // LANES), axis=1)
        wb_ref[half, rows, :] = w.astype(BF16)

    def next_tile(jj):
        return jnp.minimum(jj + 1, nj - 1)

    def forget_gate(x):
        lb = _lower_bound(lb_ref[...])
        return lb + (1.0 - lb) * _sigmoid(x)

    def identity(x):
        return x

    if sections == "hgrn":
        quarter = nj // 4
        gates = [_silu, forget_gate, identity]
        gate_id = jnp.where((j < quarter) | (j >= 3 * quarter), 0,
                            jnp.where(j < 2 * quarter, 1, 2))
    elif sections == "swa":
        q_tiles = SW_HEADS * SW_HD // tn
        gates = [lambda x: x * (SW_HD ** -0.5 * LOG2E), identity, _silu]
        gate_id = jnp.where(j < q_tiles, 0, jnp.where(j == q_tiles, 1, 2))
    else:
        gates, gate_id = [identity], None

    def per_section(body):
        if gate_id is None:
            body(gates[0])
        else:
            lax.cond(gate_id == 0, lambda: body(gates[0]),
                     lambda: lax.cond(gate_id == 1, lambda: body(gates[1]),
                                      lambda: body(gates[2])))

    def finish(out, gate, res_ref, sq_in, o, ob, sq):
        if has_scale:
            out = out * lax.rsqrt(jnp.sum(sq_in[...], axis=0) * (1.0 / kdim) + EPS)
        out = gate(out)
        if has_res:
            out = out + res_ref[...]
        o[...] = out
        if emit_stats:
            ob[...] = out.astype(BF16)
            sq[0] = jnp.sum(out * out, axis=1, keepdims=True)

    @pl.when(step == 0)
    def _():
        chunk_copy(0, 0, 0).start()
        for c in range(n_kc):
            if c + 1 < n_kc:
                chunk_copy(0, c + 1, (c + 1) % 2).start()
            chunk_copy(0, c, c % 2).wait()
            cast_chunk(c % 2, c, 0)
        for u in range(cps):
            chunk_copy(next_tile(0), u, u).start()

    for u in range(cps):
        chunk_copy(next_tile(j), i * cps + u, par * cps + u).wait()

    @pl.when(step + 1 < nj * ni)
    def _():
        wrap = i + 1 == ni
        j2 = jnp.where(wrap, j + 1, j)
        i2 = jnp.where(wrap, 0, i + 1)
        for u in range(cps):
            chunk_copy(next_tile(j2), i2 * cps + u, (1 - par) * cps + u).start()

    @pl.when(i == 0)
    def _():
        per_section(lambda gate: finish(
            _nn(as_ref[...], wb_ref[cur]), gate, rs_ref if has_res else None,
            sqs_in_ref if has_scale else None, os_ref,
            obs_ref if emit_stats else None, sqs_ref if emit_stats else None))

    def main(gate):
        for u in range(cps):
            cast_chunk(par * cps + u, i * cps + u, 1 - cur)
        finish(_nn(a_ref[...], wb_ref[cur]), gate, r_ref if has_res else None,
               sq_in_ref if has_scale else None, o_ref,
               ob_ref if emit_stats else None, sq_ref if emit_stats else None)

    per_section(main)


def _proj(a, a_small, w, res=None, res_small=None, scale=None, *, tm, tn, kc=256,
          emit_stats=False, sections=None, lbraw=None):
    m, k = a.shape
    n = w.shape[1]
    ms = a_small.shape[0]
    has_res, has_scale = res is not None, scale is not None
    ni, nj, n_kc = m // tm, n // tn, k // kc
    assert m % tm == 0 and n % tn == 0 and k % kc == 0 and n_kc % ni == 0 and nj >= 2
    cps = n_kc // ni
    in_specs = [pl.BlockSpec((tm, k), lambda j, i: (i, 0)),
                pl.BlockSpec((ms, k), lambda j, i: (0, 0)),
                pl.BlockSpec(memory_space=pl.ANY)]
    args = [a, a_small, w]
    if sections == "hgrn":
        quarter = nj // 4
        assert nj % 4 == 0 and lbraw.shape == (3, n // 4)
        in_specs += [pl.BlockSpec((3, tn), lambda j, i: (0, jnp.clip(j - quarter, 0, quarter - 1)))]
        args += [lbraw]
    if sections == "swa":
        assert tn == 2 * SW_KV * SW_HD and (SW_HEADS * SW_HD) % tn == 0
    if has_scale:
        gain, sq, sq_small = scale
        in_specs += [pl.BlockSpec((k, LANES), lambda j, i: (0, 0)),
                     pl.BlockSpec((sq.shape[0], tm, 1), lambda j, i: (0, i, 0)),
                     pl.BlockSpec((sq.shape[0], ms, 1), lambda j, i: (0, 0, 0))]
        args += [jnp.broadcast_to(gain.astype(F32)[:, None], (k, LANES)), sq, sq_small]
    if has_res:
        in_specs += [pl.BlockSpec((tm, tn), lambda j, i: (i, j)),
                     pl.BlockSpec((ms, tn), lambda j, i: (0, j))]
        args += [res, res_small]
    out_specs = [pl.BlockSpec((tm, tn), lambda j, i: (i, j)),
                 pl.BlockSpec((ms, tn), lambda j, i: (0, j))]
    out_shape = [jax.ShapeDtypeStruct((m, n), F32), jax.ShapeDtypeStruct((ms, n), F32)]
    if emit_stats:
        out_specs += [pl.BlockSpec((tm, tn), lambda j, i: (i, j)),
                      pl.BlockSpec((ms, tn), lambda j, i: (0, j)),
                      pl.BlockSpec((1, tm, 1), lambda j, i: (j, i, 0)),
                      pl.BlockSpec((1, ms, 1), lambda j, i: (j, 0, 0))]
        out_shape += [jax.ShapeDtypeStruct((m, n), BF16), jax.ShapeDtypeStruct((ms, n), BF16),
                      jax.ShapeDtypeStruct((nj, m, 1), F32), jax.ShapeDtypeStruct((nj, ms, 1), F32)]
    return pl.pallas_call(
        functools.partial(_proj_kernel, tn=tn, kc=kc, cps=cps, has_res=has_res,
                          has_scale=has_scale, emit_stats=emit_stats, sections=sections),
        grid=(nj, ni),
        in_specs=in_specs,
        out_specs=out_specs,
        out_shape=out_shape,
        scratch_shapes=[pltpu.VMEM((2, k, tn), BF16),
                        pltpu.VMEM((2 * cps, kc, tn), F32),
                        pltpu.SemaphoreType.DMA((2 * cps,))],
        compiler_params=pltpu.CompilerParams(
            dimension_semantics=("arbitrary", "arbitrary"),
            vmem_limit_bytes=V7X_VMEM_LIMIT),
        name="proj_res" if has_res else "proj",
    )(*args)


def _hgrn_consts():
    t = np.arange(CHUNK)[:, None]
    s = np.arange(CHUNK)[None, :]
    tri = s <= t
    ends = np.array([15, 47, 79, 111, 31, 95, 63, 127] + [-1] * 8)[:, None]
    mall = np.concatenate([tri, s <= ends], axis=0).astype(np.float32)
    lvl = np.full((CHUNK, CHUNK), 3, np.int32)
    lvl[(t >= 64) & (s < 64)] = 2
    lvl[(t // 64 == s // 64) & (t % 64 >= 32) & (s % 64 < 32)] = 1
    lvl[(t // 32 == s // 32) & (s <= t)] = 0
    return jnp.asarray(mall, BF16), jnp.asarray(lvl)


def _lower_bound(lbraw):
    mx = jnp.max(lbraw, axis=0, keepdims=True)
    e = jnp.exp(lbraw - mx)
    return e[0:1, :] / jnp.sum(e, axis=0, keepdims=True)


def _head_out(o, gate, onorm):
    ms = jnp.mean(o * o, axis=-1, keepdims=True)
    return (o * lax.rsqrt(ms + EPS) * onorm) * gate


def _hgrn_gates(q, f, mall, row_valid):
    logf = jnp.log2(f)
    k = 1.0 - f
    if row_valid is not None:
        logf = jnp.where(row_valid, logf, 0.0)
        k = jnp.where(row_valid, k, 0.0)

    hi = logf.astype(BF16)
    mid = (logf - hi.astype(F32)).astype(BF16)
    cs = _nn(mall, jnp.concatenate([hi, mid], axis=1))
    return q, k, cs[:, :LANES] + cs[:, LANES:]


def _hgrn_scores(q, k, cs):
    b = cs[0:CHUNK]
    ref = cs[CHUNK:CHUNK + 8]
    r = lambda i: ref[i:i + 1]
    cat = lambda parts: jnp.concatenate(parts, axis=0)
    blk = lambda x, i: x[32 * i:32 * i + 32]
    e0 = jnp.exp2(cat([blk(b, i) - r(i) for i in range(4)]))
    q0, k0 = q * e0, k * (1.0 / e0)
    c = jnp.exp2(cat([r(1) - r(4), r(3) - r(5), r(4) - r(0), r(5) - r(2),
                      r(2) - r(6), r(3) - r(6), r(6) - r(0), r(6) - r(1)]))
    ce = jnp.exp2(cat([r(0), r(1), r(2), r(3), r(7) - r(0), r(7) - r(1), r(7) - r(2), r(7) - r(3)]))
    zero32 = jnp.zeros((32, LANES), BF16)
    bf = lambda x: x.astype(BF16)
    a0 = _nt(bf(q0), bf(k0))
    a1 = _nt(bf(cat([blk(q0, 1) * c[0:1], blk(q0, 3) * c[1:2]])),
             cat([bf(blk(k0, 0) * c[2:3]), zero32, bf(blk(k0, 2) * c[3:4]), zero32]))
    a2 = _nt(bf(cat([blk(q0, 2) * c[4:5], blk(q0, 3) * c[5:6]])),
             cat([bf(blk(k0, 0) * c[6:7]), bf(blk(k0, 1) * c[7:8]), zero32, zero32]))
    qe = bf(cat([blk(q0, i) * ce[i:i + 1] for i in range(4)]))
    kd = bf(cat([blk(k0, i) * ce[4 + i:5 + i] for i in range(4)]))
    return a0, a1, a2, qe, kd, jnp.exp2(r(7))


def _hgrn_apply(a0, a1, a2, qe, kd, decay, v, st, lvl):
    is0, is1, is2 = lvl == 0, lvl == 1, lvl == 2
    a = jnp.concatenate([
        jnp.where(is0[0:32], a0[0:32], 0.0),
        jnp.where(is0[32:64], a0[32:64], jnp.where(is1[32:64], a1[0:32], 0.0)),
        jnp.where(is0[64:96], a0[64:96], jnp.where(is2[64:96], a2[0:32], 0.0)),
        jnp.where(is0[96:128], a0[96:128],
                  jnp.where(is1[96:128], a1[32:64], jnp.where(is2[96:128], a2[32:64], 0.0))),
    ], axis=0)
    vt = v.T.astype(BF16)
    lhs = jnp.concatenate([a.astype(BF16), qe], axis=1)
    rhs = jnp.concatenate([vt, st.astype(BF16)], axis=1)
    return _nt(lhs, rhs), st * decay + _nn(vt, kd)


def _hgrn_scan_kernel(q_ref, f_ref, v_ref, g_ref, on_ref, s0_ref, mall_ref, lvl_ref,
                      a_ref, s_ref, st_ref, *, hb, n_chunks, n_pad):
    t = pl.program_id(2)

    @pl.when(t == 0)
    def _():
        for h in range(hb):
            st_ref[h] = s0_ref[h].T

    row_valid = None
    if n_pad:
        row_valid = lax.broadcasted_iota(jnp.int32, (CHUNK, LANES), 0) >= n_pad
    heads = [slice(h * LANES, (h + 1) * LANES) for h in range(hb)]

    def body(c, carry):
        rows = pl.ds(pl.multiple_of(c * CHUNK, CHUNK), CHUNK)
        gates = [_hgrn_gates(q_ref[0, rows, cols], f_ref[0, rows, cols], mall_ref[...], row_valid)
                 for cols in heads]
        scores = [_hgrn_scores(*g) for g in gates]
        outs = [_hgrn_apply(*sc, v_ref[0, rows, cols], st_ref[h], lvl_ref[...])
                for h, (sc, cols) in enumerate(zip(scores, heads))]
        for h, ((o, st_new), cols) in enumerate(zip(outs, heads)):
            st_ref[h] = st_new
            a_ref[0, rows, cols] = _head_out(o, g_ref[0, rows, cols], on_ref[:, cols]).astype(BF16)
        return carry

    lax.fori_loop(0, n_chunks, body, 0)

    @pl.when(t == pl.num_programs(2) - 1)
    def _():
        for h in range(hb):
            s_ref[0, h] = st_ref[h].T


def _hgrn_scan(proj, onorm, s0, *, tb, hb, n_pad=0):
    bsz, tlen, _ = proj.shape
    mall, lvl = _hgrn_consts()
    hcols = HG_F // (hb * LANES)
    sect = lambda s: (lambda b, h, t: (b, t, s * hcols + h))
    blk = (1, tb, hb * LANES)
    return pl.pallas_call(
        functools.partial(_hgrn_scan_kernel, hb=hb, n_chunks=tb // CHUNK, n_pad=n_pad),
        grid=(bsz, HG_HEADS // hb, tlen // tb),
        in_specs=[pl.BlockSpec(blk, sect(0)), pl.BlockSpec(blk, sect(1)),
                  pl.BlockSpec(blk, sect(2)), pl.BlockSpec(blk, sect(3)),
                  pl.BlockSpec((1, hb * LANES), lambda b, h, t: (0, h)),
                  pl.BlockSpec((hb, HG_DK, HG_DK), lambda b, h, t: (h, 0, 0)),
                  pl.BlockSpec((CHUNK + 16, CHUNK), lambda b, h, t: (0, 0)),
                  pl.BlockSpec((CHUNK, CHUNK), lambda b, h, t: (0, 0))],
        out_specs=[pl.BlockSpec(blk, lambda b, h, t: (b, t, h)),
                   pl.BlockSpec((1, hb, HG_DK, HG_DK), lambda b, h, t: (b, h, 0, 0))],
        out_shape=[jax.ShapeDtypeStruct((bsz, tlen, D_MODEL), BF16),
                   jax.ShapeDtypeStruct((bsz, HG_HEADS, HG_DK, HG_DK), F32)],
        scratch_shapes=[pltpu.VMEM((hb, HG_DK, HG_DK), F32)],
        compiler_params=pltpu.CompilerParams(
            dimension_semantics=("parallel", "parallel", "arbitrary"),
            vmem_limit_bytes=V7X_VMEM_LIMIT),
        name="hgrn_scan",
    )(proj, proj, proj, proj, onorm.reshape(1, D_MODEL), s0, mall, lvl)


def _hgrn_step_kernel(q_ref, f_ref, v_ref, g_ref, on_ref, s0_ref, a_ref, s_ref, *, nb):
    def split(x):
        hi = x.astype(BF16).astype(F32)
        return hi, x - hi

    r = lax.broadcasted_iota(jnp.int32, (8, HG_DK), 0)
    ones_tail = jnp.concatenate([jnp.where((r == 3) | (r == 4), 1.0, 0.0),
                                 jnp.where((r == 5) | (r == 6), 1.0, 0.0)], axis=1).astype(BF16)
    k_hi_rows, k_lo_rows = r < 2, (r == 2) | (r == 7)
    v_hi_rows, v_lo_rows = (r == 0) | (r == 2), (r == 1) | (r == 7)
    for bi in range(nb):
        f = f_ref[bi]
        decay = jnp.exp(jnp.log(f))
        k = 1.0 - f
        (kh, kl), (dh, dl), (qh, ql), (vh, vl) = (split(k), split(decay), split(q_ref[bi]),
                                                  split(v_ref[bi]))
        rows = []
        for h in range(HG_HEADS):
            row = lambda x: jnp.broadcast_to(x[h:h + 1, :], (8, HG_DK))
            lhs = jnp.where(k_hi_rows, row(kh), jnp.where(k_lo_rows, row(kl), jnp.where(
                r == 3, row(dh), jnp.where(r == 4, row(dl), jnp.where(
                    r == 5, row(qh), row(ql))))))
            rhs_v = jnp.where(v_hi_rows, row(vh), jnp.where(v_lo_rows, row(vl), 0.0))
            rhs = jnp.concatenate([rhs_v.astype(BF16), ones_tail], axis=1)
            out = lax.dot_general(lhs.astype(BF16), rhs, (((0,), (0,)), ((), ())),
                                  preferred_element_type=F32)
            s_new = out[:, HG_DK:2 * HG_DK] * s0_ref[0, bi, h] + out[:, :HG_DK]
            s_ref[0, bi, h] = s_new
            rows.append(jnp.sum(out[:, 2 * HG_DK:] * s_new, axis=0, keepdims=True))
        o = jnp.concatenate(rows, axis=0)
        a_ref[bi] = _head_out(o, g_ref[bi], on_ref[...]).astype(BF16)


def _hgrn_step(q, f, v, g, onorm, state, *, nb):
    bsz = q.shape[0]
    vec = pl.BlockSpec((nb, HG_HEADS, HG_DK), lambda b: (b, 0, 0))
    full = pl.BlockSpec((HG_HEADS, HG_DK), lambda b: (0, 0))
    sspec = pl.BlockSpec((1, nb, HG_HEADS, HG_DK, HG_DK), lambda b: (0, b, 0, 0, 0))
    return pl.pallas_call(
        functools.partial(_hgrn_step_kernel, nb=nb),
        grid=(bsz // nb,),
        in_specs=[vec, vec, vec, vec, full, sspec],
        out_specs=[vec, sspec],
        out_shape=[jax.ShapeDtypeStruct((bsz, HG_HEADS, HG_DK), BF16),
                   jax.ShapeDtypeStruct(state.shape, state.dtype)],
        compiler_params=pltpu.CompilerParams(dimension_semantics=("parallel",),
                                             vmem_limit_bytes=V7X_VMEM_LIMIT),
        name="hgrn_step",
    )(q, f, v, g, onorm.reshape(HG_HEADS, HG_DK), state)


def _t5_bucket(dist):
    max_exact = REL_BUCKETS // 2
    d = jnp.maximum(dist, 1).astype(F32)
    large = max_exact + (jnp.log(d / max_exact) / math.log(REL_MAX_DIST / max_exact)
                         * (REL_BUCKETS - max_exact)).astype(jnp.int32)
    large = jnp.minimum(large, REL_BUCKETS - 1)
    return jnp.where(dist < max_exact, dist, large)


def _prompt_bias_kernel(w_ref, o_ref):
    band = 2 * WINDOW
    hq = WINDOW // 2
    key = lax.broadcasted_iota(jnp.int32, (band, WINDOW), 0)
    lane = lax.broadcasted_iota(jnp.int32, (band, WINDOW), 1)

    def toeplitz(g):
        row = jnp.broadcast_to(w_ref[g:g + 1, :], (band, 3 * WINDOW))
        return pltpu.roll(row, 0, 1, stride=1, stride_axis=0)[:, :WINDOW]

    for half in range(2):
        for m in range(SW_GROUP // 4):
            ta, tb = toeplitz(2 * (2 * m) + half), toeplitz(2 * (2 * m + 1) + half)
            blocks = [jnp.where(lane < hq, ta, pltpu.roll(tb, hq, axis=1)),
                      jnp.where(lane < hq, pltpu.roll(ta, hq, axis=1), tb)]
            for qh in range(2):
                c = qh * (SW_GROUP // 4) + m
                dst = (slice(half * band, (half + 1) * band), slice(c * WINDOW, (c + 1) * WINDOW))
                o_ref[1, 0, dst[0], dst[1]] = blocks[qh]
                o_ref[0, 0, dst[0], dst[1]] = jnp.where(key < WINDOW - N_META, NEG * LOG2E,
                                                        blocks[qh])


def _prompt_bias(table):
    band = 2 * WINDOW
    i = jnp.arange(3 * WINDOW)
    dist = jnp.where(i < WINDOW, i + WINDOW, i - band)
    vals = table.astype(F32)[_t5_bucket(jnp.maximum(dist, 0))]
    vals = jnp.where(((dist >= 0) & (dist <= WINDOW))[:, None], vals, NEG) * LOG2E
    return pl.pallas_call(
        _prompt_bias_kernel,
        grid=(SW_KV,),
        in_specs=[pl.BlockSpec((SW_GROUP, 3 * WINDOW), lambda n: (n, 0))],
        out_specs=pl.BlockSpec((2, 1, 2 * band, 4 * WINDOW), lambda n: (0, n, 0, 0)),
        out_shape=jax.ShapeDtypeStruct((2, SW_KV, 2 * band, 4 * WINDOW), F32),
        compiler_params=pltpu.CompilerParams(dimension_semantics=("parallel",)),
        name="prompt_bias",
    )(vals.T)


def _expand_band(pair, u):
    lane = lax.broadcasted_iota(jnp.int32, pair.shape, 1)
    rolled = pltpu.roll(pair, SW_HD, axis=1)
    lo_src, hi_src = (pair, rolled) if u == 0 else (rolled, pair)
    top = jnp.where(lane < SW_HD, lo_src, 0.0)
    bot = jnp.where(lane >= SW_HD, hi_src, 0.0)
    return jnp.concatenate([top, bot], axis=0).astype(BF16)


def _swa_probs(s, bias, sinks):
    band, hq = 2 * WINDOW, WINDOW // 2
    live_rows = band - hq
    n_half = s[0].shape[1]
    dead = jnp.zeros((hq, n_half), BF16)
    ps, ms = [], []
    for half in range(2):
        cols, mcols = [], []
        for qh in range(2):
            rows = slice(half * band + qh * hq, half * band + qh * hq + live_rows)
            lanes = slice(qh * n_half, (qh + 1) * n_half)
            sv = s[qh][half * live_rows:(half + 1) * live_rows] + bias[rows, lanes]
            m = jnp.maximum(jnp.max(sv, axis=0, keepdims=True), sinks[half][:, lanes])
            live = jnp.exp2(sv - m).astype(BF16)
            cols.append(jnp.concatenate([live, dead] if qh == 0 else [dead, live], axis=0))
            mcols.append(m)
        ps.append(jnp.concatenate(cols, axis=1))
        ms.append(jnp.concatenate(mcols, axis=1))
    return jnp.concatenate(ps, axis=0), ms


def _swa_values(pt, vxt, ms, sinks):
    ot = _nn(vxt, pt)
    outs = []
    for half in range(2):
        den = ot[2 * SW_HD + 8 * half:2 * SW_HD + 8 * half + 1] + jnp.exp2(sinks[half] - ms[half])
        outs.append(ot[SW_HD * half:SW_HD * (half + 1)] * (1.0 / den))
    return jnp.concatenate(outs, axis=0)


def _swa_prompt_kernel(sink_ref, q_ref, g_ref, k_ref, v_ref, km_ref, vm_ref, bias_ref,
                       o_ref, *, n_blocks):
    p = pl.program_id(0)
    qb = pl.program_id(2)
    n_pairs = SW_GROUP // 2
    band = 2 * WINDOW

    hq = WINDOW // 2

    def sink_rows(u):
        return [jnp.concatenate(
            [jnp.full((1, hq), sink_ref[(p * 2 + u) * SW_GROUP + 2 * j + half] * LOG2E, F32)
             for _ in range(2) for j in range(n_pairs)], axis=1) for half in range(2)]

    sinks = [sink_rows(u) for u in range(2)]
    orow = lax.broadcasted_iota(jnp.int32, (16, 2 * band), 0)
    ocol = lax.broadcasted_iota(jnp.int32, (16, 2 * band), 1)
    ones_rows = jnp.where((orow < 8) == (ocol < band), 1.0, 0.0).astype(BF16)
    zeros = jnp.zeros((SW_HD, band), BF16)

    def band_of(blk):
        tok = qb * n_blocks + blk
        cur = pl.ds(pl.multiple_of(tok * WINDOW, WINDOW), WINDOW)
        prev = pl.ds(pl.multiple_of(jnp.maximum(tok - 1, 0) * WINDOW, WINDOW), WINDOW)
        is_first = tok == 0
        kband = jnp.concatenate([jnp.where(is_first, km_ref[...], k_ref[0, prev, :]),
                                 k_ref[0, cur, :]], axis=0)
        vband = jnp.concatenate([jnp.where(is_first, vm_ref[...], v_ref[0, prev, :]),
                                 v_ref[0, cur, :]], axis=0)
        return (pl.multiple_of(blk * WINDOW, WINDOW), kband,
                vband.T.astype(BF16), jnp.where(is_first, 0, 1))

    def blocks(it, carry):
        bands = [band_of(it * SWA_BLOCKS_PER_ITER + i) for i in range(SWA_BLOCKS_PER_ITER)]
        units = [(bnd, u) for bnd in bands for u in range(2)]
        scores = []
        for (row0, kband, _, _), u in units:
            base = u * SW_GROUP * SW_HD
            q4 = jnp.concatenate(
                [q_ref[0, pl.ds(pl.multiple_of(row0 + qh * hq, hq), hq),
                       base + j * LANES:base + (j + 1) * LANES]
                 for qh in range(2) for j in range(n_pairs)],
                axis=0).astype(BF16)
            kx = _expand_band(kband, u)
            live = band - hq
            scores.append([
                _nt(jnp.concatenate([kx[qh * hq:qh * hq + live],
                                     kx[band + qh * hq:band + qh * hq + live]], axis=0),
                    q4[qh * n_pairs * hq:(qh + 1) * n_pairs * hq]) for qh in range(2)])
        probs = [_swa_probs(s, bias_ref[bias_sel, u], sinks[u])
                 for s, ((_, _, _, bias_sel), u) in zip(scores, units)]
        outs = []
        for (pt, ms), ((_, _, vbt, _), u) in zip(probs, units):
            vt = vbt[u * SW_HD:(u + 1) * SW_HD]
            vxt = jnp.concatenate([jnp.concatenate([vt, zeros], axis=1),
                                   jnp.concatenate([zeros, vt], axis=1), ones_rows], axis=0)
            outs.append(_swa_values(pt, vxt, ms, sinks[u]))
        for ot, ((row0, _, _, _), u) in zip(outs, units):
            base = u * SW_GROUP * SW_HD
            rows = pl.ds(row0, WINDOW)
            tr = [ot[:, c * LANES:(c + 1) * LANES].T for c in range(n_pairs)]
            o = jnp.concatenate(
                [jnp.concatenate([tr[qh * 2 + j // 2][(j % 2) * hq:(j % 2 + 1) * hq]
                                  for j in range(n_pairs)], axis=1) for qh in range(2)],
                axis=0)
            gate = g_ref[0, rows, base:base + SW_GROUP * SW_HD]
            o_ref[0, rows, base:base + SW_GROUP * SW_HD] = (o * gate).astype(BF16)
        return carry

    lax.fori_loop(0, n_blocks // SWA_BLOCKS_PER_ITER, blocks, 0)


def _swa_prompt(proj, meta_k, meta_v, sinks, bias, *, tq):
    bsz, tlen, _ = proj.shape
    pw = 2 * SW_GROUP * SW_HD
    qblocks = SW_HEADS * SW_HD // pw
    kcol0 = SW_HEADS * SW_HD // LANES
    vcol0 = kcol0 + SW_KV * SW_HD // LANES
    gblk0 = (SW_HEADS * SW_HD + 2 * SW_KV * SW_HD) // pw
    return pl.pallas_call(
        functools.partial(_swa_prompt_kernel, n_blocks=tq // WINDOW),
        grid=(qblocks, bsz, tlen // tq),
        in_specs=[pl.BlockSpec(memory_space=pltpu.SMEM),
                  pl.BlockSpec((1, tq, pw), lambda p, b, t: (b, t, p)),
                  pl.BlockSpec((1, tq, pw), lambda p, b, t: (b, t, gblk0 + p)),
                  pl.BlockSpec((1, tlen, LANES), lambda p, b, t: (b, 0, kcol0 + p)),
                  pl.BlockSpec((1, tlen, LANES), lambda p, b, t: (b, 0, vcol0 + p)),
                  pl.BlockSpec((WINDOW, LANES), lambda p, b, t: (0, p)),
                  pl.BlockSpec((WINDOW, LANES), lambda p, b, t: (0, p)),
                  pl.BlockSpec((2, 2, 4 * WINDOW, 4 * WINDOW), lambda p, b, t: (0, p, 0, 0))],
        out_specs=pl.BlockSpec((1, tq, pw), lambda p, b, t: (b, t, p)),
        out_shape=jax.ShapeDtypeStruct((bsz, tlen, D_MODEL), BF16),
        compiler_params=pltpu.CompilerParams(
            dimension_semantics=("parallel", "parallel", "arbitrary"),
            vmem_limit_bytes=V7X_VMEM_LIMIT),
        name="swa_prompt",
    )(sinks, proj, proj, proj, proj, meta_k, meta_v, bias)


def _swa_sample_kernel(q_ref, g_ref, kn_ref, vn_ref, ck_ref, cv_ref, bc_ref, bn_ref, sink_ref,
                       o_ref, nk_ref, nv_ref, *, nb):
    nkv = SW_KV * SW_HD
    r = ck_ref.shape[1]
    row = lax.broadcasted_iota(jnp.int32, (SW_HEADS, nkv), 0)
    col = lax.broadcasted_iota(jnp.int32, (SW_HEADS, nkv), 1)
    own = (row // SW_GROUP) == (col // SW_HD)
    last = lax.broadcasted_iota(jnp.int32, (r, nkv), 0) == r - 1
    sink = sink_ref[...] * LOG2E
    for bi in range(nb):
        q = q_ref[bi]
        qx = jnp.where(own, jnp.concatenate([q] * SW_KV, axis=1), 0.0)
        ck, cv = ck_ref[bi], cv_ref[bi]
        kn, vn = kn_ref[bi], vn_ref[bi]

        s_c = _nt(qx.astype(BF16), ck.astype(BF16)) + bc_ref[...]
        s_n = jnp.sum(qx * kn, axis=-1, keepdims=True) + bn_ref[...]
        m = jnp.maximum(jnp.maximum(jnp.max(s_c, axis=-1, keepdims=True), s_n), sink)
        p_c = jnp.exp2(s_c - m)
        p_n = jnp.exp2(s_n - m)
        den = jnp.sum(p_c, axis=-1, keepdims=True) + p_n + jnp.exp2(sink - m)
        o_all = _nn(p_c.astype(BF16), cv.astype(BF16))
        o_all = o_all + p_n * vn
        o_all = jnp.where(own, o_all, 0.0)
        o = o_all[:, 0:SW_HD]
        for n in range(1, SW_KV):
            o = o + o_all[:, n * SW_HD:(n + 1) * SW_HD]
        o_ref[bi] = ((o / den) * g_ref[bi]).astype(BF16)

        nk_ref[bi] = jnp.where(last, kn, pltpu.roll(ck, r - 1, axis=0))
        nv_ref[bi] = jnp.where(last, vn, pltpu.roll(cv, r - 1, axis=0))


def _swa_sample(q, g, kn, vn, ck, cv, bias_c, bias_n, sinks, *, nb):
    bsz, r, nkv = ck.shape
    head = pl.BlockSpec((nb, SW_HEADS, SW_HD), lambda b: (b, 0, 0))
    new = pl.BlockSpec((nb, 1, nkv), lambda b: (b, 0, 0))
    cache = pl.BlockSpec((nb, r, nkv), lambda b: (b, 0, 0))
    return pl.pallas_call(
        functools.partial(_swa_sample_kernel, nb=nb),
        grid=(bsz // nb,),
        in_specs=[head, head, new, new, cache, cache,
                  pl.BlockSpec((SW_HEADS, r), lambda b: (0, 0)),
                  pl.BlockSpec((SW_HEADS, 1), lambda b: (0, 0)),
                  pl.BlockSpec((SW_HEADS, 1), lambda b: (0, 0))],
        out_specs=[head, cache, cache],
        out_shape=[jax.ShapeDtypeStruct((bsz, SW_HEADS, SW_HD), BF16),
                   jax.ShapeDtypeStruct(ck.shape, ck.dtype),
                   jax.ShapeDtypeStruct(cv.shape, cv.dtype)],
        compiler_params=pltpu.CompilerParams(dimension_semantics=("parallel",)),
        name="swa_sample",
    )(q, g, kn, vn, ck, cv, bias_c, bias_n, sinks.reshape(SW_HEADS, 1))


def kernel(x_prompt, x_sample, state_hgrn, cache_k_win, cache_v_win, meta_tokens, rel_bias,
           hg_lower_bounds, hg_norm, hg_w_in, hg_onorm, hg_w_out,
           sw_norm, sw_w_in, sw_sinks, sw_w_out, final_norm):
    n_samp = x_sample.shape[0]
    samp = slice(N_META, N_META + n_samp)
    x_main = x_prompt.reshape(MAIN_ROWS, D_MODEL)
    x_small = jnp.concatenate(
        [meta_tokens.astype(F32), x_sample.reshape(n_samp, D_MODEL),
         jnp.zeros((N_SMALL - N_META - n_samp, D_MODEL), F32)], axis=0)

    h_main = _rmsnorm(x_main, hg_norm[0], BF16, 512)
    h_small = _rmsnorm(x_small, hg_norm[0], BF16, N_SMALL)
    p_main, p_small = _proj(h_main, h_small, hg_w_in[0], tm=PROJ_TM, tn=PROJ_TN,
                            sections="hgrn", lbraw=hg_lower_bounds.astype(F32))

    meta_proj = jnp.pad(p_small[:N_META], ((CHUNK - N_META, 0), (0, 0)))[None]
    zero_state = jnp.zeros((HG_HEADS, HG_DK, HG_DK), F32)
    a_meta, s_meta = _hgrn_scan(meta_proj, hg_onorm[0], zero_state,
                                tb=CHUNK, hb=8, n_pad=CHUNK - N_META)
    a_main, s_prompt = _hgrn_scan(p_main.reshape(BATCH, SEQ, -1), hg_onorm[0],
                                  s_meta[0], tb=512, hb=16)
    sect = lambda s: p_small[samp, s * HG_F:(s + 1) * HG_F].reshape(n_samp, HG_HEADS, HG_DK)
    a_samp, s_sample = _hgrn_step(sect(0), sect(1), sect(2), sect(3),
                                  hg_onorm[0], state_hgrn, nb=SAMPLES_PER_STEP)
    a_small = jnp.concatenate(
        [a_meta[0, CHUNK - N_META:], a_samp.reshape(n_samp, D_MODEL),
         jnp.zeros((N_SMALL - N_META - n_samp, D_MODEL), BF16)], axis=0)
    x1_main, x1_small, x1b_main, x1b_small, sq_main, sq_small = _proj(
        a_main.reshape(MAIN_ROWS, D_MODEL), a_small, hg_w_out[0], x_main, x_small,
        tm=PROJ_RES_TM, tn=PROJ_TN, emit_stats=True)

    p_main, p_small = _proj(x1b_main, x1b_small, sw_w_in[0],
                            scale=(sw_norm[0], jnp.sum(sq_main, axis=0, keepdims=True),
                                   jnp.sum(sq_small, axis=0, keepdims=True)),
                            tm=PROJ_TM, tn=PROJ_TN, sections="swa")
    nq, nkv = SW_HEADS * SW_HD, SW_KV * SW_HD
    kv_tail = p_main.reshape(BATCH, SEQ, -1)[:, -WINDOW:, nq:nq + 2 * nkv]
    k_tail = kv_tail[:, :, :nkv].reshape(1, BATCH, WINDOW, SW_KV, SW_HD)
    v_tail = kv_tail[:, :, nkv:].reshape(1, BATCH, WINDOW, SW_KV, SW_HD)
    meta_kv = jnp.pad(p_small[:N_META, nq:nq + 2 * nkv], ((WINDOW - N_META, 0), (0, 0)))
    a_main = _swa_prompt(p_main.reshape(BATCH, SEQ, -1), meta_kv[:, :nkv], meta_kv[:, nkv:],
                         sw_sinks[0], _prompt_bias(rel_bias), tq=1024)

    r = cache_k_win.shape[2]
    table = rel_bias.astype(F32)
    bias_c = table[_t5_bucket(r - jnp.arange(r))].T * LOG2E
    bias_n = table[_t5_bucket(jnp.zeros((1,), jnp.int32))].T * LOG2E
    q_s = p_small[samp, :nq].reshape(n_samp, SW_HEADS, SW_HD)
    g_s = p_small[samp, nq + 2 * nkv:].reshape(n_samp, SW_HEADS, SW_HD)
    a_samp, k_samp, v_samp = _swa_sample(
        q_s, g_s, p_small[samp, nq:nq + nkv].reshape(n_samp, 1, nkv),
        p_small[samp, nq + nkv:nq + 2 * nkv].reshape(n_samp, 1, nkv),
        cache_k_win[0].reshape(n_samp, r, nkv), cache_v_win[0].reshape(n_samp, r, nkv),
        bias_c, bias_n, sw_sinks[0], nb=SAMPLES_PER_STEP)
    a_small = jnp.concatenate(
        [jnp.zeros((N_META, D_MODEL), BF16), a_samp.reshape(n_samp, D_MODEL),
         jnp.zeros((N_SMALL - N_META - n_samp, D_MODEL), BF16)], axis=0)
    x2_main, x2_small = _proj(a_main.reshape(MAIN_ROWS, D_MODEL), a_small, sw_w_out[0],
                              x1_main, x1_small, tm=PROJ_RES_TM, tn=PROJ_TN)

    y_prompt = _rmsnorm(x2_main, final_norm, F32, 512).reshape(BATCH, SEQ, D_MODEL)
    y_sample = _rmsnorm(x2_small, final_norm, F32, N_SMALL)[samp].reshape(n_samp, 1, D_MODEL)
    return (y_prompt, y_sample, s_prompt[None],
            k_tail.astype(cache_k_win.dtype), v_tail.astype(cache_v_win.dtype),
            s_sample,
            k_samp.reshape(1, n_samp, r, SW_KV, SW_HD), v_samp.reshape(1, n_samp, r, SW_KV, SW_HD))
```

```python
import functools
import math

import numpy as np
import jax
import jax.numpy as jnp
from jax import lax
from jax.experimental import pallas as pl
from jax.experimental.pallas import tpu as pltpu

D_MODEL = 4096
BATCH = 4
SEQ = 2048
DEC_BATCH = 32
N_META = 16
HG_HEADS = 32
HG_DK = 128
HG_F = HG_HEADS * HG_DK
SW_HEADS = 64
SW_KV = 8
SW_HD = 64
SW_GROUP = SW_HEADS // SW_KV
WINDOW = 128
REL_BUCKETS = 32
REL_MAX_DIST = 128
EPS = 1e-6
NEG = -1e30
LOG2E = math.log2(math.e)

LANES = 128
V7X_VMEM_LIMIT = 58 * 1024 * 1024

PROJ_TM = 1024
PROJ_TN = 1024
PROJ_RES_TM = 512
CHUNK = 128
SWA_BLOCKS_PER_ITER = 8
SAMPLES_PER_STEP = 4
N_SMALL = 64
MAIN_ROWS = BATCH * SEQ

F32 = jnp.float32
BF16 = jnp.bfloat16


def _nt(a, b):
    return lax.dot_general(a, b, (((1,), (1,)), ((), ())), preferred_element_type=F32)


def _nn(a, b):
    return jnp.dot(a, b, preferred_element_type=F32)


def _sigmoid(x):
    return 1.0 / (1.0 + jnp.exp(-x))


def _silu(x):
    h = 0.5 * x
    return h + h * jnp.tanh(h)


def _rmsnorm_kernel(x_ref, g_ref, o_ref):
    x = x_ref[...]
    ms = jnp.mean(x * x, axis=-1, keepdims=True)
    o_ref[...] = (x * lax.rsqrt(ms + EPS) * g_ref[...]).astype(o_ref.dtype)


def _rmsnorm(x, g, out_dtype, tm):
    m, d = x.shape
    return pl.pallas_call(
        _rmsnorm_kernel,
        grid=(m // tm,),
        in_specs=[pl.BlockSpec((tm, d), lambda i: (i, 0)),
                  pl.BlockSpec((1, d), lambda i: (0, 0))],
        out_specs=pl.BlockSpec((tm, d), lambda i: (i, 0)),
        out_shape=jax.ShapeDtypeStruct((m, d), out_dtype),
        compiler_params=pltpu.CompilerParams(dimension_semantics=("parallel",),
                                             vmem_limit_bytes=V7X_VMEM_LIMIT),
        name="rmsnorm",
    )(x, g.reshape(1, d))


def _proj_kernel(*refs, tn, kc, cps, has_res, has_scale, emit_stats, sections):
    refs = list(refs)
    a_ref, as_ref, w_hbm = refs[:3]
    del refs[:3]
    if sections == "hgrn":
        lb_ref = refs.pop(0)
    if has_scale:
        gw_ref, sq_in_ref, sqs_in_ref = refs[:3]
        del refs[:3]
    if has_res:
        r_ref, rs_ref = refs[:2]
        del refs[:2]
    o_ref, os_ref = refs[:2]
    del refs[:2]
    if emit_stats:
        ob_ref, obs_ref, sq_ref, sqs_ref = refs[:4]
        del refs[:4]
    wb_ref, stage_ref, sem = refs
    j, i = pl.program_id(0), pl.program_id(1)
    nj, ni = pl.num_programs(0), pl.num_programs(1)
    kdim = w_hbm.shape[0]
    n_kc = kdim // kc
    step = j * ni + i
    cur = lax.rem(j, 2)
    par = lax.rem(step, 2)

    def aligned(x, m):
        return x * m if isinstance(x, int) else pl.multiple_of(x * m, m)

    def chunk_copy(tile, c, slot):
        return pltpu.make_async_copy(
            w_hbm.at[pl.ds(aligned(c, kc), kc), pl.ds(aligned(tile, tn), tn)],
            stage_ref.at[slot], sem.at[slot])

    def cast_chunk(slot, c, half):
        rows = pl.ds(aligned(c, kc), kc)
        w = stage_ref[slot]
        if has_scale:
            w = w * jnp.concatenate([gw_ref[rows, :]] * (tn // LANES), axis=1)
        wb_ref[half, rows, :] = w.astype(BF16)

    def next_tile(jj):
        return jnp.minimum(jj + 1, nj - 1)

    def forget_gate(x):
        lb = _lower_bound(lb_ref[...])
        return lb + (1.0 - lb) * _sigmoid(x)

    def identity(x):
        return x

    if sections == "hgrn":
        quarter = nj // 4
        gates = [_silu, forget_gate, identity]
        gate_id = jnp.where((j < quarter) | (j >= 3 * quarter), 0,
                            jnp.where(j < 2 * quarter, 1, 2))
    elif sections == "swa":
        q_tiles = SW_HEADS * SW_HD // tn
        gates = [lambda x: x * (SW_HD ** -0.5 * LOG2E), identity, _silu]
        gate_id = jnp.where(j < q_tiles, 0, jnp.where(j == q_tiles, 1, 2))
    else:
        gates, gate_id = [identity], None

    def per_section(body):
        if gate_id is None:
            body(gates[0])
        else:
            lax.cond(gate_id == 0, lambda: body(gates[0]),
                     lambda: lax.cond(gate_id == 1, lambda: body(gates[1]),
                                      lambda: body(gates[2])))

    def finish(out, gate, res_ref, sq_in, o, ob, sq):
        if has_scale:
            out = out * lax.rsqrt(jnp.sum(sq_in[...], axis=0) * (1.0 / kdim) + EPS)
        out = gate(out)
        if has_res:
            out = out + res_ref[...]
        o[...] = out
        if emit_stats:
            ob[...] = out.astype(BF16)
            sq[0] = jnp.sum(out * out, axis=1, keepdims=True)

    @pl.when(step == 0)
    def _():
        chunk_copy(0, 0, 0).start()
        for c in range(n_kc):
            if c + 1 < n_kc:
                chunk_copy(0, c + 1, (c + 1) % 2).start()
            chunk_copy(0, c, c % 2).wait()
            cast_chunk(c % 2, c, 0)
        for u in range(cps):
            chunk_copy(next_tile(0), u, u).start()

    for u in range(cps):
        chunk_copy(next_tile(j), i * cps + u, par * cps + u).wait()

    @pl.when(step + 1 < nj * ni)
    def _():
        wrap = i + 1 == ni
        j2 = jnp.where(wrap, j + 1, j)
        i2 = jnp.where(wrap, 0, i + 1)
        for u in range(cps):
            chunk_copy(next_tile(j2), i2 * cps + u, (1 - par) * cps + u).start()

    @pl.when(i == 0)
    def _():
        per_section(lambda gate: finish(
            _nn(as_ref[...], wb_ref[cur]), gate, rs_ref if has_res else None,
            sqs_in_ref if has_scale else None, os_ref,
            obs_ref if emit_stats else None, sqs_ref if emit_stats else None))

    def main(gate):
        for u in range(cps):
            cast_chunk(par * cps + u, i * cps + u, 1 - cur)
        finish(_nn(a_ref[...], wb_ref[cur]), gate, r_ref if has_res else None,
               sq_in_ref if has_scale else None, o_ref,
               ob_ref if emit_stats else None, sq_ref if emit_stats else None)

    per_section(main)


def _proj(a, a_small, w, res=None, res_small=None, scale=None, *, tm, tn, kc=256,
          emit_stats=False, sections=None, lbraw=None):
    m, k = a.shape
    n = w.shape[1]
    ms = a_small.shape[0]
    has_res, has_scale = res is not None, scale is not None
    ni, nj, n_kc = m // tm, n // tn, k // kc
    assert m % tm == 0 and n % tn == 0 and k % kc == 0 and n_kc % ni == 0 and nj >= 2
    cps = n_kc // ni
    in_specs = [pl.BlockSpec((tm, k), lambda j, i: (i, 0)),
                pl.BlockSpec((ms, k), lambda j, i: (0, 0)),
                pl.BlockSpec(memory_space=pl.ANY)]
    args = [a, a_small, w]
    if sections == "hgrn":
        quarter = nj // 4
        assert nj % 4 == 0 and lbraw.shape == (3, n // 4)
        in_specs += [pl.BlockSpec((3, tn), lambda j, i: (0, jnp.clip(j - quarter, 0, quarter - 1)))]
        args += [lbraw]
    if sections == "swa":
        assert tn == 2 * SW_KV * SW_HD and (SW_HEADS * SW_HD) % tn == 0
    if has_scale:
        gain, sq, sq_small = scale
        in_specs += [pl.BlockSpec((k, LANES), lambda j, i: (0, 0)),
                     pl.BlockSpec((sq.shape[0], tm, 1), lambda j, i: (0, i, 0)),
                     pl.BlockSpec((sq.shape[0], ms, 1), lambda j, i: (0, 0, 0))]
        args += [jnp.broadcast_to(gain.astype(F32)[:, None], (k, LANES)), sq, sq_small]
    if has_res:
        in_specs += [pl.BlockSpec((tm, tn), lambda j, i: (i, j)),
                     pl.BlockSpec((ms, tn), lambda j, i: (0, j))]
        args += [res, res_small]
    out_specs = [pl.BlockSpec((tm, tn), lambda j, i: (i, j)),
                 pl.BlockSpec((ms, tn), lambda j, i: (0, j))]
    out_shape = [jax.ShapeDtypeStruct((m, n), F32), jax.ShapeDtypeStruct((ms, n), F32)]
    if emit_stats:
        out_specs += [pl.BlockSpec((tm, tn), lambda j, i: (i, j)),
                      pl.BlockSpec((ms, tn), lambda j, i: (0, j)),
                      pl.BlockSpec((1, tm, 1), lambda j, i: (j, i, 0)),
                      pl.BlockSpec((1, ms, 1), lambda j, i: (j, 0, 0))]
        out_shape += [jax.ShapeDtypeStruct((m, n), BF16), jax.ShapeDtypeStruct((ms, n), BF16),
                      jax.ShapeDtypeStruct((nj, m, 1), F32), jax.ShapeDtypeStruct((nj, ms, 1), F32)]
    return pl.pallas_call(
        functools.partial(_proj_kernel, tn=tn, kc=kc, cps=cps, has_res=has_res,
                          has_scale=has_scale, emit_stats=emit_stats, sections=sections),
        grid=(nj, ni),
        in_specs=in_specs,
        out_specs=out_specs,
        out_shape=out_shape,
        scratch_shapes=[pltpu.VMEM((2, k, tn), BF16),
                        pltpu.VMEM((2 * cps, kc, tn), F32),
                        pltpu.SemaphoreType.DMA((2 * cps,))],
        compiler_params=pltpu.CompilerParams(
            dimension_semantics=("arbitrary", "arbitrary"),
            vmem_limit_bytes=V7X_VMEM_LIMIT),
        name="proj_res" if has_res else "proj",
    )(*args)


def _hgrn_consts():
    t = np.arange(CHUNK)[:, None]
    s = np.arange(CHUNK)[None, :]
    tri = s <= t
    ends = np.array([15, 47, 79, 111, 31, 95, 63, 127] + [-1] * 8)[:, None]
    mall = np.concatenate([tri, s <= ends], axis=0).astype(np.float32)
    lvl = np.full((CHUNK, CHUNK), 3, np.int32)
    lvl[(t >= 64) & (s < 64)] = 2
    lvl[(t // 64 == s // 64) & (t % 64 >= 32) & (s % 64 < 32)] = 1
    lvl[(t // 32 == s // 32) & (s <= t)] = 0
    return jnp.asarray(mall, BF16), jnp.asarray(lvl)


def _lower_bound(lbraw):
    mx = jnp.max(lbraw, axis=0, keepdims=True)
    e = jnp.exp(lbraw - mx)
    return e[0:1, :] / jnp.sum(e, axis=0, keepdims=True)


def _head_out(o, gate, onorm):
    ms = jnp.mean(o * o, axis=-1, keepdims=True)
    return (o * lax.rsqrt(ms + EPS) * onorm) * gate


def _hgrn_gates(q, f, mall, row_valid):
    logf = jnp.log2(f)
    k = 1.0 - f
    if row_valid is not None:
        logf = jnp.where(row_valid, logf, 0.0)
        k = jnp.where(row_valid, k, 0.0)

    hi = logf.astype(BF16)
    mid = (logf - hi.astype(F32)).astype(BF16)
    cs = _nn(mall, jnp.concatenate([hi, mid], axis=1))
    return q, k, cs[:, :LANES] + cs[:, LANES:]


def _hgrn_scores(q, k, cs):
    b = cs[0:CHUNK]
    ref = cs[CHUNK:CHUNK + 8]
    r = lambda i: ref[i:i + 1]
    cat = lambda parts: jnp.concatenate(parts, axis=0)
    blk = lambda x, i: x[32 * i:32 * i + 32]
    e0 = jnp.exp2(cat([blk(b, i) - r(i) for i in range(4)]))
    q0, k0 = q * e0, k * (1.0 / e0)
    c = jnp.exp2(cat([r(1) - r(4), r(3) - r(5), r(4) - r(0), r(5) - r(2),
                      r(2) - r(6), r(3) - r(6), r(6) - r(0), r(6) - r(1)]))
    ce = jnp.exp2(cat([r(0), r(1), r(2), r(3), r(7) - r(0), r(7) - r(1), r(7) - r(2), r(7) - r(3)]))
    zero32 = jnp.zeros((32, LANES), BF16)
    bf = lambda x: x.astype(BF16)
    a0 = _nt(bf(q0), bf(k0))
    a1 = _nt(bf(cat([blk(q0, 1) * c[0:1], blk(q0, 3) * c[1:2]])),
             cat([bf(blk(k0, 0) * c[2:3]), zero32, bf(blk(k0, 2) * c[3:4]), zero32]))
    a2 = _nt(bf(cat([blk(q0, 2) * c[4:5], blk(q0, 3) * c[5:6]])),
             cat([bf(blk(k0, 0) * c[6:7]), bf(blk(k0, 1) * c[7:8]), zero32, zero32]))
    qe = bf(cat([blk(q0, i) * ce[i:i + 1] for i in range(4)]))
    kd = bf(cat([blk(k0, i) * ce[4 + i:5 + i] for i in range(4)]))
    return a0, a1, a2, qe, kd, jnp.exp2(r(7))


def _hgrn_apply(a0, a1, a2, qe, kd, decay, v, st, lvl):
    is0, is1, is2 = lvl == 0, lvl == 1, lvl == 2
    a = jnp.concatenate([
        jnp.where(is0[0:32], a0[0:32], 0.0),
        jnp.where(is0[32:64], a0[32:64], jnp.where(is1[32:64], a1[0:32], 0.0)),
        jnp.where(is0[64:96], a0[64:96], jnp.where(is2[64:96], a2[0:32], 0.0)),
        jnp.where(is0[96:128], a0[96:128],
                  jnp.where(is1[96:128], a1[32:64], jnp.where(is2[96:128], a2[32:64], 0.0))),
    ], axis=0)
    vt = v.T.astype(BF16)
    lhs = jnp.concatenate([a.astype(BF16), qe], axis=1)
    rhs = jnp.concatenate([vt, st.astype(BF16)], axis=1)
    return _nt(lhs, rhs), st * decay + _nn(vt, kd)


def _hgrn_scan_kernel(q_ref, f_ref, v_ref, g_ref, on_ref, s0_ref, mall_ref, lvl_ref,
                      a_ref, s_ref, st_ref, *, hb, n_chunks, n_pad):
    t = pl.program_id(2)

    @pl.when(t == 0)
    def _():
        for h in range(hb):
            st_ref[h] = s0_ref[h].T

    row_valid = None
    if n_pad:
        row_valid = lax.broadcasted_iota(jnp.int32, (CHUNK, LANES), 0) >= n_pad
    heads = [slice(h * LANES, (h + 1) * LANES) for h in range(hb)]

    def body(c, carry):
        rows = pl.ds(pl.multiple_of(c * CHUNK, CHUNK), CHUNK)
        gates = [_hgrn_gates(q_ref[0, rows, cols], f_ref[0, rows, cols], mall_ref[...], row_valid)
                 for cols in heads]
        scores = [_hgrn_scores(*g) for g in gates]
        outs = [_hgrn_apply(*sc, v_ref[0, rows, cols], st_ref[h], lvl_ref[...])
                for h, (sc, cols) in enumerate(zip(scores, heads))]
        for h, ((o, st_new), cols) in enumerate(zip(outs, heads)):
            st_ref[h] = st_new
            a_ref[0, rows, cols] = _head_out(o, g_ref[0, rows, cols], on_ref[:, cols]).astype(BF16)
        return carry

    lax.fori_loop(0, n_chunks, body, 0)

    @pl.when(t == pl.num_programs(2) - 1)
    def _():
        for h in range(hb):
            s_ref[0, h] = st_ref[h].T


def _hgrn_scan(proj, onorm, s0, *, tb, hb, n_pad=0):
    bsz, tlen, _ = proj.shape
    mall, lvl = _hgrn_consts()
    hcols = HG_F // (hb * LANES)
    sect = lambda s: (lambda b, h, t: (b, t, s * hcols + h))
    blk = (1, tb, hb * LANES)
    return pl.pallas_call(
        functools.partial(_hgrn_scan_kernel, hb=hb, n_chunks=tb // CHUNK, n_pad=n_pad),
        grid=(bsz, HG_HEADS // hb, tlen // tb),
        in_specs=[pl.BlockSpec(blk, sect(0)), pl.BlockSpec(blk, sect(1)),
                  pl.BlockSpec(blk, sect(2)), pl.BlockSpec(blk, sect(3)),
                  pl.BlockSpec((1, hb * LANES), lambda b, h, t: (0, h)),
                  pl.BlockSpec((hb, HG_DK, HG_DK), lambda b, h, t: (h, 0, 0)),
                  pl.BlockSpec((CHUNK + 16, CHUNK), lambda b, h, t: (0, 0)),
                  pl.BlockSpec((CHUNK, CHUNK), lambda b, h, t: (0, 0))],
        out_specs=[pl.BlockSpec(blk, lambda b, h, t: (b, t, h)),
                   pl.BlockSpec((1, hb, HG_DK, HG_DK), lambda b, h, t: (b, h, 0, 0))],
        out_shape=[jax.ShapeDtypeStruct((bsz, tlen, D_MODEL), BF16),
                   jax.ShapeDtypeStruct((bsz, HG_HEADS, HG_DK, HG_DK), F32)],
        scratch_shapes=[pltpu.VMEM((hb, HG_DK, HG_DK), F32)],
        compiler_params=pltpu.CompilerParams(
            dimension_semantics=("parallel", "parallel", "arbitrary"),
            vmem_limit_bytes=V7X_VMEM_LIMIT),
        name="hgrn_scan",
    )(proj, proj, proj, proj, onorm.reshape(1, D_MODEL), s0, mall, lvl)


def _hgrn_step_kernel(q_ref, f_ref, v_ref, g_ref, on_ref, s0_ref, a_ref, s_ref, *, nb):
    def split(x):
        hi = x.astype(BF16).astype(F32)
        return hi, x - hi

    r = lax.broadcasted_iota(jnp.int32, (8, HG_DK), 0)
    ones_tail = jnp.concatenate([jnp.where((r == 3) | (r == 4), 1.0, 0.0),
                                 jnp.where((r == 5) | (r == 6), 1.0, 0.0)], axis=1).astype(BF16)
    k_hi_rows, k_lo_rows = r < 2, (r == 2) | (r == 7)
    v_hi_rows, v_lo_rows = (r == 0) | (r == 2), (r == 1) | (r == 7)
    for bi in range(nb):
        f = f_ref[bi]
        decay = jnp.exp(jnp.log(f))
        k = 1.0 - f
        (kh, kl), (dh, dl), (qh, ql), (vh, vl) = (split(k), split(decay), split(q_ref[bi]),
                                                  split(v_ref[bi]))
        rows = []
        for h in range(HG_HEADS):
            row = lambda x: jnp.broadcast_to(x[h:h + 1, :], (8, HG_DK))
            lhs = jnp.where(k_hi_rows, row(kh), jnp.where(k_lo_rows, row(kl), jnp.where(
                r == 3, row(dh), jnp.where(r == 4, row(dl), jnp.where(
                    r == 5, row(qh), row(ql))))))
            rhs_v = jnp.where(v_hi_rows, row(vh), jnp.where(v_lo_rows, row(vl), 0.0))
            rhs = jnp.concatenate([rhs_v.astype(BF16), ones_tail], axis=1)
            out = lax.dot_general(lhs.astype(BF16), rhs, (((0,), (0,)), ((), ())),
                                  preferred_element_type=F32)
            s_new = out[:, HG_DK:2 * HG_DK] * s0_ref[0, bi, h] + out[:, :HG_DK]
            s_ref[0, bi, h] = s_new
            rows.append(jnp.sum(out[:, 2 * HG_DK:] * s_new, axis=0, keepdims=True))
        o = jnp.concatenate(rows, axis=0)
        a_ref[bi] = _head_out(o, g_ref[bi], on_ref[...]).astype(BF16)


def _hgrn_step(q, f, v, g, onorm, state, *, nb):
    bsz = q.shape[0]
    vec = pl.BlockSpec((nb, HG_HEADS, HG_DK), lambda b: (b, 0, 0))
    full = pl.BlockSpec((HG_HEADS, HG_DK), lambda b: (0, 0))
    sspec = pl.BlockSpec((1, nb, HG_HEADS, HG_DK, HG_DK), lambda b: (0, b, 0, 0, 0))
    return pl.pallas_call(
        functools.partial(_hgrn_step_kernel, nb=nb),
        grid=(bsz // nb,),
        in_specs=[vec, vec, vec, vec, full, sspec],
        out_specs=[vec, sspec],
        out_shape=[jax.ShapeDtypeStruct((bsz, HG_HEADS, HG_DK), BF16),
                   jax.ShapeDtypeStruct(state.shape, state.dtype)],
        compiler_params=pltpu.CompilerParams(dimension_semantics=("parallel",),
                                             vmem_limit_bytes=V7X_VMEM_LIMIT),
        name="hgrn_step",
    )(q, f, v, g, onorm.reshape(HG_HEADS, HG_DK), state)


def _t5_bucket(dist):
    max_exact = REL_BUCKETS // 2
    d = jnp.maximum(dist, 1).astype(F32)
    large = max_exact + (jnp.log(d / max_exact) / math.log(REL_MAX_DIST / max_exact)
                         * (REL_BUCKETS - max_exact)).astype(jnp.int32)
    large = jnp.minimum(large, REL_BUCKETS - 1)
    return jnp.where(dist < max_exact, dist, large)


def _prompt_bias_kernel(w_ref, o_ref):
    band = 2 * WINDOW
    hq = WINDOW // 2
    key = lax.broadcasted_iota(jnp.int32, (band, WINDOW), 0)
    lane = lax.broadcasted_iota(jnp.int32, (band, WINDOW), 1)

    def toeplitz(g):
        row = jnp.broadcast_to(w_ref[g:g + 1, :], (band, 3 * WINDOW))
        return pltpu.roll(row, 0, 1, stride=1, stride_axis=0)[:, :WINDOW]

    for half in range(2):
        for m in range(SW_GROUP // 4):
            ta, tb = toeplitz(2 * (2 * m) + half), toeplitz(2 * (2 * m + 1) + half)
            blocks = [jnp.where(lane < hq, ta, pltpu.roll(tb, hq, axis=1)),
                      jnp.where(lane < hq, pltpu.roll(ta, hq, axis=1), tb)]
            for qh in range(2):
                c = qh * (SW_GROUP // 4) + m
                dst = (slice(half * band, (half + 1) * band), slice(c * WINDOW, (c + 1) * WINDOW))
                o_ref[1, 0, dst[0], dst[1]] = blocks[qh]
                o_ref[0, 0, dst[0], dst[1]] = jnp.where(key < WINDOW - N_META, NEG * LOG2E,
                                                        blocks[qh])


def _prompt_bias(table):
    band = 2 * WINDOW
    i = jnp.arange(3 * WINDOW)
    dist = jnp.where(i < WINDOW, i + WINDOW, i - band)
    vals = table.astype(F32)[_t5_bucket(jnp.maximum(dist, 0))]
    vals = jnp.where(((dist >= 0) & (dist <= WINDOW))[:, None], vals, NEG) * LOG2E
    return pl.pallas_call(
        _prompt_bias_kernel,
        grid=(SW_KV,),
        in_specs=[pl.BlockSpec((SW_GROUP, 3 * WINDOW), lambda n: (n, 0))],
        out_specs=pl.BlockSpec((2, 1, 2 * band, 4 * WINDOW), lambda n: (0, n, 0, 0)),
        out_shape=jax.ShapeDtypeStruct((2, SW_KV, 2 * band, 4 * WINDOW), F32),
        compiler_params=pltpu.CompilerParams(dimension_semantics=("parallel",)),
        name="prompt_bias",
    )(vals.T)


def _expand_band(pair, u):
    lane = lax.broadcasted_iota(jnp.int32, pair.shape, 1)
    rolled = pltpu.roll(pair, SW_HD, axis=1)
    lo_src, hi_src = (pair, rolled) if u == 0 else (rolled, pair)
    top = jnp.where(lane < SW_HD, lo_src, 0.0)
    bot = jnp.where(lane >= SW_HD, hi_src, 0.0)
    return jnp.concatenate([top, bot], axis=0).astype(BF16)


def _swa_probs(s, bias, sinks):
    band, hq = 2 * WINDOW, WINDOW // 2
    live_rows = band - hq
    n_half = s[0].shape[1]
    dead = jnp.zeros((hq, n_half), BF16)
    ps, ms = [], []
    for half in range(2):
        cols, mcols = [], []
        for qh in range(2):
            rows = slice(half * band + qh * hq, half * band + qh * hq + live_rows)
            lanes = slice(qh * n_half, (qh + 1) * n_half)
            sv = s[qh][half * live_rows:(half + 1) * live_rows] + bias[rows, lanes]
            m = jnp.maximum(jnp.max(sv, axis=0, keepdims=True), sinks[half][:, lanes])
            live = jnp.exp2(sv - m).astype(BF16)
            cols.append(jnp.concatenate([live, dead] if qh == 0 else [dead, live], axis=0))
            mcols.append(m)
        ps.append(jnp.concatenate(cols, axis=1))
        ms.append(jnp.concatenate(mcols, axis=1))
    return jnp.concatenate(ps, axis=0), ms


def _swa_values(pt, vxt, ms, sinks):
    ot = _nn(vxt, pt)
    outs = []
    for half in range(2):
        den = ot[2 * SW_HD + 8 * half:2 * SW_HD + 8 * half + 1] + jnp.exp2(sinks[half] - ms[half])
        outs.append(ot[SW_HD * half:SW_HD * (half + 1)] * (1.0 / den))
    return jnp.concatenate(outs, axis=0)


def _swa_prompt_kernel(sink_ref, q_ref, g_ref, k_ref, v_ref, km_ref, vm_ref, bias_ref,
                       o_ref, *, n_blocks):
    p = pl.program_id(0)
    qb = pl.program_id(2)
    n_pairs = SW_GROUP // 2
    band = 2 * WINDOW

    hq = WINDOW // 2

    def sink_rows(u):
        return [jnp.concatenate(
            [jnp.full((1, hq), sink_ref[(p * 2 + u) * SW_GROUP + 2 * j + half] * LOG2E, F32)
             for _ in range(2) for j in range(n_pairs)], axis=1) for half in range(2)]

    sinks = [sink_rows(u) for u in range(2)]
    orow = lax.broadcasted_iota(jnp.int32, (16, 2 * band), 0)
    ocol = lax.broadcasted_iota(jnp.int32, (16, 2 * band), 1)
    ones_rows = jnp.where((orow < 8) == (ocol < band), 1.0, 0.0).astype(BF16)
    zeros = jnp.zeros((SW_HD, band), BF16)

    def band_of(blk):
        tok = qb * n_blocks + blk
        cur = pl.ds(pl.multiple_of(tok * WINDOW, WINDOW), WINDOW)
        prev = pl.ds(pl.multiple_of(jnp.maximum(tok - 1, 0) * WINDOW, WINDOW), WINDOW)
        is_first = tok == 0
        kband = jnp.concatenate([jnp.where(is_first, km_ref[...], k_ref[0, prev, :]),
                                 k_ref[0, cur, :]], axis=0)
        vband = jnp.concatenate([jnp.where(is_first, vm_ref[...], v_ref[0, prev, :]),
                                 v_ref[0, cur, :]], axis=0)
        return (pl.multiple_of(blk * WINDOW, WINDOW), kband,
                vband.T.astype(BF16), jnp.where(is_first, 0, 1))

    def blocks(it, carry):
        bands = [band_of(it * SWA_BLOCKS_PER_ITER + i) for i in range(SWA_BLOCKS_PER_ITER)]
        units = [(bnd, u) for bnd in bands for u in range(2)]
        scores = []
        for (row0, kband, _, _), u in units:
            base = u * SW_GROUP * SW_HD
            q4 = jnp.concatenate(
                [q_ref[0, pl.ds(pl.multiple_of(row0 + qh * hq, hq), hq),
                       base + j * LANES:base + (j + 1) * LANES]
                 for qh in range(2) for j in range(n_pairs)],
                axis=0).astype(BF16)
            kx = _expand_band(kband, u)
            live = band - hq
            scores.append([
                _nt(jnp.concatenate([kx[qh * hq:qh * hq + live],
                                     kx[band + qh * hq:band + qh * hq + live]], axis=0),
                    q4[qh * n_pairs * hq:(qh + 1) * n_pairs * hq]) for qh in range(2)])
        probs = [_swa_probs(s, bias_ref[bias_sel, u], sinks[u])
                 for s, ((_, _, _, bias_sel), u) in zip(scores, units)]
        outs = []
        for (pt, ms), ((_, _, vbt, _), u) in zip(probs, units):
            vt = vbt[u * SW_HD:(u + 1) * SW_HD]
            vxt = jnp.concatenate([jnp.concatenate([vt, zeros], axis=1),
                                   jnp.concatenate([zeros, vt], axis=1), ones_rows], axis=0)
            outs.append(_swa_values(pt, vxt, ms, sinks[u]))
        for ot, ((row0, _, _, _), u) in zip(outs, units):
            base = u * SW_GROUP * SW_HD
            rows = pl.ds(row0, WINDOW)
            tr = [ot[:, c * LANES:(c + 1) * LANES].T for c in range(n_pairs)]
            o = jnp.concatenate(
                [jnp.concatenate([tr[qh * 2 + j // 2][(j % 2) * hq:(j % 2 + 1) * hq]
                                  for j in range(n_pairs)], axis=1) for qh in range(2)],
                axis=0)
            gate = g_ref[0, rows, base:base + SW_GROUP * SW_HD]
            o_ref[0, rows, base:base + SW_GROUP * SW_HD] = (o * gate).astype(BF16)
        return carry

    lax.fori_loop(0, n_blocks // SWA_BLOCKS_PER_ITER, blocks, 0)


def _swa_prompt(proj, meta_k, meta_v, sinks, bias, *, tq):
    bsz, tlen, _ = proj.shape
    pw = 2 * SW_GROUP * SW_HD
    qblocks = SW_HEADS * SW_HD // pw
    kcol0 = SW_HEADS * SW_HD // LANES
    vcol0 = kcol0 + SW_KV * SW_HD // LANES
    gblk0 = (SW_HEADS * SW_HD + 2 * SW_KV * SW_HD) // pw
    return pl.pallas_call(
        functools.partial(_swa_prompt_kernel, n_blocks=tq // WINDOW),
        grid=(qblocks, bsz, tlen // tq),
        in_specs=[pl.BlockSpec(memory_space=pltpu.SMEM),
                  pl.BlockSpec((1, tq, pw), lambda p, b, t: (b, t, p)),
                  pl.BlockSpec((1, tq, pw), lambda p, b, t: (b, t, gblk0 + p)),
                  pl.BlockSpec((1, tlen, LANES), lambda p, b, t: (b, 0, kcol0 + p)),
                  pl.BlockSpec((1, tlen, LANES), lambda p, b, t: (b, 0, vcol0 + p)),
                  pl.BlockSpec((WINDOW, LANES), lambda p, b, t: (0, p)),
                  pl.BlockSpec((WINDOW, LANES), lambda p, b, t: (0, p)),
                  pl.BlockSpec((2, 2, 4 * WINDOW, 4 * WINDOW), lambda p, b, t: (0, p, 0, 0))],
        out_specs=pl.BlockSpec((1, tq, pw), lambda p, b, t: (b, t, p)),
        out_shape=jax.ShapeDtypeStruct((bsz, tlen, D_MODEL), BF16),
        compiler_params=pltpu.CompilerParams(
            dimension_semantics=("parallel", "parallel", "arbitrary"),
            vmem_limit_bytes=V7X_VMEM_LIMIT),
        name="swa_prompt",
    )(sinks, proj, proj, proj, proj, meta_k, meta_v, bias)


def _swa_sample_kernel(q_ref, g_ref, kn_ref, vn_ref, ck_ref, cv_ref, bc_ref, bn_ref, sink_ref,
                       o_ref, nk_ref, nv_ref, *, nb):
    nkv = SW_KV * SW_HD
    r = ck_ref.shape[1]
    row = lax.broadcasted_iota(jnp.int32, (SW_HEADS, nkv), 0)
    col = lax.broadcasted_iota(jnp.int32, (SW_HEADS, nkv), 1)
    own = (row // SW_GROUP) == (col // SW_HD)
    last = lax.broadcasted_iota(jnp.int32, (r, nkv), 0) == r - 1
    sink = sink_ref[...] * LOG2E
    for bi in range(nb):
        q = q_ref[bi]
        qx = jnp.where(own, jnp.concatenate([q] * SW_KV, axis=1), 0.0)
        ck, cv = ck_ref[bi], cv_ref[bi]
        kn, vn = kn_ref[bi], vn_ref[bi]

        s_c = _nt(qx.astype(BF16), ck.astype(BF16)) + bc_ref[...]
        s_n = jnp.sum(qx * kn, axis=-1, keepdims=True) + bn_ref[...]
        m = jnp.maximum(jnp.maximum(jnp.max(s_c, axis=-1, keepdims=True), s_n), sink)
        p_c = jnp.exp2(s_c - m)
        p_n = jnp.exp2(s_n - m)
        den = jnp.sum(p_c, axis=-1, keepdims=True) + p_n + jnp.exp2(sink - m)
        o_all = _nn(p_c.astype(BF16), cv.astype(BF16))
        o_all = o_all + p_n * vn
        o_all = jnp.where(own, o_all, 0.0)
        o = o_all[:, 0:SW_HD]
        for n in range(1, SW_KV):
            o = o + o_all[:, n * SW_HD:(n + 1) * SW_HD]
        o_ref[bi] = ((o / den) * g_ref[bi]).astype(BF16)

        nk_ref[bi] = jnp.where(last, kn, pltpu.roll(ck, r - 1, axis=0))
        nv_ref[bi] = jnp.where(last, vn, pltpu.roll(cv, r - 1, axis=0))


def _swa_sample(q, g, kn, vn, ck, cv, bias_c, bias_n, sinks, *, nb):
    bsz, r, nkv = ck.shape
    head = pl.BlockSpec((nb, SW_HEADS, SW_HD), lambda b: (b, 0, 0))
    new = pl.BlockSpec((nb, 1, nkv), lambda b: (b, 0, 0))
    cache = pl.BlockSpec((nb, r, nkv), lambda b: (b, 0, 0))
    return pl.pallas_call(
        functools.partial(_swa_sample_kernel, nb=nb),
        grid=(bsz // nb,),
        in_specs=[head, head, new, new, cache, cache,
                  pl.BlockSpec((SW_HEADS, r), lambda b: (0, 0)),
                  pl.BlockSpec((SW_HEADS, 1), lambda b: (0, 0)),
                  pl.BlockSpec((SW_HEADS, 1), lambda b: (0, 0))],
        out_specs=[head, cache, cache],
        out_shape=[jax.ShapeDtypeStruct((bsz, SW_HEADS, SW_HD), BF16),
                   jax.ShapeDtypeStruct(ck.shape, ck.dtype),
                   jax.ShapeDtypeStruct(cv.shape, cv.dtype)],
        compiler_params=pltpu.CompilerParams(dimension_semantics=("parallel",)),
        name="swa_sample",
    )(q, g, kn, vn, ck, cv, bias_c, bias_n, sinks.reshape(SW_HEADS, 1))


def kernel(x_prompt, x_sample, state_hgrn, cache_k_win, cache_v_win, meta_tokens, rel_bias,
           hg_lower_bounds, hg_norm, hg_w_in, hg_onorm, hg_w_out,
           sw_norm, sw_w_in, sw_sinks, sw_w_out, final_norm):
    n_samp = x_sample.shape[0]
    samp = slice(N_META, N_META + n_samp)
    x_main = x_prompt.reshape(MAIN_ROWS, D_MODEL)
    x_small = jnp.concatenate(
        [meta_tokens.astype(F32), x_sample.reshape(n_samp, D_MODEL),
         jnp.zeros((N_SMALL - N_META - n_samp, D_MODEL), F32)], axis=0)

    h_main = _rmsnorm(x_main, hg_norm[0], BF16, 512)
    h_small = _rmsnorm(x_small, hg_norm[0], BF16, N_SMALL)
    p_main, p_small = _proj(h_main, h_small, hg_w_in[0], tm=PROJ_TM, tn=PROJ_TN,
                            sections="hgrn", lbraw=hg_lower_bounds.astype(F32))

    meta_proj = jnp.pad(p_small[:N_META], ((CHUNK - N_META, 0), (0, 0)))[None]
    zero_state = jnp.zeros((HG_HEADS, HG_DK, HG_DK), F32)
    a_meta, s_meta = _hgrn_scan(meta_proj, hg_onorm[0], zero_state,
                                tb=CHUNK, hb=16, n_pad=CHUNK - N_META)
    a_main, s_prompt = _hgrn_scan(p_main.reshape(BATCH, SEQ, -1), hg_onorm[0],
                                  s_meta[0], tb=512, hb=16)
    sect = lambda s: p_small[samp, s * HG_F:(s + 1) * HG_F].reshape(n_samp, HG_HEADS, HG_DK)
    a_samp, s_sample = _hgrn_step(sect(0), sect(1), sect(2), sect(3),
                                  hg_onorm[0], state_hgrn, nb=SAMPLES_PER_STEP)
    a_small = jnp.concatenate(
        [a_meta[0, CHUNK - N_META:], a_samp.reshape(n_samp, D_MODEL),
         jnp.zeros((N_SMALL - N_META - n_samp, D_MODEL), BF16)], axis=0)
    x1_main, x1_small, x1b_main, x1b_small, sq_main, sq_small = _proj(
        a_main.reshape(MAIN_ROWS, D_MODEL), a_small, hg_w_out[0], x_main, x_small,
        tm=PROJ_RES_TM, tn=PROJ_TN, emit_stats=True)

    p_main, p_small = _proj(x1b_main, x1b_small, sw_w_in[0],
                            scale=(sw_norm[0], jnp.sum(sq_main, axis=0, keepdims=True),
                                   jnp.sum(sq_small, axis=0, keepdims=True)),
                            tm=PROJ_TM, tn=PROJ_TN, sections="swa")
    nq, nkv = SW_HEADS * SW_HD, SW_KV * SW_HD
    kv_tail = p_main.reshape(BATCH, SEQ, -1)[:, -WINDOW:, nq:nq + 2 * nkv]
    k_tail = kv_tail[:, :, :nkv].reshape(1, BATCH, WINDOW, SW_KV, SW_HD)
    v_tail = kv_tail[:, :, nkv:].reshape(1, BATCH, WINDOW, SW_KV, SW_HD)
    meta_kv = jnp.pad(p_small[:N_META, nq:nq + 2 * nkv], ((WINDOW - N_META, 0), (0, 0)))
    a_main = _swa_prompt(p_main.reshape(BATCH, SEQ, -1), meta_kv[:, :nkv], meta_kv[:, nkv:],
                         sw_sinks[0], _prompt_bias(rel_bias), tq=1024)

    r = cache_k_win.shape[2]
    table = rel_bias.astype(F32)
    bias_c = table[_t5_bucket(r - jnp.arange(r))].T * LOG2E
    bias_n = table[_t5_bucket(jnp.zeros((1,), jnp.int32))].T * LOG2E
    q_s = p_small[samp, :nq].reshape(n_samp, SW_HEADS, SW_HD)
    g_s = p_small[samp, nq + 2 * nkv:].reshape(n_samp, SW_HEADS, SW_HD)
    a_samp, k_samp, v_samp = _swa_sample(
        q_s, g_s, p_small[samp, nq:nq + nkv].reshape(n_samp, 1, nkv),
        p_small[samp, nq + nkv:nq + 2 * nkv].reshape(n_samp, 1, nkv),
        cache_k_win[0].reshape(n_samp, r, nkv), cache_v_win[0].reshape(n_samp, r, nkv),
        bias_c, bias_n, sw_sinks[0], nb=SAMPLES_PER_STEP)
    a_small = jnp.concatenate(
        [jnp.zeros((N_META, D_MODEL), BF16), a_samp.reshape(n_samp, D_MODEL),
         jnp.zeros((N_SMALL - N_META - n_samp, D_MODEL), BF16)], axis=0)
    x2_main, x2_small = _proj(a_main.reshape(MAIN_ROWS, D_MODEL), a_small, sw_w_out[0],
                              x1_main, x1_small, tm=PROJ_RES_TM, tn=PROJ_TN)

    y_prompt = _rmsnorm(x2_main, final_norm, F32, 512).reshape(BATCH, SEQ, D_MODEL)
    y_sample = _rmsnorm(x2_small, final_norm, F32, N_SMALL)[samp].reshape(n_samp, 1, D_MODEL)
    return (y_prompt, y_sample, s_prompt[None],
            k_tail.astype(cache_k_win.dtype), v_tail.astype(cache_v_win.dtype),
            s_sample,
            k_samp.reshape(1, n_samp, r, SW_KV, SW_HD), v_samp.reshape(1, n_samp, r, SW_KV, SW_HD))
```

```python
import functools
import math

import numpy as np
import jax
import jax.numpy as jnp
from jax import lax
from jax.experimental import pallas as pl
from jax.experimental.pallas import tpu as pltpu

D_MODEL = 4096
BATCH = 4
SEQ = 2048
DEC_BATCH = 32
N_META = 16
HG_HEADS = 32
HG_DK = 128
HG_F = HG_HEADS * HG_DK
SW_HEADS = 64
SW_KV = 8
SW_HD = 64
SW_GROUP = SW_HEADS // SW_KV
WINDOW = 128
REL_BUCKETS = 32
REL_MAX_DIST = 128
EPS = 1e-6
NEG = -1e30
LOG2E = math.log2(math.e)

LANES = 128
V7X_VMEM_LIMIT = 58 * 1024 * 1024

PROJ_TM = 1024
PROJ_TN = 1024
PROJ_RES_TM = 512
CHUNK = 128
SWA_BLOCKS_PER_ITER = 4
SAMPLES_PER_STEP = 4
N_SMALL = 64
MAIN_ROWS = BATCH * SEQ

F32 = jnp.float32
BF16 = jnp.bfloat16


def _nt(a, b):
    return lax.dot_general(a, b, (((1,), (1,)), ((), ())), preferred_element_type=F32)


def _nn(a, b):
    return jnp.dot(a, b, preferred_element_type=F32)


def _sigmoid(x):
    return 1.0 / (1.0 + jnp.exp(-x))


def _silu(x):
    h = 0.5 * x
    return h + h * jnp.tanh(h)


def _rmsnorm_kernel(x_ref, g_ref, o_ref):
    x = x_ref[...]
    ms = jnp.mean(x * x, axis=-1, keepdims=True)
    o_ref[...] = (x * lax.rsqrt(ms + EPS) * g_ref[...]).astype(o_ref.dtype)


def _rmsnorm_ring_kernel(x_hbm, g_ref, o_ref, xbuf, sem, *, tm, n_steps):
    s = pl.program_id(0)

    def fetch(step, slot):
        return pltpu.make_async_copy(
            x_hbm.at[pl.ds(pl.multiple_of(step * tm, tm), tm), :], xbuf.at[slot], sem.at[slot])

    @pl.when(s == 0)
    def _():
        fetch(0, 0).start()
        fetch(1, 1).start()

    @pl.when(s + 2 < n_steps)
    def _():
        fetch(s + 2, lax.rem(s + 2, 3)).start()

    slot = lax.rem(s, 3)
    fetch(s, slot).wait()
    x = xbuf[slot]
    ms = jnp.mean(x * x, axis=-1, keepdims=True)
    o_ref[...] = (x * lax.rsqrt(ms + EPS) * g_ref[...]).astype(o_ref.dtype)


def _rmsnorm(x, g, out_dtype, tm):
    m, d = x.shape
    if m // tm >= 3:
        return pl.pallas_call(
            functools.partial(_rmsnorm_ring_kernel, tm=tm, n_steps=m // tm),
            grid=(m // tm,),
            in_specs=[pl.BlockSpec(memory_space=pl.ANY),
                      pl.BlockSpec((1, d), lambda i: (0, 0))],
            out_specs=pl.BlockSpec((tm, d), lambda i: (i, 0)),
            out_shape=jax.ShapeDtypeStruct((m, d), out_dtype),
            scratch_shapes=[pltpu.VMEM((3, tm, d), x.dtype), pltpu.SemaphoreType.DMA((3,))],
            compiler_params=pltpu.CompilerParams(dimension_semantics=("arbitrary",),
                                                 vmem_limit_bytes=V7X_VMEM_LIMIT),
            name="rmsnorm_ring",
        )(x, g.reshape(1, d))
    return pl.pallas_call(
        _rmsnorm_kernel,
        grid=(m // tm,),
        in_specs=[pl.BlockSpec((tm, d), lambda i: (i, 0)),
                  pl.BlockSpec((1, d), lambda i: (0, 0))],
        out_specs=pl.BlockSpec((tm, d), lambda i: (i, 0)),
        out_shape=jax.ShapeDtypeStruct((m, d), out_dtype),
        compiler_params=pltpu.CompilerParams(dimension_semantics=("parallel",),
                                             vmem_limit_bytes=V7X_VMEM_LIMIT),
        name="rmsnorm",
    )(x, g.reshape(1, d))


def _proj_kernel(*refs, tn, kc, cps, has_res, has_scale, emit_stats, sections):
    refs = list(refs)
    a_ref, as_ref, w_hbm = refs[:3]
    del refs[:3]
    if sections == "hgrn":
        lb_ref = refs.pop(0)
    if has_scale:
        gw_ref, sq_in_ref, sqs_in_ref = refs[:3]
        del refs[:3]
    if has_res:
        r_ref, rs_ref = refs[:2]
        del refs[:2]
    o_ref, os_ref = refs[:2]
    del refs[:2]
    if emit_stats:
        ob_ref, obs_ref, sq_ref, sqs_ref = refs[:4]
        del refs[:4]
    wb_ref, stage_ref, sem = refs
    j, i = pl.program_id(0), pl.program_id(1)
    nj, ni = pl.num_programs(0), pl.num_programs(1)
    kdim = w_hbm.shape[0]
    n_kc = kdim // kc
    step = j * ni + i
    cur = lax.rem(j, 2)
    par = lax.rem(step, 2)

    def aligned(x, m):
        return x * m if isinstance(x, int) else pl.multiple_of(x * m, m)

    def chunk_copy(tile, c, slot):
        return pltpu.make_async_copy(
            w_hbm.at[pl.ds(aligned(c, kc), kc), pl.ds(aligned(tile, tn), tn)],
            stage_ref.at[slot], sem.at[slot])

    def cast_chunk(slot, c, half):
        rows = pl.ds(aligned(c, kc), kc)
        w = stage_ref[slot]
        if has_scale:
            w = w * jnp.concatenate([gw_ref[rows, :]] * (tn // LANES), axis=1)
        wb_ref[half, rows, :] = w.astype(BF16)

    def next_tile(jj):
        return jnp.minimum(jj + 1, nj - 1)

    def forget_gate(x):
        lb = _lower_bound(lb_ref[...])
        return lb + (1.0 - lb) * _sigmoid(x)

    def identity(x):
        return x

    if sections == "hgrn":
        quarter = nj // 4
        gates = [_silu, forget_gate, identity]
        gate_id = jnp.where((j < quarter) | (j >= 3 * quarter), 0,
                            jnp.where(j < 2 * quarter, 1, 2))
    elif sections == "swa":
        q_tiles = SW_HEADS * SW_HD // tn
        gates = [lambda x: x * (SW_HD ** -0.5 * LOG2E), identity, _silu]
        gate_id = jnp.where(j < q_tiles, 0, jnp.where(j == q_tiles, 1, 2))
    else:
        gates, gate_id = [identity], None

    def per_section(body):
        if gate_id is None:
            body(gates[0])
        else:
            lax.cond(gate_id == 0, lambda: body(gates[0]),
                     lambda: lax.cond(gate_id == 1, lambda: body(gates[1]),
                                      lambda: body(gates[2])))

    def finish(out, gate, res_ref, sq_in, o, ob, sq):
        if has_scale:
            out = out * lax.rsqrt(jnp.sum(sq_in[...], axis=0) * (1.0 / kdim) + EPS)
        out = gate(out)
        if has_res:
            out = out + res_ref[...]
        o[...] = out
        if emit_stats:
            ob[...] = out.astype(BF16)
            sq[0] = jnp.sum(out * out, axis=1, keepdims=True)

    @pl.when(step == 0)
    def _():
        chunk_copy(0, 0, 0).start()
        for c in range(n_kc):
            if c + 1 < n_kc:
                chunk_copy(0, c + 1, (c + 1) % 2).start()
            chunk_copy(0, c, c % 2).wait()
            cast_chunk(c % 2, c, 0)
        for u in range(cps):
            chunk_copy(next_tile(0), u, u).start()

    for u in range(cps):
        chunk_copy(next_tile(j), i * cps + u, par * cps + u).wait()

    @pl.when(step + 1 < nj * ni)
    def _():
        wrap = i + 1 == ni
        j2 = jnp.where(wrap, j + 1, j)
        i2 = jnp.where(wrap, 0, i + 1)
        for u in range(cps):
            chunk_copy(next_tile(j2), i2 * cps + u, (1 - par) * cps + u).start()

    @pl.when(i == 0)
    def _():
        per_section(lambda gate: finish(
            _nn(as_ref[...], wb_ref[cur]), gate, rs_ref if has_res else None,
            sqs_in_ref if has_scale else None, os_ref,
            obs_ref if emit_stats else None, sqs_ref if emit_stats else None))

    def main(gate):
        for u in range(cps):
            cast_chunk(par * cps + u, i * cps + u, 1 - cur)
        finish(_nn(a_ref[...], wb_ref[cur]), gate, r_ref if has_res else None,
               sq_in_ref if has_scale else None, o_ref,
               ob_ref if emit_stats else None, sq_ref if emit_stats else None)

    per_section(main)


def _proj(a, a_small, w, res=None, res_small=None, scale=None, *, tm, tn, kc=256,
          emit_stats=False, sections=None, lbraw=None):
    m, k = a.shape
    n = w.shape[1]
    ms = a_small.shape[0]
    has_res, has_scale = res is not None, scale is not None
    ni, nj, n_kc = m // tm, n // tn, k // kc
    assert m % tm == 0 and n % tn == 0 and k % kc == 0 and n_kc % ni == 0 and nj >= 2
    cps = n_kc // ni
    in_specs = [pl.BlockSpec((tm, k), lambda j, i: (i, 0)),
                pl.BlockSpec((ms, k), lambda j, i: (0, 0)),
                pl.BlockSpec(memory_space=pl.ANY)]
    args = [a, a_small, w]
    if sections == "hgrn":
        quarter = nj // 4
        assert nj % 4 == 0 and lbraw.shape == (3, n // 4)
        in_specs += [pl.BlockSpec((3, tn), lambda j, i: (0, jnp.clip(j - quarter, 0, quarter - 1)))]
        args += [lbraw]
    if sections == "swa":
        assert tn == 2 * SW_KV * SW_HD and (SW_HEADS * SW_HD) % tn == 0
    if has_scale:
        gain, sq, sq_small = scale
        in_specs += [pl.BlockSpec((k, LANES), lambda j, i: (0, 0)),
                     pl.BlockSpec((sq.shape[0], tm, 1), lambda j, i: (0, i, 0)),
                     pl.BlockSpec((sq.shape[0], ms, 1), lambda j, i: (0, 0, 0))]
        args += [jnp.broadcast_to(gain.astype(F32)[:, None], (k, LANES)), sq, sq_small]
    if has_res:
        in_specs += [pl.BlockSpec((tm, tn), lambda j, i: (i, j)),
                     pl.BlockSpec((ms, tn), lambda j, i: (0, j))]
        args += [res, res_small]
    out_specs = [pl.BlockSpec((tm, tn), lambda j, i: (i, j)),
                 pl.BlockSpec((ms, tn), lambda j, i: (0, j))]
    out_shape = [jax.ShapeDtypeStruct((m, n), F32), jax.ShapeDtypeStruct((ms, n), F32)]
    if emit_stats:
        out_specs += [pl.BlockSpec((tm, tn), lambda j, i: (i, j)),
                      pl.BlockSpec((ms, tn), lambda j, i: (0, j)),
                      pl.BlockSpec((1, tm, 1), lambda j, i: (j, i, 0)),
                      pl.BlockSpec((1, ms, 1), lambda j, i: (j, 0, 0))]
        out_shape += [jax.ShapeDtypeStruct((m, n), BF16), jax.ShapeDtypeStruct((ms, n), BF16),
                      jax.ShapeDtypeStruct((nj, m, 1), F32), jax.ShapeDtypeStruct((nj, ms, 1), F32)]
    return pl.pallas_call(
        functools.partial(_proj_kernel, tn=tn, kc=kc, cps=cps, has_res=has_res,
                          has_scale=has_scale, emit_stats=emit_stats, sections=sections),
        grid=(nj, ni),
        in_specs=in_specs,
        out_specs=out_specs,
        out_shape=out_shape,
        scratch_shapes=[pltpu.VMEM((2, k, tn), BF16),
                        pltpu.VMEM((2 * cps, kc, tn), F32),
                        pltpu.SemaphoreType.DMA((2 * cps,))],
        compiler_params=pltpu.CompilerParams(
            dimension_semantics=("arbitrary", "arbitrary"),
            vmem_limit_bytes=V7X_VMEM_LIMIT),
        name="proj_res" if has_res else "proj",
    )(*args)


def _hgrn_consts():
    t = np.arange(CHUNK)[:, None]
    s = np.arange(CHUNK)[None, :]
    tri = s <= t
    ends = np.array([15, 47, 79, 111, 31, 95, 63, 127] + [-1] * 8)[:, None]
    mall = np.concatenate([tri, s <= ends], axis=0).astype(np.float32)
    lvl = np.full((CHUNK, CHUNK), 3, np.int32)
    lvl[(t >= 64) & (s < 64)] = 2
    lvl[(t // 64 == s // 64) & (t % 64 >= 32) & (s % 64 < 32)] = 1
    lvl[(t // 32 == s // 32) & (s <= t)] = 0
    return jnp.asarray(mall, BF16), jnp.asarray(lvl)


def _lower_bound(lbraw):
    mx = jnp.max(lbraw, axis=0, keepdims=True)
    e = jnp.exp(lbraw - mx)
    return e[0:1, :] / jnp.sum(e, axis=0, keepdims=True)


def _head_out(o, gate, onorm):
    ms = jnp.mean(o * o, axis=-1, keepdims=True)
    return (o * lax.rsqrt(ms + EPS) * onorm) * gate


def _hgrn_gates(q, f, mall, row_valid):
    logf = jnp.log2(f)
    k = 1.0 - f
    if row_valid is not None:
        logf = jnp.where(row_valid, logf, 0.0)
        k = jnp.where(row_valid, k, 0.0)

    hi = logf.astype(BF16)
    mid = (logf - hi.astype(F32)).astype(BF16)
    cs = _nn(mall, jnp.concatenate([hi, mid], axis=1))
    return q, k, cs[:, :LANES] + cs[:, LANES:]


def _hgrn_scores(q, k, cs):
    b = cs[0:CHUNK]
    ref = cs[CHUNK:CHUNK + 8]
    r = lambda i: ref[i:i + 1]
    cat = lambda parts: jnp.concatenate(parts, axis=0)
    blk = lambda x, i: x[32 * i:32 * i + 32]
    e0 = jnp.exp2(cat([blk(b, i) - r(i) for i in range(4)]))
    q0, k0 = q * e0, k * (1.0 / e0)
    c = jnp.exp2(cat([r(1) - r(4), r(3) - r(5), r(4) - r(0), r(5) - r(2),
                      r(2) - r(6), r(3) - r(6), r(6) - r(0), r(6) - r(1)]))
    ce = jnp.exp2(cat([r(0), r(1), r(2), r(3), r(7) - r(0), r(7) - r(1), r(7) - r(2), r(7) - r(3)]))
    zero32 = jnp.zeros((32, LANES), BF16)
    bf = lambda x: x.astype(BF16)
    a0 = _nt(bf(q0), bf(k0))
    a1 = _nt(bf(cat([blk(q0, 1) * c[0:1], blk(q0, 3) * c[1:2]])),
             cat([bf(blk(k0, 0) * c[2:3]), zero32, bf(blk(k0, 2) * c[3:4]), zero32]))
    a2 = _nt(bf(cat([blk(q0, 2) * c[4:5], blk(q0, 3) * c[5:6]])),
             cat([bf(blk(k0, 0) * c[6:7]), bf(blk(k0, 1) * c[7:8]), zero32, zero32]))
    qe = bf(cat([blk(q0, i) * ce[i:i + 1] for i in range(4)]))
    kd = bf(cat([blk(k0, i) * ce[4 + i:5 + i] for i in range(4)]))
    return a0, a1, a2, qe, kd, jnp.exp2(r(7))


def _hgrn_apply(a0, a1, a2, qe, kd, decay, v, st, lvl):
    is0, is1, is2 = lvl == 0, lvl == 1, lvl == 2
    a = jnp.concatenate([
        jnp.where(is0[0:32], a0[0:32], 0.0),
        jnp.where(is0[32:64], a0[32:64], jnp.where(is1[32:64], a1[0:32], 0.0)),
        jnp.where(is0[64:96], a0[64:96], jnp.where(is2[64:96], a2[0:32], 0.0)),
        jnp.where(is0[96:128], a0[96:128],
                  jnp.where(is1[96:128], a1[32:64], jnp.where(is2[96:128], a2[32:64], 0.0))),
    ], axis=0)
    vt = v.T.astype(BF16)
    lhs = jnp.concatenate([a.astype(BF16), qe], axis=1)
    rhs = jnp.concatenate([vt, st.astype(BF16)], axis=1)
    return _nt(lhs, rhs), st * decay + _nn(vt, kd)


def _hgrn_scan_kernel(q_ref, f_ref, v_ref, g_ref, on_ref, s0_ref, mall_ref, lvl_ref,
                      a_ref, s_ref, st_ref, *, hb, n_chunks, n_pad):
    t = pl.program_id(2)

    @pl.when(t == 0)
    def _():
        for h in range(hb):
            st_ref[h] = s0_ref[h].T

    row_valid = None
    if n_pad:
        row_valid = lax.broadcasted_iota(jnp.int32, (CHUNK, LANES), 0) >= n_pad
    heads = [slice(h * LANES, (h + 1) * LANES) for h in range(hb)]

    def body(c, carry):
        rows = pl.ds(pl.multiple_of(c * CHUNK, CHUNK), CHUNK)
        gates = [_hgrn_gates(q_ref[0, rows, cols], f_ref[0, rows, cols], mall_ref[...], row_valid)
                 for cols in heads]
        scores = [_hgrn_scores(*g) for g in gates]
        outs = [_hgrn_apply(*sc, v_ref[0, rows, cols], st_ref[h], lvl_ref[...])
                for h, (sc, cols) in enumerate(zip(scores, heads))]
        for h, ((o, st_new), cols) in enumerate(zip(outs, heads)):
            st_ref[h] = st_new
            a_ref[0, rows, cols] = _head_out(o, g_ref[0, rows, cols], on_ref[:, cols]).astype(BF16)
        return carry

    lax.fori_loop(0, n_chunks, body, 0)

    @pl.when(t == pl.num_programs(2) - 1)
    def _():
        for h in range(hb):
            s_ref[0, h] = st_ref[h].T


def _hgrn_scan(proj, onorm, s0, *, tb, hb, n_pad=0):
    bsz, tlen, _ = proj.shape
    mall, lvl = _hgrn_consts()
    hcols = HG_F // (hb * LANES)
    sect = lambda s: (lambda b, h, t: (b, t, s * hcols + h))
    blk = (1, tb, hb * LANES)
    return pl.pallas_call(
        functools.partial(_hgrn_scan_kernel, hb=hb, n_chunks=tb // CHUNK, n_pad=n_pad),
        grid=(bsz, HG_HEADS // hb, tlen // tb),
        in_specs=[pl.BlockSpec(blk, sect(0)), pl.BlockSpec(blk, sect(1)),
                  pl.BlockSpec(blk, sect(2)), pl.BlockSpec(blk, sect(3)),
                  pl.BlockSpec((1, hb * LANES), lambda b, h, t: (0, h)),
                  pl.BlockSpec((hb, HG_DK, HG_DK), lambda b, h, t: (h, 0, 0)),
                  pl.BlockSpec((CHUNK + 16, CHUNK), lambda b, h, t: (0, 0)),
                  pl.BlockSpec((CHUNK, CHUNK), lambda b, h, t: (0, 0))],
        out_specs=[pl.BlockSpec(blk, lambda b, h, t: (b, t, h)),
                   pl.BlockSpec((1, hb, HG_DK, HG_DK), lambda b, h, t: (b, h, 0, 0))],
        out_shape=[jax.ShapeDtypeStruct((bsz, tlen, D_MODEL), BF16),
                   jax.ShapeDtypeStruct((bsz, HG_HEADS, HG_DK, HG_DK), F32)],
        scratch_shapes=[pltpu.VMEM((hb, HG_DK, HG_DK), F32)],
        compiler_params=pltpu.CompilerParams(
            dimension_semantics=("parallel", "parallel", "arbitrary"),
            vmem_limit_bytes=V7X_VMEM_LIMIT),
        name="hgrn_scan",
    )(proj, proj, proj, proj, onorm.reshape(1, D_MODEL), s0, mall, lvl)


def _hgrn_step_kernel(q_ref, f_ref, v_ref, g_ref, on_ref, s0_ref, a_ref, s_ref, *, nb):
    def split(x):
        hi = x.astype(BF16).astype(F32)
        return hi, x - hi

    r = lax.broadcasted_iota(jnp.int32, (8, HG_DK), 0)
    ones_tail = jnp.concatenate([jnp.where((r == 3) | (r == 4), 1.0, 0.0),
                                 jnp.where((r == 5) | (r == 6), 1.0, 0.0)], axis=1).astype(BF16)
    k_hi_rows, k_lo_rows = r < 2, (r == 2) | (r == 7)
    v_hi_rows, v_lo_rows = (r == 0) | (r == 2), (r == 1) | (r == 7)
    for bi in range(nb):
        f = f_ref[bi]
        decay = jnp.exp(jnp.log(f))
        k = 1.0 - f
        (kh, kl), (dh, dl), (qh, ql), (vh, vl) = (split(k), split(decay), split(q_ref[bi]),
                                                  split(v_ref[bi]))
        rows = []
        for h in range(HG_HEADS):
            row = lambda x: jnp.broadcast_to(x[h:h + 1, :], (8, HG_DK))
            lhs = jnp.where(k_hi_rows, row(kh), jnp.where(k_lo_rows, row(kl), jnp.where(
                r == 3, row(dh), jnp.where(r == 4, row(dl), jnp.where(
                    r == 5, row(qh), row(ql))))))
            rhs_v = jnp.where(v_hi_rows, row(vh), jnp.where(v_lo_rows, row(vl), 0.0))
            rhs = jnp.concatenate([rhs_v.astype(BF16), ones_tail], axis=1)
            out = lax.dot_general(lhs.astype(BF16), rhs, (((0,), (0,)), ((), ())),
                                  preferred_element_type=F32)
            s_new = out[:, HG_DK:2 * HG_DK] * s0_ref[0, bi, h] + out[:, :HG_DK]
            s_ref[0, bi, h] = s_new
            rows.append(jnp.sum(out[:, 2 * HG_DK:] * s_new, axis=0, keepdims=True))
        o = jnp.concatenate(rows, axis=0)
        a_ref[bi] = _head_out(o, g_ref[bi], on_ref[...]).astype(BF16)


def _hgrn_step(q, f, v, g, onorm, state, *, nb):
    bsz = q.shape[0]
    vec = pl.BlockSpec((nb, HG_HEADS, HG_DK), lambda b: (b, 0, 0))
    full = pl.BlockSpec((HG_HEADS, HG_DK), lambda b: (0, 0))
    sspec = pl.BlockSpec((1, nb, HG_HEADS, HG_DK, HG_DK), lambda b: (0, b, 0, 0, 0))
    return pl.pallas_call(
        functools.partial(_hgrn_step_kernel, nb=nb),
        grid=(bsz // nb,),
        in_specs=[vec, vec, vec, vec, full, sspec],
        out_specs=[vec, sspec],
        out_shape=[jax.ShapeDtypeStruct((bsz, HG_HEADS, HG_DK), BF16),
                   jax.ShapeDtypeStruct(state.shape, state.dtype)],
        compiler_params=pltpu.CompilerParams(dimension_semantics=("parallel",),
                                             vmem_limit_bytes=V7X_VMEM_LIMIT),
        name="hgrn_step",
    )(q, f, v, g, onorm.reshape(HG_HEADS, HG_DK), state)


def _t5_bucket(dist):
    max_exact = REL_BUCKETS // 2
    d = jnp.maximum(dist, 1).astype(F32)
    large = max_exact + (jnp.log(d / max_exact) / math.log(REL_MAX_DIST / max_exact)
                         * (REL_BUCKETS - max_exact)).astype(jnp.int32)
    large = jnp.minimum(large, REL_BUCKETS - 1)
    return jnp.where(dist < max_exact, dist, large)


def _prompt_bias_kernel(w_ref, o_ref):
    band = 2 * WINDOW
    hq = WINDOW // 2
    key = lax.broadcasted_iota(jnp.int32, (band, WINDOW), 0)
    lane = lax.broadcasted_iota(jnp.int32, (band, WINDOW), 1)

    def toeplitz(g):
        row = jnp.broadcast_to(w_ref[g:g + 1, :], (band, 3 * WINDOW))
        return pltpu.roll(row, 0, 1, stride=1, stride_axis=0)[:, :WINDOW]

    for half in range(2):
        for m in range(SW_GROUP // 4):
            ta, tb = toeplitz(2 * (2 * m) + half), toeplitz(2 * (2 * m + 1) + half)
            blocks = [jnp.where(lane < hq, ta, pltpu.roll(tb, hq, axis=1)),
                      jnp.where(lane < hq, pltpu.roll(ta, hq, axis=1), tb)]
            for qh in range(2):
                c = qh * (SW_GROUP // 4) + m
                dst = (slice(half * band, (half + 1) * band), slice(c * WINDOW, (c + 1) * WINDOW))
                o_ref[1, 0, dst[0], dst[1]] = blocks[qh]
                o_ref[0, 0, dst[0], dst[1]] = jnp.where(key < WINDOW - N_META, NEG * LOG2E,
                                                        blocks[qh])


def _prompt_bias(table):
    band = 2 * WINDOW
    i = jnp.arange(3 * WINDOW)
    dist = jnp.where(i < WINDOW, i + WINDOW, i - band)
    vals = table.astype(F32)[_t5_bucket(jnp.maximum(dist, 0))]
    vals = jnp.where(((dist >= 0) & (dist <= WINDOW))[:, None], vals, NEG) * LOG2E
    return pl.pallas_call(
        _prompt_bias_kernel,
        grid=(SW_KV,),
        in_specs=[pl.BlockSpec((SW_GROUP, 3 * WINDOW), lambda n: (n, 0))],
        out_specs=pl.BlockSpec((2, 1, 2 * band, 4 * WINDOW), lambda n: (0, n, 0, 0)),
        out_shape=jax.ShapeDtypeStruct((2, SW_KV, 2 * band, 4 * WINDOW), F32),
        compiler_params=pltpu.CompilerParams(dimension_semantics=("parallel",)),
        name="prompt_bias",
    )(vals.T)


def _expand_band(pair, u):
    lane = lax.broadcasted_iota(jnp.int32, pair.shape, 1)
    rolled = pltpu.roll(pair, SW_HD, axis=1)
    lo_src, hi_src = (pair, rolled) if u == 0 else (rolled, pair)
    top = jnp.where(lane < SW_HD, lo_src, 0.0)
    bot = jnp.where(lane >= SW_HD, hi_src, 0.0)
    return jnp.concatenate([top, bot], axis=0).astype(BF16)


def _swa_probs(s, bias, sinks):
    band, hq = 2 * WINDOW, WINDOW // 2
    live_rows = band - hq
    n_half = s[0].shape[1]
    dead = jnp.zeros((hq, n_half), BF16)
    ps, ms = [], []
    for half in range(2):
        cols, mcols = [], []
        for qh in range(2):
            rows = slice(half * band + qh * hq, half * band + qh * hq + live_rows)
            lanes = slice(qh * n_half, (qh + 1) * n_half)
            sv = s[qh][half * live_rows:(half + 1) * live_rows] + bias[rows, lanes]
            m = jnp.maximum(jnp.max(sv, axis=0, keepdims=True), sinks[half][:, lanes])
            live = jnp.exp2(sv - m).astype(BF16)
            cols.append(jnp.concatenate([live, dead] if qh == 0 else [dead, live], axis=0))
            mcols.append(m)
        ps.append(jnp.concatenate(cols, axis=1))
        ms.append(jnp.concatenate(mcols, axis=1))
    return jnp.concatenate(ps, axis=0), ms


def _swa_values(pt, vxt, ms, sinks):
    ot = _nn(vxt, pt)
    outs = []
    for half in range(2):
        den = ot[2 * SW_HD + 8 * half:2 * SW_HD + 8 * half + 1] + jnp.exp2(sinks[half] - ms[half])
        outs.append(ot[SW_HD * half:SW_HD * (half + 1)] * (1.0 / den))
    return jnp.concatenate(outs, axis=0)


def _swa_prompt_kernel(sink_ref, q_ref, g_ref, k_ref, v_ref, km_ref, vm_ref, bias_ref,
                       o_ref, *, n_blocks):
    p = pl.program_id(0)
    qb = pl.program_id(2)
    n_pairs = SW_GROUP // 2
    band = 2 * WINDOW

    hq = WINDOW // 2

    def sink_rows(u):
        return [jnp.concatenate(
            [jnp.full((1, hq), sink_ref[(p * 2 + u) * SW_GROUP + 2 * j + half] * LOG2E, F32)
             for _ in range(2) for j in range(n_pairs)], axis=1) for half in range(2)]

    sinks = [sink_rows(u) for u in range(2)]
    orow = lax.broadcasted_iota(jnp.int32, (16, 2 * band), 0)
    ocol = lax.broadcasted_iota(jnp.int32, (16, 2 * band), 1)
    ones_rows = jnp.where((orow < 8) == (ocol < band), 1.0, 0.0).astype(BF16)
    zeros = jnp.zeros((SW_HD, band), BF16)

    def band_of(blk):
        tok = qb * n_blocks + blk
        cur = pl.ds(pl.multiple_of(tok * WINDOW, WINDOW), WINDOW)
        prev = pl.ds(pl.multiple_of(jnp.maximum(tok - 1, 0) * WINDOW, WINDOW), WINDOW)
        is_first = tok == 0
        kband = jnp.concatenate([jnp.where(is_first, km_ref[...], k_ref[0, prev, :]),
                                 k_ref[0, cur, :]], axis=0)
        vband = jnp.concatenate([jnp.where(is_first, vm_ref[...], v_ref[0, prev, :]),
                                 v_ref[0, cur, :]], axis=0)
        return (pl.multiple_of(blk * WINDOW, WINDOW), kband,
                vband.T.astype(BF16), jnp.where(is_first, 0, 1))

    def blocks(it, carry):
        bands = [band_of(it * SWA_BLOCKS_PER_ITER + i) for i in range(SWA_BLOCKS_PER_ITER)]
        units = [(bnd, u) for bnd in bands for u in range(2)]
        scores = []
        for (row0, kband, _, _), u in units:
            base = u * SW_GROUP * SW_HD
            q4 = jnp.concatenate(
                [q_ref[0, pl.ds(pl.multiple_of(row0 + qh * hq, hq), hq),
                       base + j * LANES:base + (j + 1) * LANES]
                 for qh in range(2) for j in range(n_pairs)],
                axis=0).astype(BF16)
            kx = _expand_band(kband, u)
            live = band - hq
            scores.append([
                _nt(jnp.concatenate([kx[qh * hq:qh * hq + live],
                                     kx[band + qh * hq:band + qh * hq + live]], axis=0),
                    q4[qh * n_pairs * hq:(qh + 1) * n_pairs * hq]) for qh in range(2)])
        probs = [_swa_probs(s, bias_ref[bias_sel, u], sinks[u])
                 for s, ((_, _, _, bias_sel), u) in zip(scores, units)]
        outs = []
        for (pt, ms), ((_, _, vbt, _), u) in zip(probs, units):
            vt = vbt[u * SW_HD:(u + 1) * SW_HD]
            vxt = jnp.concatenate([jnp.concatenate([vt, zeros], axis=1),
                                   jnp.concatenate([zeros, vt], axis=1), ones_rows], axis=0)
            outs.append(_swa_values(pt, vxt, ms, sinks[u]))
        for ot, ((row0, _, _, _), u) in zip(outs, units):
            base = u * SW_GROUP * SW_HD
            rows = pl.ds(row0, WINDOW)
            tr = [ot[:, c * LANES:(c + 1) * LANES].T for c in range(n_pairs)]
            o = jnp.concatenate(
                [jnp.concatenate([tr[qh * 2 + j // 2][(j % 2) * hq:(j % 2 + 1) * hq]
                                  for j in range(n_pairs)], axis=1) for qh in range(2)],
                axis=0)
            gate = g_ref[0, rows, base:base + SW_GROUP * SW_HD]
            o_ref[0, rows, base:base + SW_GROUP * SW_HD] = (o * gate).astype(BF16)
        return carry

    lax.fori_loop(0, n_blocks // SWA_BLOCKS_PER_ITER, blocks, 0)


def _swa_prompt(proj, meta_k, meta_v, sinks, bias, *, tq):
    bsz, tlen, _ = proj.shape
    pw = 2 * SW_GROUP * SW_HD
    qblocks = SW_HEADS * SW_HD // pw
    kcol0 = SW_HEADS * SW_HD // LANES
    vcol0 = kcol0 + SW_KV * SW_HD // LANES
    gblk0 = (SW_HEADS * SW_HD + 2 * SW_KV * SW_HD) // pw
    return pl.pallas_call(
        functools.partial(_swa_prompt_kernel, n_blocks=tq // WINDOW),
        grid=(qblocks, bsz, tlen // tq),
        in_specs=[pl.BlockSpec(memory_space=pltpu.SMEM),
                  pl.BlockSpec((1, tq, pw), lambda p, b, t: (b, t, p)),
                  pl.BlockSpec((1, tq, pw), lambda p, b, t: (b, t, gblk0 + p)),
                  pl.BlockSpec((1, tlen, LANES), lambda p, b, t: (b, 0, kcol0 + p)),
                  pl.BlockSpec((1, tlen, LANES), lambda p, b, t: (b, 0, vcol0 + p)),
                  pl.BlockSpec((WINDOW, LANES), lambda p, b, t: (0, p)),
                  pl.BlockSpec((WINDOW, LANES), lambda p, b, t: (0, p)),
                  pl.BlockSpec((2, 2, 4 * WINDOW, 4 * WINDOW), lambda p, b, t: (0, p, 0, 0))],
        out_specs=pl.BlockSpec((1, tq, pw), lambda p, b, t: (b, t, p)),
        out_shape=jax.ShapeDtypeStruct((bsz, tlen, D_MODEL), BF16),
        compiler_params=pltpu.CompilerParams(
            dimension_semantics=("parallel", "parallel", "arbitrary"),
            vmem_limit_bytes=V7X_VMEM_LIMIT),
        name="swa_prompt",
    )(sinks, proj, proj, proj, proj, meta_k, meta_v, bias)


def _swa_sample_kernel(q_ref, g_ref, kn_ref, vn_ref, ck_ref, cv_ref, bc_ref, bn_ref, sink_ref,
                       o_ref, nk_ref, nv_ref, *, nb):
    nkv = SW_KV * SW_HD
    r = ck_ref.shape[1]
    row = lax.broadcasted_iota(jnp.int32, (SW_HEADS, nkv), 0)
    col = lax.broadcasted_iota(jnp.int32, (SW_HEADS, nkv), 1)
    own = (row // SW_GROUP) == (col // SW_HD)
    last = lax.broadcasted_iota(jnp.int32, (r, nkv), 0) == r - 1
    sink = sink_ref[...] * LOG2E
    for bi in range(nb):
        q = q_ref[bi]
        qx = jnp.where(own, jnp.concatenate([q] * SW_KV, axis=1), 0.0)
        ck, cv = ck_ref[bi], cv_ref[bi]
        kn, vn = kn_ref[bi], vn_ref[bi]

        s_c = _nt(qx.astype(BF16), ck.astype(BF16)) + bc_ref[...]
        s_n = jnp.sum(qx * kn, axis=-1, keepdims=True) + bn_ref[...]
        m = jnp.maximum(jnp.maximum(jnp.max(s_c, axis=-1, keepdims=True), s_n), sink)
        p_c = jnp.exp2(s_c - m)
        p_n = jnp.exp2(s_n - m)
        den = jnp.sum(p_c, axis=-1, keepdims=True) + p_n + jnp.exp2(sink - m)
        o_all = _nn(p_c.astype(BF16), cv.astype(BF16))
        o_all = o_all + p_n * vn
        o_all = jnp.where(own, o_all, 0.0)
        o = o_all[:, 0:SW_HD]
        for n in range(1, SW_KV):
            o = o + o_all[:, n * SW_HD:(n + 1) * SW_HD]
        o_ref[bi] = ((o / den) * g_ref[bi]).astype(BF16)

        nk_ref[bi] = jnp.where(last, kn, pltpu.roll(ck, r - 1, axis=0))
        nv_ref[bi] = jnp.where(last, vn, pltpu.roll(cv, r - 1, axis=0))


def _swa_sample(q, g, kn, vn, ck, cv, bias_c, bias_n, sinks, *, nb):
    bsz, r, nkv = ck.shape
    head = pl.BlockSpec((nb, SW_HEADS, SW_HD), lambda b: (b, 0, 0))
    new = pl.BlockSpec((nb, 1, nkv), lambda b: (b, 0, 0))
    cache = pl.BlockSpec((nb, r, nkv), lambda b: (b, 0, 0))
    return pl.pallas_call(
        functools.partial(_swa_sample_kernel, nb=nb),
        grid=(bsz // nb,),
        in_specs=[head, head, new, new, cache, cache,
                  pl.BlockSpec((SW_HEADS, r), lambda b: (0, 0)),
                  pl.BlockSpec((SW_HEADS, 1), lambda b: (0, 0)),
                  pl.BlockSpec((SW_HEADS, 1), lambda b: (0, 0))],
        out_specs=[head, cache, cache],
        out_shape=[jax.ShapeDtypeStruct((bsz, SW_HEADS, SW_HD), BF16),
                   jax.ShapeDtypeStruct(ck.shape, ck.dtype),
                   jax.ShapeDtypeStruct(cv.shape, cv.dtype)],
        compiler_params=pltpu.CompilerParams(dimension_semantics=("parallel",)),
        name="swa_sample",
    )(q, g, kn, vn, ck, cv, bias_c, bias_n, sinks.reshape(SW_HEADS, 1))


def kernel(x_prompt, x_sample, state_hgrn, cache_k_win, cache_v_win, meta_tokens, rel_bias,
           hg_lower_bounds, hg_norm, hg_w_in, hg_onorm, hg_w_out,
           sw_norm, sw_w_in, sw_sinks, sw_w_out, final_norm):
    n_samp = x_sample.shape[0]
    samp = slice(N_META, N_META + n_samp)
    x_main = x_prompt.reshape(MAIN_ROWS, D_MODEL)
    x_small = jnp.concatenate(
        [meta_tokens.astype(F32), x_sample.reshape(n_samp, D_MODEL),
         jnp.zeros((N_SMALL - N_META - n_samp, D_MODEL), F32)], axis=0)

    h_main = _rmsnorm(x_main, hg_norm[0], BF16, 512)
    h_small = _rmsnorm(x_small, hg_norm[0], BF16, N_SMALL)
    p_main, p_small = _proj(h_main, h_small, hg_w_in[0], tm=PROJ_TM, tn=PROJ_TN,
                            sections="hgrn", lbraw=hg_lower_bounds.astype(F32))

    meta_proj = jnp.pad(p_small[:N_META], ((CHUNK - N_META, 0), (0, 0)))[None]
    zero_state = jnp.zeros((HG_HEADS, HG_DK, HG_DK), F32)
    a_meta, s_meta = _hgrn_scan(meta_proj, hg_onorm[0], zero_state,
                                tb=CHUNK, hb=8, n_pad=CHUNK - N_META)
    a_main, s_prompt = _hgrn_scan(p_main.reshape(BATCH, SEQ, -1), hg_onorm[0],
                                  s_meta[0], tb=512, hb=16)
    sect = lambda s: p_small[samp, s * HG_F:(s + 1) * HG_F].reshape(n_samp, HG_HEADS, HG_DK)
    a_samp, s_sample = _hgrn_step(sect(0), sect(1), sect(2), sect(3),
                                  hg_onorm[0], state_hgrn, nb=SAMPLES_PER_STEP)
    a_small = jnp.concatenate(
        [a_meta[0, CHUNK - N_META:], a_samp.reshape(n_samp, D_MODEL),
         jnp.zeros((N_SMALL - N_META - n_samp, D_MODEL), BF16)], axis=0)
    x1_main, x1_small, x1b_main, x1b_small, sq_main, sq_small = _proj(
        a_main.reshape(MAIN_ROWS, D_MODEL), a_small, hg_w_out[0], x_main, x_small,
        tm=PROJ_RES_TM, tn=PROJ_TN, emit_stats=True)

    p_main, p_small = _proj(x1b_main, x1b_small, sw_w_in[0],
                            scale=(sw_norm[0], jnp.sum(sq_main, axis=0, keepdims=True),
                                   jnp.sum(sq_small, axis=0, keepdims=True)),
                            tm=PROJ_TM, tn=PROJ_TN, sections="swa")
    nq, nkv = SW_HEADS * SW_HD, SW_KV * SW_HD
    kv_tail = p_main.reshape(BATCH, SEQ, -1)[:, -WINDOW:, nq:nq + 2 * nkv]
    k_tail = kv_tail[:, :, :nkv].reshape(1, BATCH, WINDOW, SW_KV, SW_HD)
    v_tail = kv_tail[:, :, nkv:].reshape(1, BATCH, WINDOW, SW_KV, SW_HD)
    meta_kv = jnp.pad(p_small[:N_META, nq:nq + 2 * nkv], ((WINDOW - N_META, 0), (0, 0)))
    a_main = _swa_prompt(p_main.reshape(BATCH, SEQ, -1), meta_kv[:, :nkv], meta_kv[:, nkv:],
                         sw_sinks[0], _prompt_bias(rel_bias), tq=1024)

    r = cache_k_win.shape[2]
    table = rel_bias.astype(F32)
    bias_c = table[_t5_bucket(r - jnp.arange(r))].T * LOG2E
    bias_n = table[_t5_bucket(jnp.zeros((1,), jnp.int32))].T * LOG2E
    q_s = p_small[samp, :nq].reshape(n_samp, SW_HEADS, SW_HD)
    g_s = p_small[samp, nq + 2 * nkv:].reshape(n_samp, SW_HEADS, SW_HD)
    a_samp, k_samp, v_samp = _swa_sample(
        q_s, g_s, p_small[samp, nq:nq + nkv].reshape(n_samp, 1, nkv),
        p_small[samp, nq + nkv:nq + 2 * nkv].reshape(n_samp, 1, nkv),
        cache_k_win[0].reshape(n_samp, r, nkv), cache_v_win[0].reshape(n_samp, r, nkv),
        bias_c, bias_n, sw_sinks[0], nb=SAMPLES_PER_STEP)
    a_small = jnp.concatenate(
        [jnp.zeros((N_META, D_MODEL), BF16), a_samp.reshape(n_samp, D_MODEL),
         jnp.zeros((N_SMALL - N_META - n_samp, D_MODEL), BF16)], axis=0)
    x2_main, x2_small = _proj(a_main.reshape(MAIN_ROWS, D_MODEL), a_small, sw_w_out[0],
                              x1_main, x1_small, tm=PROJ_RES_TM, tn=PROJ_TN)

    y_prompt = _rmsnorm(x2_main, final_norm, F32, 512).reshape(BATCH, SEQ, D_MODEL)
    y_sample = _rmsnorm(x2_small, final_norm, F32, N_SMALL)[samp].reshape(n_samp, 1, D_MODEL)
    return (y_prompt, y_sample, s_prompt[None],
            k_tail.astype(cache_k_win.dtype), v_tail.astype(cache_v_win.dtype),
            s_sample,
            k_samp.reshape(1, n_samp, r, SW_KV, SW_HD), v_samp.reshape(1, n_samp, r, SW_KV, SW_HD))
```

```python
import functools
import math

import numpy as np
import jax
import jax.numpy as jnp
from jax import lax
from jax.experimental import pallas as pl
from jax.experimental.pallas import tpu as pltpu

D_MODEL = 4096
BATCH = 4
SEQ = 2048
DEC_BATCH = 32
N_META = 16
HG_HEADS = 32
HG_DK = 128
HG_F = HG_HEADS * HG_DK
SW_HEADS = 64
SW_KV = 8
SW_HD = 64
SW_GROUP = SW_HEADS // SW_KV
WINDOW = 128
REL_BUCKETS = 32
REL_MAX_DIST = 128
EPS = 1e-6
NEG = -1e30
LOG2E = math.log2(math.e)

LANES = 128
V7X_VMEM_LIMIT = 58 * 1024 * 1024

PROJ_TM = 1024
PROJ_TN = 1024
PROJ_RES_TM = 512
CHUNK = 128
SWA_BLOCKS_PER_ITER = 4
SAMPLES_PER_STEP = 4
N_SMALL = 64
MAIN_ROWS = BATCH * SEQ

F32 = jnp.float32
BF16 = jnp.bfloat16


def _nt(a, b):
    return lax.dot_general(a, b, (((1,), (1,)), ((), ())), preferred_element_type=F32)


def _nn(a, b):
    return jnp.dot(a, b, preferred_element_type=F32)


def _sigmoid(x):
    return 1.0 / (1.0 + jnp.exp(-x))


def _silu(x):
    h = 0.5 * x
    return h + h * jnp.tanh(h)


def _rmsnorm_kernel(x_ref, g_ref, o_ref):
    x = x_ref[...]
    ms = jnp.mean(x * x, axis=-1, keepdims=True)
    o_ref[...] = (x * lax.rsqrt(ms + EPS) * g_ref[...]).astype(o_ref.dtype)


def _rmsnorm_ring_kernel(x_hbm, g_ref, o_ref, xbuf, sem, *, tm, n_steps):
    s = pl.program_id(0)

    def fetch(step, slot):
        return pltpu.make_async_copy(
            x_hbm.at[pl.ds(pl.multiple_of(step * tm, tm), tm), :], xbuf.at[slot], sem.at[slot])

    @pl.when(s == 0)
    def _():
        fetch(0, 0).start()
        fetch(1, 1).start()

    @pl.when(s + 2 < n_steps)
    def _():
        fetch(s + 2, lax.rem(s + 2, 3)).start()

    slot = lax.rem(s, 3)
    fetch(s, slot).wait()
    x = xbuf[slot]
    ms = jnp.mean(x * x, axis=-1, keepdims=True)
    o_ref[...] = (x * lax.rsqrt(ms + EPS) * g_ref[...]).astype(o_ref.dtype)


def _rmsnorm(x, g, out_dtype, tm):
    m, d = x.shape
    if m // tm >= 3:
        return pl.pallas_call(
            functools.partial(_rmsnorm_ring_kernel, tm=tm, n_steps=m // tm),
            grid=(m // tm,),
            in_specs=[pl.BlockSpec(memory_space=pl.ANY),
                      pl.BlockSpec((1, d), lambda i: (0, 0))],
            out_specs=pl.BlockSpec((tm, d), lambda i: (i, 0)),
            out_shape=jax.ShapeDtypeStruct((m, d), out_dtype),
            scratch_shapes=[pltpu.VMEM((3, tm, d), x.dtype), pltpu.SemaphoreType.DMA((3,))],
            compiler_params=pltpu.CompilerParams(dimension_semantics=("arbitrary",),
                                                 vmem_limit_bytes=V7X_VMEM_LIMIT),
            name="rmsnorm_ring",
        )(x, g.reshape(1, d))
    return pl.pallas_call(
        _rmsnorm_kernel,
        grid=(m // tm,),
        in_specs=[pl.BlockSpec((tm, d), lambda i: (i, 0)),
                  pl.BlockSpec((1, d), lambda i: (0, 0))],
        out_specs=pl.BlockSpec((tm, d), lambda i: (i, 0)),
        out_shape=jax.ShapeDtypeStruct((m, d), out_dtype),
        compiler_params=pltpu.CompilerParams(dimension_semantics=("parallel",),
                                             vmem_limit_bytes=V7X_VMEM_LIMIT),
        name="rmsnorm",
    )(x, g.reshape(1, d))


def _proj_kernel(*refs, tn, kc, cps, has_res, has_scale, emit_stats, sections):
    refs = list(refs)
    a_ref, as_ref, w_hbm = refs[:3]
    del refs[:3]
    if sections == "hgrn":
        lb_ref = refs.pop(0)
    if has_scale:
        gw_ref, sq_in_ref, sqs_in_ref = refs[:3]
        del refs[:3]
    if has_res:
        r_ref, rs_ref = refs[:2]
        del refs[:2]
    o_ref, os_ref = refs[:2]
    del refs[:2]
    if emit_stats:
        ob_ref, obs_ref, sq_ref, sqs_ref = refs[:4]
        del refs[:4]
    wb_ref, stage_ref, sem = refs
    j, i = pl.program_id(0), pl.program_id(1)
    nj, ni = pl.num_programs(0), pl.num_programs(1)
    kdim = w_hbm.shape[0]
    n_kc = kdim // kc
    step = j * ni + i
    cur = lax.rem(j, 2)
    par = lax.rem(step, 2)

    def aligned(x, m):
        return x * m if isinstance(x, int) else pl.multiple_of(x * m, m)

    def chunk_copy(tile, c, slot):
        return pltpu.make_async_copy(
            w_hbm.at[pl.ds(aligned(c, kc), kc), pl.ds(aligned(tile, tn), tn)],
            stage_ref.at[slot], sem.at[slot])

    def cast_chunk(slot, c, half):
        rows = pl.ds(aligned(c, kc), kc)
        w = stage_ref[slot]
        if has_scale:
            w = w * jnp.concatenate([gw_ref[rows, :]] * (tn // LANES), axis=1)
        wb_ref[half, rows, :] = w.astype(BF16)

    def next_tile(jj):
        return jnp.minimum(jj + 1, nj - 1)

    def forget_gate(x):
        lb = _lower_bound(lb_ref[...])
        return lb + (1.0 - lb) * _sigmoid(x)

    def identity(x):
        return x

    if sections == "hgrn":
        quarter = nj // 4
        gates = [_silu, forget_gate, identity]
        gate_id = jnp.where((j < quarter) | (j >= 3 * quarter), 0,
                            jnp.where(j < 2 * quarter, 1, 2))
    elif sections == "swa":
        q_tiles = SW_HEADS * SW_HD // tn
        gates = [lambda x: x * (SW_HD ** -0.5 * LOG2E), identity, _silu]
        gate_id = jnp.where(j < q_tiles, 0, jnp.where(j == q_tiles, 1, 2))
    else:
        gates, gate_id = [identity], None

    def per_section(body):
        if gate_id is None:
            body(gates[0])
        else:
            lax.cond(gate_id == 0, lambda: body(gates[0]),
                     lambda: lax.cond(gate_id == 1, lambda: body(gates[1]),
                                      lambda: body(gates[2])))

    def finish(out, gate, res_ref, sq_in, o, ob, sq):
        if has_scale:
            out = out * lax.rsqrt(jnp.sum(sq_in[...], axis=0) * (1.0 / kdim) + EPS)
        out = gate(out)
        if has_res:
            out = out + res_ref[...]
        o[...] = out
        if emit_stats:
            ob[...] = out.astype(BF16)
            sq[0] = jnp.sum(out * out, axis=1, keepdims=True)

    @pl.when(step == 0)
    def _():
        chunk_copy(0, 0, 0).start()
        for c in range(n_kc):
            if c + 1 < n_kc:
                chunk_copy(0, c + 1, (c + 1) % 2).start()
            chunk_copy(0, c, c % 2).wait()
            cast_chunk(c % 2, c, 0)
        for u in range(cps):
            chunk_copy(next_tile(0), u, u).start()

    for u in range(cps):
        chunk_copy(next_tile(j), i * cps + u, par * cps + u).wait()

    @pl.when(step + 1 < nj * ni)
    def _():
        wrap = i + 1 == ni
        j2 = jnp.where(wrap, j + 1, j)
        i2 = jnp.where(wrap, 0, i + 1)
        for u in range(cps):
            chunk_copy(next_tile(j2), i2 * cps + u, (1 - par) * cps + u).start()

    @pl.when(i == 0)
    def _():
        per_section(lambda gate: finish(
            _nn(as_ref[...], wb_ref[cur]), gate, rs_ref if has_res else None,
            sqs_in_ref if has_scale else None, os_ref,
            obs_ref if emit_stats else None, sqs_ref if emit_stats else None))

    def main(gate):
        for u in range(cps):
            cast_chunk(par * cps + u, i * cps + u, 1 - cur)
        finish(_nn(a_ref[...], wb_ref[cur]), gate, r_ref if has_res else None,
               sq_in_ref if has_scale else None, o_ref,
               ob_ref if emit_stats else None, sq_ref if emit_stats else None)

    per_section(main)


def _proj(a, a_small, w, res=None, res_small=None, scale=None, *, tm, tn, kc=256,
          emit_stats=False, sections=None, lbraw=None):
    m, k = a.shape
    n = w.shape[1]
    ms = a_small.shape[0]
    has_res, has_scale = res is not None, scale is not None
    ni, nj, n_kc = m // tm, n // tn, k // kc
    assert m % tm == 0 and n % tn == 0 and k % kc == 0 and n_kc % ni == 0 and nj >= 2
    cps = n_kc // ni
    in_specs = [pl.BlockSpec((tm, k), lambda j, i: (i, 0)),
                pl.BlockSpec((ms, k), lambda j, i: (0, 0)),
                pl.BlockSpec(memory_space=pl.ANY)]
    args = [a, a_small, w]
    if sections == "hgrn":
        quarter = nj // 4
        assert nj % 4 == 0 and lbraw.shape == (3, n // 4)
        in_specs += [pl.BlockSpec((3, tn), lambda j, i: (0, jnp.clip(j - quarter, 0, quarter - 1)))]
        args += [lbraw]
    if sections == "swa":
        assert tn == 2 * SW_KV * SW_HD and (SW_HEADS * SW_HD) % tn == 0
    if has_scale:
        gain, sq, sq_small = scale
        in_specs += [pl.BlockSpec((k, LANES), lambda j, i: (0, 0)),
                     pl.BlockSpec((sq.shape[0], tm, 1), lambda j, i: (0, i, 0)),
                     pl.BlockSpec((sq.shape[0], ms, 1), lambda j, i: (0, 0, 0))]
        args += [jnp.broadcast_to(gain.astype(F32)[:, None], (k, LANES)), sq, sq_small]
    if has_res:
        in_specs += [pl.BlockSpec((tm, tn), lambda j, i: (i, j)),
                     pl.BlockSpec((ms, tn), lambda j, i: (0, j))]
        args += [res, res_small]
    out_specs = [pl.BlockSpec((tm, tn), lambda j, i: (i, j)),
                 pl.BlockSpec((ms, tn), lambda j, i: (0, j))]
    out_shape = [jax.ShapeDtypeStruct((m, n), F32), jax.ShapeDtypeStruct((ms, n), F32)]
    if emit_stats:
        out_specs += [pl.BlockSpec((tm, tn), lambda j, i: (i, j)),
                      pl.BlockSpec((ms, tn), lambda j, i: (0, j)),
                      pl.BlockSpec((1, tm, 1), lambda j, i: (j, i, 0)),
                      pl.BlockSpec((1, ms, 1), lambda j, i: (j, 0, 0))]
        out_shape += [jax.ShapeDtypeStruct((m, n), BF16), jax.ShapeDtypeStruct((ms, n), BF16),
                      jax.ShapeDtypeStruct((nj, m, 1), F32), jax.ShapeDtypeStruct((nj, ms, 1), F32)]
    return pl.pallas_call(
        functools.partial(_proj_kernel, tn=tn, kc=kc, cps=cps, has_res=has_res,
                          has_scale=has_scale, emit_stats=emit_stats, sections=sections),
        grid=(nj, ni),
        in_specs=in_specs,
        out_specs=out_specs,
        out_shape=out_shape,
        scratch_shapes=[pltpu.VMEM((2, k, tn), BF16),
                        pltpu.VMEM((2 * cps, kc, tn), F32),
                        pltpu.SemaphoreType.DMA((2 * cps,))],
        compiler_params=pltpu.CompilerParams(
            dimension_semantics=("arbitrary", "arbitrary"),
            vmem_limit_bytes=V7X_VMEM_LIMIT),
        name="proj_res" if has_res else "proj",
    )(*args)


def _hgrn_consts():
    t = np.arange(CHUNK)[:, None]
    s = np.arange(CHUNK)[None, :]
    tri = s <= t
    ends = np.array([15, 47, 79, 111, 31, 95, 63, 127] + [-1] * 8)[:, None]
    mall = np.concatenate([tri, s <= ends], axis=0).astype(np.float32)
    lvl = np.full((CHUNK, CHUNK), 3, np.int32)
    lvl[(t >= 64) & (s < 64)] = 2
    lvl[(t // 64 == s // 64) & (t % 64 >= 32) & (s % 64 < 32)] = 1
    lvl[(t // 32 == s // 32) & (s <= t)] = 0
    return jnp.asarray(mall, BF16), jnp.asarray(lvl)


def _lower_bound(lbraw):
    mx = jnp.max(lbraw, axis=0, keepdims=True)
    e = jnp.exp(lbraw - mx)
    return e[0:1, :] / jnp.sum(e, axis=0, keepdims=True)


def _head_out(o, gate, onorm):
    ms = jnp.mean(o * o, axis=-1, keepdims=True)
    return (o * lax.rsqrt(ms + EPS) * onorm) * gate


def _hgrn_gates(q, f, mall, row_valid):
    logf = jnp.log2(f)
    k = 1.0 - f
    if row_valid is not None:
        logf = jnp.where(row_valid, logf, 0.0)
        k = jnp.where(row_valid, k, 0.0)

    hi = logf.astype(BF16)
    mid = (logf - hi.astype(F32)).astype(BF16)
    cs = _nn(mall, jnp.concatenate([hi, mid], axis=1))
    return q, k, cs[:, :LANES] + cs[:, LANES:]


def _hgrn_scores(q, k, cs):
    b = cs[0:CHUNK]
    ref = cs[CHUNK:CHUNK + 8]
    r = lambda i: ref[i:i + 1]
    cat = lambda parts: jnp.concatenate(parts, axis=0)
    blk = lambda x, i: x[32 * i:32 * i + 32]
    e0 = jnp.exp2(cat([blk(b, i) - r(i) for i in range(4)]))
    q0, k0 = q * e0, k * (1.0 / e0)
    c = jnp.exp2(cat([r(1) - r(4), r(3) - r(5), r(4) - r(0), r(5) - r(2),
                      r(2) - r(6), r(3) - r(6), r(6) - r(0), r(6) - r(1)]))
    ce = jnp.exp2(cat([r(0), r(1), r(2), r(3), r(7) - r(0), r(7) - r(1), r(7) - r(2), r(7) - r(3)]))
    zero32 = jnp.zeros((32, LANES), BF16)
    bf = lambda x: x.astype(BF16)
    a0 = _nt(bf(q0), bf(k0))
    a1 = _nt(bf(cat([blk(q0, 1) * c[0:1], blk(q0, 3) * c[1:2]])),
             cat([bf(blk(k0, 0) * c[2:3]), zero32, bf(blk(k0, 2) * c[3:4]), zero32]))
    a2 = _nt(bf(cat([blk(q0, 2) * c[4:5], blk(q0, 3) * c[5:6]])),
             cat([bf(blk(k0, 0) * c[6:7]), bf(blk(k0, 1) * c[7:8]), zero32, zero32]))
    qe = bf(cat([blk(q0, i) * ce[i:i + 1] for i in range(4)]))
    kd = bf(cat([blk(k0, i) * ce[4 + i:5 + i] for i in range(4)]))
    return a0, a1, a2, qe, kd, jnp.exp2(r(7))


def _hgrn_apply(a0, a1, a2, qe, kd, decay, v, st, lvl):
    is0, is1, is2 = lvl == 0, lvl == 1, lvl == 2
    a = jnp.concatenate([
        jnp.where(is0[0:32], a0[0:32], 0.0),
        jnp.where(is0[32:64], a0[32:64], jnp.where(is1[32:64], a1[0:32], 0.0)),
        jnp.where(is0[64:96], a0[64:96], jnp.where(is2[64:96], a2[0:32], 0.0)),
        jnp.where(is0[96:128], a0[96:128],
                  jnp.where(is1[96:128], a1[32:64], jnp.where(is2[96:128], a2[32:64], 0.0))),
    ], axis=0)
    vt = v.T.astype(BF16)
    lhs = jnp.concatenate([a.astype(BF16), qe], axis=1)
    rhs = jnp.concatenate([vt, st.astype(BF16)], axis=1)
    return _nt(lhs, rhs), st * decay + _nn(vt, kd)


def _hgrn_scan_kernel(q_ref, f_ref, v_ref, g_ref, on_ref, s0_ref, mall_ref, lvl_ref,
                      a_ref, s_ref, st_ref, *, hb, n_chunks, n_pad):
    t = pl.program_id(2)

    @pl.when(t == 0)
    def _():
        for h in range(hb):
            st_ref[h] = s0_ref[h].T

    row_valid = None
    if n_pad:
        row_valid = lax.broadcasted_iota(jnp.int32, (CHUNK, LANES), 0) >= n_pad
    heads = [slice(h * LANES, (h + 1) * LANES) for h in range(hb)]

    def body(c, carry):
        rows = pl.ds(pl.multiple_of(c * CHUNK, CHUNK), CHUNK)
        gates = [_hgrn_gates(q_ref[0, rows, cols], f_ref[0, rows, cols], mall_ref[...], row_valid)
                 for cols in heads]
        scores = [_hgrn_scores(*g) for g in gates]
        outs = [_hgrn_apply(*sc, v_ref[0, rows, cols], st_ref[h], lvl_ref[...])
                for h, (sc, cols) in enumerate(zip(scores, heads))]
        for h, ((o, st_new), cols) in enumerate(zip(outs, heads)):
            st_ref[h] = st_new
            a_ref[0, rows, cols] = _head_out(o, g_ref[0, rows, cols], on_ref[:, cols]).astype(BF16)
        return carry

    lax.fori_loop(0, n_chunks, body, 0)

    @pl.when(t == pl.num_programs(2) - 1)
    def _():
        for h in range(hb):
            s_ref[0, h] = st_ref[h].T


def _hgrn_scan(proj, onorm, s0, *, tb, hb, n_pad=0):
    bsz, tlen, _ = proj.shape
    mall, lvl = _hgrn_consts()
    hcols = HG_F // (hb * LANES)
    sect = lambda s: (lambda b, h, t: (b, t, s * hcols + h))
    blk = (1, tb, hb * LANES)
    return pl.pallas_call(
        functools.partial(_hgrn_scan_kernel, hb=hb, n_chunks=tb // CHUNK, n_pad=n_pad),
        grid=(bsz, HG_HEADS // hb, tlen // tb),
        in_specs=[pl.BlockSpec(blk, sect(0)), pl.BlockSpec(blk, sect(1)),
                  pl.BlockSpec(blk, sect(2)), pl.BlockSpec(blk, sect(3)),
                  pl.BlockSpec((1, hb * LANES), lambda b, h, t: (0, h)),
                  pl.BlockSpec((hb, HG_DK, HG_DK), lambda b, h, t: (h, 0, 0)),
                  pl.BlockSpec((CHUNK + 16, CHUNK), lambda b, h, t: (0, 0)),
                  pl.BlockSpec((CHUNK, CHUNK), lambda b, h, t: (0, 0))],
        out_specs=[pl.BlockSpec(blk, lambda b, h, t: (b, t, h)),
                   pl.BlockSpec((1, hb, HG_DK, HG_DK), lambda b, h, t: (b, h, 0, 0))],
        out_shape=[jax.ShapeDtypeStruct((bsz, tlen, D_MODEL), BF16),
                   jax.ShapeDtypeStruct((bsz, HG_HEADS, HG_DK, HG_DK), F32)],
        scratch_shapes=[pltpu.VMEM((hb, HG_DK, HG_DK), F32)],
        compiler_params=pltpu.CompilerParams(
            dimension_semantics=("parallel", "parallel", "arbitrary"),
            vmem_limit_bytes=V7X_VMEM_LIMIT),
        name="hgrn_scan",
    )(proj, proj, proj, proj, onorm.reshape(1, D_MODEL), s0, mall, lvl)


def _hgrn_step_kernel(q_ref, f_ref, v_ref, g_ref, on_ref, s0_ref, a_ref, s_ref, *, nb):
    def split(x):
        hi = x.astype(BF16).astype(F32)
        return hi, x - hi

    r = lax.broadcasted_iota(jnp.int32, (8, HG_DK), 0)
    ones_tail = jnp.concatenate([jnp.where((r == 3) | (r == 4), 1.0, 0.0),
                                 jnp.where((r == 5) | (r == 6), 1.0, 0.0)], axis=1).astype(BF16)
    k_hi_rows, k_lo_rows = r < 2, (r == 2) | (r == 7)
    v_hi_rows, v_lo_rows = (r == 0) | (r == 2), (r == 1) | (r == 7)
    for bi in range(nb):
        f = f_ref[bi]
        decay = jnp.exp(jnp.log(f))
        k = 1.0 - f
        (kh, kl), (dh, dl), (qh, ql), (vh, vl) = (split(k), split(decay), split(q_ref[bi]),
                                                  split(v_ref[bi]))
        rows = []
        for h in range(HG_HEADS):
            row = lambda x: jnp.broadcast_to(x[h:h + 1, :], (8, HG_DK))
            lhs = jnp.where(k_hi_rows, row(kh), jnp.where(k_lo_rows, row(kl), jnp.where(
                r == 3, row(dh), jnp.where(r == 4, row(dl), jnp.where(
                    r == 5, row(qh), row(ql))))))
            rhs_v = jnp.where(v_hi_rows, row(vh), jnp.where(v_lo_rows, row(vl), 0.0))
            rhs = jnp.concatenate([rhs_v.astype(BF16), ones_tail], axis=1)
            out = lax.dot_general(lhs.astype(BF16), rhs, (((0,), (0,)), ((), ())),
                                  preferred_element_type=F32)
            s_new = out[:, HG_DK:2 * HG_DK] * s0_ref[0, bi, h] + out[:, :HG_DK]
            s_ref[0, bi, h] = s_new
            rows.append(jnp.sum(out[:, 2 * HG_DK:] * s_new, axis=0, keepdims=True))
        o = jnp.concatenate(rows, axis=0)
        a_ref[bi] = _head_out(o, g_ref[bi], on_ref[...]).astype(BF16)


def _hgrn_step(q, f, v, g, onorm, state, *, nb):
    bsz = q.shape[0]
    vec = pl.BlockSpec((nb, HG_HEADS, HG_DK), lambda b: (b, 0, 0))
    full = pl.BlockSpec((HG_HEADS, HG_DK), lambda b: (0, 0))
    sspec = pl.BlockSpec((1, nb, HG_HEADS, HG_DK, HG_DK), lambda b: (0, b, 0, 0, 0))
    return pl.pallas_call(
        functools.partial(_hgrn_step_kernel, nb=nb),
        grid=(bsz // nb,),
        in_specs=[vec, vec, vec, vec, full, sspec],
        out_specs=[vec, sspec],
        out_shape=[jax.ShapeDtypeStruct((bsz, HG_HEADS, HG_DK), BF16),
                   jax.ShapeDtypeStruct(state.shape, state.dtype)],
        compiler_params=pltpu.CompilerParams(dimension_semantics=("parallel",),
                                             vmem_limit_bytes=V7X_VMEM_LIMIT),
        name="hgrn_step",
    )(q, f, v, g, onorm.reshape(HG_HEADS, HG_DK), state)


def _t5_bucket(dist):
    max_exact = REL_BUCKETS // 2
    d = jnp.maximum(dist, 1).astype(F32)
    large = max_exact + (jnp.log(d / max_exact) / math.log(REL_MAX_DIST / max_exact)
                         * (REL_BUCKETS - max_exact)).astype(jnp.int32)
    large = jnp.minimum(large, REL_BUCKETS - 1)
    return jnp.where(dist < max_exact, dist, large)


def _prompt_bias_kernel(w_ref, o_ref):
    band = 2 * WINDOW
    hq = WINDOW // 2
    key = lax.broadcasted_iota(jnp.int32, (band, WINDOW), 0)
    lane = lax.broadcasted_iota(jnp.int32, (band, WINDOW), 1)

    def toeplitz(g):
        row = jnp.broadcast_to(w_ref[g:g + 1, :], (band, 3 * WINDOW))
        return pltpu.roll(row, 0, 1, stride=1, stride_axis=0)[:, :WINDOW]

    for half in range(2):
        for m in range(SW_GROUP // 4):
            ta, tb = toeplitz(2 * (2 * m) + half), toeplitz(2 * (2 * m + 1) + half)
            blocks = [jnp.where(lane < hq, ta, pltpu.roll(tb, hq, axis=1)),
                      jnp.where(lane < hq, pltpu.roll(ta, hq, axis=1), tb)]
            for qh in range(2):
                c = qh * (SW_GROUP // 4) + m
                dst = (slice(half * band, (half + 1) * band), slice(c * WINDOW, (c + 1) * WINDOW))
                o_ref[1, 0, dst[0], dst[1]] = blocks[qh]
                o_ref[0, 0, dst[0], dst[1]] = jnp.where(key < WINDOW - N_META, NEG * LOG2E,
                                                        blocks[qh])


def _prompt_bias(table):
    band = 2 * WINDOW
    i = jnp.arange(3 * WINDOW)
    dist = jnp.where(i < WINDOW, i + WINDOW, i - band)
    vals = table.astype(F32)[_t5_bucket(jnp.maximum(dist, 0))]
    vals = jnp.where(((dist >= 0) & (dist <= WINDOW))[:, None], vals, NEG) * LOG2E
    return pl.pallas_call(
        _prompt_bias_kernel,
        grid=(SW_KV,),
        in_specs=[pl.BlockSpec((SW_GROUP, 3 * WINDOW), lambda n: (n, 0))],
        out_specs=pl.BlockSpec((2, 1, 2 * band, 4 * WINDOW), lambda n: (0, n, 0, 0)),
        out_shape=jax.ShapeDtypeStruct((2, SW_KV, 2 * band, 4 * WINDOW), F32),
        compiler_params=pltpu.CompilerParams(dimension_semantics=("parallel",)),
        name="prompt_bias",
    )(vals.T)


def _expand_band(pair, u):
    lane = lax.broadcasted_iota(jnp.int32, pair.shape, 1)
    rolled = pltpu.roll(pair, SW_HD, axis=1)
    lo_src, hi_src = (pair, rolled) if u == 0 else (rolled, pair)
    top = jnp.where(lane < SW_HD, lo_src, 0.0)
    bot = jnp.where(lane >= SW_HD, hi_src, 0.0)
    return jnp.concatenate([top, bot], axis=0).astype(BF16)


def _swa_probs(s, bias, sinks):
    band, hq = 2 * WINDOW, WINDOW // 2
    live_rows = band - hq
    n_half = s[0].shape[1]
    dead = jnp.zeros((hq, n_half), BF16)
    ps, ms = [], []
    for half in range(2):
        cols, mcols = [], []
        for qh in range(2):
            rows = slice(half * band + qh * hq, half * band + qh * hq + live_rows)
            lanes = slice(qh * n_half, (qh + 1) * n_half)
            sv = s[qh][half * live_rows:(half + 1) * live_rows] + bias[rows, lanes]
            m = jnp.maximum(jnp.max(sv, axis=0, keepdims=True), sinks[half][:, lanes])
            live = jnp.exp2(sv - m).astype(BF16)
            cols.append(jnp.concatenate([live, dead] if qh == 0 else [dead, live], axis=0))
            mcols.append(m)
        ps.append(jnp.concatenate(cols, axis=1))
        ms.append(jnp.concatenate(mcols, axis=1))
    return jnp.concatenate(ps, axis=0), ms


def _swa_values(pt, vxt, ms, sinks):
    ot = _nn(vxt, pt)
    outs = []
    for half in range(2):
        den = ot[2 * SW_HD + 8 * half:2 * SW_HD + 8 * half + 1] + jnp.exp2(sinks[half] - ms[half])
        outs.append(ot[SW_HD * half:SW_HD * (half + 1)] * (1.0 / den))
    return jnp.concatenate(outs, axis=0)


def _swa_prompt_kernel(sink_ref, p_hbm, k_ref, v_ref, km_ref, vm_ref, bias_ref,
                       o_ref, qbuf, gbuf, sem, *, n_blocks, gblk0):
    p = pl.program_id(0)
    qb = pl.program_id(2)
    n_pairs = SW_GROUP // 2
    band = 2 * WINDOW

    nb_, nt_ = pl.num_programs(1), pl.num_programs(2)
    step = (p * nb_ + pl.program_id(1)) * nt_ + qb
    n_steps = pl.num_programs(0) * nb_ * nt_
    tq, pw = qbuf.shape[1], qbuf.shape[2]

    def fetch(x, slot):
        t_, b_, p_ = lax.rem(x, nt_), lax.rem(x // nt_, nb_), x // (nt_ * nb_)
        rows_ = pl.ds(pl.multiple_of(t_ * tq, tq), tq)
        return [pltpu.make_async_copy(
            p_hbm.at[b_, rows_, pl.ds(pl.multiple_of((c0 + p_) * pw, pw), pw)],
            buf.at[slot], sem.at[slot, i]) for i, (buf, c0) in enumerate(((qbuf, 0), (gbuf, gblk0)))]

    @pl.when(step == 0)
    def _():
        for x in range(2):
            for cp in fetch(x, x):
                cp.start()

    @pl.when(step + 2 < n_steps)
    def _():
        for cp in fetch(step + 2, lax.rem(step + 2, 3)):
            cp.start()

    slot = lax.rem(step, 3)
    for cp in fetch(step, slot):
        cp.wait()
    q_ref, g_ref = qbuf.at[slot], gbuf.at[slot]

    hq = WINDOW // 2

    def sink_rows(u):
        return [jnp.concatenate(
            [jnp.full((1, hq), sink_ref[(p * 2 + u) * SW_GROUP + 2 * j + half] * LOG2E, F32)
             for _ in range(2) for j in range(n_pairs)], axis=1) for half in range(2)]

    sinks = [sink_rows(u) for u in range(2)]
    orow = lax.broadcasted_iota(jnp.int32, (16, 2 * band), 0)
    ocol = lax.broadcasted_iota(jnp.int32, (16, 2 * band), 1)
    ones_rows = jnp.where((orow < 8) == (ocol < band), 1.0, 0.0).astype(BF16)
    zeros = jnp.zeros((SW_HD, band), BF16)

    def band_of(blk):
        tok = qb * n_blocks + blk
        cur = pl.ds(pl.multiple_of(tok * WINDOW, WINDOW), WINDOW)
        prev = pl.ds(pl.multiple_of(jnp.maximum(tok - 1, 0) * WINDOW, WINDOW), WINDOW)
        is_first = tok == 0
        kband = jnp.concatenate([jnp.where(is_first, km_ref[...], k_ref[0, prev, :]),
                                 k_ref[0, cur, :]], axis=0)
        vband = jnp.concatenate([jnp.where(is_first, vm_ref[...], v_ref[0, prev, :]),
                                 v_ref[0, cur, :]], axis=0)
        return (pl.multiple_of(blk * WINDOW, WINDOW), kband,
                vband.T.astype(BF16), jnp.where(is_first, 0, 1))

    def blocks(it, carry):
        bands = [band_of(it * SWA_BLOCKS_PER_ITER + i) for i in range(SWA_BLOCKS_PER_ITER)]
        units = [(bnd, u) for bnd in bands for u in range(2)]
        scores = []
        for (row0, kband, _, _), u in units:
            base = u * SW_GROUP * SW_HD
            q4 = jnp.concatenate(
                [q_ref[pl.ds(pl.multiple_of(row0 + qh * hq, hq), hq),
                       base + j * LANES:base + (j + 1) * LANES]
                 for qh in range(2) for j in range(n_pairs)],
                axis=0).astype(BF16)
            kx = _expand_band(kband, u)
            live = band - hq
            scores.append([
                _nt(jnp.concatenate([kx[qh * hq:qh * hq + live],
                                     kx[band + qh * hq:band + qh * hq + live]], axis=0),
                    q4[qh * n_pairs * hq:(qh + 1) * n_pairs * hq]) for qh in range(2)])
        probs = [_swa_probs(s, bias_ref[bias_sel, u], sinks[u])
                 for s, ((_, _, _, bias_sel), u) in zip(scores, units)]
        outs = []
        for (pt, ms), ((_, _, vbt, _), u) in zip(probs, units):
            vt = vbt[u * SW_HD:(u + 1) * SW_HD]
            vxt = jnp.concatenate([jnp.concatenate([vt, zeros], axis=1),
                                   jnp.concatenate([zeros, vt], axis=1), ones_rows], axis=0)
            outs.append(_swa_values(pt, vxt, ms, sinks[u]))
        for ot, ((row0, _, _, _), u) in zip(outs, units):
            base = u * SW_GROUP * SW_HD
            rows = pl.ds(row0, WINDOW)
            tr = [ot[:, c * LANES:(c + 1) * LANES].T for c in range(n_pairs)]
            o = jnp.concatenate(
                [jnp.concatenate([tr[qh * 2 + j // 2][(j % 2) * hq:(j % 2 + 1) * hq]
                                  for j in range(n_pairs)], axis=1) for qh in range(2)],
                axis=0)
            gate = g_ref[rows, base:base + SW_GROUP * SW_HD]
            o_ref[0, rows, base:base + SW_GROUP * SW_HD] = (o * gate).astype(BF16)
        return carry

    lax.fori_loop(0, n_blocks // SWA_BLOCKS_PER_ITER, blocks, 0)


def _swa_prompt(proj, meta_k, meta_v, sinks, bias, *, tq):
    bsz, tlen, _ = proj.shape
    pw = 2 * SW_GROUP * SW_HD
    qblocks = SW_HEADS * SW_HD // pw
    kcol0 = SW_HEADS * SW_HD // LANES
    vcol0 = kcol0 + SW_KV * SW_HD // LANES
    gblk0 = (SW_HEADS * SW_HD + 2 * SW_KV * SW_HD) // pw
    return pl.pallas_call(
        functools.partial(_swa_prompt_kernel, n_blocks=tq // WINDOW, gblk0=gblk0),
        grid=(qblocks, bsz, tlen // tq),
        in_specs=[pl.BlockSpec(memory_space=pltpu.SMEM),
                  pl.BlockSpec(memory_space=pl.ANY),
                  pl.BlockSpec((1, tlen, LANES), lambda p, b, t: (b, 0, kcol0 + p)),
                  pl.BlockSpec((1, tlen, LANES), lambda p, b, t: (b, 0, vcol0 + p)),
                  pl.BlockSpec((WINDOW, LANES), lambda p, b, t: (0, p)),
                  pl.BlockSpec((WINDOW, LANES), lambda p, b, t: (0, p)),
                  pl.BlockSpec((2, 2, 4 * WINDOW, 4 * WINDOW), lambda p, b, t: (0, p, 0, 0))],
        out_specs=pl.BlockSpec((1, tq, pw), lambda p, b, t: (b, t, p)),
        out_shape=jax.ShapeDtypeStruct((bsz, tlen, D_MODEL), BF16),
        scratch_shapes=[pltpu.VMEM((3, tq, pw), F32), pltpu.VMEM((3, tq, pw), F32),
                        pltpu.SemaphoreType.DMA((3, 2))],
        compiler_params=pltpu.CompilerParams(
            dimension_semantics=("arbitrary", "arbitrary", "arbitrary"),
            vmem_limit_bytes=V7X_VMEM_LIMIT),
        name="swa_prompt",
    )(sinks, proj, proj, proj, meta_k, meta_v, bias)


def _swa_sample_kernel(q_ref, g_ref, kn_ref, vn_ref, ck_ref, cv_ref, bc_ref, bn_ref, sink_ref,
                       o_ref, nk_ref, nv_ref, *, nb):
    nkv = SW_KV * SW_HD
    r = ck_ref.shape[1]
    row = lax.broadcasted_iota(jnp.int32, (SW_HEADS, nkv), 0)
    col = lax.broadcasted_iota(jnp.int32, (SW_HEADS, nkv), 1)
    own = (row // SW_GROUP) == (col // SW_HD)
    last = lax.broadcasted_iota(jnp.int32, (r, nkv), 0) == r - 1
    sink = sink_ref[...] * LOG2E
    for bi in range(nb):
        q = q_ref[bi]
        qx = jnp.where(own, jnp.concatenate([q] * SW_KV, axis=1), 0.0)
        ck, cv = ck_ref[bi], cv_ref[bi]
        kn, vn = kn_ref[bi], vn_ref[bi]

        s_c = _nt(qx.astype(BF16), ck.astype(BF16)) + bc_ref[...]
        s_n = jnp.sum(qx * kn, axis=-1, keepdims=True) + bn_ref[...]
        m = jnp.maximum(jnp.maximum(jnp.max(s_c, axis=-1, keepdims=True), s_n), sink)
        p_c = jnp.exp2(s_c - m)
        p_n = jnp.exp2(s_n - m)
        den = jnp.sum(p_c, axis=-1, keepdims=True) + p_n + jnp.exp2(sink - m)
        o_all = _nn(p_c.astype(BF16), cv.astype(BF16))
        o_all = o_all + p_n * vn
        o_all = jnp.where(own, o_all, 0.0)
        o = o_all[:, 0:SW_HD]
        for n in range(1, SW_KV):
            o = o + o_all[:, n * SW_HD:(n + 1) * SW_HD]
        o_ref[bi] = ((o / den) * g_ref[bi]).astype(BF16)

        nk_ref[bi] = jnp.where(last, kn, pltpu.roll(ck, r - 1, axis=0))
        nv_ref[bi] = jnp.where(last, vn, pltpu.roll(cv, r - 1, axis=0))


def _swa_sample(q, g, kn, vn, ck, cv, bias_c, bias_n, sinks, *, nb):
    bsz, r, nkv = ck.shape
    head = pl.BlockSpec((nb, SW_HEADS, SW_HD), lambda b: (b, 0, 0))
    new = pl.BlockSpec((nb, 1, nkv), lambda b: (b, 0, 0))
    cache = pl.BlockSpec((nb, r, nkv), lambda b: (b, 0, 0))
    return pl.pallas_call(
        functools.partial(_swa_sample_kernel, nb=nb),
        grid=(bsz // nb,),
        in_specs=[head, head, new, new, cache, cache,
                  pl.BlockSpec((SW_HEADS, r), lambda b: (0, 0)),
                  pl.BlockSpec((SW_HEADS, 1), lambda b: (0, 0)),
                  pl.BlockSpec((SW_HEADS, 1), lambda b: (0, 0))],
        out_specs=[head, cache, cache],
        out_shape=[jax.ShapeDtypeStruct((bsz, SW_HEADS, SW_HD), BF16),
                   jax.ShapeDtypeStruct(ck.shape, ck.dtype),
                   jax.ShapeDtypeStruct(cv.shape, cv.dtype)],
        compiler_params=pltpu.CompilerParams(dimension_semantics=("parallel",)),
        name="swa_sample",
    )(q, g, kn, vn, ck, cv, bias_c, bias_n, sinks.reshape(SW_HEADS, 1))


def kernel(x_prompt, x_sample, state_hgrn, cache_k_win, cache_v_win, meta_tokens, rel_bias,
           hg_lower_bounds, hg_norm, hg_w_in, hg_onorm, hg_w_out,
           sw_norm, sw_w_in, sw_sinks, sw_w_out, final_norm):
    n_samp = x_sample.shape[0]
    samp = slice(N_META, N_META + n_samp)
    x_main = x_prompt.reshape(MAIN_ROWS, D_MODEL)
    x_small = jnp.concatenate(
        [meta_tokens.astype(F32), x_sample.reshape(n_samp, D_MODEL),
         jnp.zeros((N_SMALL - N_META - n_samp, D_MODEL), F32)], axis=0)

    h_main = _rmsnorm(x_main, hg_norm[0], BF16, 512)
    h_small = _rmsnorm(x_small, hg_norm[0], BF16, N_SMALL)
    p_main, p_small = _proj(h_main, h_small, hg_w_in[0], tm=PROJ_TM, tn=PROJ_TN,
                            sections="hgrn", lbraw=hg_lower_bounds.astype(F32))

    meta_proj = jnp.pad(p_small[:N_META], ((CHUNK - N_META, 0), (0, 0)))[None]
    zero_state = jnp.zeros((HG_HEADS, HG_DK, HG_DK), F32)
    a_meta, s_meta = _hgrn_scan(meta_proj, hg_onorm[0], zero_state,
                                tb=CHUNK, hb=8, n_pad=CHUNK - N_META)
    a_main, s_prompt = _hgrn_scan(p_main.reshape(BATCH, SEQ, -1), hg_onorm[0],
                                  s_meta[0], tb=512, hb=16)
    sect = lambda s: p_small[samp, s * HG_F:(s + 1) * HG_F].reshape(n_samp, HG_HEADS, HG_DK)
    a_samp, s_sample = _hgrn_step(sect(0), sect(1), sect(2), sect(3),
                                  hg_onorm[0], state_hgrn, nb=SAMPLES_PER_STEP)
    a_small = jnp.concatenate(
        [a_meta[0, CHUNK - N_META:], a_samp.reshape(n_samp, D_MODEL),
         jnp.zeros((N_SMALL - N_META - n_samp, D_MODEL), BF16)], axis=0)
    x1_main, x1_small, x1b_main, x1b_small, sq_main, sq_small = _proj(
        a_main.reshape(MAIN_ROWS, D_MODEL), a_small, hg_w_out[0], x_main, x_small,
        tm=PROJ_RES_TM, tn=PROJ_TN, emit_stats=True)

    p_main, p_small = _proj(x1b_main, x1b_small, sw_w_in[0],
                            scale=(sw_norm[0], jnp.sum(sq_main, axis=0, keepdims=True),
                                   jnp.sum(sq_small, axis=0, keepdims=True)),
                            tm=PROJ_TM, tn=PROJ_TN, sections="swa")
    nq, nkv = SW_HEADS * SW_HD, SW_KV * SW_HD
    kv_tail = p_main.reshape(BATCH, SEQ, -1)[:, -WINDOW:, nq:nq + 2 * nkv]
    k_tail = kv_tail[:, :, :nkv].reshape(1, BATCH, WINDOW, SW_KV, SW_HD)
    v_tail = kv_tail[:, :, nkv:].reshape(1, BATCH, WINDOW, SW_KV, SW_HD)
    meta_kv = jnp.pad(p_small[:N_META, nq:nq + 2 * nkv], ((WINDOW - N_META, 0), (0, 0)))
    a_main = _swa_prompt(p_main.reshape(BATCH, SEQ, -1), meta_kv[:, :nkv], meta_kv[:, nkv:],
                         sw_sinks[0], _prompt_bias(rel_bias), tq=1024)

    r = cache_k_win.shape[2]
    table = rel_bias.astype(F32)
    bias_c = table[_t5_bucket(r - jnp.arange(r))].T * LOG2E
    bias_n = table[_t5_bucket(jnp.zeros((1,), jnp.int32))].T * LOG2E
    q_s = p_small[samp, :nq].reshape(n_samp, SW_HEADS, SW_HD)
    g_s = p_small[samp, nq + 2 * nkv:].reshape(n_samp, SW_HEADS, SW_HD)
    a_samp, k_samp, v_samp = _swa_sample(
        q_s, g_s, p_small[samp, nq:nq + nkv].reshape(n_samp, 1, nkv),
        p_small[samp, nq + nkv:nq + 2 * nkv].reshape(n_samp, 1, nkv),
        cache_k_win[0].reshape(n_samp, r, nkv), cache_v_win[0].reshape(n_samp, r, nkv),
        bias_c, bias_n, sw_sinks[0], nb=SAMPLES_PER_STEP)
    a_small = jnp.concatenate(
        [jnp.zeros((N_META, D_MODEL), BF16), a_samp.reshape(n_samp, D_MODEL),
         jnp.zeros((N_SMALL - N_META - n_samp, D_MODEL), BF16)], axis=0)
    x2_main, x2_small = _proj(a_main.reshape(MAIN_ROWS, D_MODEL), a_small, sw_w_out[0],
                              x1_main, x1_small, tm=PROJ_RES_TM, tn=PROJ_TN)

    y_prompt = _rmsnorm(x2_main, final_norm, F32, 512).reshape(BATCH, SEQ, D_MODEL)
    y_sample = _rmsnorm(x2_small, final_norm, F32, N_SMALL)[samp].reshape(n_samp, 1, D_MODEL)
    return (y_prompt, y_sample, s_prompt[None],
            k_tail.astype(cache_k_win.dtype), v_tail.astype(cache_v_win.dtype),
            s_sample,
            k_samp.reshape(1, n_samp, r, SW_KV, SW_HD), v_samp.reshape(1, n_samp, r, SW_KV, SW_HD))
```

```python
import functools
import math

import numpy as np
import jax
import jax.numpy as jnp
from jax import lax
from jax.experimental import pallas as pl
from jax.experimental.pallas import tpu as pltpu

D_MODEL = 4096
BATCH = 4
SEQ = 2048
DEC_BATCH = 32
N_META = 16
HG_HEADS = 32
HG_DK = 128
HG_F = HG_HEADS * HG_DK
SW_HEADS = 64
SW_KV = 8
SW_HD = 64
SW_GROUP = SW_HEADS // SW_KV
WINDOW = 128
REL_BUCKETS = 32
REL_MAX_DIST = 128
EPS = 1e-6
NEG = -1e30
LOG2E = math.log2(math.e)

LANES = 128
V7X_VMEM_LIMIT = 58 * 1024 * 1024

PROJ_TM = 1024
PROJ_TN = 1024
PROJ_RES_TM = 512
CHUNK = 128
SWA_BLOCKS_PER_ITER = 8
SAMPLES_PER_STEP = 4
N_SMALL = 64
MAIN_ROWS = BATCH * SEQ

F32 = jnp.float32
BF16 = jnp.bfloat16


def _nt(a, b):
    return lax.dot_general(a, b, (((1,), (1,)), ((), ())), preferred_element_type=F32)


def _nn(a, b):
    return jnp.dot(a, b, preferred_element_type=F32)


def _sigmoid(x):
    return 1.0 / (1.0 + jnp.exp(-x))


def _silu(x):
    h = 0.5 * x
    return h + h * jnp.tanh(h)


def _rmsnorm_kernel(x_ref, g_ref, o_ref):
    x = x_ref[...]
    ms = jnp.mean(x * x, axis=-1, keepdims=True)
    o_ref[...] = (x * lax.rsqrt(ms + EPS) * g_ref[...]).astype(o_ref.dtype)


def _rmsnorm_ring_kernel(x_hbm, g_ref, o_ref, xbuf, sem, *, tm, n_steps):
    s = pl.program_id(0)

    def fetch(step, slot):
        return pltpu.make_async_copy(
            x_hbm.at[pl.ds(pl.multiple_of(step * tm, tm), tm), :], xbuf.at[slot], sem.at[slot])

    @pl.when(s == 0)
    def _():
        fetch(0, 0).start()
        fetch(1, 1).start()

    @pl.when(s + 2 < n_steps)
    def _():
        fetch(s + 2, lax.rem(s + 2, 3)).start()

    slot = lax.rem(s, 3)
    fetch(s, slot).wait()
    x = xbuf[slot]
    ms = jnp.mean(x * x, axis=-1, keepdims=True)
    o_ref[...] = (x * lax.rsqrt(ms + EPS) * g_ref[...]).astype(o_ref.dtype)


def _rmsnorm(x, g, out_dtype, tm):
    m, d = x.shape
    if m // tm >= 3:
        return pl.pallas_call(
            functools.partial(_rmsnorm_ring_kernel, tm=tm, n_steps=m // tm),
            grid=(m // tm,),
            in_specs=[pl.BlockSpec(memory_space=pl.ANY),
                      pl.BlockSpec((1, d), lambda i: (0, 0))],
            out_specs=pl.BlockSpec((tm, d), lambda i: (i, 0)),
            out_shape=jax.ShapeDtypeStruct((m, d), out_dtype),
            scratch_shapes=[pltpu.VMEM((3, tm, d), x.dtype), pltpu.SemaphoreType.DMA((3,))],
            compiler_params=pltpu.CompilerParams(dimension_semantics=("arbitrary",),
                                                 vmem_limit_bytes=V7X_VMEM_LIMIT),
            name="rmsnorm_ring",
        )(x, g.reshape(1, d))
    return pl.pallas_call(
        _rmsnorm_kernel,
        grid=(m // tm,),
        in_specs=[pl.BlockSpec((tm, d), lambda i: (i, 0)),
                  pl.BlockSpec((1, d), lambda i: (0, 0))],
        out_specs=pl.BlockSpec((tm, d), lambda i: (i, 0)),
        out_shape=jax.ShapeDtypeStruct((m, d), out_dtype),
        compiler_params=pltpu.CompilerParams(dimension_semantics=("parallel",),
                                             vmem_limit_bytes=V7X_VMEM_LIMIT),
        name="rmsnorm",
    )(x, g.reshape(1, d))


def _proj_kernel(*refs, tn, kc, cps, has_res, has_scale, emit_stats, sections):
    refs = list(refs)
    a_ref, as_ref, w_hbm = refs[:3]
    del refs[:3]
    if sections == "hgrn":
        lb_ref = refs.pop(0)
    if has_scale:
        gw_ref, sq_in_ref, sqs_in_ref = refs[:3]
        del refs[:3]
    if has_res:
        r_ref, rs_ref = refs[:2]
        del refs[:2]
    o_ref, os_ref = refs[:2]
    del refs[:2]
    if emit_stats:
        ob_ref, obs_ref, sq_ref, sqs_ref = refs[:4]
        del refs[:4]
    wb_ref, stage_ref, sem = refs
    j, i = pl.program_id(0), pl.program_id(1)
    nj, ni = pl.num_programs(0), pl.num_programs(1)
    kdim = w_hbm.shape[0]
    n_kc = kdim // kc
    step = j * ni + i
    cur = lax.rem(j, 2)
    par = lax.rem(step, 2)

    def aligned(x, m):
        return x * m if isinstance(x, int) else pl.multiple_of(x * m, m)

    def chunk_copy(tile, c, slot):
        return pltpu.make_async_copy(
            w_hbm.at[pl.ds(aligned(c, kc), kc), pl.ds(aligned(tile, tn), tn)],
            stage_ref.at[slot], sem.at[slot])

    def cast_chunk(slot, c, half):
        rows = pl.ds(aligned(c, kc), kc)
        w = stage_ref[slot]
        if has_scale:
            w = w * jnp.concatenate([gw_ref[rows, :]] * (tn // LANES), axis=1)
        wb_ref[half, rows, :] = w.astype(BF16)

    def next_tile(jj):
        return jnp.minimum(jj + 1, nj - 1)

    def forget_gate(x):
        lb = _lower_bound(lb_ref[...])
        return lb + (1.0 - lb) * _sigmoid(x)

    def identity(x):
        return x

    if sections == "hgrn":
        quarter = nj // 4
        gates = [_silu, forget_gate, identity]
        gate_id = jnp.where((j < quarter) | (j >= 3 * quarter), 0,
                            jnp.where(j < 2 * quarter, 1, 2))
    elif sections == "swa":
        q_tiles = SW_HEADS * SW_HD // tn
        gates = [lambda x: x * (SW_HD ** -0.5 * LOG2E), identity, _silu]
        gate_id = jnp.where(j < q_tiles, 0, jnp.where(j == q_tiles, 1, 2))
    else:
        gates, gate_id = [identity], None

    def per_section(body):
        if gate_id is None:
            body(gates[0])
        else:
            lax.cond(gate_id == 0, lambda: body(gates[0]),
                     lambda: lax.cond(gate_id == 1, lambda: body(gates[1]),
                                      lambda: body(gates[2])))

    def finish(out, gate, res_ref, sq_in, o, ob, sq):
        if has_scale:
            out = out * lax.rsqrt(jnp.sum(sq_in[...], axis=0) * (1.0 / kdim) + EPS)
        out = gate(out)
        if has_res:
            out = out + res_ref[...]
        o[...] = out
        if emit_stats:
            ob[...] = out.astype(BF16)
            sq[0] = jnp.sum(out * out, axis=1, keepdims=True)

    @pl.when(step == 0)
    def _():
        chunk_copy(0, 0, 0).start()
        for c in range(n_kc):
            if c + 1 < n_kc:
                chunk_copy(0, c + 1, (c + 1) % 2).start()
            chunk_copy(0, c, c % 2).wait()
            cast_chunk(c % 2, c, 0)
        for u in range(cps):
            chunk_copy(next_tile(0), u, u).start()

    for u in range(cps):
        chunk_copy(next_tile(j), i * cps + u, par * cps + u).wait()

    @pl.when(step + 1 < nj * ni)
    def _():
        wrap = i + 1 == ni
        j2 = jnp.where(wrap, j + 1, j)
        i2 = jnp.where(wrap, 0, i + 1)
        for u in range(cps):
            chunk_copy(next_tile(j2), i2 * cps + u, (1 - par) * cps + u).start()

    @pl.when(i == 0)
    def _():
        per_section(lambda gate: finish(
            _nn(as_ref[...], wb_ref[cur]), gate, rs_ref if has_res else None,
            sqs_in_ref if has_scale else None, os_ref,
            obs_ref if emit_stats else None, sqs_ref if emit_stats else None))

    def main(gate):
        for u in range(cps):
            cast_chunk(par * cps + u, i * cps + u, 1 - cur)
        finish(_nn(a_ref[...], wb_ref[cur]), gate, r_ref if has_res else None,
               sq_in_ref if has_scale else None, o_ref,
               ob_ref if emit_stats else None, sq_ref if emit_stats else None)

    per_section(main)


def _proj(a, a_small, w, res=None, res_small=None, scale=None, *, tm, tn, kc=256,
          emit_stats=False, sections=None, lbraw=None):
    m, k = a.shape
    n = w.shape[1]
    ms = a_small.shape[0]
    has_res, has_scale = res is not None, scale is not None
    ni, nj, n_kc = m // tm, n // tn, k // kc
    assert m % tm == 0 and n % tn == 0 and k % kc == 0 and n_kc % ni == 0 and nj >= 2
    cps = n_kc // ni
    in_specs = [pl.BlockSpec((tm, k), lambda j, i: (i, 0)),
                pl.BlockSpec((ms, k), lambda j, i: (0, 0)),
                pl.BlockSpec(memory_space=pl.ANY)]
    args = [a, a_small, w]
    if sections == "hgrn":
        quarter = nj // 4
        assert nj % 4 == 0 and lbraw.shape == (3, n // 4)
        in_specs += [pl.BlockSpec((3, tn), lambda j, i: (0, jnp.clip(j - quarter, 0, quarter - 1)))]
        args += [lbraw]
    if sections == "swa":
        assert tn == 2 * SW_KV * SW_HD and (SW_HEADS * SW_HD) % tn == 0
    if has_scale:
        gain, sq, sq_small = scale
        in_specs += [pl.BlockSpec((k, LANES), lambda j, i: (0, 0)),
                     pl.BlockSpec((sq.shape[0], tm, 1), lambda j, i: (0, i, 0)),
                     pl.BlockSpec((sq.shape[0], ms, 1), lambda j, i: (0, 0, 0))]
        args += [jnp.broadcast_to(gain.astype(F32)[:, None], (k, LANES)), sq, sq_small]
    if has_res:
        in_specs += [pl.BlockSpec((tm, tn), lambda j, i: (i, j)),
                     pl.BlockSpec((ms, tn), lambda j, i: (0, j))]
        args += [res, res_small]
    out_specs = [pl.BlockSpec((tm, tn), lambda j, i: (i, j)),
                 pl.BlockSpec((ms, tn), lambda j, i: (0, j))]
    out_shape = [jax.ShapeDtypeStruct((m, n), F32), jax.ShapeDtypeStruct((ms, n), F32)]
    if emit_stats:
        out_specs += [pl.BlockSpec((tm, tn), lambda j, i: (i, j)),
                      pl.BlockSpec((ms, tn), lambda j, i: (0, j)),
                      pl.BlockSpec((1, tm, 1), lambda j, i: (j, i, 0)),
                      pl.BlockSpec((1, ms, 1), lambda j, i: (j, 0, 0))]
        out_shape += [jax.ShapeDtypeStruct((m, n), BF16), jax.ShapeDtypeStruct((ms, n), BF16),
                      jax.ShapeDtypeStruct((nj, m, 1), F32), jax.ShapeDtypeStruct((nj, ms, 1), F32)]
    return pl.pallas_call(
        functools.partial(_proj_kernel, tn=tn, kc=kc, cps=cps, has_res=has_res,
                          has_scale=has_scale, emit_stats=emit_stats, sections=sections),
        grid=(nj, ni),
        in_specs=in_specs,
        out_specs=out_specs,
        out_shape=out_shape,
        scratch_shapes=[pltpu.VMEM((2, k, tn), BF16),
                        pltpu.VMEM((2 * cps, kc, tn), F32),
                        pltpu.SemaphoreType.DMA((2 * cps,))],
        compiler_params=pltpu.CompilerParams(
            dimension_semantics=("arbitrary", "arbitrary"),
            vmem_limit_bytes=V7X_VMEM_LIMIT),
        name="proj_res" if has_res else "proj",
    )(*args)


def _hgrn_consts():
    t = np.arange(CHUNK)[:, None]
    s = np.arange(CHUNK)[None, :]
    tri = s <= t
    ends = np.array([15, 47, 79, 111, 31, 95, 63, 127] + [-1] * 8)[:, None]
    mall = np.concatenate([tri, s <= ends], axis=0).astype(np.float32)
    lvl = np.full((CHUNK, CHUNK), 3, np.int32)
    lvl[(t >= 64) & (s < 64)] = 2
    lvl[(t // 64 == s // 64) & (t % 64 >= 32) & (s % 64 < 32)] = 1
    lvl[(t // 32 == s // 32) & (s <= t)] = 0
    return jnp.asarray(mall, BF16), jnp.asarray(lvl)


def _lower_bound(lbraw):
    mx = jnp.max(lbraw, axis=0, keepdims=True)
    e = jnp.exp(lbraw - mx)
    return e[0:1, :] / jnp.sum(e, axis=0, keepdims=True)


def _head_out(o, gate, onorm):
    ms = jnp.mean(o * o, axis=-1, keepdims=True)
    return (o * lax.rsqrt(ms + EPS) * onorm) * gate


def _hgrn_gates(q, f, mall, row_valid):
    logf = jnp.log2(f)
    k = 1.0 - f
    if row_valid is not None:
        logf = jnp.where(row_valid, logf, 0.0)
        k = jnp.where(row_valid, k, 0.0)

    hi = logf.astype(BF16)
    mid = (logf - hi.astype(F32)).astype(BF16)
    cs = _nn(mall, jnp.concatenate([hi, mid], axis=1))
    return q, k, cs[:, :LANES] + cs[:, LANES:]


def _hgrn_scores(q, k, cs):
    b = cs[0:CHUNK]
    ref = cs[CHUNK:CHUNK + 8]
    r = lambda i: ref[i:i + 1]
    cat = lambda parts: jnp.concatenate(parts, axis=0)
    blk = lambda x, i: x[32 * i:32 * i + 32]
    e0 = jnp.exp2(cat([blk(b, i) - r(i) for i in range(4)]))
    q0, k0 = q * e0, k * (1.0 / e0)
    c = jnp.exp2(cat([r(1) - r(4), r(3) - r(5), r(4) - r(0), r(5) - r(2),
                      r(2) - r(6), r(3) - r(6), r(6) - r(0), r(6) - r(1)]))
    ce = jnp.exp2(cat([r(0), r(1), r(2), r(3), r(7) - r(0), r(7) - r(1), r(7) - r(2), r(7) - r(3)]))
    zero32 = jnp.zeros((32, LANES), BF16)
    bf = lambda x: x.astype(BF16)
    a0 = _nt(bf(q0), bf(k0))
    a1 = _nt(bf(cat([blk(q0, 1) * c[0:1], blk(q0, 3) * c[1:2]])),
             cat([bf(blk(k0, 0) * c[2:3]), zero32, bf(blk(k0, 2) * c[3:4]), zero32]))
    a2 = _nt(bf(cat([blk(q0, 2) * c[4:5], blk(q0, 3) * c[5:6]])),
             cat([bf(blk(k0, 0) * c[6:7]), bf(blk(k0, 1) * c[7:8]), zero32, zero32]))
    qe = bf(cat([blk(q0, i) * ce[i:i + 1] for i in range(4)]))
    kd = bf(cat([blk(k0, i) * ce[4 + i:5 + i] for i in range(4)]))
    return a0, a1, a2, qe, kd, jnp.exp2(r(7))


def _hgrn_apply(a0, a1, a2, qe, kd, decay, v, st, lvl):
    is0, is1, is2 = lvl == 0, lvl == 1, lvl == 2
    a = jnp.concatenate([
        jnp.where(is0[0:32], a0[0:32], 0.0),
        jnp.where(is0[32:64], a0[32:64], jnp.where(is1[32:64], a1[0:32], 0.0)),
        jnp.where(is0[64:96], a0[64:96], jnp.where(is2[64:96], a2[0:32], 0.0)),
        jnp.where(is0[96:128], a0[96:128],
                  jnp.where(is1[96:128], a1[32:64], jnp.where(is2[96:128], a2[32:64], 0.0))),
    ], axis=0)
    vt = v.T.astype(BF16)
    lhs = jnp.concatenate([a.astype(BF16), qe], axis=1)
    rhs = jnp.concatenate([vt, st.astype(BF16)], axis=1)
    return _nt(lhs, rhs), st * decay + _nn(vt, kd)


def _hgrn_scan_kernel(q_ref, f_ref, v_ref, g_ref, on_ref, s0_ref, mall_ref, lvl_ref,
                      a_ref, s_ref, st_ref, *, hb, n_chunks, n_pad):
    t = pl.program_id(2)

    @pl.when(t == 0)
    def _():
        for h in range(hb):
            st_ref[h] = s0_ref[h].T

    row_valid = None
    if n_pad:
        row_valid = lax.broadcasted_iota(jnp.int32, (CHUNK, LANES), 0) >= n_pad
    heads = [slice(h * LANES, (h + 1) * LANES) for h in range(hb)]

    def body(c, carry):
        rows = pl.ds(pl.multiple_of(c * CHUNK, CHUNK), CHUNK)
        gates = [_hgrn_gates(q_ref[0, rows, cols], f_ref[0, rows, cols], mall_ref[...], row_valid)
                 for cols in heads]
        scores = [_hgrn_scores(*g) for g in gates]
        outs = [_hgrn_apply(*sc, v_ref[0, rows, cols], st_ref[h], lvl_ref[...])
                for h, (sc, cols) in enumerate(zip(scores, heads))]
        for h, ((o, st_new), cols) in enumerate(zip(outs, heads)):
            st_ref[h] = st_new
            a_ref[0, rows, cols] = _head_out(o, g_ref[0, rows, cols], on_ref[:, cols]).astype(BF16)
        return carry

    lax.fori_loop(0, n_chunks, body, 0)

    @pl.when(t == pl.num_programs(2) - 1)
    def _():
        for h in range(hb):
            s_ref[0, h] = st_ref[h].T


def _hgrn_scan(proj, onorm, s0, *, tb, hb, n_pad=0):
    bsz, tlen, _ = proj.shape
    mall, lvl = _hgrn_consts()
    hcols = HG_F // (hb * LANES)
    sect = lambda s: (lambda b, h, t: (b, t, s * hcols + h))
    blk = (1, tb, hb * LANES)
    return pl.pallas_call(
        functools.partial(_hgrn_scan_kernel, hb=hb, n_chunks=tb // CHUNK, n_pad=n_pad),
        grid=(bsz, HG_HEADS // hb, tlen // tb),
        in_specs=[pl.BlockSpec(blk, sect(0)), pl.BlockSpec(blk, sect(1)),
                  pl.BlockSpec(blk, sect(2)), pl.BlockSpec(blk, sect(3)),
                  pl.BlockSpec((1, hb * LANES), lambda b, h, t: (0, h)),
                  pl.BlockSpec((hb, HG_DK, HG_DK), lambda b, h, t: (h, 0, 0)),
                  pl.BlockSpec((CHUNK + 16, CHUNK), lambda b, h, t: (0, 0)),
                  pl.BlockSpec((CHUNK, CHUNK), lambda b, h, t: (0, 0))],
        out_specs=[pl.BlockSpec(blk, lambda b, h, t: (b, t, h)),
                   pl.BlockSpec((1, hb, HG_DK, HG_DK), lambda b, h, t: (b, h, 0, 0))],
        out_shape=[jax.ShapeDtypeStruct((bsz, tlen, D_MODEL), BF16),
                   jax.ShapeDtypeStruct((bsz, HG_HEADS, HG_DK, HG_DK), F32)],
        scratch_shapes=[pltpu.VMEM((hb, HG_DK, HG_DK), F32)],
        compiler_params=pltpu.CompilerParams(
            dimension_semantics=("parallel", "parallel", "arbitrary"),
            vmem_limit_bytes=V7X_VMEM_LIMIT),
        name="hgrn_scan",
    )(proj, proj, proj, proj, onorm.reshape(1, D_MODEL), s0, mall, lvl)


def _hgrn_step_kernel(q_ref, f_ref, v_ref, g_ref, on_ref, s0_ref, a_ref, s_ref, *, nb):
    def split(x):
        hi = x.astype(BF16).astype(F32)
        return hi, x - hi

    r = lax.broadcasted_iota(jnp.int32, (8, HG_DK), 0)
    ones_tail = jnp.concatenate([jnp.where((r == 3) | (r == 4), 1.0, 0.0),
                                 jnp.where((r == 5) | (r == 6), 1.0, 0.0)], axis=1).astype(BF16)
    k_hi_rows, k_lo_rows = r < 2, (r == 2) | (r == 7)
    v_hi_rows, v_lo_rows = (r == 0) | (r == 2), (r == 1) | (r == 7)
    for bi in range(nb):
        f = f_ref[bi]
        decay = jnp.exp(jnp.log(f))
        k = 1.0 - f
        (kh, kl), (dh, dl), (qh, ql), (vh, vl) = (split(k), split(decay), split(q_ref[bi]),
                                                  split(v_ref[bi]))
        rows = []
        for h in range(HG_HEADS):
            row = lambda x: jnp.broadcast_to(x[h:h + 1, :], (8, HG_DK))
            lhs = jnp.where(k_hi_rows, row(kh), jnp.where(k_lo_rows, row(kl), jnp.where(
                r == 3, row(dh), jnp.where(r == 4, row(dl), jnp.where(
                    r == 5, row(qh), row(ql))))))
            rhs_v = jnp.where(v_hi_rows, row(vh), jnp.where(v_lo_rows, row(vl), 0.0))
            rhs = jnp.concatenate([rhs_v.astype(BF16), ones_tail], axis=1)
            out = lax.dot_general(lhs.astype(BF16), rhs, (((0,), (0,)), ((), ())),
                                  preferred_element_type=F32)
            s_new = out[:, HG_DK:2 * HG_DK] * s0_ref[0, bi, h] + out[:, :HG_DK]
            s_ref[0, bi, h] = s_new
            rows.append(jnp.sum(out[:, 2 * HG_DK:] * s_new, axis=0, keepdims=True))
        o = jnp.concatenate(rows, axis=0)
        a_ref[bi] = _head_out(o, g_ref[bi], on_ref[...]).astype(BF16)


def _hgrn_step(q, f, v, g, onorm, state, *, nb):
    bsz = q.shape[0]
    vec = pl.BlockSpec((nb, HG_HEADS, HG_DK), lambda b: (b, 0, 0))
    full = pl.BlockSpec((HG_HEADS, HG_DK), lambda b: (0, 0))
    sspec = pl.BlockSpec((1, nb, HG_HEADS, HG_DK, HG_DK), lambda b: (0, b, 0, 0, 0))
    return pl.pallas_call(
        functools.partial(_hgrn_step_kernel, nb=nb),
        grid=(bsz // nb,),
        in_specs=[vec, vec, vec, vec, full, sspec],
        out_specs=[vec, sspec],
        out_shape=[jax.ShapeDtypeStruct((bsz, HG_HEADS, HG_DK), BF16),
                   jax.ShapeDtypeStruct(state.shape, state.dtype)],
        compiler_params=pltpu.CompilerParams(dimension_semantics=("parallel",),
                                             vmem_limit_bytes=V7X_VMEM_LIMIT),
        name="hgrn_step",
    )(q, f, v, g, onorm.reshape(HG_HEADS, HG_DK), state)


def _t5_bucket(dist):
    max_exact = REL_BUCKETS // 2
    d = jnp.maximum(dist, 1).astype(F32)
    large = max_exact + (jnp.log(d / max_exact) / math.log(REL_MAX_DIST / max_exact)
                         * (REL_BUCKETS - max_exact)).astype(jnp.int32)
    large = jnp.minimum(large, REL_BUCKETS - 1)
    return jnp.where(dist < max_exact, dist, large)


def _prompt_bias_kernel(w_ref, o_ref):
    band = 2 * WINDOW
    hq = WINDOW // 2
    key = lax.broadcasted_iota(jnp.int32, (band, WINDOW), 0)
    lane = lax.broadcasted_iota(jnp.int32, (band, WINDOW), 1)

    def toeplitz(g):
        row = jnp.broadcast_to(w_ref[g:g + 1, :], (band, 3 * WINDOW))
        return pltpu.roll(row, 0, 1, stride=1, stride_axis=0)[:, :WINDOW]

    for half in range(2):
        for m in range(SW_GROUP // 4):
            ta, tb = toeplitz(2 * (2 * m) + half), toeplitz(2 * (2 * m + 1) + half)
            blocks = [jnp.where(lane < hq, ta, pltpu.roll(tb, hq, axis=1)),
                      jnp.where(lane < hq, pltpu.roll(ta, hq, axis=1), tb)]
            for qh in range(2):
                c = qh * (SW_GROUP // 4) + m
                dst = (slice(half * band, (half + 1) * band), slice(c * WINDOW, (c + 1) * WINDOW))
                o_ref[1, 0, dst[0], dst[1]] = blocks[qh]
                o_ref[0, 0, dst[0], dst[1]] = jnp.where(key < WINDOW - N_META, NEG * LOG2E,
                                                        blocks[qh])


def _prompt_bias(table):
    band = 2 * WINDOW
    i = jnp.arange(3 * WINDOW)
    dist = jnp.where(i < WINDOW, i + WINDOW, i - band)
    vals = table.astype(F32)[_t5_bucket(jnp.maximum(dist, 0))]
    vals = jnp.where(((dist >= 0) & (dist <= WINDOW))[:, None], vals, NEG) * LOG2E
    return pl.pallas_call(
        _prompt_bias_kernel,
        grid=(SW_KV,),
        in_specs=[pl.BlockSpec((SW_GROUP, 3 * WINDOW), lambda n: (n, 0))],
        out_specs=pl.BlockSpec((2, 1, 2 * band, 4 * WINDOW), lambda n: (0, n, 0, 0)),
        out_shape=jax.ShapeDtypeStruct((2, SW_KV, 2 * band, 4 * WINDOW), F32),
        compiler_params=pltpu.CompilerParams(dimension_semantics=("parallel",)),
        name="prompt_bias",
    )(vals.T)


def _expand_band(pair, u):
    lane = lax.broadcasted_iota(jnp.int32, pair.shape, 1)
    rolled = pltpu.roll(pair, SW_HD, axis=1)
    lo_src, hi_src = (pair, rolled) if u == 0 else (rolled, pair)
    top = jnp.where(lane < SW_HD, lo_src, 0.0)
    bot = jnp.where(lane >= SW_HD, hi_src, 0.0)
    return jnp.concatenate([top, bot], axis=0).astype(BF16)


def _swa_probs(s, bias, sinks):
    band, hq = 2 * WINDOW, WINDOW // 2
    live_rows = band - hq
    n_half = s[0].shape[1]
    dead = jnp.zeros((hq, n_half), BF16)
    ps, ms = [], []
    for half in range(2):
        cols, mcols = [], []
        for qh in range(2):
            rows = slice(half * band + qh * hq, half * band + qh * hq + live_rows)
            lanes = slice(qh * n_half, (qh + 1) * n_half)
            sv = s[qh][half * live_rows:(half + 1) * live_rows] + bias[rows, lanes]
            m = jnp.maximum(jnp.max(sv, axis=0, keepdims=True), sinks[half][:, lanes])
            live = jnp.exp2(sv - m).astype(BF16)
            cols.append(jnp.concatenate([live, dead] if qh == 0 else [dead, live], axis=0))
            mcols.append(m)
        ps.append(jnp.concatenate(cols, axis=1))
        ms.append(jnp.concatenate(mcols, axis=1))
    return jnp.concatenate(ps, axis=0), ms


def _swa_values(pt, vxt, ms, sinks):
    ot = _nn(vxt, pt)
    outs = []
    for half in range(2):
        den = ot[2 * SW_HD + 8 * half:2 * SW_HD + 8 * half + 1] + jnp.exp2(sinks[half] - ms[half])
        outs.append(ot[SW_HD * half:SW_HD * (half + 1)] * (1.0 / den))
    return jnp.concatenate(outs, axis=0)


def _swa_prompt_kernel(sink_ref, p_hbm, k_ref, v_ref, km_ref, vm_ref, bias_ref,
                       o_ref, qbuf, gbuf, sem, *, n_blocks, gblk0):
    p = pl.program_id(0)
    qb = pl.program_id(2)
    n_pairs = SW_GROUP // 2
    band = 2 * WINDOW

    nb_, nt_ = pl.num_programs(1), pl.num_programs(2)
    step = (p * nb_ + pl.program_id(1)) * nt_ + qb
    n_steps = pl.num_programs(0) * nb_ * nt_
    tq, pw = qbuf.shape[1], qbuf.shape[2]

    def fetch(x, slot):
        t_, b_, p_ = lax.rem(x, nt_), lax.rem(x // nt_, nb_), x // (nt_ * nb_)
        rows_ = pl.ds(pl.multiple_of(t_ * tq, tq), tq)
        return [pltpu.make_async_copy(
            p_hbm.at[b_, rows_, pl.ds(pl.multiple_of((c0 + p_) * pw, pw), pw)],
            buf.at[slot], sem.at[slot, i]) for i, (buf, c0) in enumerate(((qbuf, 0), (gbuf, gblk0)))]

    @pl.when(step == 0)
    def _():
        for x in range(2):
            for cp in fetch(x, x):
                cp.start()

    @pl.when(step + 2 < n_steps)
    def _():
        for cp in fetch(step + 2, lax.rem(step + 2, 3)):
            cp.start()

    slot = lax.rem(step, 3)
    for cp in fetch(step, slot):
        cp.wait()
    q_ref, g_ref = qbuf.at[slot], gbuf.at[slot]

    hq = WINDOW // 2

    def sink_rows(u):
        return [jnp.concatenate(
            [jnp.full((1, hq), sink_ref[(p * 2 + u) * SW_GROUP + 2 * j + half] * LOG2E, F32)
             for _ in range(2) for j in range(n_pairs)], axis=1) for half in range(2)]

    sinks = [sink_rows(u) for u in range(2)]
    orow = lax.broadcasted_iota(jnp.int32, (16, 2 * band), 0)
    ocol = lax.broadcasted_iota(jnp.int32, (16, 2 * band), 1)
    ones_rows = jnp.where((orow < 8) == (ocol < band), 1.0, 0.0).astype(BF16)
    zeros = jnp.zeros((SW_HD, band), BF16)

    def band_of(blk):
        tok = qb * n_blocks + blk
        cur = pl.ds(pl.multiple_of(tok * WINDOW, WINDOW), WINDOW)
        prev = pl.ds(pl.multiple_of(jnp.maximum(tok - 1, 0) * WINDOW, WINDOW), WINDOW)
        is_first = tok == 0
        kband = jnp.concatenate([jnp.where(is_first, km_ref[...], k_ref[0, prev, :]),
                                 k_ref[0, cur, :]], axis=0)
        vband = jnp.concatenate([jnp.where(is_first, vm_ref[...], v_ref[0, prev, :]),
                                 v_ref[0, cur, :]], axis=0)
        return (pl.multiple_of(blk * WINDOW, WINDOW), kband,
                vband.T.astype(BF16), jnp.where(is_first, 0, 1))

    def blocks(it, carry):
        bands = [band_of(it * SWA_BLOCKS_PER_ITER + i) for i in range(SWA_BLOCKS_PER_ITER)]
        units = [(bnd, u) for bnd in bands for u in range(2)]
        scores = []
        for (row0, kband, _, _), u in units:
            base = u * SW_GROUP * SW_HD
            q4 = jnp.concatenate(
                [q_ref[pl.ds(pl.multiple_of(row0 + qh * hq, hq), hq),
                       base + j * LANES:base + (j + 1) * LANES]
                 for qh in range(2) for j in range(n_pairs)],
                axis=0).astype(BF16)
            kx = _expand_band(kband, u)
            live = band - hq
            scores.append([
                _nt(jnp.concatenate([kx[qh * hq:qh * hq + live],
                                     kx[band + qh * hq:band + qh * hq + live]], axis=0),
                    q4[qh * n_pairs * hq:(qh + 1) * n_pairs * hq]) for qh in range(2)])
        probs = [_swa_probs(s, bias_ref[bias_sel, u], sinks[u])
                 for s, ((_, _, _, bias_sel), u) in zip(scores, units)]
        outs = []
        for (pt, ms), ((_, _, vbt, _), u) in zip(probs, units):
            vt = vbt[u * SW_HD:(u + 1) * SW_HD]
            vxt = jnp.concatenate([jnp.concatenate([vt, zeros], axis=1),
                                   jnp.concatenate([zeros, vt], axis=1), ones_rows], axis=0)
            outs.append(_swa_values(pt, vxt, ms, sinks[u]))
        for ot, ((row0, _, _, _), u) in zip(outs, units):
            base = u * SW_GROUP * SW_HD
            rows = pl.ds(row0, WINDOW)
            tr = [ot[:, c * LANES:(c + 1) * LANES].T for c in range(n_pairs)]
            o = jnp.concatenate(
                [jnp.concatenate([tr[qh * 2 + j // 2][(j % 2) * hq:(j % 2 + 1) * hq]
                                  for j in range(n_pairs)], axis=1) for qh in range(2)],
                axis=0)
            gate = g_ref[rows, base:base + SW_GROUP * SW_HD]
            o_ref[0, rows, base:base + SW_GROUP * SW_HD] = (o * gate).astype(BF16)
        return carry

    lax.fori_loop(0, n_blocks // SWA_BLOCKS_PER_ITER, blocks, 0)


def _swa_prompt(proj, meta_k, meta_v, sinks, bias, *, tq):
    bsz, tlen, _ = proj.shape
    pw = 2 * SW_GROUP * SW_HD
    qblocks = SW_HEADS * SW_HD // pw
    kcol0 = SW_HEADS * SW_HD // LANES
    vcol0 = kcol0 + SW_KV * SW_HD // LANES
    gblk0 = (SW_HEADS * SW_HD + 2 * SW_KV * SW_HD) // pw
    return pl.pallas_call(
        functools.partial(_swa_prompt_kernel, n_blocks=tq // WINDOW, gblk0=gblk0),
        grid=(qblocks, bsz, tlen // tq),
        in_specs=[pl.BlockSpec(memory_space=pltpu.SMEM),
                  pl.BlockSpec(memory_space=pl.ANY),
                  pl.BlockSpec((1, tlen, LANES), lambda p, b, t: (b, 0, kcol0 + p)),
                  pl.BlockSpec((1, tlen, LANES), lambda p, b, t: (b, 0, vcol0 + p)),
                  pl.BlockSpec((WINDOW, LANES), lambda p, b, t: (0, p)),
                  pl.BlockSpec((WINDOW, LANES), lambda p, b, t: (0, p)),
                  pl.BlockSpec((2, 2, 4 * WINDOW, 4 * WINDOW), lambda p, b, t: (0, p, 0, 0))],
        out_specs=pl.BlockSpec((1, tq, pw), lambda p, b, t: (b, t, p)),
        out_shape=jax.ShapeDtypeStruct((bsz, tlen, D_MODEL), BF16),
        scratch_shapes=[pltpu.VMEM((3, tq, pw), F32), pltpu.VMEM((3, tq, pw), F32),
                        pltpu.SemaphoreType.DMA((3, 2))],
        compiler_params=pltpu.CompilerParams(
            dimension_semantics=("arbitrary", "arbitrary", "arbitrary"),
            vmem_limit_bytes=V7X_VMEM_LIMIT),
        name="swa_prompt",
    )(sinks, proj, proj, proj, meta_k, meta_v, bias)


def _swa_sample_kernel(q_ref, g_ref, kn_ref, vn_ref, ck_ref, cv_ref, bc_ref, bn_ref, sink_ref,
                       o_ref, nk_ref, nv_ref, *, nb):
    nkv = SW_KV * SW_HD
    r = ck_ref.shape[1]
    row = lax.broadcasted_iota(jnp.int32, (SW_HEADS, nkv), 0)
    col = lax.broadcasted_iota(jnp.int32, (SW_HEADS, nkv), 1)
    own = (row // SW_GROUP) == (col // SW_HD)
    last = lax.broadcasted_iota(jnp.int32, (r, nkv), 0) == r - 1
    sink = sink_ref[...] * LOG2E
    for bi in range(nb):
        q = q_ref[bi]
        qx = jnp.where(own, jnp.concatenate([q] * SW_KV, axis=1), 0.0)
        ck, cv = ck_ref[bi], cv_ref[bi]
        kn, vn = kn_ref[bi], vn_ref[bi]

        s_c = _nt(qx.astype(BF16), ck.astype(BF16)) + bc_ref[...]
        s_n = jnp.sum(qx * kn, axis=-1, keepdims=True) + bn_ref[...]
        m = jnp.maximum(jnp.maximum(jnp.max(s_c, axis=-1, keepdims=True), s_n), sink)
        p_c = jnp.exp2(s_c - m)
        p_n = jnp.exp2(s_n - m)
        den = jnp.sum(p_c, axis=-1, keepdims=True) + p_n + jnp.exp2(sink - m)
        o_all = _nn(p_c.astype(BF16), cv.astype(BF16))
        o_all = o_all + p_n * vn
        o_all = jnp.where(own, o_all, 0.0)
        o = o_all[:, 0:SW_HD]
        for n in range(1, SW_KV):
            o = o + o_all[:, n * SW_HD:(n + 1) * SW_HD]
        o_ref[bi] = ((o / den) * g_ref[bi]).astype(BF16)

        nk_ref[bi] = jnp.where(last, kn, pltpu.roll(ck, r - 1, axis=0))
        nv_ref[bi] = jnp.where(last, vn, pltpu.roll(cv, r - 1, axis=0))


def _swa_sample(q, g, kn, vn, ck, cv, bias_c, bias_n, sinks, *, nb):
    bsz, r, nkv = ck.shape
    head = pl.BlockSpec((nb, SW_HEADS, SW_HD), lambda b: (b, 0, 0))
    new = pl.BlockSpec((nb, 1, nkv), lambda b: (b, 0, 0))
    cache = pl.BlockSpec((nb, r, nkv), lambda b: (b, 0, 0))
    return pl.pallas_call(
        functools.partial(_swa_sample_kernel, nb=nb),
        grid=(bsz // nb,),
        in_specs=[head, head, new, new, cache, cache,
                  pl.BlockSpec((SW_HEADS, r), lambda b: (0, 0)),
                  pl.BlockSpec((SW_HEADS, 1), lambda b: (0, 0)),
                  pl.BlockSpec((SW_HEADS, 1), lambda b: (0, 0))],
        out_specs=[head, cache, cache],
        out_shape=[jax.ShapeDtypeStruct((bsz, SW_HEADS, SW_HD), BF16),
                   jax.ShapeDtypeStruct(ck.shape, ck.dtype),
                   jax.ShapeDtypeStruct(cv.shape, cv.dtype)],
        compiler_params=pltpu.CompilerParams(dimension_semantics=("parallel",)),
        name="swa_sample",
    )(q, g, kn, vn, ck, cv, bias_c, bias_n, sinks.reshape(SW_HEADS, 1))


def kernel(x_prompt, x_sample, state_hgrn, cache_k_win, cache_v_win, meta_tokens, rel_bias,
           hg_lower_bounds, hg_norm, hg_w_in, hg_onorm, hg_w_out,
           sw_norm, sw_w_in, sw_sinks, sw_w_out, final_norm):
    n_samp = x_sample.shape[0]
    samp = slice(N_META, N_META + n_samp)
    x_main = x_prompt.reshape(MAIN_ROWS, D_MODEL)
    x_small = jnp.concatenate(
        [meta_tokens.astype(F32), x_sample.reshape(n_samp, D_MODEL),
         jnp.zeros((N_SMALL - N_META - n_samp, D_MODEL), F32)], axis=0)

    h_main = _rmsnorm(x_main, hg_norm[0], BF16, 512)
    h_small = _rmsnorm(x_small, hg_norm[0], BF16, N_SMALL)
    p_main, p_small = _proj(h_main, h_small, hg_w_in[0], tm=PROJ_TM, tn=PROJ_TN,
                            sections="hgrn", lbraw=hg_lower_bounds.astype(F32))

    meta_proj = jnp.pad(p_small[:N_META], ((CHUNK - N_META, 0), (0, 0)))[None]
    zero_state = jnp.zeros((HG_HEADS, HG_DK, HG_DK), F32)
    a_meta, s_meta = _hgrn_scan(meta_proj, hg_onorm[0], zero_state,
                                tb=CHUNK, hb=8, n_pad=CHUNK - N_META)
    a_main, s_prompt = _hgrn_scan(p_main.reshape(BATCH, SEQ, -1), hg_onorm[0],
                                  s_meta[0], tb=512, hb=16)
    sect = lambda s: p_small[samp, s * HG_F:(s + 1) * HG_F].reshape(n_samp, HG_HEADS, HG_DK)
    a_samp, s_sample = _hgrn_step(sect(0), sect(1), sect(2), sect(3),
                                  hg_onorm[0], state_hgrn, nb=SAMPLES_PER_STEP)
    a_small = jnp.concatenate(
        [a_meta[0, CHUNK - N_META:], a_samp.reshape(n_samp, D_MODEL),
         jnp.zeros((N_SMALL - N_META - n_samp, D_MODEL), BF16)], axis=0)
    x1_main, x1_small, x1b_main, x1b_small, sq_main, sq_small = _proj(
        a_main.reshape(MAIN_ROWS, D_MODEL), a_small, hg_w_out[0], x_main, x_small,
        tm=PROJ_RES_TM, tn=PROJ_TN, emit_stats=True)

    p_main, p_small = _proj(x1b_main, x1b_small, sw_w_in[0],
                            scale=(sw_norm[0], jnp.sum(sq_main, axis=0, keepdims=True),
                                   jnp.sum(sq_small, axis=0, keepdims=True)),
                            tm=PROJ_TM, tn=PROJ_TN, sections="swa")
    nq, nkv = SW_HEADS * SW_HD, SW_KV * SW_HD
    kv_tail = p_main.reshape(BATCH, SEQ, -1)[:, -WINDOW:, nq:nq + 2 * nkv]
    k_tail = kv_tail[:, :, :nkv].reshape(1, BATCH, WINDOW, SW_KV, SW_HD)
    v_tail = kv_tail[:, :, nkv:].reshape(1, BATCH, WINDOW, SW_KV, SW_HD)
    meta_kv = jnp.pad(p_small[:N_META, nq:nq + 2 * nkv], ((WINDOW - N_META, 0), (0, 0)))
    a_main = _swa_prompt(p_main.reshape(BATCH, SEQ, -1), meta_kv[:, :nkv], meta_kv[:, nkv:],
                         sw_sinks[0], _prompt_bias(rel_bias), tq=1024)

    r = cache_k_win.shape[2]
    table = rel_bias.astype(F32)
    bias_c = table[_t5_bucket(r - jnp.arange(r))].T * LOG2E
    bias_n = table[_t5_bucket(jnp.zeros((1,), jnp.int32))].T * LOG2E
    q_s = p_small[samp, :nq].reshape(n_samp, SW_HEADS, SW_HD)
    g_s = p_small[samp, nq + 2 * nkv:].reshape(n_samp, SW_HEADS, SW_HD)
    a_samp, k_samp, v_samp = _swa_sample(
        q_s, g_s, p_small[samp, nq:nq + nkv].reshape(n_samp, 1, nkv),
        p_small[samp, nq + nkv:nq + 2 * nkv].reshape(n_samp, 1, nkv),
        cache_k_win[0].reshape(n_samp, r, nkv), cache_v_win[0].reshape(n_samp, r, nkv),
        bias_c, bias_n, sw_sinks[0], nb=SAMPLES_PER_STEP)
    a_small = jnp.concatenate(
        [jnp.zeros((N_META, D_MODEL), BF16), a_samp.reshape(n_samp, D_MODEL),
         jnp.zeros((N_SMALL - N_META - n_samp, D_MODEL), BF16)], axis=0)
    x2_main, x2_small = _proj(a_main.reshape(MAIN_ROWS, D_MODEL), a_small, sw_w_out[0],
                              x1_main, x1_small, tm=PROJ_RES_TM, tn=PROJ_TN)

    y_prompt = _rmsnorm(x2_main, final_norm, F32, 512).reshape(BATCH, SEQ, D_MODEL)
    y_sample = _rmsnorm(x2_small, final_norm, F32, N_SMALL)[samp].reshape(n_samp, 1, D_MODEL)
    return (y_prompt, y_sample, s_prompt[None],
            k_tail.astype(cache_k_win.dtype), v_tail.astype(cache_v_win.dtype),
            s_sample,
            k_samp.reshape(1, n_samp, r, SW_KV, SW_HD), v_samp.reshape(1, n_samp, r, SW_KV, SW_HD))
```
